```python
import jax, jax.numpy as jnp
from jax import lax
import numpy as np

D_MODEL = 1024
BATCH = 8
SEQ = 2048
DEPTH = 2

CHUNK = 64
GLA_HEADS = 4
GLA_DK = 64
GLA_DV = 128
GLA_GATE_RANK = 16
GLA_TAU = 16.0
RET_HEADS = 4
RET_DK = 64
RET_DV = 128
ROPE_BASE = 10000.0
GDN_HEADS = 8
GDN_DK = 128
GDN_DV = 128
CONV_WIDTH = 4
D_FF = 3584
N_EXPERTS = 8
TOP_K = 2
NORM_EPS = 1e-5
L2_EPS = 1e-6
DEEPNORM_ALPHA = (2.0 * DEPTH) ** 0.25
DEEPNORM_BETA = (8.0 * DEPTH) ** -0.25
N_EVEN = (DEPTH + 1) // 2
N_ODD = DEPTH // 2

GLA_QK = GLA_HEADS * GLA_DK
GLA_V = GLA_HEADS * GLA_DV
RET_QK = RET_HEADS * RET_DK
RET_V = RET_HEADS * RET_DV
MIX_AB = GLA_V + RET_V
AB_SIZES = (GLA_QK, GLA_QK, GLA_V, GLA_GATE_RANK, GLA_V, RET_QK, RET_QK, RET_V, RET_V)
GDN_QK = GDN_HEADS * GDN_DK
GDN_V = GDN_HEADS * GDN_DV
GDN_CONV_CH = 2 * GDN_QK + GDN_V
C_SIZES = (GDN_CONV_CH, GDN_HEADS, GDN_HEADS, GDN_V)

kernel_name = "hybrid_gla_retnet_gdn_moe_deepnorm"


def _offsets(sizes):
    return [int(s) for s in np.cumsum(sizes)[:-1]]


def layer_norm(x, g, b):
    xf = x.astype(jnp.float32)
    mu = jnp.mean(xf, -1, keepdims=True)
    xc = xf - mu
    y = xc * lax.rsqrt(jnp.mean(xc * xc, -1, keepdims=True) + NORM_EPS) * g + b
    return y.astype(x.dtype)


def head_rms_norm(x, g):
    return x * lax.rsqrt(jnp.mean(x * x, -1, keepdims=True) + NORM_EPS) * g


def head_group_norm(x, g):
    xc = x - jnp.mean(x, -1, keepdims=True)
    return xc * lax.rsqrt(jnp.mean(xc * xc, -1, keepdims=True) + NORM_EPS) * g


def l2_normalize(x):
    return x * lax.rsqrt(jnp.sum(x * x, -1, keepdims=True) + L2_EPS)


def rotary(x, positions):
    d = x.shape[-1]
    inv_freq = ROPE_BASE ** (-jnp.arange(0, d, 2, dtype=jnp.float32) / d)
    ang = positions.astype(jnp.float32)[..., None] * inv_freq
    cos, sin = jnp.cos(ang)[:, :, None, :], jnp.sin(ang)[:, :, None, :]
    x1, x2 = x[..., : d // 2], x[..., d // 2:]
    return jnp.concatenate([x1 * cos - x2 * sin, x1 * sin + x2 * cos], -1)


def to_chunks(t):
    return t.reshape(t.shape[0], t.shape[1] // CHUNK, CHUNK, *t.shape[2:])


def chunk_states(decay, d_state):
    def step(s, inp):
        dec, ds = inp
        return s * dec[..., None] + ds, s
    s0 = jnp.zeros_like(d_state[:, 0])
    _, s_in = lax.scan(step, s0, (jnp.moveaxis(decay, 1, 0), jnp.moveaxis(d_state, 1, 0)))
    return jnp.moveaxis(s_in, 0, 1)


def gla_chunked(q, k, v, log_a):
    B, T, H, dk = q.shape
    q, k, v, log_a = to_chunks(q * dk ** -0.5), to_chunks(k), to_chunks(v), to_chunks(log_a)
    b = jnp.cumsum(log_a, axis=2)
    b_last = b[:, :, -1]
    q_dec = q * jnp.exp(b)
    causal = jnp.tril(jnp.ones((CHUNK, CHUNK), dtype=bool))
    att = jnp.einsum('bnihk,bnjhk->bnhij', q_dec, k * jnp.exp(-b))
    att = jnp.where(causal, att, 0.0)
    o_intra = jnp.einsum('bnhij,bnjhv->bnihv', att, v)
    k_end = k * jnp.exp(b_last[:, :, None] - b)
    d_state = jnp.einsum('bnjhk,bnjhv->bnhkv', k_end, v)
    s_in = chunk_states(jnp.exp(b_last), d_state)
    o_inter = jnp.einsum('bnihk,bnhkv->bnihv', q_dec, s_in)
    return (o_intra + o_inter).reshape(B, T, H, -1)


def retention_chunked(q, k, v):
    B, T, H, dk = q.shape
    log_gamma = jnp.log(1.0 - 2.0 ** (-5.0 - jnp.arange(H, dtype=jnp.float32)))
    q, k, v = to_chunks(q * dk ** -0.5), to_chunks(k), to_chunks(v)
    pos = jnp.arange(CHUNK, dtype=jnp.float32)
    diff = pos[:, None] - pos[None, :]
    dmat = jnp.where(diff >= 0, jnp.exp(log_gamma[:, None, None] * jnp.maximum(diff, 0.0)), 0.0)
    att = jnp.einsum('bnihk,bnjhk->bnhij', q, k) * dmat
    o_intra = jnp.einsum('bnhij,bnjhv->bnihv', att, v)
    xi = jnp.exp(log_gamma[None, :] * (pos[:, None] + 1.0))[:, :, None]
    zeta = jnp.exp(log_gamma[None, :] * (CHUNK - 1.0 - pos[:, None]))[:, :, None]
    d_state = jnp.einsum('bnjhk,bnjhv->bnhkv', k * zeta, v)
    decay = jnp.broadcast_to(jnp.exp(log_gamma * CHUNK)[:, None], d_state.shape[:-1])
    s_in = chunk_states(decay, d_state)
    o_inter = jnp.einsum('bnihk,bnhkv->bnihv', q, s_in) * xi
    return (o_intra + o_inter).reshape(B, T, H, -1)


def causal_conv(x, w):
    return lax.conv_general_dilated(
        x, w[:, None, :], window_strides=(1,), padding=[(CONV_WIDTH - 1, 0)],
        dimension_numbers=('NWC', 'WIO', 'NWC'), feature_group_count=x.shape[-1])


def gated_delta_rule_chunked(q, k, v, beta, g):
    B, T, H, dk = q.shape
    dv = v.shape[-1]
    n = T // CHUNK
    hc = lambda t: jnp.swapaxes(to_chunks(t), 2, 3)
    qc, kc, vc = hc(q * dk ** -0.5), hc(k), hc(v)
    bc, gc = hc(beta), jnp.cumsum(hc(g), axis=-1)
    incl = jnp.tril(jnp.ones((CHUNK, CHUNK), dtype=bool))
    strict = jnp.tril(jnp.ones((CHUNK, CHUNK), dtype=bool), -1)
    decay = jnp.exp(jnp.where(incl, gc[..., :, None] - gc[..., None, :], -jnp.inf))
    k_beta = kc * bc[..., None]
    lower = jnp.where(strict, jnp.einsum('bnhik,bnhjk->bnhij', k_beta, kc) * decay, 0.0)
    eye = jnp.broadcast_to(jnp.eye(CHUNK, dtype=jnp.float32), lower.shape)
    t_inv = lax.linalg.triangular_solve(lower, eye, left_side=True, lower=True, unit_diagonal=True)
    u = jnp.einsum('bnhij,bnhjv->bnhiv', t_inv, vc * bc[..., None])
    w = jnp.einsum('bnhij,bnhjk->bnhik', t_inv, k_beta * jnp.exp(gc)[..., None])
    a_qk = jnp.einsum('bnhik,bnhjk->bnhij', qc, kc) * decay
    q_dec = qc * jnp.exp(gc)[..., None]
    g_last = gc[..., -1]
    k_dec = kc * jnp.exp(g_last[..., None] - gc)[..., None]

    def step(s, inp):
        u_n, w_n, a_n, qd_n, kd_n, gl_n = inp
        v_new = u_n - jnp.einsum('bhck,bhkv->bhcv', w_n, s)
        o_n = jnp.einsum('bhck,bhkv->bhcv', qd_n, s) + jnp.einsum('bhij,bhjv->bhiv', a_n, v_new)
        s = s * jnp.exp(gl_n)[..., None, None] + jnp.einsum('bhck,bhcv->bhkv', kd_n, v_new)
        return s, o_n

    s0 = jnp.zeros((B, H, dk, dv), jnp.float32)
    xs = tuple(jnp.moveaxis(t, 1, 0) for t in (u, w, a_qk, q_dec, k_dec, g_last))
    _, o = lax.scan(step, s0, xs)
    return jnp.transpose(o, (1, 0, 3, 2, 4)).reshape(B, T, H, dv)


def mixer_gla_retention(x, positions, w_in, gla_w_gate2, gla_b_gate, gla_norm, ret_norm, w_out):
    B, T, _ = x.shape
    f32 = jnp.float32
    h = x @ w_in
    gq, gk, gv, ga, gr, rq, rk, rv, rg = jnp.split(h, _offsets(AB_SIZES), axis=-1)
    heads = lambda t, nh: t.astype(f32).reshape(B, T, nh, -1)
    log_a = jax.nn.log_sigmoid((ga @ gla_w_gate2 + gla_b_gate).astype(f32)) / GLA_TAU
    o_a = gla_chunked(heads(gq, GLA_HEADS), heads(gk, GLA_HEADS), heads(gv, GLA_HEADS), heads(log_a, GLA_HEADS))
    o_a = head_rms_norm(o_a, gla_norm).reshape(B, T, GLA_V) * jax.nn.silu(gr.astype(f32))
    q_r = rotary(heads(rq, RET_HEADS), positions)
    k_r = rotary(heads(rk, RET_HEADS), positions)
    o_b = retention_chunked(q_r, k_r, heads(rv, RET_HEADS))
    o_b = head_group_norm(o_b, ret_norm).reshape(B, T, RET_V) * jax.nn.silu(rg.astype(f32))
    o = jnp.concatenate([o_a, o_b], axis=-1).astype(x.dtype)
    return o @ w_out


def mixer_gated_deltanet(x, w_in, conv_w, a_log, dt_bias, out_norm, w_out):
    B, T, _ = x.shape
    f32 = jnp.float32
    h = x @ w_in
    qkv, a_in, b_in, gate = jnp.split(h, _offsets(C_SIZES), axis=-1)
    qkv = jax.nn.silu(causal_conv(qkv, conv_w)).astype(f32)
    q, k, v = jnp.split(qkv, [GDN_QK, 2 * GDN_QK], axis=-1)
    q = l2_normalize(q.reshape(B, T, GDN_HEADS, GDN_DK))
    k = l2_normalize(k.reshape(B, T, GDN_HEADS, GDN_DK))
    v = v.reshape(B, T, GDN_HEADS, GDN_DV)
    beta = jax.nn.sigmoid(b_in.astype(f32))
    g = -jnp.exp(a_log.astype(f32)) * jax.nn.softplus(a_in.astype(f32) + dt_bias)
    o = gated_delta_rule_chunked(q, k, v, beta, g)
    o = head_rms_norm(o, out_norm).reshape(B, T, GDN_V) * jax.nn.silu(gate.astype(f32))
    return o.astype(x.dtype) @ w_out


def swiglu(x, w_gu, w_down):
    gt, up = jnp.split(x @ w_gu, 2, axis=-1)
    return (jax.nn.silu(gt) * up) @ w_down


def moe_swiglu(x, w_router, b_router, w_gu, w_down):
    B, T, D = x.shape
    xt = x.reshape(B * T, D)
    logits = (xt @ w_router + b_router).astype(jnp.float32)
    top_val, top_idx = lax.top_k(logits, TOP_K)
    gates = jax.nn.softmax(top_val, axis=-1)
    combine = jnp.sum(jax.nn.one_hot(top_idx, N_EXPERTS, dtype=jnp.float32) * gates[..., None], axis=1)
    y = jnp.zeros_like(xt)
    for e in range(N_EXPERTS):
        y = y + combine[:, e:e + 1].astype(x.dtype) * swiglu(xt, w_gu[e], w_down[e])
    return y.reshape(B, T, D)


def setup_inputs(seed: int = 0) -> dict:
    key = jax.random.key(seed)
    ks = iter(jax.random.split(key, 40))
    nrm = lambda shape, scale: jax.random.normal(next(ks), shape, jnp.float32) * scale
    D, E, O = D_MODEL, N_EVEN, N_ODD
    x = jax.random.normal(next(ks), (BATCH, SEQ, D), jnp.float32)
    offset = jax.random.randint(next(ks), (BATCH, 1), 0, 4096, dtype=jnp.int32)
    positions = offset + jnp.arange(SEQ, dtype=jnp.int32)[None, :]
    ab_w_in = nrm((E, D, sum(AB_SIZES)), D ** -0.5)
    gla_w_gate2 = nrm((E, GLA_GATE_RANK, GLA_QK), GLA_GATE_RANK ** -0.5)
    gla_b_gate = nrm((E, GLA_QK), 0.1)
    gla_norm = 1.0 + nrm((E, GLA_HEADS, GLA_DV), 0.02)
    ret_norm = 1.0 + nrm((E, RET_HEADS, RET_DV), 0.02)
    ab_w_out = nrm((E, MIX_AB, D), MIX_AB ** -0.5 * DEEPNORM_BETA)
    ab_ln1_g = 1.0 + nrm((E, D), 0.02)
    ab_ln1_b = nrm((E, D), 0.02)
    ffn_w_gu = nrm((E, D, 2 * D_FF), D ** -0.5)
    ffn_w_down = nrm((E, D_FF, D), D_FF ** -0.5 * DEEPNORM_BETA)
    ab_ln2_g = 1.0 + nrm((E, D), 0.02)
    ab_ln2_b = nrm((E, D), 0.02)
    c_w_in = nrm((O, D, sum(C_SIZES)), D ** -0.5)
    c_conv_w = nrm((O, CONV_WIDTH, GDN_CONV_CH), CONV_WIDTH ** -0.5)
    c_a_log = jnp.log(jax.random.uniform(next(ks), (O, GDN_HEADS), jnp.float32, 1.0, 16.0))
    dt = jnp.exp(jax.random.uniform(next(ks), (O, GDN_HEADS), jnp.float32,
                                    float(np.log(1e-3)), float(np.log(1e-1))))
    c_dt_bias = dt + jnp.log(-jnp.expm1(-dt))
    c_norm = 1.0 + nrm((O, GDN_HEADS, GDN_DV), 0.02)
    c_w_out = nrm((O, GDN_V, D), GDN_V ** -0.5 * DEEPNORM_BETA)
    c_ln1_g = 1.0 + nrm((O, D), 0.02)
    c_ln1_b = nrm((O, D), 0.02)
    moe_w_router = nrm((O, D, N_EXPERTS), D ** -0.5)
    moe_b_router = nrm((O, N_EXPERTS), 0.01)
    moe_w_gu = nrm((O, N_EXPERTS, D, 2 * D_FF), D ** -0.5)
    moe_w_down = nrm((O, N_EXPERTS, D_FF, D), D_FF ** -0.5 * DEEPNORM_BETA)
    c_ln2_g = 1.0 + nrm((O, D), 0.02)
    c_ln2_b = nrm((O, D), 0.02)
    return {"x": x, "positions": positions,
            "ab_w_in": ab_w_in, "gla_w_gate2": gla_w_gate2, "gla_b_gate": gla_b_gate,
            "gla_norm": gla_norm, "ret_norm": ret_norm, "ab_w_out": ab_w_out,
            "ab_ln1_g": ab_ln1_g, "ab_ln1_b": ab_ln1_b, "ffn_w_gu": ffn_w_gu, "ffn_w_down": ffn_w_down,
            "ab_ln2_g": ab_ln2_g, "ab_ln2_b": ab_ln2_b,
            "c_w_in": c_w_in, "c_conv_w": c_conv_w, "c_a_log": c_a_log, "c_dt_bias": c_dt_bias,
            "c_norm": c_norm, "c_w_out": c_w_out, "c_ln1_g": c_ln1_g, "c_ln1_b": c_ln1_b,
            "moe_w_router": moe_w_router, "moe_b_router": moe_b_router,
            "moe_w_gu": moe_w_gu, "moe_w_down": moe_w_down, "c_ln2_g": c_ln2_g, "c_ln2_b": c_ln2_b}


def reference(x, positions, ab_w_in, gla_w_gate2, gla_b_gate, gla_norm, ret_norm, ab_w_out,
              ab_ln1_g, ab_ln1_b, ffn_w_gu, ffn_w_down, ab_ln2_g, ab_ln2_b,
              c_w_in, c_conv_w, c_a_log, c_dt_bias, c_norm, c_w_out, c_ln1_g, c_ln1_b,
              moe_w_router, moe_b_router, moe_w_gu, moe_w_down, c_ln2_g, c_ln2_b):
    for layer in range(DEPTH):
        i = layer // 2
        if layer % 2 == 0:
            h = mixer_gla_retention(x, positions, ab_w_in[i], gla_w_gate2[i], gla_b_gate[i],
                                    gla_norm[i], ret_norm[i], ab_w_out[i])
            x = layer_norm(DEEPNORM_ALPHA * x + h, ab_ln1_g[i], ab_ln1_b[i])
            x = layer_norm(DEEPNORM_ALPHA * x + swiglu(x, ffn_w_gu[i], ffn_w_down[i]),
                           ab_ln2_g[i], ab_ln2_b[i])
        else:
            h = mixer_gated_deltanet(x, c_w_in[i], c_conv_w[i], c_a_log[i], c_dt_bias[i],
                                     c_norm[i], c_w_out[i])
            x = layer_norm(DEEPNORM_ALPHA * x + h, c_ln1_g[i], c_ln1_b[i])
            y = moe_swiglu(x, moe_w_router[i], moe_b_router[i], moe_w_gu[i], moe_w_down[i])
            x = layer_norm(DEEPNORM_ALPHA * x + y, c_ln2_g[i], c_ln2_b[i])
    return x
```

```python
import functools

import numpy as np
import jax
import jax.numpy as jnp
from jax import lax
from jax.experimental import pallas as pl
from jax.experimental.pallas import tpu as pltpu

F32 = jnp.float32
BF16 = jnp.bfloat16

D_MODEL = 1024
DEPTH = 2
CHUNK = 64
GLA_HEADS, GLA_DK, GLA_DV, GLA_GATE_RANK, GLA_TAU = 4, 64, 128, 16, 16.0
RET_HEADS, RET_DK, RET_DV = 4, 64, 128
ROPE_BASE = 10000.0
GDN_HEADS, GDN_DK, GDN_DV = 8, 128, 128
CONV_WIDTH = 4
D_FF = 3584
N_EXPERTS = 8
TOP_K = 2
NORM_EPS = 1e-5
L2_EPS = 1e-6
DEEPNORM_ALPHA = (2.0 * DEPTH) ** 0.25

GLA_QK = GLA_HEADS * GLA_DK
GLA_V = GLA_HEADS * GLA_DV
RET_QK = RET_HEADS * RET_DK
RET_V = RET_HEADS * RET_DV
GDN_QK = GDN_HEADS * GDN_DK
GDN_V = GDN_HEADS * GDN_DV

LANES = 128
VMEM_LIMIT_BYTES = 56 * 1024 * 1024
NEG_BIG = -1e30

L0_TIME_TILE = 256
L1_TIME_TILE = 256
FFN_ROW_TILE = 1024
MOE_ROW_TILE = 512
FF_TILE = 512
ROUTE_TILE = 512


def _mm(a, b):
    return jnp.dot(a.astype(BF16), b.astype(BF16), preferred_element_type=F32)


def _mm_nt(a, b):
    return lax.dot_general(a.astype(BF16), b.astype(BF16), (((1,), (1,)), ((), ())),
                           preferred_element_type=F32)


def _mm_tn(a, b):
    return lax.dot_general(a.astype(BF16), b.astype(BF16), (((0,), (0,)), ((), ())),
                           preferred_element_type=F32)


def _split3(x):
    hi = x.astype(BF16)
    r1 = x - hi.astype(F32)
    mid = r1.astype(BF16)
    lo = (r1 - mid.astype(F32)).astype(BF16)
    return hi, mid, lo


def _mm_exact_lhs01(m01, x):
    hi, mid, lo = _split3(x)
    return (jnp.dot(m01, hi, preferred_element_type=F32)
            + jnp.dot(m01, mid, preferred_element_type=F32)
            + jnp.dot(m01, lo, preferred_element_type=F32))


def _sigmoid(x):
    return 1.0 / (1.0 + jnp.exp(-x))


def _silu(x):
    return x * _sigmoid(x)


def _softplus(x):
    return jnp.maximum(x, 0.0) + jnp.log(1.0 + jnp.exp(-jnp.abs(x)))


def _layer_norm(x, g, b):
    mu = jnp.mean(x, axis=-1, keepdims=True)
    xc = x - mu
    var = jnp.mean(xc * xc, axis=-1, keepdims=True)
    return xc * lax.rsqrt(var + NORM_EPS) * g + b


def _chunk_tril_np(tt):
    i = np.arange(tt)
    same = (i[:, None] // CHUNK) == (i[None, :] // CHUNK)
    return (same & (i[None, :] <= i[:, None])).astype(np.float32)


def _const_spec(shape):
    nd = len(shape)
    return pl.BlockSpec(shape, lambda *_: (0,) * nd)


L0_GQ, L0_GK, L0_GV, L0_GR = 0, 256, 512, 1024
L0_RQ, L0_RK, L0_RV, L0_RG = 1536, 1792, 2048, 2560
L0_GA = 3072
L0_COLS = 3200


def _pack_l0_w_in(w):
    offs = np.cumsum([0, GLA_QK, GLA_QK, GLA_V, GLA_GATE_RANK, GLA_V, RET_QK, RET_QK, RET_V, RET_V])
    gq, gk, gv, ga, gr, rq, rk, rv, rg = [w[:, offs[i]:offs[i + 1]] for i in range(9)]
    ga = jnp.pad(ga, ((0, 0), (0, LANES - GLA_GATE_RANK)))
    return jnp.concatenate([gq, gk, gv, gr, rq, rk, rv, rg, ga], axis=1)


def _ret_tables():
    h = np.arange(RET_HEADS, dtype=np.float64)
    log_gamma = np.log(1.0 - 2.0 ** (-5.0 - h))
    pos = np.arange(CHUNK, dtype=np.float64)
    diff = pos[:, None] - pos[None, :]
    dmat = np.where(diff >= 0, np.exp(log_gamma[:, None, None] * np.maximum(diff, 0.0)), 0.0)
    xi = np.exp(log_gamma[None, :] * (pos[:, None] + 1.0))
    zeta = np.exp(log_gamma[None, :] * (CHUNK - 1.0 - pos[:, None]))
    decay = np.exp(log_gamma * CHUNK)
    xi_full = np.repeat(xi, RET_DV, axis=1)
    zeta_full = np.repeat(zeta, RET_DK, axis=1)
    decay_full = np.repeat(decay, RET_DK)[None, :]
    return (dmat.astype(np.float32), xi_full.astype(np.float32), zeta_full.astype(np.float32),
            decay_full.astype(np.float32))


def _rope_tables():
    half = RET_DK // 2
    inv_freq = ROPE_BASE ** (-np.arange(0, RET_DK, 2, dtype=np.float32) / RET_DK)
    per_head = np.concatenate([inv_freq, inv_freq])
    freq_full = np.tile(per_head, RET_HEADS)[None, :].astype(np.float32)
    sign = np.tile(np.concatenate([-np.ones(half), np.ones(half)]), RET_HEADS)[None, :].astype(np.float32)
    return freq_full, sign


def _l0_mixer_kernel(x_ref, pos_ref, w_in_ref, wg2_ref, bg_ref, gnorm_ref, rnorm_ref, w_out_ref,
                     lng_ref, lnb_ref, tril_ref, dmat_ref, xi_ref, zeta_ref, rdecay_ref,
                     freq_ref, sign_ref, o_ref, h_ref, mix_ref, sg_ref, sr_ref, *, tt):
    ti = pl.program_id(1)

    @pl.when(ti == 0)
    def _():
        sg_ref[...] = jnp.zeros_like(sg_ref)
        sr_ref[...] = jnp.zeros_like(sr_ref)

    x = x_ref[...]
    h_ref[...] = _mm(x, w_in_ref[...])

    lane = lax.broadcasted_iota(jnp.int32, (1, LANES), 1)
    lo_half = lane < GLA_DK
    ci = lax.broadcasted_iota(jnp.int32, (CHUNK, CHUNK), 0)
    cj = lax.broadcasted_iota(jnp.int32, (CHUNK, CHUNK), 1)
    causal = cj <= ci

    z = _mm(h_ref[:, L0_GA:L0_GA + LANES], wg2_ref[...]) + bg_ref[...]
    log_a = -_softplus(-z) * (1.0 / GLA_TAU)
    b = _mm_exact_lhs01(tril_ref[...], log_a)
    eb = jnp.exp(b)
    q_dec = h_ref[:, L0_GQ:L0_GQ + GLA_QK] * (GLA_DK ** -0.5) * eb
    k_all = h_ref[:, L0_GK:L0_GK + GLA_QK]
    k_neg = k_all * jnp.exp(-b)

    ang = pos_ref[...] * freq_ref[...]
    cos = jnp.cos(ang)
    sin = jnp.sin(ang) * sign_ref[...]
    half = RET_DK // 2
    lane256 = lax.broadcasted_iota(jnp.int32, (1, RET_QK), 1)
    first_half = (lane256 & (RET_DK - 1)) < half

    def rope(t):
        swapped = jnp.where(first_half, pltpu.roll(t, RET_QK - half, 1), pltpu.roll(t, half, 1))
        return t * cos + swapped * sin

    rq = rope(h_ref[:, L0_RQ:L0_RQ + RET_QK]) * (RET_DK ** -0.5)
    rk = rope(h_ref[:, L0_RK:L0_RK + RET_QK])

    n_chunks = tt // CHUNK
    for c in range(n_chunks):
        rows = slice(c * CHUNK, (c + 1) * CHUNK)
        b_last = b[(c + 1) * CHUNK - 1:(c + 1) * CHUNK, :]
        k_end_c = k_all[rows] * jnp.exp(b_last - b[rows])
        dec_c = jnp.exp(b_last)
        rk_z = rk[rows] * zeta_ref[...]
        for pair in range(GLA_HEADS // 2):
            pl_ = slice(pair * LANES, (pair + 1) * LANES)
            qd_p, kn_p, ke_p = q_dec[rows, pl_], k_neg[rows, pl_], k_end_c[:, pl_]
            s_pair = sg_ref[pair]
            d_halves = []
            for sub in range(2):
                hd = 2 * pair + sub
                msk = lo_half if sub == 0 else jnp.logical_not(lo_half)
                qd = jnp.where(msk, qd_p, 0.0)
                v = h_ref[rows, L0_GV + hd * GLA_DV:L0_GV + (hd + 1) * GLA_DV]
                att = jnp.where(causal, _mm_nt(qd, kn_p), 0.0)
                o = _mm(att, v) + _mm_nt(qd, s_pair)
                mix_ref[rows, hd * GLA_DV:(hd + 1) * GLA_DV] = o
                d_halves.append(_mm_tn(v, ke_p))
            d_pair = jnp.where(lo_half, d_halves[0], d_halves[1])
            sg_ref[pair] = s_pair * dec_c[:, pl_] + d_pair
            rq_p, rk_p, rkz_p = rq[rows, pl_], rk[rows, pl_], rk_z[:, pl_]
            r_pair = sr_ref[pair]
            d_halves = []
            for sub in range(2):
                hd = 2 * pair + sub
                msk = lo_half if sub == 0 else jnp.logical_not(lo_half)
                qh = jnp.where(msk, rq_p, 0.0)
                v = h_ref[rows, L0_RV + hd * RET_DV:L0_RV + (hd + 1) * RET_DV]
                att = _mm_nt(qh, rk_p) * dmat_ref[hd]
                o = _mm(att, v) + _mm_nt(qh, r_pair) * xi_ref[:, hd * RET_DV:(hd + 1) * RET_DV]
                mix_ref[rows, GLA_V + hd * RET_DV:GLA_V + (hd + 1) * RET_DV] = o
                d_halves.append(_mm_tn(v, rkz_p))
            d_pair = jnp.where(lo_half, d_halves[0], d_halves[1])
            sr_ref[pair] = r_pair * rdecay_ref[:, pl_] + d_pair

    for hd in range(GLA_HEADS):
        sl = slice(hd * GLA_DV, (hd + 1) * GLA_DV)
        o = mix_ref[:, sl]
        o = o * lax.rsqrt(jnp.mean(o * o, axis=-1, keepdims=True) + NORM_EPS) * gnorm_ref[:, sl]
        mix_ref[:, sl] = o * _silu(h_ref[:, L0_GR + hd * GLA_DV:L0_GR + (hd + 1) * GLA_DV])
    for hd in range(RET_HEADS):
        sl = slice(hd * RET_DV, (hd + 1) * RET_DV)
        o = mix_ref[:, GLA_V + hd * RET_DV:GLA_V + (hd + 1) * RET_DV]
        oc = o - jnp.mean(o, axis=-1, keepdims=True)
        o = oc * lax.rsqrt(jnp.mean(oc * oc, axis=-1, keepdims=True) + NORM_EPS) * rnorm_ref[:, sl]
        mix_ref[:, GLA_V + hd * RET_DV:GLA_V + (hd + 1) * RET_DV] = (
            o * _silu(h_ref[:, L0_RG + hd * RET_DV:L0_RG + (hd + 1) * RET_DV]))

    y = _mm(mix_ref[...], w_out_ref[...])
    o_ref[...] = _layer_norm(DEEPNORM_ALPHA * x + y, lng_ref[...], lnb_ref[...])


def _l0_mixer(x, positions, w_in, w_gate2, b_gate, gla_norm, ret_norm, w_out, ln_g, ln_b, *, tt):
    bsz, seq, d = x.shape
    w_in_p = _pack_l0_w_in(w_in).astype(BF16)
    wg2 = jnp.pad(w_gate2, ((0, LANES - GLA_GATE_RANK), (0, 0))).astype(BF16)
    dmat, xi_full, zeta_full, rdecay = _ret_tables()
    freq_full, sign = _rope_tables()
    pos_f = positions.astype(F32)[..., None]
    consts = [jnp.asarray(_chunk_tril_np(tt), BF16), jnp.asarray(dmat), jnp.asarray(xi_full),
              jnp.asarray(zeta_full), jnp.asarray(rdecay), jnp.asarray(freq_full), jnp.asarray(sign)]
    params = [w_in_p, wg2, b_gate[None, :], gla_norm.reshape(1, GLA_V), ret_norm.reshape(1, RET_V),
              w_out.astype(BF16), ln_g[None, :], ln_b[None, :]]
    tile = lambda w: pl.BlockSpec((None, tt, w), lambda b, t: (b, t, 0))
    return pl.pallas_call(
        functools.partial(_l0_mixer_kernel, tt=tt),
        grid=(bsz, seq // tt),
        in_specs=[tile(d), tile(1)] + [_const_spec(a.shape) for a in params + consts],
        out_specs=tile(d),
        out_shape=jax.ShapeDtypeStruct((bsz, seq, d), F32),
        scratch_shapes=[pltpu.VMEM((tt, L0_COLS), F32), pltpu.VMEM((tt, GLA_V + RET_V), F32),
                        pltpu.VMEM((GLA_HEADS // 2, GLA_DV, LANES), F32),
                        pltpu.VMEM((RET_HEADS // 2, RET_DV, LANES), F32)],
        compiler_params=pltpu.CompilerParams(dimension_semantics=("arbitrary", "arbitrary"),
                                             vmem_limit_bytes=VMEM_LIMIT_BYTES),
        name="l0_mixer",
    )(x, pos_f, *params, *consts)


def _ffn_kernel(x_ref, wg_ref, wu_ref, wd_ref, lng_ref, lnb_ref, o_ref, xb_ref, acc_ref):
    j = pl.program_id(1)

    @pl.when(j == 0)
    def _():
        xb_ref[...] = x_ref[...].astype(BF16)
        acc_ref[...] = jnp.zeros_like(acc_ref)

    xb = xb_ref[...]
    gt = jnp.dot(xb, wg_ref[...], preferred_element_type=F32)
    up = jnp.dot(xb, wu_ref[...], preferred_element_type=F32)
    acc_ref[...] += _mm(_silu(gt) * up, wd_ref[...])

    @pl.when(j == pl.num_programs(1) - 1)
    def _():
        o_ref[...] = _layer_norm(DEEPNORM_ALPHA * x_ref[...] + acc_ref[...], lng_ref[...], lnb_ref[...])


def _ffn(x2d, w_gu, w_down, ln_g, ln_b, *, tm, tf):
    n, d = x2d.shape
    nf = D_FF // tf
    w_gu_b, w_down_b = w_gu.astype(BF16), w_down.astype(BF16)
    return pl.pallas_call(
        _ffn_kernel,
        grid=(n // tm, nf),
        in_specs=[pl.BlockSpec((tm, d), lambda i, j: (i, 0)),
                  pl.BlockSpec((d, tf), lambda i, j: (0, j)),
                  pl.BlockSpec((d, tf), lambda i, j: (0, j + nf)),
                  pl.BlockSpec((tf, d), lambda i, j: (j, 0)),
                  _const_spec((1, d)), _const_spec((1, d))],
        out_specs=pl.BlockSpec((tm, d), lambda i, j: (i, 0)),
        out_shape=jax.ShapeDtypeStruct((n, d), F32),
        scratch_shapes=[pltpu.VMEM((tm, d), BF16), pltpu.VMEM((tm, d), F32)],
        compiler_params=pltpu.CompilerParams(dimension_semantics=("arbitrary", "arbitrary"),
                                             vmem_limit_bytes=VMEM_LIMIT_BYTES),
        name="ffn",
    )(x2d, w_gu_b, w_gu_b, w_down_b, ln_g[None, :], ln_b[None, :])


L1_Q, L1_K, L1_V, L1_GATE, L1_AB = 0, 1024, 2048, 3072, 4096
L1_CONV = 3 * GDN_QK
L1_COLS = 4224
L1_BETA_LANE = GDN_HEADS
CONV_PAD = 8
ROUTE_LANES = LANES


def _pack_l1_w_in(w):
    offs = np.cumsum([0, L1_CONV, GDN_HEADS, GDN_HEADS, GDN_V])
    qkv, a_in, b_in, gate = [w[:, offs[i]:offs[i + 1]] for i in range(4)]
    ab = jnp.pad(jnp.concatenate([a_in, b_in], axis=1), ((0, 0), (0, LANES - 2 * GDN_HEADS)))
    return jnp.concatenate([qkv, gate, ab], axis=1)


def _l1_mixer_kernel(x_ref, w_in_ref, conv_ref, alog_ref, dtb_ref, cnorm_ref, w_out_ref, lng_ref, lnb_ref,
                     tril_ref, wr_hi_ref, wr_lo_ref, br_ref,
                     o_ref, route_ref, ext_ref, h2_ref, qkv_ref, mix_ref, s_ref, *, tt):
    ti = pl.program_id(1)

    @pl.when(ti == 0)
    def _():
        s_ref[...] = jnp.zeros_like(s_ref)
        ext_ref[0:CONV_PAD, :] = jnp.zeros((CONV_PAD, L1_CONV), F32)

    x = x_ref[...]
    xb = x.astype(BF16)
    ext_ref[CONV_PAD:CONV_PAD + tt, :] = jnp.dot(xb, w_in_ref[:, 0:L1_CONV], preferred_element_type=F32)
    h2_ref[...] = jnp.dot(xb, w_in_ref[:, L1_CONV:L1_COLS], preferred_element_type=F32)

    conv = ext_ref[CONV_PAD:CONV_PAD + tt, :] * conv_ref[CONV_WIDTH - 1:CONV_WIDTH, :]
    for j in range(CONV_WIDTH - 1):
        back = CONV_WIDTH - 1 - j
        conv = conv + ext_ref[CONV_PAD - back:CONV_PAD - back + tt, :] * conv_ref[j:j + 1, :]
    ext_ref[0:CONV_PAD, :] = ext_ref[tt:tt + CONV_PAD, :]
    qkv_ref[...] = _silu(conv)

    for hd in range(GDN_HEADS):
        for base, scale in ((L1_Q, GDN_DK ** -0.5), (L1_K, 1.0)):
            sl = slice(base + hd * GDN_DK, base + (hd + 1) * GDN_DK)
            t = qkv_ref[:, sl]
            qkv_ref[:, sl] = t * (lax.rsqrt(jnp.sum(t * t, axis=-1, keepdims=True) + L2_EPS) * scale)

    ab = h2_ref[:, L1_AB - L1_CONV:L1_AB - L1_CONV + LANES]
    g_blk = -jnp.exp(alog_ref[...]) * _softplus(ab + dtb_ref[...])
    beta_blk = _sigmoid(ab)
    gc_blk = _mm_exact_lhs01(tril_ref[...], g_blk)
    eg_blk = jnp.exp(gc_blk)

    ci = lax.broadcasted_iota(jnp.int32, (CHUNK, CHUNK), 0)
    cj = lax.broadcasted_iota(jnp.int32, (CHUNK, CHUNK), 1)
    incl = cj <= ci
    strict = cj < ci

    for c in range(tt // CHUNK):
        rows = slice(c * CHUNK, (c + 1) * CHUNK)
        gc_c = gc_blk[rows]
        gc_t = gc_c.T
        g_last_row = gc_c[CHUNK - 1:CHUNK, :]
        kdec_scale = jnp.exp(g_last_row - gc_c)
        e_last = jnp.exp(g_last_row)
        for hd in range(GDN_HEADS):
            qh = qkv_ref[rows, L1_Q + hd * GDN_DK:L1_Q + (hd + 1) * GDN_DK]
            kh = qkv_ref[rows, L1_K + hd * GDN_DK:L1_K + (hd + 1) * GDN_DK]
            vh = qkv_ref[rows, L1_V + hd * GDN_DV:L1_V + (hd + 1) * GDN_DV]
            beta_col = beta_blk[rows, L1_BETA_LANE + hd:L1_BETA_LANE + hd + 1]
            gc_col = gc_c[:, hd:hd + 1]
            eg_col = eg_blk[rows, hd:hd + 1]
            decay = jnp.exp(jnp.where(incl, gc_col - gc_t[hd:hd + 1, :], NEG_BIG))
            kb = kh * beta_col
            low = jnp.where(strict, _mm_nt(kb, kh) * decay, 0.0)
            a_m = -low
            m = _mm(low, low)
            for it in range(5):
                a_m = a_m + m + _mm(a_m, m)
                if it < 4:
                    m = _mm(m, m)
            rhs = jnp.concatenate([vh * beta_col, kb * eg_col], axis=1)
            uw = rhs + _mm(a_m, rhs)
            u, w = uw[:, 0:GDN_DV], uw[:, GDN_DV:GDN_DV + GDN_DK]
            a_qk = _mm_nt(qh, kh) * decay
            q_dec = qh * eg_col
            k_dec = kh * kdec_scale[:, hd:hd + 1]
            s = s_ref[hd]
            ws_qs = _mm(jnp.concatenate([w, q_dec], axis=0), s)
            v_new = u - ws_qs[0:CHUNK]
            mix_ref[rows, hd * GDN_DV:(hd + 1) * GDN_DV] = ws_qs[CHUNK:2 * CHUNK] + _mm(a_qk, v_new)
            s_ref[hd] = s * e_last[:, hd:hd + 1] + _mm_tn(k_dec, v_new)

    for hd in range(GDN_HEADS):
        sl = slice(hd * GDN_DV, (hd + 1) * GDN_DV)
        o = mix_ref[:, sl]
        o = o * lax.rsqrt(jnp.mean(o * o, axis=-1, keepdims=True) + NORM_EPS) * cnorm_ref[:, sl]
        mix_ref[:, sl] = o * _silu(h2_ref[:, L1_GATE - L1_CONV + hd * GDN_DV:L1_GATE - L1_CONV + (hd + 1) * GDN_DV])
    y = _mm(mix_ref[...], w_out_ref[...])
    x1 = _layer_norm(DEEPNORM_ALPHA * x + y, lng_ref[...], lnb_ref[...])
    o_ref[...] = x1

    x_hi = x1.astype(BF16)
    x_lo = (x1 - x_hi.astype(F32)).astype(BF16)
    logits = (jnp.dot(x_hi, wr_hi_ref[...], preferred_element_type=F32)
              + jnp.dot(x_hi, wr_lo_ref[...], preferred_element_type=F32)
              + jnp.dot(x_lo, wr_hi_ref[...], preferred_element_type=F32)) + br_ref[...]
    lane = lax.broadcasted_iota(jnp.int32, (tt, ROUTE_LANES), 1)
    lane_f = lane.astype(F32)
    logits = jnp.where(lane < N_EXPERTS, logits, NEG_BIG)
    m1 = jnp.max(logits, axis=-1, keepdims=True)
    i1 = jnp.min(jnp.where(logits == m1, lane_f, float(ROUTE_LANES)), axis=-1, keepdims=True)
    rest = jnp.where(lane_f == i1, NEG_BIG, logits)
    m2 = jnp.max(rest, axis=-1, keepdims=True)
    i2 = jnp.min(jnp.where(rest == m2, lane_f, float(ROUTE_LANES)), axis=-1, keepdims=True)
    e21 = jnp.exp(m2 - m1)
    g1 = 1.0 / (1.0 + e21)
    g2 = e21 * g1
    route_ref[...] = jnp.where(lane == 0, i1, jnp.where(lane == 1, i2, jnp.where(lane == 2, g1, jnp.where(lane == 3, g2, 0.0))))


def _l1_mixer(x, w_in, conv_w, a_log, dt_bias, c_norm, w_out, ln_g, ln_b, w_router, b_router, *, tt):
    bsz, seq, d = x.shape
    w_in_p = _pack_l1_w_in(w_in).astype(BF16)
    lane_pad = lambda v: jnp.pad(v[None, :], ((0, 0), (0, LANES - v.shape[0])))
    wr = jnp.pad(w_router, ((0, 0), (0, ROUTE_LANES - N_EXPERTS)))
    wr_hi = wr.astype(BF16)
    wr_lo = (wr - wr_hi.astype(F32)).astype(BF16)
    params = [w_in_p, conv_w, lane_pad(a_log), lane_pad(dt_bias), c_norm.reshape(1, GDN_V), w_out.astype(BF16),
              ln_g[None, :], ln_b[None, :], jnp.asarray(_chunk_tril_np(tt), BF16), wr_hi, wr_lo, lane_pad(b_router)]
    tile = lambda w: pl.BlockSpec((None, tt, w), lambda b, t: (b, t, 0))
    return pl.pallas_call(
        functools.partial(_l1_mixer_kernel, tt=tt),
        grid=(bsz, seq // tt),
        in_specs=[tile(d)] + [_const_spec(a.shape) for a in params],
        out_specs=[tile(d), tile(ROUTE_LANES)],
        out_shape=[jax.ShapeDtypeStruct((bsz, seq, d), F32), jax.ShapeDtypeStruct((bsz, seq, ROUTE_LANES), F32)],
        scratch_shapes=[pltpu.VMEM((tt + CONV_PAD, L1_CONV), F32), pltpu.VMEM((tt, L1_COLS - L1_CONV), F32),
                        pltpu.VMEM((tt, L1_CONV), F32), pltpu.VMEM((tt, GDN_V), F32),
                        pltpu.VMEM((GDN_HEADS, GDN_DK, GDN_DV), F32)],
        compiler_params=pltpu.CompilerParams(dimension_semantics=("arbitrary", "arbitrary"),
                                             vmem_limit_bytes=VMEM_LIMIT_BYTES),
        name="l1_mixer",
    )(x, *params)


ID_ROWS = 4


def _rank_kernel(route_ref, stril_ref, ids_ref, counts_ref, carry_ref):
    @pl.when(pl.program_id(0) == 0)
    def _():
        carry_ref[...] = jnp.zeros_like(carry_ref)

    r = route_ref[...]
    tr = r.shape[0]
    e1, e2 = r[:, 0:1], r[:, 1:2]
    lane = lax.broadcasted_iota(jnp.int32, (tr, ROUTE_LANES), 1)
    lane_f = lane.astype(F32)
    oh1 = (lane_f == e1).astype(F32)
    oh2 = (lane_f == e2).astype(F32)
    both = oh1 + oh2
    before = jnp.dot(stril_ref[...], both.astype(BF16), preferred_element_type=F32) + carry_ref[...]
    rank1 = jnp.sum(oh1 * before, axis=-1, keepdims=True)
    rank2 = jnp.sum(oh2 * before, axis=-1, keepdims=True)
    table = jnp.where(lane == 0, e1, jnp.where(lane == 1, e2, jnp.where(lane == 2, rank1,
                      jnp.where(lane == 3, rank2, 0.0))))
    ids_ref[...] = table.T[0:ID_ROWS, :].astype(jnp.int32)
    carry_ref[...] += jnp.sum(both, axis=0, keepdims=True)
    counts_ref[...] = carry_ref[...]


def _rank(route2d, *, tr):
    n = route2d.shape[0]
    i = np.arange(tr)
    stril = jnp.asarray((i[None, :] < i[:, None]).astype(np.float32), BF16)
    return pl.pallas_call(
        _rank_kernel,
        grid=(n // tr,),
        in_specs=[pl.BlockSpec((tr, ROUTE_LANES), lambda i: (i, 0)), _const_spec((tr, tr))],
        out_specs=[pl.BlockSpec((ID_ROWS, tr), lambda i: (0, i)), _const_spec((1, ROUTE_LANES))],
        out_shape=[jax.ShapeDtypeStruct((ID_ROWS, n), jnp.int32), jax.ShapeDtypeStruct((1, ROUTE_LANES), F32)],
        scratch_shapes=[pltpu.VMEM((1, ROUTE_LANES), F32)],
        compiler_params=pltpu.CompilerParams(dimension_semantics=("arbitrary",)),
        name="moe_rank",
    )(route2d, stril)


def _row_copy(src_hbm, src_row, dst_ref, dst_row, sem):
    return pltpu.make_async_copy(src_hbm.at[pl.ds(src_row, 1), :], dst_ref.at[pl.ds(dst_row, 1), :], sem)


def _dispatch_kernel(offs_ref, ids_ref, x_hbm, init_hbm, xs_hbm, sem, *, td):
    del init_hbm
    base = pl.program_id(0) * td

    def issue(t, carry):
        for k in range(TOP_K):
            dst = offs_ref[ids_ref[k, t]] + ids_ref[TOP_K + k, t]
            _row_copy(x_hbm, base + t, xs_hbm, dst, sem).start()
        return carry

    lax.fori_loop(0, td, issue, 0)

    def drain(t, carry):
        _row_copy(x_hbm, 0, xs_hbm, 0, sem).wait()
        return carry

    lax.fori_loop(0, TOP_K * td, drain, 0)


def _dispatch(offs, ids, x2d, n_rows, *, td):
    n, d = x2d.shape
    init = jnp.zeros((n_rows, d), F32)
    grid_spec = pltpu.PrefetchScalarGridSpec(
        num_scalar_prefetch=1,
        grid=(n // td,),
        in_specs=[pl.BlockSpec((ID_ROWS, td), lambda i, offs: (0, i), memory_space=pltpu.SMEM),
                  pl.BlockSpec(memory_space=pl.ANY), pl.BlockSpec(memory_space=pl.ANY)],
        out_specs=pl.BlockSpec(memory_space=pl.ANY),
        scratch_shapes=[pltpu.SemaphoreType.DMA(())],
    )
    return pl.pallas_call(
        functools.partial(_dispatch_kernel, td=td),
        grid_spec=grid_spec,
        out_shape=jax.ShapeDtypeStruct((n_rows, d), F32),
        input_output_aliases={3: 0},
        compiler_params=pltpu.CompilerParams(dimension_semantics=("arbitrary",)),
        name="moe_dispatch",
    )(offs, ids, x2d, init)


def _gmm_kernel(te_ref, used_ref, x_ref, wg_ref, wu_ref, wd_ref, o_ref, xb_ref):
    i, j = pl.program_id(0), pl.program_id(1)

    @pl.when(i < used_ref[0])
    def _():
        @pl.when(j == 0)
        def _():
            xb_ref[...] = x_ref[...].astype(BF16)

        xb = xb_ref[...]
        gt = jnp.dot(xb, wg_ref[...], preferred_element_type=F32)
        up = jnp.dot(xb, wu_ref[...], preferred_element_type=F32)
        part = _mm(_silu(gt) * up, wd_ref[...])

        @pl.when(j == 0)
        def _():
            o_ref[...] = part

        @pl.when(j > 0)
        def _():
            o_ref[...] += part

    @pl.when(jnp.logical_and(i >= used_ref[0], j == 0))
    def _():
        o_ref[...] = jnp.zeros_like(o_ref)


def _gmm(tile_expert, n_used, xs, w_gu_b, w_down_b, *, tm, tf):
    n_rows, d = xs.shape
    nf = D_FF // tf
    n_tiles = n_rows // tm
    row = lambda i, j, te, used: (jnp.minimum(i, used[0] - 1), 0)
    jj = lambda i, j, used: jnp.where(i < used[0], j, nf - 1)
    grid_spec = pltpu.PrefetchScalarGridSpec(
        num_scalar_prefetch=2,
        grid=(n_tiles, nf),
        in_specs=[pl.BlockSpec((tm, d), row),
                  pl.BlockSpec((None, d, tf), lambda i, j, te, used: (te[i], 0, jj(i, j, used))),
                  pl.BlockSpec((None, d, tf), lambda i, j, te, used: (te[i], 0, jj(i, j, used) + nf)),
                  pl.BlockSpec((None, tf, d), lambda i, j, te, used: (te[i], jj(i, j, used), 0))],
        out_specs=pl.BlockSpec((tm, d), lambda i, j, te, used: (i, 0)),
        scratch_shapes=[pltpu.VMEM((tm, d), BF16)],
    )
    return pl.pallas_call(
        _gmm_kernel,
        grid_spec=grid_spec,
        out_shape=jax.ShapeDtypeStruct((n_rows, d), F32),
        compiler_params=pltpu.CompilerParams(dimension_semantics=("arbitrary", "arbitrary"),
                                             vmem_limit_bytes=VMEM_LIMIT_BYTES),
        name="moe_gmm",
    )(tile_expert, n_used, xs, w_gu_b, w_gu_b, w_down_b)


def _combine_kernel(offs_ref, ids_ref, x_ref, route_ref, lng_ref, lnb_ref, y_hbm, o_ref, ybuf, sem, *, tc):
    def issue(t, carry):
        for k in range(TOP_K):
            src = offs_ref[ids_ref[k, t]] + ids_ref[TOP_K + k, t]
            _row_copy(y_hbm, src, ybuf.at[k], t, sem).start()
        return carry

    lax.fori_loop(0, tc, issue, 0)

    def drain(t, carry):
        _row_copy(y_hbm, 0, ybuf.at[0], 0, sem).wait()
        return carry

    lax.fori_loop(0, TOP_K * tc, drain, 0)

    r = route_ref[...]
    y = r[:, 2:3] * ybuf[0] + r[:, 3:4] * ybuf[1]
    o_ref[...] = _layer_norm(DEEPNORM_ALPHA * x_ref[...] + y, lng_ref[...], lnb_ref[...])


def _combine(offs, ids, x2d, route2d, ys, ln_g, ln_b, *, tc):
    n, d = x2d.shape
    grid_spec = pltpu.PrefetchScalarGridSpec(
        num_scalar_prefetch=1,
        grid=(n // tc,),
        in_specs=[pl.BlockSpec((ID_ROWS, tc), lambda i, offs: (0, i), memory_space=pltpu.SMEM),
                  pl.BlockSpec((tc, d), lambda i, offs: (i, 0)),
                  pl.BlockSpec((tc, ROUTE_LANES), lambda i, offs: (i, 0)),
                  pl.BlockSpec((1, d), lambda i, offs: (0, 0)), pl.BlockSpec((1, d), lambda i, offs: (0, 0)),
                  pl.BlockSpec(memory_space=pl.ANY)],
        out_specs=pl.BlockSpec((tc, d), lambda i, offs: (i, 0)),
        scratch_shapes=[pltpu.VMEM((TOP_K, tc, d), F32), pltpu.SemaphoreType.DMA(())],
    )
    return pl.pallas_call(
        functools.partial(_combine_kernel, tc=tc),
        grid_spec=grid_spec,
        out_shape=jax.ShapeDtypeStruct((n, d), F32),
        compiler_params=pltpu.CompilerParams(dimension_semantics=("arbitrary",),
                                             vmem_limit_bytes=VMEM_LIMIT_BYTES),
        name="moe_combine",
    )(offs, ids, x2d, route2d, ln_g[None, :], ln_b[None, :], ys)


def _moe(x2d, route2d, w_gu, w_down, ln_g, ln_b, *, tm, tf, tr, td, tc):
    n, d = x2d.shape
    ids, counts = _rank(route2d, tr=tr)
    cnt = counts[0, :N_EXPERTS].astype(jnp.int32)
    tiles_per = (cnt + tm - 1) // tm
    tile_end = jnp.cumsum(tiles_per)
    offs = (tile_end - tiles_per) * tm
    n_tiles = (n * TOP_K) // tm + N_EXPERTS
    n_used = tile_end[-1:]
    t_idx = jnp.arange(n_tiles, dtype=jnp.int32)
    tile_expert = jnp.sum(jnp.minimum(t_idx, n_used - 1)[:, None] >= tile_end[None, :], axis=1).astype(jnp.int32)
    xs = _dispatch(offs, ids, x2d, n_tiles * tm, td=td)
    ys = _gmm(tile_expert, n_used, xs, w_gu.astype(BF16), w_down.astype(BF16), tm=tm, tf=tf)
    return _combine(offs, ids, x2d, route2d, ys, ln_g, ln_b, tc=tc)


def kernel(x, positions, ab_w_in, gla_w_gate2, gla_b_gate, gla_norm, ret_norm, ab_w_out, ab_ln1_g, ab_ln1_b, ffn_w_gu, ffn_w_down, ab_ln2_g, ab_ln2_b, c_w_in, c_conv_w, c_a_log, c_dt_bias, c_norm, c_w_out, c_ln1_g, c_ln1_b, moe_w_router, moe_b_router, moe_w_gu, moe_w_down, c_ln2_g, c_ln2_b):
    bsz, seq, d = x.shape
    for layer in range(DEPTH):
        i = layer // 2
        if layer % 2 == 0:
            x = _l0_mixer(x, positions, ab_w_in[i], gla_w_gate2[i], gla_b_gate[i], gla_norm[i], ret_norm[i],
                          ab_w_out[i], ab_ln1_g[i], ab_ln1_b[i], tt=L0_TIME_TILE)
            x = _ffn(x.reshape(bsz * seq, d), ffn_w_gu[i], ffn_w_down[i], ab_ln2_g[i], ab_ln2_b[i],
                     tm=FFN_ROW_TILE, tf=FF_TILE).reshape(bsz, seq, d)
        else:
            x, route = _l1_mixer(x, c_w_in[i], c_conv_w[i], c_a_log[i], c_dt_bias[i], c_norm[i], c_w_out[i],
                                 c_ln1_g[i], c_ln1_b[i], moe_w_router[i], moe_b_router[i], tt=L1_TIME_TILE)
            x = _moe(x.reshape(bsz * seq, d), route.reshape(bsz * seq, ROUTE_LANES), moe_w_gu[i], moe_w_down[i],
                     c_ln2_g[i], c_ln2_b[i], tm=MOE_ROW_TILE, tf=FF_TILE, tr=ROUTE_TILE, td=ROUTE_TILE,
                     tc=ROUTE_TILE).reshape(bsz, seq, d)
    return x
```

```python
import functools

import numpy as np
import jax
import jax.numpy as jnp
from jax import lax
from jax.experimental import pallas as pl
from jax.experimental.pallas import tpu as pltpu

F32 = jnp.float32
BF16 = jnp.bfloat16

D_MODEL = 1024
DEPTH = 2
CHUNK = 64
GLA_HEADS, GLA_DK, GLA_DV, GLA_GATE_RANK, GLA_TAU = 4, 64, 128, 16, 16.0
RET_HEADS, RET_DK, RET_DV = 4, 64, 128
ROPE_BASE = 10000.0
GDN_HEADS, GDN_DK, GDN_DV = 8, 128, 128
CONV_WIDTH = 4
D_FF = 3584
N_EXPERTS = 8
TOP_K = 2
NORM_EPS = 1e-5
L2_EPS = 1e-6
DEEPNORM_ALPHA = (2.0 * DEPTH) ** 0.25

GLA_QK = GLA_HEADS * GLA_DK
GLA_V = GLA_HEADS * GLA_DV
RET_QK = RET_HEADS * RET_DK
RET_V = RET_HEADS * RET_DV
GDN_QK = GDN_HEADS * GDN_DK
GDN_V = GDN_HEADS * GDN_DV

LANES = 128
VMEM_LIMIT_BYTES = 56 * 1024 * 1024
NEG_BIG = -1e30

L0_TIME_TILE = 256
L1_TIME_TILE = 256
FFN_ROW_TILE = 1024
MOE_ROW_TILE = 512
FF_TILE = 512
ROUTE_TILE = 512


def _mm(a, b):
    return jnp.dot(a.astype(BF16), b.astype(BF16), preferred_element_type=F32)


def _mm_nt(a, b):
    return lax.dot_general(a.astype(BF16), b.astype(BF16), (((1,), (1,)), ((), ())),
                           preferred_element_type=F32)


def _mm_tn(a, b):
    return lax.dot_general(a.astype(BF16), b.astype(BF16), (((0,), (0,)), ((), ())),
                           preferred_element_type=F32)


def _split3(x):
    hi = x.astype(BF16)
    r1 = x - hi.astype(F32)
    mid = r1.astype(BF16)
    lo = (r1 - mid.astype(F32)).astype(BF16)
    return hi, mid, lo


def _mm_exact_lhs01(m01, x):
    hi, mid, lo = _split3(x)
    return (jnp.dot(m01, hi, preferred_element_type=F32)
            + jnp.dot(m01, mid, preferred_element_type=F32)
            + jnp.dot(m01, lo, preferred_element_type=F32))


def _sigmoid(x):
    return 1.0 / (1.0 + jnp.exp(-x))


def _silu(x):
    return x * _sigmoid(x)


def _softplus(x):
    return jnp.maximum(x, 0.0) + jnp.log(1.0 + jnp.exp(-jnp.abs(x)))


def _layer_norm(x, g, b):
    mu = jnp.mean(x, axis=-1, keepdims=True)
    xc = x - mu
    var = jnp.mean(xc * xc, axis=-1, keepdims=True)
    return xc * lax.rsqrt(var + NORM_EPS) * g + b


def _chunk_tril_np(tt):
    i = np.arange(tt)
    same = (i[:, None] // CHUNK) == (i[None, :] // CHUNK)
    return (same & (i[None, :] <= i[:, None])).astype(np.float32)


def _const_spec(shape):
    nd = len(shape)
    return pl.BlockSpec(shape, lambda *_: (0,) * nd)


L0_GQ, L0_GK, L0_GV, L0_GR = 0, 256, 512, 1024
L0_RQ, L0_RK, L0_RV, L0_RG = 1536, 1792, 2048, 2560
L0_GA = 3072
L0_COLS = 3200


def _pack_l0_w_in(w):
    offs = np.cumsum([0, GLA_QK, GLA_QK, GLA_V, GLA_GATE_RANK, GLA_V, RET_QK, RET_QK, RET_V, RET_V])
    gq, gk, gv, ga, gr, rq, rk, rv, rg = [w[:, offs[i]:offs[i + 1]] for i in range(9)]
    ga = jnp.pad(ga, ((0, 0), (0, LANES - GLA_GATE_RANK)))
    return jnp.concatenate([gq, gk, gv, gr, rq, rk, rv, rg, ga], axis=1)


def _ret_tables():
    h = np.arange(RET_HEADS, dtype=np.float64)
    log_gamma = np.log(1.0 - 2.0 ** (-5.0 - h))
    pos = np.arange(CHUNK, dtype=np.float64)
    diff = pos[:, None] - pos[None, :]
    dmat = np.where(diff >= 0, np.exp(log_gamma[:, None, None] * np.maximum(diff, 0.0)), 0.0)
    xi = np.exp(log_gamma[None, :] * (pos[:, None] + 1.0))
    zeta = np.exp(log_gamma[None, :] * (CHUNK - 1.0 - pos[:, None]))
    decay = np.exp(log_gamma * CHUNK)
    xi_full = np.repeat(xi, RET_DV, axis=1)
    zeta_full = np.repeat(zeta, RET_DK, axis=1)
    decay_full = np.repeat(decay, RET_DK)[None, :]
    return (dmat.astype(np.float32), xi_full.astype(np.float32), zeta_full.astype(np.float32),
            decay_full.astype(np.float32))


def _rope_tables():
    half = RET_DK // 2
    inv_freq = ROPE_BASE ** (-np.arange(0, RET_DK, 2, dtype=np.float32) / RET_DK)
    per_head = np.concatenate([inv_freq, inv_freq])
    freq_full = np.tile(per_head, RET_HEADS)[None, :].astype(np.float32)
    sign = np.tile(np.concatenate([-np.ones(half), np.ones(half)]), RET_HEADS)[None, :].astype(np.float32)
    return freq_full, sign


def _l0_mixer_kernel(x_ref, pos_ref, w_in_ref, wg2_ref, bg_ref, gnorm_ref, rnorm_ref, w_out_ref,
                     lng_ref, lnb_ref, tril_ref, dmat_ref, xi_ref, zeta_ref, rdecay_ref,
                     freq_ref, sign_ref, o_ref, h_ref, mix_ref, sg_ref, sr_ref, *, tt):
    ti = pl.program_id(1)

    @pl.when(ti == 0)
    def _():
        sg_ref[...] = jnp.zeros_like(sg_ref)
        sr_ref[...] = jnp.zeros_like(sr_ref)

    x = x_ref[...]
    h_ref[...] = _mm(x, w_in_ref[...])

    lane = lax.broadcasted_iota(jnp.int32, (1, LANES), 1)
    lo_half = lane < GLA_DK
    ci = lax.broadcasted_iota(jnp.int32, (CHUNK, CHUNK), 0)
    cj = lax.broadcasted_iota(jnp.int32, (CHUNK, CHUNK), 1)
    causal = cj <= ci

    z = _mm(h_ref[:, L0_GA:L0_GA + LANES], wg2_ref[...]) + bg_ref[...]
    log_a = -_softplus(-z) * (1.0 / GLA_TAU)
    b = _mm_exact_lhs01(tril_ref[...], log_a)
    eb = jnp.exp(b)
    q_dec = h_ref[:, L0_GQ:L0_GQ + GLA_QK] * (GLA_DK ** -0.5) * eb
    k_all = h_ref[:, L0_GK:L0_GK + GLA_QK]
    k_neg = k_all * jnp.exp(-b)

    ang = pos_ref[...] * freq_ref[...]
    cos = jnp.cos(ang)
    sin = jnp.sin(ang) * sign_ref[...]
    half = RET_DK // 2
    lane256 = lax.broadcasted_iota(jnp.int32, (1, RET_QK), 1)
    first_half = (lane256 & (RET_DK - 1)) < half

    def rope(t):
        swapped = jnp.where(first_half, pltpu.roll(t, RET_QK - half, 1), pltpu.roll(t, half, 1))
        return t * cos + swapped * sin

    rq = rope(h_ref[:, L0_RQ:L0_RQ + RET_QK]) * (RET_DK ** -0.5)
    rk = rope(h_ref[:, L0_RK:L0_RK + RET_QK])

    n_chunks = tt // CHUNK
    heads = range(GLA_HEADS)
    pairs = range(GLA_HEADS // 2)
    pair_lanes = lambda hd: slice((hd // 2) * LANES, (hd // 2 + 1) * LANES)
    head_mask = lambda hd: lo_half if hd % 2 == 0 else jnp.logical_not(lo_half)
    vcol = lambda base, hd: slice(base + hd * GLA_DV, base + (hd + 1) * GLA_DV)
    sg = [sg_ref[p] for p in pairs]
    sr = [sr_ref[p] for p in pairs]
    for c in range(n_chunks):
        rows = slice(c * CHUNK, (c + 1) * CHUNK)
        b_last = b[(c + 1) * CHUNK - 1:(c + 1) * CHUNK, :]
        k_end_c = k_all[rows] * jnp.exp(b_last - b[rows])
        dec_c = jnp.exp(b_last)
        rk_c = rk[rows]
        rk_z = rk_c * zeta_ref[...]
        g_q = [jnp.where(head_mask(hd), q_dec[rows, pair_lanes(hd)], 0.0) for hd in heads]
        r_q = [jnp.where(head_mask(hd), rq[rows, pair_lanes(hd)], 0.0) for hd in heads]
        g_v = [h_ref[rows, vcol(L0_GV, hd)] for hd in heads]
        r_v = [h_ref[rows, vcol(L0_RV, hd)] for hd in heads]
        g_att = [jnp.where(causal, _mm_nt(g_q[hd], k_neg[rows, pair_lanes(hd)]), 0.0) for hd in heads]
        r_att = [_mm_nt(r_q[hd], rk_c[:, pair_lanes(hd)]) * dmat_ref[hd] for hd in heads]
        g_o = [_mm(g_att[hd], g_v[hd]) + _mm_nt(g_q[hd], sg[hd // 2]) for hd in heads]
        r_o = [_mm(r_att[hd], r_v[hd]) + _mm_nt(r_q[hd], sr[hd // 2]) * xi_ref[:, vcol(0, hd)] for hd in heads]
        g_d = [_mm_tn(g_v[hd], k_end_c[:, pair_lanes(hd)]) for hd in heads]
        r_d = [_mm_tn(r_v[hd], rk_z[:, pair_lanes(hd)]) for hd in heads]
        for hd in heads:
            mix_ref[rows, vcol(0, hd)] = g_o[hd]
            mix_ref[rows, vcol(GLA_V, hd)] = r_o[hd]
        sg = [sg[p] * dec_c[:, p * LANES:(p + 1) * LANES] + jnp.where(lo_half, g_d[2 * p], g_d[2 * p + 1])
              for p in pairs]
        sr = [sr[p] * rdecay_ref[:, p * LANES:(p + 1) * LANES] + jnp.where(lo_half, r_d[2 * p], r_d[2 * p + 1])
              for p in pairs]
    for p in pairs:
        sg_ref[p] = sg[p]
        sr_ref[p] = sr[p]

    for hd in range(GLA_HEADS):
        sl = slice(hd * GLA_DV, (hd + 1) * GLA_DV)
        o = mix_ref[:, sl]
        o = o * lax.rsqrt(jnp.mean(o * o, axis=-1, keepdims=True) + NORM_EPS) * gnorm_ref[:, sl]
        mix_ref[:, sl] = o * _silu(h_ref[:, L0_GR + hd * GLA_DV:L0_GR + (hd + 1) * GLA_DV])
    for hd in range(RET_HEADS):
        sl = slice(hd * RET_DV, (hd + 1) * RET_DV)
        o = mix_ref[:, GLA_V + hd * RET_DV:GLA_V + (hd + 1) * RET_DV]
        oc = o - jnp.mean(o, axis=-1, keepdims=True)
        o = oc * lax.rsqrt(jnp.mean(oc * oc, axis=-1, keepdims=True) + NORM_EPS) * rnorm_ref[:, sl]
        mix_ref[:, GLA_V + hd * RET_DV:GLA_V + (hd + 1) * RET_DV] = (
            o * _silu(h_ref[:, L0_RG + hd * RET_DV:L0_RG + (hd + 1) * RET_DV]))

    y = _mm(mix_ref[...], w_out_ref[...])
    o_ref[...] = _layer_norm(DEEPNORM_ALPHA * x + y, lng_ref[...], lnb_ref[...])


def _l0_mixer(x, positions, w_in, w_gate2, b_gate, gla_norm, ret_norm, w_out, ln_g, ln_b, *, tt):
    bsz, seq, d = x.shape
    w_in_p = _pack_l0_w_in(w_in).astype(BF16)
    wg2 = jnp.pad(w_gate2, ((0, LANES - GLA_GATE_RANK), (0, 0))).astype(BF16)
    dmat, xi_full, zeta_full, rdecay = _ret_tables()
    freq_full, sign = _rope_tables()
    pos_f = positions.astype(F32)[..., None]
    consts = [jnp.asarray(_chunk_tril_np(tt), BF16), jnp.asarray(dmat), jnp.asarray(xi_full),
              jnp.asarray(zeta_full), jnp.asarray(rdecay), jnp.asarray(freq_full), jnp.asarray(sign)]
    params = [w_in_p, wg2, b_gate[None, :], gla_norm.reshape(1, GLA_V), ret_norm.reshape(1, RET_V),
              w_out.astype(BF16), ln_g[None, :], ln_b[None, :]]
    tile = lambda w: pl.BlockSpec((None, tt, w), lambda b, t: (b, t, 0))
    return pl.pallas_call(
        functools.partial(_l0_mixer_kernel, tt=tt),
        grid=(bsz, seq // tt),
        in_specs=[tile(d), tile(1)] + [_const_spec(a.shape) for a in params + consts],
        out_specs=tile(d),
        out_shape=jax.ShapeDtypeStruct((bsz, seq, d), F32),
        scratch_shapes=[pltpu.VMEM((tt, L0_COLS), F32), pltpu.VMEM((tt, GLA_V + RET_V), F32),
                        pltpu.VMEM((GLA_HEADS // 2, GLA_DV, LANES), F32),
                        pltpu.VMEM((RET_HEADS // 2, RET_DV, LANES), F32)],
        compiler_params=pltpu.CompilerParams(dimension_semantics=("arbitrary", "arbitrary"),
                                             vmem_limit_bytes=VMEM_LIMIT_BYTES),
        name="l0_mixer",
    )(x, pos_f, *params, *consts)


def _ffn_kernel(x_ref, wg_ref, wu_ref, wd_ref, lng_ref, lnb_ref, o_ref, xb_ref, acc_ref):
    j = pl.program_id(1)

    @pl.when(j == 0)
    def _():
        xb_ref[...] = x_ref[...].astype(BF16)
        acc_ref[...] = jnp.zeros_like(acc_ref)

    xb = xb_ref[...]
    gt = jnp.dot(xb, wg_ref[...], preferred_element_type=F32)
    up = jnp.dot(xb, wu_ref[...], preferred_element_type=F32)
    acc_ref[...] += _mm(_silu(gt) * up, wd_ref[...])

    @pl.when(j == pl.num_programs(1) - 1)
    def _():
        o_ref[...] = _layer_norm(DEEPNORM_ALPHA * x_ref[...] + acc_ref[...], lng_ref[...], lnb_ref[...])


def _ffn(x2d, w_gu, w_down, ln_g, ln_b, *, tm, tf):
    n, d = x2d.shape
    nf = D_FF // tf
    w_gu_b, w_down_b = w_gu.astype(BF16), w_down.astype(BF16)
    return pl.pallas_call(
        _ffn_kernel,
        grid=(n // tm, nf),
        in_specs=[pl.BlockSpec((tm, d), lambda i, j: (i, 0)),
                  pl.BlockSpec((d, tf), lambda i, j: (0, j)),
                  pl.BlockSpec((d, tf), lambda i, j: (0, j + nf)),
                  pl.BlockSpec((tf, d), lambda i, j: (j, 0)),
                  _const_spec((1, d)), _const_spec((1, d))],
        out_specs=pl.BlockSpec((tm, d), lambda i, j: (i, 0)),
        out_shape=jax.ShapeDtypeStruct((n, d), F32),
        scratch_shapes=[pltpu.VMEM((tm, d), BF16), pltpu.VMEM((tm, d), F32)],
        compiler_params=pltpu.CompilerParams(dimension_semantics=("arbitrary", "arbitrary"),
                                             vmem_limit_bytes=VMEM_LIMIT_BYTES),
        name="ffn",
    )(x2d, w_gu_b, w_gu_b, w_down_b, ln_g[None, :], ln_b[None, :])


L1_Q, L1_K, L1_V, L1_GATE, L1_AB = 0, 1024, 2048, 3072, 4096
L1_CONV = 3 * GDN_QK
L1_COLS = 4224
L1_BETA_LANE = GDN_HEADS
CONV_PAD = 8
ROUTE_LANES = LANES


def _pack_l1_w_in(w):
    offs = np.cumsum([0, L1_CONV, GDN_HEADS, GDN_HEADS, GDN_V])
    qkv, a_in, b_in, gate = [w[:, offs[i]:offs[i + 1]] for i in range(4)]
    ab = jnp.pad(jnp.concatenate([a_in, b_in], axis=1), ((0, 0), (0, LANES - 2 * GDN_HEADS)))
    return jnp.concatenate([qkv, gate, ab], axis=1)


def _l1_mixer_kernel(x_ref, w_in_ref, conv_ref, alog_ref, dtb_ref, cnorm_ref, w_out_ref, lng_ref, lnb_ref,
                     tril_ref, wr_hi_ref, wr_lo_ref, br_ref,
                     o_ref, route_ref, ext_ref, h2_ref, qkv_ref, mix_ref, s_ref, u_ref, w_ref, aqk_ref, *, tt):
    ti = pl.program_id(1)

    @pl.when(ti == 0)
    def _():
        s_ref[...] = jnp.zeros_like(s_ref)
        ext_ref[0:CONV_PAD, :] = jnp.zeros((CONV_PAD, L1_CONV), F32)

    x = x_ref[...]
    xb = x.astype(BF16)
    ext_ref[CONV_PAD:CONV_PAD + tt, :] = jnp.dot(xb, w_in_ref[:, 0:L1_CONV], preferred_element_type=F32)
    h2_ref[...] = jnp.dot(xb, w_in_ref[:, L1_CONV:L1_COLS], preferred_element_type=F32)

    conv = ext_ref[CONV_PAD:CONV_PAD + tt, :] * conv_ref[CONV_WIDTH - 1:CONV_WIDTH, :]
    for j in range(CONV_WIDTH - 1):
        back = CONV_WIDTH - 1 - j
        conv = conv + ext_ref[CONV_PAD - back:CONV_PAD - back + tt, :] * conv_ref[j:j + 1, :]
    ext_ref[0:CONV_PAD, :] = ext_ref[tt:tt + CONV_PAD, :]
    qkv_ref[...] = _silu(conv)

    for hd in range(GDN_HEADS):
        for base, scale in ((L1_Q, GDN_DK ** -0.5), (L1_K, 1.0)):
            sl = slice(base + hd * GDN_DK, base + (hd + 1) * GDN_DK)
            t = qkv_ref[:, sl]
            qkv_ref[:, sl] = t * (lax.rsqrt(jnp.sum(t * t, axis=-1, keepdims=True) + L2_EPS) * scale)

    ab = h2_ref[:, L1_AB - L1_CONV:L1_AB - L1_CONV + LANES]
    g_blk = -jnp.exp(alog_ref[...]) * _softplus(ab + dtb_ref[...])
    beta_blk = _sigmoid(ab)
    gc_blk = _mm_exact_lhs01(tril_ref[...], g_blk)
    eg_blk = jnp.exp(gc_blk)

    ci = lax.broadcasted_iota(jnp.int32, (CHUNK, CHUNK), 0)
    cj = lax.broadcasted_iota(jnp.int32, (CHUNK, CHUNK), 1)
    incl = cj <= ci
    strict = cj < ci

    heads = range(GDN_HEADS)
    col = lambda base, hd: slice(base + hd * GDN_DK, base + (hd + 1) * GDN_DK)
    n_chunks = tt // CHUNK

    for c in range(n_chunks):
        rows = slice(c * CHUNK, (c + 1) * CHUNK)
        gc_c = gc_blk[rows]
        gc_t = gc_c.T
        kdec_scale = jnp.exp(gc_c[CHUNK - 1:CHUNK, :] - gc_c)
        q_h = [qkv_ref[rows, col(L1_Q, hd)] for hd in heads]
        k_h = [qkv_ref[rows, col(L1_K, hd)] for hd in heads]
        v_h = [qkv_ref[rows, col(L1_V, hd)] for hd in heads]
        beta = [beta_blk[rows, L1_BETA_LANE + hd:L1_BETA_LANE + hd + 1] for hd in heads]
        eg = [eg_blk[rows, hd:hd + 1] for hd in heads]
        decay = [jnp.exp(jnp.where(incl, gc_c[:, hd:hd + 1] - gc_t[hd:hd + 1, :], NEG_BIG)) for hd in heads]
        kb = [k_h[hd] * beta[hd] for hd in heads]
        low = [jnp.where(strict, _mm_nt(kb[hd], k_h[hd]) * decay[hd], 0.0) for hd in heads]
        a_qk = [_mm_nt(q_h[hd], k_h[hd]) * decay[hd] for hd in heads]
        a_m = [-low[hd] for hd in heads]
        m = [_mm(low[hd], low[hd]) for hd in heads]
        for it in range(5):
            am = [_mm(a_m[hd], m[hd]) for hd in heads]
            a_m = [a_m[hd] + m[hd] + am[hd] for hd in heads]
            if it < 4:
                m = [_mm(m[hd], m[hd]) for hd in heads]
        rhs = [jnp.concatenate([v_h[hd] * beta[hd], kb[hd] * eg[hd]], axis=1) for hd in heads]
        uw = [rhs[hd] + _mm(a_m[hd], rhs[hd]) for hd in heads]
        for hd in heads:
            u_ref[rows, col(0, hd)] = uw[hd][:, 0:GDN_DV]
            w_ref[rows, col(0, hd)] = uw[hd][:, GDN_DV:GDN_DV + GDN_DK]
            aqk_ref[rows, hd * LANES:hd * LANES + CHUNK] = a_qk[hd]
            qkv_ref[rows, col(L1_Q, hd)] = q_h[hd] * eg[hd]
            qkv_ref[rows, col(L1_K, hd)] = k_h[hd] * kdec_scale[:, hd:hd + 1]

    s = [s_ref[hd] for hd in heads]
    for c in range(n_chunks):
        rows = slice(c * CHUNK, (c + 1) * CHUNK)
        e_last = jnp.exp(gc_blk[(c + 1) * CHUNK - 1:(c + 1) * CHUNK, :])
        ws_qs = [_mm(jnp.concatenate([w_ref[rows, col(0, hd)], qkv_ref[rows, col(L1_Q, hd)]], axis=0), s[hd])
                 for hd in heads]
        v_new = [u_ref[rows, col(0, hd)] - ws_qs[hd][0:CHUNK] for hd in heads]
        o_h = [ws_qs[hd][CHUNK:2 * CHUNK] + _mm(aqk_ref[rows, hd * LANES:hd * LANES + CHUNK], v_new[hd])
               for hd in heads]
        s = [s[hd] * e_last[:, hd:hd + 1] + _mm_tn(qkv_ref[rows, col(L1_K, hd)], v_new[hd]) for hd in heads]
        for hd in heads:
            mix_ref[rows, col(0, hd)] = o_h[hd]
    for hd in heads:
        s_ref[hd] = s[hd]

    for hd in range(GDN_HEADS):
        sl = slice(hd * GDN_DV, (hd + 1) * GDN_DV)
        o = mix_ref[:, sl]
        o = o * lax.rsqrt(jnp.mean(o * o, axis=-1, keepdims=True) + NORM_EPS) * cnorm_ref[:, sl]
        mix_ref[:, sl] = o * _silu(h2_ref[:, L1_GATE - L1_CONV + hd * GDN_DV:L1_GATE - L1_CONV + (hd + 1) * GDN_DV])
    y = _mm(mix_ref[...], w_out_ref[...])
    x1 = _layer_norm(DEEPNORM_ALPHA * x + y, lng_ref[...], lnb_ref[...])
    o_ref[...] = x1

    x_hi = x1.astype(BF16)
    x_lo = (x1 - x_hi.astype(F32)).astype(BF16)
    logits = (jnp.dot(x_hi, wr_hi_ref[...], preferred_element_type=F32)
              + jnp.dot(x_hi, wr_lo_ref[...], preferred_element_type=F32)
              + jnp.dot(x_lo, wr_hi_ref[...], preferred_element_type=F32)) + br_ref[...]
    lane = lax.broadcasted_iota(jnp.int32, (tt, ROUTE_LANES), 1)
    lane_f = lane.astype(F32)
    logits = jnp.where(lane < N_EXPERTS, logits, NEG_BIG)
    m1 = jnp.max(logits, axis=-1, keepdims=True)
    i1 = jnp.min(jnp.where(logits == m1, lane_f, float(ROUTE_LANES)), axis=-1, keepdims=True)
    rest = jnp.where(lane_f == i1, NEG_BIG, logits)
    m2 = jnp.max(rest, axis=-1, keepdims=True)
    i2 = jnp.min(jnp.where(rest == m2, lane_f, float(ROUTE_LANES)), axis=-1, keepdims=True)
    e21 = jnp.exp(m2 - m1)
    g1 = 1.0 / (1.0 + e21)
    g2 = e21 * g1
    route_ref[...] = jnp.where(lane == 0, i1, jnp.where(lane == 1, i2, jnp.where(lane == 2, g1, jnp.where(lane == 3, g2, 0.0))))


def _l1_mixer(x, w_in, conv_w, a_log, dt_bias, c_norm, w_out, ln_g, ln_b, w_router, b_router, *, tt):
    bsz, seq, d = x.shape
    w_in_p = _pack_l1_w_in(w_in).astype(BF16)
    lane_pad = lambda v: jnp.pad(v[None, :], ((0, 0), (0, LANES - v.shape[0])))
    wr = jnp.pad(w_router, ((0, 0), (0, ROUTE_LANES - N_EXPERTS)))
    wr_hi = wr.astype(BF16)
    wr_lo = (wr - wr_hi.astype(F32)).astype(BF16)
    params = [w_in_p, conv_w, lane_pad(a_log), lane_pad(dt_bias), c_norm.reshape(1, GDN_V), w_out.astype(BF16),
              ln_g[None, :], ln_b[None, :], jnp.asarray(_chunk_tril_np(tt), BF16), wr_hi, wr_lo, lane_pad(b_router)]
    tile = lambda w: pl.BlockSpec((None, tt, w), lambda b, t: (b, t, 0))
    return pl.pallas_call(
        functools.partial(_l1_mixer_kernel, tt=tt),
        grid=(bsz, seq // tt),
        in_specs=[tile(d)] + [_const_spec(a.shape) for a in params],
        out_specs=[tile(d), tile(ROUTE_LANES)],
        out_shape=[jax.ShapeDtypeStruct((bsz, seq, d), F32), jax.ShapeDtypeStruct((bsz, seq, ROUTE_LANES), F32)],
        scratch_shapes=[pltpu.VMEM((tt + CONV_PAD, L1_CONV), F32), pltpu.VMEM((tt, L1_COLS - L1_CONV), F32),
                        pltpu.VMEM((tt, L1_CONV), F32), pltpu.VMEM((tt, GDN_V), F32),
                        pltpu.VMEM((GDN_HEADS, GDN_DK, GDN_DV), F32),
                        pltpu.VMEM((tt, GDN_V), F32), pltpu.VMEM((tt, GDN_QK), F32),
                        pltpu.VMEM((tt, GDN_HEADS * LANES), F32)],
        compiler_params=pltpu.CompilerParams(dimension_semantics=("arbitrary", "arbitrary"),
                                             vmem_limit_bytes=VMEM_LIMIT_BYTES),
        name="l1_mixer",
    )(x, *params)


ID_ROWS = 4


def _rank_kernel(route_ref, stril_ref, ids_ref, counts_ref, carry_ref):
    @pl.when(pl.program_id(0) == 0)
    def _():
        carry_ref[...] = jnp.zeros_like(carry_ref)

    r = route_ref[...]
    tr = r.shape[0]
    e1, e2 = r[:, 0:1], r[:, 1:2]
    lane = lax.broadcasted_iota(jnp.int32, (tr, ROUTE_LANES), 1)
    lane_f = lane.astype(F32)
    oh1 = (lane_f == e1).astype(F32)
    oh2 = (lane_f == e2).astype(F32)
    both = oh1 + oh2
    before = jnp.dot(stril_ref[...], both.astype(BF16), preferred_element_type=F32) + carry_ref[...]
    rank1 = jnp.sum(oh1 * before, axis=-1, keepdims=True)
    rank2 = jnp.sum(oh2 * before, axis=-1, keepdims=True)
    table = jnp.where(lane == 0, e1, jnp.where(lane == 1, e2, jnp.where(lane == 2, rank1,
                      jnp.where(lane == 3, rank2, 0.0))))
    ids_ref[...] = table.T[0:ID_ROWS, :].astype(jnp.int32)
    carry_ref[...] += jnp.sum(both, axis=0, keepdims=True)
    counts_ref[...] = carry_ref[...]


def _rank(route2d, *, tr):
    n = route2d.shape[0]
    i = np.arange(tr)
    stril = jnp.asarray((i[None, :] < i[:, None]).astype(np.float32), BF16)
    return pl.pallas_call(
        _rank_kernel,
        grid=(n // tr,),
        in_specs=[pl.BlockSpec((tr, ROUTE_LANES), lambda i: (i, 0)), _const_spec((tr, tr))],
        out_specs=[pl.BlockSpec((ID_ROWS, tr), lambda i: (0, i)), _const_spec((1, ROUTE_LANES))],
        out_shape=[jax.ShapeDtypeStruct((ID_ROWS, n), jnp.int32), jax.ShapeDtypeStruct((1, ROUTE_LANES), F32)],
        scratch_shapes=[pltpu.VMEM((1, ROUTE_LANES), F32)],
        compiler_params=pltpu.CompilerParams(dimension_semantics=("arbitrary",)),
        name="moe_rank",
    )(route2d, stril)


ROW_SUB = D_MODEL // LANES


def _to_row_tiles(dst_ref, src, n):
    for lb in range(ROW_SUB):
        dst_ref[pl.ds(lb, n, stride=ROW_SUB), :] = src[:, lb * LANES:(lb + 1) * LANES]


def _from_row_tiles(src_ref, n):
    return [src_ref[pl.ds(lb, n, stride=ROW_SUB), :] for lb in range(ROW_SUB)]


def _tile_copy(src_ref, src_row, dst_ref, dst_row, sem):
    src = src_ref.at[pl.ds(pl.multiple_of(src_row * ROW_SUB, ROW_SUB), ROW_SUB), :]
    dst = dst_ref.at[pl.ds(pl.multiple_of(dst_row * ROW_SUB, ROW_SUB), ROW_SUB), :]
    return pltpu.make_async_copy(src, dst, sem)


def _dispatch_kernel(offs_ref, ids_ref, x_ref, init_hbm, xs_hbm, xt_ref, sem, *, td):
    del init_hbm
    _to_row_tiles(xt_ref, x_ref[...], td)

    def issue(t, carry):
        for k in range(TOP_K):
            dst = offs_ref[ids_ref[k, t]] + ids_ref[TOP_K + k, t]
            _tile_copy(xt_ref, t, xs_hbm, dst, sem).start()
        return carry

    lax.fori_loop(0, td, issue, 0)

    def drain(t, carry):
        _tile_copy(xt_ref, 0, xs_hbm, 0, sem).wait()
        return carry

    lax.fori_loop(0, TOP_K * td, drain, 0)


def _dispatch(offs, ids, x2d, n_rows, *, td):
    n, d = x2d.shape
    init = jnp.zeros((n_rows * ROW_SUB, LANES), F32)
    grid_spec = pltpu.PrefetchScalarGridSpec(
        num_scalar_prefetch=1,
        grid=(n // td,),
        in_specs=[pl.BlockSpec((ID_ROWS, td), lambda i, offs: (0, i), memory_space=pltpu.SMEM),
                  pl.BlockSpec((td, d), lambda i, offs: (i, 0)), pl.BlockSpec(memory_space=pl.ANY)],
        out_specs=pl.BlockSpec(memory_space=pl.ANY),
        scratch_shapes=[pltpu.VMEM((td * ROW_SUB, LANES), F32), pltpu.SemaphoreType.DMA(())],
    )
    return pl.pallas_call(
        functools.partial(_dispatch_kernel, td=td),
        grid_spec=grid_spec,
        out_shape=jax.ShapeDtypeStruct((n_rows * ROW_SUB, LANES), F32),
        input_output_aliases={3: 0},
        compiler_params=pltpu.CompilerParams(dimension_semantics=("arbitrary",)),
        name="moe_dispatch",
    )(offs, ids, x2d, init)


def _gmm_kernel(te_ref, used_ref, x_ref, wg_ref, wu_ref, wd_ref, o_ref, xb_ref, acc_ref, *, tm):
    i, j = pl.program_id(0), pl.program_id(1)

    @pl.when(i < used_ref[0])
    def _():
        @pl.when(j == 0)
        def _():
            for lb, blk in enumerate(_from_row_tiles(x_ref, tm)):
                xb_ref[:, lb * LANES:(lb + 1) * LANES] = blk.astype(BF16)

        xb = xb_ref[...]
        gt = jnp.dot(xb, wg_ref[...], preferred_element_type=F32)
        up = jnp.dot(xb, wu_ref[...], preferred_element_type=F32)
        part = _mm(_silu(gt) * up, wd_ref[...])

        @pl.when(j == 0)
        def _():
            acc_ref[...] = part

        @pl.when(j > 0)
        def _():
            acc_ref[...] += part

        @pl.when(j == pl.num_programs(1) - 1)
        def _():
            _to_row_tiles(o_ref, acc_ref[...], tm)

    @pl.when(jnp.logical_and(i >= used_ref[0], j == 0))
    def _():
        o_ref[...] = jnp.zeros_like(o_ref)


def _gmm(tile_expert, n_used, xs, w_gu_b, w_down_b, *, tm, tf):
    d = D_MODEL
    n_rows = xs.shape[0] // ROW_SUB
    nf = D_FF // tf
    n_tiles = n_rows // tm
    row = lambda i, j, te, used: (jnp.minimum(i, used[0] - 1), 0)
    jj = lambda i, j, used: jnp.where(i < used[0], j, nf - 1)
    grid_spec = pltpu.PrefetchScalarGridSpec(
        num_scalar_prefetch=2,
        grid=(n_tiles, nf),
        in_specs=[pl.BlockSpec((tm * ROW_SUB, LANES), row),
                  pl.BlockSpec((None, d, tf), lambda i, j, te, used: (te[i], 0, jj(i, j, used))),
                  pl.BlockSpec((None, d, tf), lambda i, j, te, used: (te[i], 0, jj(i, j, used) + nf)),
                  pl.BlockSpec((None, tf, d), lambda i, j, te, used: (te[i], jj(i, j, used), 0))],
        out_specs=pl.BlockSpec((tm * ROW_SUB, LANES), lambda i, j, te, used: (i, 0)),
        scratch_shapes=[pltpu.VMEM((tm, d), BF16), pltpu.VMEM((tm, d), F32)],
    )
    return pl.pallas_call(
        functools.partial(_gmm_kernel, tm=tm),
        grid_spec=grid_spec,
        out_shape=jax.ShapeDtypeStruct((n_rows * ROW_SUB, LANES), F32),
        compiler_params=pltpu.CompilerParams(dimension_semantics=("arbitrary", "arbitrary"),
                                             vmem_limit_bytes=VMEM_LIMIT_BYTES),
        name="moe_gmm",
    )(tile_expert, n_used, xs, w_gu_b, w_gu_b, w_down_b)


def _combine_kernel(offs_ref, ids_ref, x_ref, route_ref, lng_ref, lnb_ref, y_hbm, o_ref, ya_ref, yb_ref, sem, *, tc):
    bufs = (ya_ref, yb_ref)

    def issue(t, carry):
        for k in range(TOP_K):
            src = offs_ref[ids_ref[k, t]] + ids_ref[TOP_K + k, t]
            _tile_copy(y_hbm, src, bufs[k], t, sem).start()
        return carry

    lax.fori_loop(0, tc, issue, 0)

    def drain(t, carry):
        _tile_copy(y_hbm, 0, ya_ref, 0, sem).wait()
        return carry

    lax.fori_loop(0, TOP_K * tc, drain, 0)

    r = route_ref[...]
    g1, g2 = r[:, 2:3], r[:, 3:4]
    y = jnp.concatenate([g1 * a + g2 * b for a, b in zip(_from_row_tiles(ya_ref, tc), _from_row_tiles(yb_ref, tc))],
                        axis=1)
    o_ref[...] = _layer_norm(DEEPNORM_ALPHA * x_ref[...] + y, lng_ref[...], lnb_ref[...])


def _combine(offs, ids, x2d, route2d, ys, ln_g, ln_b, *, tc):
    n, d = x2d.shape
    grid_spec = pltpu.PrefetchScalarGridSpec(
        num_scalar_prefetch=1,
        grid=(n // tc,),
        in_specs=[pl.BlockSpec((ID_ROWS, tc), lambda i, offs: (0, i), memory_space=pltpu.SMEM),
                  pl.BlockSpec((tc, d), lambda i, offs: (i, 0)),
                  pl.BlockSpec((tc, ROUTE_LANES), lambda i, offs: (i, 0)),
                  pl.BlockSpec((1, d), lambda i, offs: (0, 0)), pl.BlockSpec((1, d), lambda i, offs: (0, 0)),
                  pl.BlockSpec(memory_space=pl.ANY)],
        out_specs=pl.BlockSpec((tc, d), lambda i, offs: (i, 0)),
        scratch_shapes=[pltpu.VMEM((tc * ROW_SUB, LANES), F32), pltpu.VMEM((tc * ROW_SUB, LANES), F32),
                        pltpu.SemaphoreType.DMA(())],
    )
    return pl.pallas_call(
        functools.partial(_combine_kernel, tc=tc),
        grid_spec=grid_spec,
        out_shape=jax.ShapeDtypeStruct((n, d), F32),
        compiler_params=pltpu.CompilerParams(dimension_semantics=("arbitrary",),
                                             vmem_limit_bytes=VMEM_LIMIT_BYTES),
        name="moe_combine",
    )(offs, ids, x2d, route2d, ln_g[None, :], ln_b[None, :], ys)


def _moe(x2d, route2d, w_gu, w_down, ln_g, ln_b, *, tm, tf, tr, td, tc):
    n, d = x2d.shape
    ids, counts = _rank(route2d, tr=tr)
    cnt = counts[0, :N_EXPERTS].astype(jnp.int32)
    tiles_per = (cnt + tm - 1) // tm
    tile_end = jnp.cumsum(tiles_per)
    offs = (tile_end - tiles_per) * tm
    n_tiles = (n * TOP_K) // tm + N_EXPERTS
    n_used = tile_end[-1:]
    t_idx = jnp.arange(n_tiles, dtype=jnp.int32)
    tile_expert = jnp.sum(jnp.minimum(t_idx, n_used - 1)[:, None] >= tile_end[None, :], axis=1).astype(jnp.int32)
    xs = _dispatch(offs, ids, x2d, n_tiles * tm, td=td)
    ys = _gmm(tile_expert, n_used, xs, w_gu.astype(BF16), w_down.astype(BF16), tm=tm, tf=tf)
    return _combine(offs, ids, x2d, route2d, ys, ln_g, ln_b, tc=tc)


def kernel(x, positions, ab_w_in, gla_w_gate2, gla_b_gate, gla_norm, ret_norm, ab_w_out, ab_ln1_g, ab_ln1_b, ffn_w_gu, ffn_w_down, ab_ln2_g, ab_ln2_b, c_w_in, c_conv_w, c_a_log, c_dt_bias, c_norm, c_w_out, c_ln1_g, c_ln1_b, moe_w_router, moe_b_router, moe_w_gu, moe_w_down, c_ln2_g, c_ln2_b):
    bsz, seq, d = x.shape
    for layer in range(DEPTH):
        i = layer // 2
        if layer % 2 == 0:
            x = _l0_mixer(x, positions, ab_w_in[i], gla_w_gate2[i], gla_b_gate[i], gla_norm[i], ret_norm[i],
                          ab_w_out[i], ab_ln1_g[i], ab_ln1_b[i], tt=L0_TIME_TILE)
            x = _ffn(x.reshape(bsz * seq, d), ffn_w_gu[i], ffn_w_down[i], ab_ln2_g[i], ab_ln2_b[i],
                     tm=FFN_ROW_TILE, tf=FF_TILE).reshape(bsz, seq, d)
        else:
            x, route = _l1_mixer(x, c_w_in[i], c_conv_w[i], c_a_log[i], c_dt_bias[i], c_norm[i], c_w_out[i],
                                 c_ln1_g[i], c_ln1_b[i], moe_w_router[i], moe_b_router[i], tt=L1_TIME_TILE)
            x = _moe(x.reshape(bsz * seq, d), route.reshape(bsz * seq, ROUTE_LANES), moe_w_gu[i], moe_w_down[i],
                     c_ln2_g[i], c_ln2_b[i], tm=MOE_ROW_TILE, tf=FF_TILE, tr=ROUTE_TILE, td=ROUTE_TILE,
                     tc=ROUTE_TILE).reshape(bsz, seq, d)
    return x
```

```python
import functools

import numpy as np
import jax
import jax.numpy as jnp
from jax import lax
from jax.experimental import pallas as pl
from jax.experimental.pallas import tpu as pltpu

F32 = jnp.float32
BF16 = jnp.bfloat16

D_MODEL = 1024
DEPTH = 2
CHUNK = 64
GLA_HEADS, GLA_DK, GLA_DV, GLA_GATE_RANK, GLA_TAU = 4, 64, 128, 16, 16.0
RET_HEADS, RET_DK, RET_DV = 4, 64, 128
ROPE_BASE = 10000.0
GDN_HEADS, GDN_DK, GDN_DV = 8, 128, 128
CONV_WIDTH = 4
D_FF = 3584
N_EXPERTS = 8
TOP_K = 2
NORM_EPS = 1e-5
L2_EPS = 1e-6
DEEPNORM_ALPHA = (2.0 * DEPTH) ** 0.25

GLA_QK = GLA_HEADS * GLA_DK
GLA_V = GLA_HEADS * GLA_DV
RET_QK = RET_HEADS * RET_DK
RET_V = RET_HEADS * RET_DV
GDN_QK = GDN_HEADS * GDN_DK
GDN_V = GDN_HEADS * GDN_DV

LANES = 128
VMEM_LIMIT_BYTES = 56 * 1024 * 1024
NEG_BIG = -1e30

L0_TIME_TILE = 256
L1_TIME_TILE = 256
FFN_ROW_TILE = 1024
MOE_ROW_TILE = 512
FF_TILE = 512
ROUTE_TILE = 512


def _mm(a, b):
    return jnp.dot(a.astype(BF16), b.astype(BF16), preferred_element_type=F32)


def _mm_nt(a, b):
    return lax.dot_general(a.astype(BF16), b.astype(BF16), (((1,), (1,)), ((), ())),
                           preferred_element_type=F32)


def _mm_tn(a, b):
    return lax.dot_general(a.astype(BF16), b.astype(BF16), (((0,), (0,)), ((), ())),
                           preferred_element_type=F32)


def _split3(x):
    hi = x.astype(BF16)
    r1 = x - hi.astype(F32)
    mid = r1.astype(BF16)
    lo = (r1 - mid.astype(F32)).astype(BF16)
    return hi, mid, lo


def _mm_exact_lhs01(m01, x):
    hi, mid, lo = _split3(x)
    return (jnp.dot(m01, hi, preferred_element_type=F32)
            + jnp.dot(m01, mid, preferred_element_type=F32)
            + jnp.dot(m01, lo, preferred_element_type=F32))


def _sigmoid(x):
    return 1.0 / (1.0 + jnp.exp(-x))


def _silu(x):
    return x * _sigmoid(x)


def _softplus(x):
    return jnp.maximum(x, 0.0) + jnp.log(1.0 + jnp.exp(-jnp.abs(x)))


def _layer_norm(x, g, b):
    mu = jnp.mean(x, axis=-1, keepdims=True)
    xc = x - mu
    var = jnp.mean(xc * xc, axis=-1, keepdims=True)
    return xc * lax.rsqrt(var + NORM_EPS) * g + b


def _chunk_tril_np(tt):
    i = np.arange(tt)
    same = (i[:, None] // CHUNK) == (i[None, :] // CHUNK)
    return (same & (i[None, :] <= i[:, None])).astype(np.float32)


def _const_spec(shape):
    nd = len(shape)
    return pl.BlockSpec(shape, lambda *_: (0,) * nd)


L0_GQ, L0_GK, L0_GV, L0_GR = 0, 256, 512, 1024
L0_RQ, L0_RK, L0_RV, L0_RG = 1536, 1792, 2048, 2560
L0_GA = 3072
L0_COLS = 3200


def _pack_l0_w_in(w):
    offs = np.cumsum([0, GLA_QK, GLA_QK, GLA_V, GLA_GATE_RANK, GLA_V, RET_QK, RET_QK, RET_V, RET_V])
    gq, gk, gv, ga, gr, rq, rk, rv, rg = [w[:, offs[i]:offs[i + 1]] for i in range(9)]
    ga = jnp.pad(ga, ((0, 0), (0, LANES - GLA_GATE_RANK)))
    return jnp.concatenate([gq, gk, gv, gr, rq, rk, rv, rg, ga], axis=1)


def _ret_tables():
    h = np.arange(RET_HEADS, dtype=np.float64)
    log_gamma = np.log(1.0 - 2.0 ** (-5.0 - h))
    pos = np.arange(CHUNK, dtype=np.float64)
    diff = pos[:, None] - pos[None, :]
    dmat = np.where(diff >= 0, np.exp(log_gamma[:, None, None] * np.maximum(diff, 0.0)), 0.0)
    xi = np.exp(log_gamma[None, :] * (pos[:, None] + 1.0))
    zeta = np.exp(log_gamma[None, :] * (CHUNK - 1.0 - pos[:, None]))
    decay = np.exp(log_gamma * CHUNK)
    xi_full = np.repeat(xi, RET_DV, axis=1)
    zeta_full = np.repeat(zeta, RET_DK, axis=1)
    decay_full = np.repeat(decay, RET_DK)[None, :]
    return (dmat.astype(np.float32), xi_full.astype(np.float32), zeta_full.astype(np.float32),
            decay_full.astype(np.float32))


def _rope_tables():
    half = RET_DK // 2
    inv_freq = ROPE_BASE ** (-np.arange(0, RET_DK, 2, dtype=np.float32) / RET_DK)
    per_head = np.concatenate([inv_freq, inv_freq])
    freq_full = np.tile(per_head, RET_HEADS)[None, :].astype(np.float32)
    sign = np.tile(np.concatenate([-np.ones(half), np.ones(half)]), RET_HEADS)[None, :].astype(np.float32)
    return freq_full, sign


def _l0_mixer_kernel(x_ref, pos_ref, w_in_ref, wg2_ref, bg_ref, gnorm_ref, rnorm_ref, w_out_ref,
                     lng_ref, lnb_ref, tril_ref, dmat_ref, xi_ref, zeta_ref, rdecay_ref,
                     freq_ref, sign_ref, o_ref, h_ref, mix_ref, sg_ref, sr_ref, *, tt):
    ti = pl.program_id(1)

    @pl.when(ti == 0)
    def _():
        sg_ref[...] = jnp.zeros_like(sg_ref)
        sr_ref[...] = jnp.zeros_like(sr_ref)

    x = x_ref[...]
    h_ref[...] = _mm(x, w_in_ref[...])

    lane = lax.broadcasted_iota(jnp.int32, (1, LANES), 1)
    lo_half = lane < GLA_DK
    ci = lax.broadcasted_iota(jnp.int32, (CHUNK, CHUNK), 0)
    cj = lax.broadcasted_iota(jnp.int32, (CHUNK, CHUNK), 1)
    causal = cj <= ci

    z = _mm(h_ref[:, L0_GA:L0_GA + LANES], wg2_ref[...]) + bg_ref[...]
    log_a = -_softplus(-z) * (1.0 / GLA_TAU)
    b = _mm_exact_lhs01(tril_ref[...], log_a)
    eb = jnp.exp(b)
    q_dec = h_ref[:, L0_GQ:L0_GQ + GLA_QK] * (GLA_DK ** -0.5) * eb
    k_all = h_ref[:, L0_GK:L0_GK + GLA_QK]
    k_neg = k_all * jnp.exp(-b)

    ang = pos_ref[...] * freq_ref[...]
    cos = jnp.cos(ang)
    sin = jnp.sin(ang) * sign_ref[...]
    half = RET_DK // 2
    lane256 = lax.broadcasted_iota(jnp.int32, (1, RET_QK), 1)
    first_half = (lane256 & (RET_DK - 1)) < half

    def rope(t):
        swapped = jnp.where(first_half, pltpu.roll(t, RET_QK - half, 1), pltpu.roll(t, half, 1))
        return t * cos + swapped * sin

    rq = rope(h_ref[:, L0_RQ:L0_RQ + RET_QK]) * (RET_DK ** -0.5)
    rk = rope(h_ref[:, L0_RK:L0_RK + RET_QK])

    n_chunks = tt // CHUNK
    heads = range(GLA_HEADS)
    pairs = range(GLA_HEADS // 2)
    pair_lanes = lambda hd: slice((hd // 2) * LANES, (hd // 2 + 1) * LANES)
    head_mask = lambda hd: lo_half if hd % 2 == 0 else jnp.logical_not(lo_half)
    vcol = lambda base, hd: slice(base + hd * GLA_DV, base + (hd + 1) * GLA_DV)
    sg = [sg_ref[p] for p in pairs]
    sr = [sr_ref[p] for p in pairs]
    for c in range(n_chunks):
        rows = slice(c * CHUNK, (c + 1) * CHUNK)
        b_last = b[(c + 1) * CHUNK - 1:(c + 1) * CHUNK, :]
        k_end_c = k_all[rows] * jnp.exp(b_last - b[rows])
        dec_c = jnp.exp(b_last)
        rk_c = rk[rows]
        rk_z = rk_c * zeta_ref[...]
        g_q = [jnp.where(head_mask(hd), q_dec[rows, pair_lanes(hd)], 0.0) for hd in heads]
        r_q = [jnp.where(head_mask(hd), rq[rows, pair_lanes(hd)], 0.0) for hd in heads]
        g_v = [h_ref[rows, vcol(L0_GV, hd)] for hd in heads]
        r_v = [h_ref[rows, vcol(L0_RV, hd)] for hd in heads]
        g_att = [jnp.where(causal, _mm_nt(g_q[hd], k_neg[rows, pair_lanes(hd)]), 0.0) for hd in heads]
        r_att = [_mm_nt(r_q[hd], rk_c[:, pair_lanes(hd)]) * dmat_ref[hd] for hd in heads]
        g_o = [_mm(g_att[hd], g_v[hd]) + _mm_nt(g_q[hd], sg[hd // 2]) for hd in heads]
        r_o = [_mm(r_att[hd], r_v[hd]) + _mm_nt(r_q[hd], sr[hd // 2]) * xi_ref[:, vcol(0, hd)] for hd in heads]
        g_d = [_mm_tn(g_v[hd], k_end_c[:, pair_lanes(hd)]) for hd in heads]
        r_d = [_mm_tn(r_v[hd], rk_z[:, pair_lanes(hd)]) for hd in heads]
        for hd in heads:
            mix_ref[rows, vcol(0, hd)] = g_o[hd]
            mix_ref[rows, vcol(GLA_V, hd)] = r_o[hd]
        sg = [sg[p] * dec_c[:, p * LANES:(p + 1) * LANES] + jnp.where(lo_half, g_d[2 * p], g_d[2 * p + 1])
              for p in pairs]
        sr = [sr[p] * rdecay_ref[:, p * LANES:(p + 1) * LANES] + jnp.where(lo_half, r_d[2 * p], r_d[2 * p + 1])
              for p in pairs]
    for p in pairs:
        sg_ref[p] = sg[p]
        sr_ref[p] = sr[p]

    for hd in range(GLA_HEADS):
        sl = slice(hd * GLA_DV, (hd + 1) * GLA_DV)
        o = mix_ref[:, sl]
        o = o * lax.rsqrt(jnp.mean(o * o, axis=-1, keepdims=True) + NORM_EPS) * gnorm_ref[:, sl]
        mix_ref[:, sl] = o * _silu(h_ref[:, L0_GR + hd * GLA_DV:L0_GR + (hd + 1) * GLA_DV])
    for hd in range(RET_HEADS):
        sl = slice(hd * RET_DV, (hd + 1) * RET_DV)
        o = mix_ref[:, GLA_V + hd * RET_DV:GLA_V + (hd + 1) * RET_DV]
        oc = o - jnp.mean(o, axis=-1, keepdims=True)
        o = oc * lax.rsqrt(jnp.mean(oc * oc, axis=-1, keepdims=True) + NORM_EPS) * rnorm_ref[:, sl]
        mix_ref[:, GLA_V + hd * RET_DV:GLA_V + (hd + 1) * RET_DV] = (
            o * _silu(h_ref[:, L0_RG + hd * RET_DV:L0_RG + (hd + 1) * RET_DV]))

    y = _mm(mix_ref[...], w_out_ref[...])
    o_ref[...] = _layer_norm(DEEPNORM_ALPHA * x + y, lng_ref[...], lnb_ref[...])


def _l0_mixer(x, positions, w_in, w_gate2, b_gate, gla_norm, ret_norm, w_out, ln_g, ln_b, *, tt):
    bsz, seq, d = x.shape
    w_in_p = _pack_l0_w_in(w_in).astype(BF16)
    wg2 = jnp.pad(w_gate2, ((0, LANES - GLA_GATE_RANK), (0, 0))).astype(BF16)
    dmat, xi_full, zeta_full, rdecay = _ret_tables()
    freq_full, sign = _rope_tables()
    pos_f = positions.astype(F32)[..., None]
    consts = [jnp.asarray(_chunk_tril_np(tt), BF16), jnp.asarray(dmat), jnp.asarray(xi_full),
              jnp.asarray(zeta_full), jnp.asarray(rdecay), jnp.asarray(freq_full), jnp.asarray(sign)]
    params = [w_in_p, wg2, b_gate[None, :], gla_norm.reshape(1, GLA_V), ret_norm.reshape(1, RET_V),
              w_out.astype(BF16), ln_g[None, :], ln_b[None, :]]
    tile = lambda w: pl.BlockSpec((None, tt, w), lambda b, t: (b, t, 0))
    return pl.pallas_call(
        functools.partial(_l0_mixer_kernel, tt=tt),
        grid=(bsz, seq // tt),
        in_specs=[tile(d), tile(1)] + [_const_spec(a.shape) for a in params + consts],
        out_specs=tile(d),
        out_shape=jax.ShapeDtypeStruct((bsz, seq, d), F32),
        scratch_shapes=[pltpu.VMEM((tt, L0_COLS), F32), pltpu.VMEM((tt, GLA_V + RET_V), F32),
                        pltpu.VMEM((GLA_HEADS // 2, GLA_DV, LANES), F32),
                        pltpu.VMEM((RET_HEADS // 2, RET_DV, LANES), F32)],
        compiler_params=pltpu.CompilerParams(dimension_semantics=("arbitrary", "arbitrary"),
                                             vmem_limit_bytes=VMEM_LIMIT_BYTES),
        name="l0_mixer",
    )(x, pos_f, *params, *consts)


def _ffn_kernel(x_ref, wg_ref, wu_ref, wd_ref, lng_ref, lnb_ref, o_ref, xb_ref, acc_ref):
    j = pl.program_id(1)

    @pl.when(j == 0)
    def _():
        xb_ref[...] = x_ref[...].astype(BF16)
        acc_ref[...] = jnp.zeros_like(acc_ref)

    xb = xb_ref[...]
    gt = jnp.dot(xb, wg_ref[...], preferred_element_type=F32)
    up = jnp.dot(xb, wu_ref[...], preferred_element_type=F32)
    acc_ref[...] += _mm(_silu(gt) * up, wd_ref[...])

    @pl.when(j == pl.num_programs(1) - 1)
    def _():
        o_ref[...] = _layer_norm(DEEPNORM_ALPHA * x_ref[...] + acc_ref[...], lng_ref[...], lnb_ref[...])


def _ffn(x2d, w_gu, w_down, ln_g, ln_b, *, tm, tf):
    n, d = x2d.shape
    nf = D_FF // tf
    w_gu_b, w_down_b = w_gu.astype(BF16), w_down.astype(BF16)
    return pl.pallas_call(
        _ffn_kernel,
        grid=(n // tm, nf),
        in_specs=[pl.BlockSpec((tm, d), lambda i, j: (i, 0)),
                  pl.BlockSpec((d, tf), lambda i, j: (0, j)),
                  pl.BlockSpec((d, tf), lambda i, j: (0, j + nf)),
                  pl.BlockSpec((tf, d), lambda i, j: (j, 0)),
                  _const_spec((1, d)), _const_spec((1, d))],
        out_specs=pl.BlockSpec((tm, d), lambda i, j: (i, 0)),
        out_shape=jax.ShapeDtypeStruct((n, d), F32),
        scratch_shapes=[pltpu.VMEM((tm, d), BF16), pltpu.VMEM((tm, d), F32)],
        compiler_params=pltpu.CompilerParams(dimension_semantics=("arbitrary", "arbitrary"),
                                             vmem_limit_bytes=VMEM_LIMIT_BYTES),
        name="ffn",
    )(x2d, w_gu_b, w_gu_b, w_down_b, ln_g[None, :], ln_b[None, :])


L1_Q, L1_K, L1_V, L1_GATE, L1_AB = 0, 1024, 2048, 3072, 4096
L1_CONV = 3 * GDN_QK
L1_COLS = 4224
L1_BETA_LANE = GDN_HEADS
CONV_PAD = 8
ROUTE_LANES = LANES


def _pack_l1_w_in(w):
    offs = np.cumsum([0, L1_CONV, GDN_HEADS, GDN_HEADS, GDN_V])
    qkv, a_in, b_in, gate = [w[:, offs[i]:offs[i + 1]] for i in range(4)]
    ab = jnp.pad(jnp.concatenate([a_in, b_in], axis=1), ((0, 0), (0, LANES - 2 * GDN_HEADS)))
    return jnp.concatenate([qkv, gate, ab], axis=1)


def _l1_mixer_kernel(x_ref, w_in_ref, conv_ref, alog_ref, dtb_ref, cnorm_ref, w_out_ref, lng_ref, lnb_ref,
                     tril_ref, wr_hi_ref, wr_lo_ref, br_ref,
                     o_ref, route_ref, ext_ref, h2_ref, qkv_ref, mix_ref, s_ref, u_ref, w_ref, aqk_ref, *, tt):
    ti = pl.program_id(1)

    @pl.when(ti == 0)
    def _():
        s_ref[...] = jnp.zeros_like(s_ref)
        ext_ref[0:CONV_PAD, :] = jnp.zeros((CONV_PAD, L1_CONV), F32)

    x = x_ref[...]
    xb = x.astype(BF16)
    ext_ref[CONV_PAD:CONV_PAD + tt, :] = jnp.dot(xb, w_in_ref[:, 0:L1_CONV], preferred_element_type=F32)
    h2_ref[...] = jnp.dot(xb, w_in_ref[:, L1_CONV:L1_COLS], preferred_element_type=F32)

    conv = ext_ref[CONV_PAD:CONV_PAD + tt, :] * conv_ref[CONV_WIDTH - 1:CONV_WIDTH, :]
    for j in range(CONV_WIDTH - 1):
        back = CONV_WIDTH - 1 - j
        conv = conv + ext_ref[CONV_PAD - back:CONV_PAD - back + tt, :] * conv_ref[j:j + 1, :]
    ext_ref[0:CONV_PAD, :] = ext_ref[tt:tt + CONV_PAD, :]
    qkv_ref[...] = _silu(conv)

    for hd in range(GDN_HEADS):
        for base, scale in ((L1_Q, GDN_DK ** -0.5), (L1_K, 1.0)):
            sl = slice(base + hd * GDN_DK, base + (hd + 1) * GDN_DK)
            t = qkv_ref[:, sl]
            qkv_ref[:, sl] = t * (lax.rsqrt(jnp.sum(t * t, axis=-1, keepdims=True) + L2_EPS) * scale)

    ab = h2_ref[:, L1_AB - L1_CONV:L1_AB - L1_CONV + LANES]
    g_blk = -jnp.exp(alog_ref[...]) * _softplus(ab + dtb_ref[...])
    beta_blk = _sigmoid(ab)
    gc_blk = _mm_exact_lhs01(tril_ref[...], g_blk)
    eg_blk = jnp.exp(gc_blk)

    ci = lax.broadcasted_iota(jnp.int32, (CHUNK, CHUNK), 0)
    cj = lax.broadcasted_iota(jnp.int32, (CHUNK, CHUNK), 1)
    incl = cj <= ci
    strict = cj < ci

    heads = range(GDN_HEADS)
    col = lambda base, hd: slice(base + hd * GDN_DK, base + (hd + 1) * GDN_DK)
    n_chunks = tt // CHUNK

    for c in range(n_chunks):
        rows = slice(c * CHUNK, (c + 1) * CHUNK)
        gc_c = gc_blk[rows]
        gc_t = gc_c.T
        kdec_scale = jnp.exp(gc_c[CHUNK - 1:CHUNK, :] - gc_c)
        q_h = [qkv_ref[rows, col(L1_Q, hd)] for hd in heads]
        k_h = [qkv_ref[rows, col(L1_K, hd)] for hd in heads]
        v_h = [qkv_ref[rows, col(L1_V, hd)] for hd in heads]
        beta = [beta_blk[rows, L1_BETA_LANE + hd:L1_BETA_LANE + hd + 1] for hd in heads]
        eg = [eg_blk[rows, hd:hd + 1] for hd in heads]
        decay = [jnp.exp(jnp.where(incl, gc_c[:, hd:hd + 1] - gc_t[hd:hd + 1, :], NEG_BIG)) for hd in heads]
        kb = [k_h[hd] * beta[hd] for hd in heads]
        low = [jnp.where(strict, _mm_nt(kb[hd], k_h[hd]) * decay[hd], 0.0) for hd in heads]
        a_qk = [_mm_nt(q_h[hd], k_h[hd]) * decay[hd] for hd in heads]
        a_m = [-low[hd] for hd in heads]
        m = [_mm(low[hd], low[hd]) for hd in heads]
        for it in range(5):
            am = [_mm(a_m[hd], m[hd]) for hd in heads]
            a_m = [a_m[hd] + m[hd] + am[hd] for hd in heads]
            if it < 4:
                m = [_mm(m[hd], m[hd]) for hd in heads]
        rhs = [jnp.concatenate([v_h[hd] * beta[hd], kb[hd] * eg[hd]], axis=1) for hd in heads]
        uw = [rhs[hd] + _mm(a_m[hd], rhs[hd]) for hd in heads]
        for hd in heads:
            u_ref[rows, col(0, hd)] = uw[hd][:, 0:GDN_DV]
            w_ref[rows, col(0, hd)] = uw[hd][:, GDN_DV:GDN_DV + GDN_DK]
            aqk_ref[rows, hd * LANES:hd * LANES + CHUNK] = a_qk[hd]
            qkv_ref[rows, col(L1_Q, hd)] = q_h[hd] * eg[hd]
            qkv_ref[rows, col(L1_K, hd)] = k_h[hd] * kdec_scale[:, hd:hd + 1]

    s = [s_ref[hd] for hd in heads]
    for c in range(n_chunks):
        rows = slice(c * CHUNK, (c + 1) * CHUNK)
        e_last = jnp.exp(gc_blk[(c + 1) * CHUNK - 1:(c + 1) * CHUNK, :])
        ws_qs = [_mm(jnp.concatenate([w_ref[rows, col(0, hd)], qkv_ref[rows, col(L1_Q, hd)]], axis=0), s[hd])
                 for hd in heads]
        v_new = [u_ref[rows, col(0, hd)] - ws_qs[hd][0:CHUNK] for hd in heads]
        o_h = [ws_qs[hd][CHUNK:2 * CHUNK] + _mm(aqk_ref[rows, hd * LANES:hd * LANES + CHUNK], v_new[hd])
               for hd in heads]
        s = [s[hd] * e_last[:, hd:hd + 1] + _mm_tn(qkv_ref[rows, col(L1_K, hd)], v_new[hd]) for hd in heads]
        for hd in heads:
            mix_ref[rows, col(0, hd)] = o_h[hd]
    for hd in heads:
        s_ref[hd] = s[hd]

    for hd in range(GDN_HEADS):
        sl = slice(hd * GDN_DV, (hd + 1) * GDN_DV)
        o = mix_ref[:, sl]
        o = o * lax.rsqrt(jnp.mean(o * o, axis=-1, keepdims=True) + NORM_EPS) * cnorm_ref[:, sl]
        mix_ref[:, sl] = o * _silu(h2_ref[:, L1_GATE - L1_CONV + hd * GDN_DV:L1_GATE - L1_CONV + (hd + 1) * GDN_DV])
    y = _mm(mix_ref[...], w_out_ref[...])
    x1 = _layer_norm(DEEPNORM_ALPHA * x + y, lng_ref[...], lnb_ref[...])
    o_ref[...] = x1

    x_hi = x1.astype(BF16)
    x_lo = (x1 - x_hi.astype(F32)).astype(BF16)
    logits = (jnp.dot(x_hi, wr_hi_ref[...], preferred_element_type=F32)
              + jnp.dot(x_hi, wr_lo_ref[...], preferred_element_type=F32)
              + jnp.dot(x_lo, wr_hi_ref[...], preferred_element_type=F32)) + br_ref[...]
    lane = lax.broadcasted_iota(jnp.int32, (tt, ROUTE_LANES), 1)
    lane_f = lane.astype(F32)
    logits = jnp.where(lane < N_EXPERTS, logits, NEG_BIG)
    m1 = jnp.max(logits, axis=-1, keepdims=True)
    i1 = jnp.min(jnp.where(logits == m1, lane_f, float(ROUTE_LANES)), axis=-1, keepdims=True)
    rest = jnp.where(lane_f == i1, NEG_BIG, logits)
    m2 = jnp.max(rest, axis=-1, keepdims=True)
    i2 = jnp.min(jnp.where(rest == m2, lane_f, float(ROUTE_LANES)), axis=-1, keepdims=True)
    e21 = jnp.exp(m2 - m1)
    g1 = 1.0 / (1.0 + e21)
    g2 = e21 * g1
    route_ref[...] = jnp.where(lane == 0, i1, jnp.where(lane == 1, i2, jnp.where(lane == 2, g1, jnp.where(lane == 3, g2, 0.0))))


def _l1_mixer(x, w_in, conv_w, a_log, dt_bias, c_norm, w_out, ln_g, ln_b, w_router, b_router, *, tt):
    bsz, seq, d = x.shape
    w_in_p = _pack_l1_w_in(w_in).astype(BF16)
    lane_pad = lambda v: jnp.pad(v[None, :], ((0, 0), (0, LANES - v.shape[0])))
    wr = jnp.pad(w_router, ((0, 0), (0, ROUTE_LANES - N_EXPERTS)))
    wr_hi = wr.astype(BF16)
    wr_lo = (wr - wr_hi.astype(F32)).astype(BF16)
    params = [w_in_p, conv_w, lane_pad(a_log), lane_pad(dt_bias), c_norm.reshape(1, GDN_V), w_out.astype(BF16),
              ln_g[None, :], ln_b[None, :], jnp.asarray(_chunk_tril_np(tt), BF16), wr_hi, wr_lo, lane_pad(b_router)]
    tile = lambda w: pl.BlockSpec((None, tt, w), lambda b, t: (b, t, 0))
    return pl.pallas_call(
        functools.partial(_l1_mixer_kernel, tt=tt),
        grid=(bsz, seq // tt),
        in_specs=[tile(d)] + [_const_spec(a.shape) for a in params],
        out_specs=[tile(d), tile(ROUTE_LANES)],
        out_shape=[jax.ShapeDtypeStruct((bsz, seq, d), F32), jax.ShapeDtypeStruct((bsz, seq, ROUTE_LANES), F32)],
        scratch_shapes=[pltpu.VMEM((tt + CONV_PAD, L1_CONV), F32), pltpu.VMEM((tt, L1_COLS - L1_CONV), F32),
                        pltpu.VMEM((tt, L1_CONV), F32), pltpu.VMEM((tt, GDN_V), F32),
                        pltpu.VMEM((GDN_HEADS, GDN_DK, GDN_DV), F32),
                        pltpu.VMEM((tt, GDN_V), F32), pltpu.VMEM((tt, GDN_QK), F32),
                        pltpu.VMEM((tt, GDN_HEADS * LANES), F32)],
        compiler_params=pltpu.CompilerParams(dimension_semantics=("arbitrary", "arbitrary"),
                                             vmem_limit_bytes=VMEM_LIMIT_BYTES),
        name="l1_mixer",
    )(x, *params)


ID_ROWS = 4


def _rank_kernel(route_ref, stril_ref, ids_ref, counts_ref, carry_ref):
    @pl.when(pl.program_id(0) == 0)
    def _():
        carry_ref[...] = jnp.zeros_like(carry_ref)

    r = route_ref[...]
    tr = r.shape[0]
    e1, e2 = r[:, 0:1], r[:, 1:2]
    lane = lax.broadcasted_iota(jnp.int32, (tr, ROUTE_LANES), 1)
    lane_f = lane.astype(F32)
    oh1 = (lane_f == e1).astype(F32)
    oh2 = (lane_f == e2).astype(F32)
    both = oh1 + oh2
    before = jnp.dot(stril_ref[...], both.astype(BF16), preferred_element_type=F32) + carry_ref[...]
    rank1 = jnp.sum(oh1 * before, axis=-1, keepdims=True)
    rank2 = jnp.sum(oh2 * before, axis=-1, keepdims=True)
    table = jnp.where(lane == 0, e1, jnp.where(lane == 1, e2, jnp.where(lane == 2, rank1,
                      jnp.where(lane == 3, rank2, 0.0))))
    ids_ref[...] = table.T[0:ID_ROWS, :].astype(jnp.int32)
    carry_ref[...] += jnp.sum(both, axis=0, keepdims=True)
    counts_ref[...] = carry_ref[...]


def _rank(route2d, *, tr):
    n = route2d.shape[0]
    i = np.arange(tr)
    stril = jnp.asarray((i[None, :] < i[:, None]).astype(np.float32), BF16)
    return pl.pallas_call(
        _rank_kernel,
        grid=(n // tr,),
        in_specs=[pl.BlockSpec((tr, ROUTE_LANES), lambda i: (i, 0)), _const_spec((tr, tr))],
        out_specs=[pl.BlockSpec((ID_ROWS, tr), lambda i: (0, i)), _const_spec((1, ROUTE_LANES))],
        out_shape=[jax.ShapeDtypeStruct((ID_ROWS, n), jnp.int32), jax.ShapeDtypeStruct((1, ROUTE_LANES), F32)],
        scratch_shapes=[pltpu.VMEM((1, ROUTE_LANES), F32)],
        compiler_params=pltpu.CompilerParams(dimension_semantics=("arbitrary",)),
        name="moe_rank",
    )(route2d, stril)


ROW_SUB = D_MODEL // LANES


def _to_row_tiles(dst_ref, src, n):
    for lb in range(ROW_SUB):
        dst_ref[pl.ds(lb, n, stride=ROW_SUB), :] = src[:, lb * LANES:(lb + 1) * LANES]


def _from_row_tiles(src_ref, n):
    return [src_ref[pl.ds(lb, n, stride=ROW_SUB), :] for lb in range(ROW_SUB)]


def _tile_copy(src_ref, src_sub, dst_ref, dst_sub, sem):
    src = src_ref.at[pl.ds(pl.multiple_of(src_sub, ROW_SUB), ROW_SUB), :]
    dst = dst_ref.at[pl.ds(pl.multiple_of(dst_sub, ROW_SUB), ROW_SUB), :]
    return pltpu.make_async_copy(src, dst, sem)


def _wait_rows(hbm_ref, buf_ref, sem, n, copies):
    for _ in range(copies):
        pltpu.make_async_copy(hbm_ref.at[pl.ds(0, n * ROW_SUB), :], buf_ref, sem).wait()


def _dispatch_kernel(p0_ref, p1_ref, x_ref, init_hbm, xs_hbm, xt0_ref, xt1_ref, sems, *, td):
    del init_hbm
    p_refs = (p0_ref, p1_ref)
    i = pl.program_id(0)
    last = pl.num_programs(0) - 1

    def step(xt_ref, sem):
        @pl.when(i >= 2)
        def _():
            _wait_rows(xs_hbm, xt_ref, sem, td, TOP_K)

        _to_row_tiles(xt_ref, x_ref[...], td)

        def issue(t, carry):
            for k in range(TOP_K):
                _tile_copy(xt_ref, t * ROW_SUB, xs_hbm, p_refs[k][t], sem).start()
            return carry

        lax.fori_loop(0, td, issue, 0, unroll=8)

    for slot, (xt_ref, sem) in enumerate(((xt0_ref, sems.at[0]), (xt1_ref, sems.at[1]))):
        @pl.when(i % 2 == slot)
        def _():
            step(xt_ref, sem)

    @pl.when(i == last)
    def _():
        _wait_rows(xs_hbm, xt0_ref, sems.at[0], td, TOP_K)
        _wait_rows(xs_hbm, xt1_ref, sems.at[1], td, TOP_K)


def _dispatch(pos, x2d, n_rows, *, td):
    n, d = x2d.shape
    init = jnp.zeros((n_rows * ROW_SUB, LANES), F32)
    smem_rows = pl.BlockSpec((td,), lambda i: (i,), memory_space=pltpu.SMEM)
    assert n // td >= 2
    return pl.pallas_call(
        functools.partial(_dispatch_kernel, td=td),
        grid=(n // td,),
        in_specs=[smem_rows, smem_rows, pl.BlockSpec((td, d), lambda i: (i, 0)), pl.BlockSpec(memory_space=pl.ANY)],
        out_specs=pl.BlockSpec(memory_space=pl.ANY),
        scratch_shapes=[pltpu.VMEM((td * ROW_SUB, LANES), F32), pltpu.VMEM((td * ROW_SUB, LANES), F32),
                        pltpu.SemaphoreType.DMA((2,))],
        out_shape=jax.ShapeDtypeStruct((n_rows * ROW_SUB, LANES), F32),
        input_output_aliases={3: 0},
        compiler_params=pltpu.CompilerParams(dimension_semantics=("arbitrary",)),
        name="moe_dispatch",
    )(pos[0], pos[1], x2d, init)


def _gmm_kernel(te_ref, used_ref, first_ref, x_ref, wg_ref, wu_ref, wd_ref, o_ref,
                xb_ref, acc_ref, wgb_ref, wub_ref, wdb_ref, *, tm):
    i, j = pl.program_id(0), pl.program_id(1)

    @pl.when(i < used_ref[0])
    def _():
        @pl.when(j == 0)
        def _():
            for lb, blk in enumerate(_from_row_tiles(x_ref, tm)):
                xb_ref[:, lb * LANES:(lb + 1) * LANES] = blk.astype(BF16)
            acc_ref[...] = jnp.zeros_like(acc_ref)

        @pl.when(first_ref[i] == 1)
        def _():
            wgb_ref[j] = wg_ref[...].astype(BF16)
            wub_ref[j] = wu_ref[...].astype(BF16)
            wdb_ref[j] = wd_ref[...].astype(BF16)

        xb = xb_ref[...]
        gt = jnp.dot(xb, wgb_ref[j], preferred_element_type=F32)
        up = jnp.dot(xb, wub_ref[j], preferred_element_type=F32)
        acc_ref[...] += jnp.dot((_silu(gt) * up).astype(BF16), wdb_ref[j], preferred_element_type=F32)

        @pl.when(j == pl.num_programs(1) - 1)
        def _():
            _to_row_tiles(o_ref, acc_ref[...], tm)

    @pl.when(jnp.logical_and(i >= used_ref[0], j == 0))
    def _():
        o_ref[...] = jnp.zeros_like(o_ref)


def _gmm(tile_expert, n_used, tile_first, xs, w_gu, w_down, *, tm, tf):
    d = D_MODEL
    n_rows = xs.shape[0] // ROW_SUB
    nf = D_FF // tf
    n_tiles = n_rows // tm
    row = lambda i, j, te, used, first: (jnp.minimum(i, used[0] - 1), 0)
    jj = lambda i, j, first: jnp.where(first[i] == 1, j, nf - 1)
    grid_spec = pltpu.PrefetchScalarGridSpec(
        num_scalar_prefetch=3,
        grid=(n_tiles, nf),
        in_specs=[pl.BlockSpec((tm * ROW_SUB, LANES), row),
                  pl.BlockSpec((None, d, tf), lambda i, j, te, used, first: (te[i], 0, jj(i, j, first))),
                  pl.BlockSpec((None, d, tf), lambda i, j, te, used, first: (te[i], 0, jj(i, j, first) + nf)),
                  pl.BlockSpec((None, tf, d), lambda i, j, te, used, first: (te[i], jj(i, j, first), 0))],
        out_specs=pl.BlockSpec((tm * ROW_SUB, LANES), lambda i, j, te, used, first: (i, 0)),
        scratch_shapes=[pltpu.VMEM((tm, d), BF16), pltpu.VMEM((tm, d), F32),
                        pltpu.VMEM((nf, d, tf), BF16), pltpu.VMEM((nf, d, tf), BF16),
                        pltpu.VMEM((nf, tf, d), BF16)],
    )
    return pl.pallas_call(
        functools.partial(_gmm_kernel, tm=tm),
        grid_spec=grid_spec,
        out_shape=jax.ShapeDtypeStruct((n_rows * ROW_SUB, LANES), F32),
        compiler_params=pltpu.CompilerParams(dimension_semantics=("arbitrary", "arbitrary"),
                                             vmem_limit_bytes=VMEM_LIMIT_BYTES),
        name="moe_gmm",
    )(tile_expert, n_used, tile_first, xs, w_gu, w_gu, w_down)


def _combine_kernel(p0_ref, p1_ref, p0_next_ref, p1_next_ref, x_ref, route_ref, lng_ref, lnb_ref, y_hbm, o_ref,
                    ya0_ref, yb0_ref, ya1_ref, yb1_ref, sems, *, tc):
    i = pl.program_id(0)
    n = pl.num_programs(0)
    slots = ((ya0_ref, yb0_ref, sems.at[0]), (ya1_ref, yb1_ref, sems.at[1]))

    def issue(p_refs, slot):
        ya_ref, yb_ref, sem = slots[slot]

        def body(t, carry):
            for p_ref, buf in zip(p_refs, (ya_ref, yb_ref)):
                _tile_copy(y_hbm, p_ref[t], buf, t * ROW_SUB, sem).start()
            return carry

        lax.fori_loop(0, tc, body, 0, unroll=8)

    @pl.when(i == 0)
    def _():
        issue((p0_ref, p1_ref), 0)

    for slot in range(2):
        @pl.when(jnp.logical_and(i + 1 < n, (i + 1) % 2 == slot))
        def _():
            issue((p0_next_ref, p1_next_ref), slot)

    for slot in range(2):
        @pl.when(i % 2 == slot)
        def _():
            ya_ref, yb_ref, sem = slots[slot]
            _wait_rows(y_hbm, ya_ref, sem, tc, TOP_K)
            r = route_ref[...]
            g1, g2 = r[:, 2:3], r[:, 3:4]
            y = jnp.concatenate(
                [g1 * a + g2 * b for a, b in zip(_from_row_tiles(ya_ref, tc), _from_row_tiles(yb_ref, tc))], axis=1)
            o_ref[...] = _layer_norm(DEEPNORM_ALPHA * x_ref[...] + y, lng_ref[...], lnb_ref[...])


def _combine(pos, x2d, route2d, ys, ln_g, ln_b, *, tc):
    n, d = x2d.shape
    steps = n // tc
    smem_rows = pl.BlockSpec((tc,), lambda i: (i,), memory_space=pltpu.SMEM)
    smem_next = pl.BlockSpec((tc,), lambda i: (jnp.minimum(i + 1, steps - 1),), memory_space=pltpu.SMEM)
    return pl.pallas_call(
        functools.partial(_combine_kernel, tc=tc),
        grid=(steps,),
        in_specs=[smem_rows, smem_rows, smem_next, smem_next,
                  pl.BlockSpec((tc, d), lambda i: (i, 0)),
                  pl.BlockSpec((tc, ROUTE_LANES), lambda i: (i, 0)),
                  _const_spec((1, d)), _const_spec((1, d)),
                  pl.BlockSpec(memory_space=pl.ANY)],
        out_specs=pl.BlockSpec((tc, d), lambda i: (i, 0)),
        scratch_shapes=[pltpu.VMEM((tc * ROW_SUB, LANES), F32) for _ in range(2 * TOP_K)]
                       + [pltpu.SemaphoreType.DMA((2,))],
        out_shape=jax.ShapeDtypeStruct((n, d), F32),
        compiler_params=pltpu.CompilerParams(dimension_semantics=("arbitrary",),
                                             vmem_limit_bytes=VMEM_LIMIT_BYTES),
        name="moe_combine",
    )(pos[0], pos[1], pos[0], pos[1], x2d, route2d, ln_g[None, :], ln_b[None, :], ys)


def _moe(x2d, route2d, w_gu, w_down, ln_g, ln_b, *, tm, tf, tr, td, tc):
    n, d = x2d.shape
    ids, counts = _rank(route2d, tr=tr)
    cnt = counts[0, :N_EXPERTS].astype(jnp.int32)
    tiles_per = (cnt + tm - 1) // tm
    tile_end = jnp.cumsum(tiles_per)
    offs = (tile_end - tiles_per) * tm
    n_tiles = (n * TOP_K) // tm + N_EXPERTS
    n_used = tile_end[-1:]
    t_idx = jnp.arange(n_tiles, dtype=jnp.int32)
    tile_expert = jnp.sum(jnp.minimum(t_idx, n_used - 1)[:, None] >= tile_end[None, :], axis=1).astype(jnp.int32)
    tile_start = tile_end - tiles_per
    tile_first = jnp.logical_and(t_idx == tile_start[tile_expert], t_idx < n_used).astype(jnp.int32)
    pos = (offs[ids[0:TOP_K]] + ids[TOP_K:2 * TOP_K]) * ROW_SUB
    xs = _dispatch(pos, x2d, n_tiles * tm, td=td)
    ys = _gmm(tile_expert, n_used, tile_first, xs, w_gu, w_down, tm=tm, tf=tf)
    return _combine(pos, x2d, route2d, ys, ln_g, ln_b, tc=tc)


def kernel(x, positions, ab_w_in, gla_w_gate2, gla_b_gate, gla_norm, ret_norm, ab_w_out, ab_ln1_g, ab_ln1_b, ffn_w_gu, ffn_w_down, ab_ln2_g, ab_ln2_b, c_w_in, c_conv_w, c_a_log, c_dt_bias, c_norm, c_w_out, c_ln1_g, c_ln1_b, moe_w_router, moe_b_router, moe_w_gu, moe_w_down, c_ln2_g, c_ln2_b):
    bsz, seq, d = x.shape
    for layer in range(DEPTH):
        i = layer // 2
        if layer % 2 == 0:
            x = _l0_mixer(x, positions, ab_w_in[i], gla_w_gate2[i], gla_b_gate[i], gla_norm[i], ret_norm[i],
                          ab_w_out[i], ab_ln1_g[i], ab_ln1_b[i], tt=L0_TIME_TILE)
            x = _ffn(x.reshape(bsz * seq, d), ffn_w_gu[i], ffn_w_down[i], ab_ln2_g[i], ab_ln2_b[i],
                     tm=FFN_ROW_TILE, tf=FF_TILE).reshape(bsz, seq, d)
        else:
            x, route = _l1_mixer(x, c_w_in[i], c_conv_w[i], c_a_log[i], c_dt_bias[i], c_norm[i], c_w_out[i],
                                 c_ln1_g[i], c_ln1_b[i], moe_w_router[i], moe_b_router[i], tt=L1_TIME_TILE)
            x = _moe(x.reshape(bsz * seq, d), route.reshape(bsz * seq, ROUTE_LANES), moe_w_gu[i], moe_w_down[i],
                     c_ln2_g[i], c_ln2_b[i], tm=MOE_ROW_TILE, tf=FF_TILE, tr=ROUTE_TILE, td=ROUTE_TILE,
                     tc=ROUTE_TILE).reshape(bsz, seq, d)
    return x
```

```python
import functools

import numpy as np
import jax
import jax.numpy as jnp
from jax import lax
from jax.experimental import pallas as pl
from jax.experimental.pallas import tpu as pltpu

F32 = jnp.float32
BF16 = jnp.bfloat16

D_MODEL = 1024
DEPTH = 2
CHUNK = 64
GLA_HEADS, GLA_DK, GLA_DV, GLA_GATE_RANK, GLA_TAU = 4, 64, 128, 16, 16.0
RET_HEADS, RET_DK, RET_DV = 4, 64, 128
ROPE_BASE = 10000.0
GDN_HEADS, GDN_DK, GDN_DV = 8, 128, 128
CONV_WIDTH = 4
D_FF = 3584
N_EXPERTS = 8
TOP_K = 2
NORM_EPS = 1e-5
L2_EPS = 1e-6
DEEPNORM_ALPHA = (2.0 * DEPTH) ** 0.25

GLA_QK = GLA_HEADS * GLA_DK
GLA_V = GLA_HEADS * GLA_DV
RET_QK = RET_HEADS * RET_DK
RET_V = RET_HEADS * RET_DV
GDN_QK = GDN_HEADS * GDN_DK
GDN_V = GDN_HEADS * GDN_DV

LANES = 128
VMEM_LIMIT_BYTES = 56 * 1024 * 1024
NEG_BIG = -1e30

L0_TIME_TILE = 256
L1_TIME_TILE = 256
FFN_ROW_TILE = 1024
MOE_ROW_TILE = 512
FF_TILE = 512
ROUTE_TILE = 512


def _mm(a, b):
    return jnp.dot(a.astype(BF16), b.astype(BF16), preferred_element_type=F32)


def _mm_nt(a, b):
    return lax.dot_general(a.astype(BF16), b.astype(BF16), (((1,), (1,)), ((), ())),
                           preferred_element_type=F32)


def _mm_tn(a, b):
    return lax.dot_general(a.astype(BF16), b.astype(BF16), (((0,), (0,)), ((), ())),
                           preferred_element_type=F32)


def _split3(x):
    hi = x.astype(BF16)
    r1 = x - hi.astype(F32)
    mid = r1.astype(BF16)
    lo = (r1 - mid.astype(F32)).astype(BF16)
    return hi, mid, lo


def _mm_exact_lhs01(m01, x):
    hi, mid, lo = _split3(x)
    return (jnp.dot(m01, hi, preferred_element_type=F32)
            + jnp.dot(m01, mid, preferred_element_type=F32)
            + jnp.dot(m01, lo, preferred_element_type=F32))


def _sigmoid(x):
    return 1.0 / (1.0 + jnp.exp(-x))


def _silu(x):
    return x * _sigmoid(x)


def _softplus(x):
    return jnp.maximum(x, 0.0) + jnp.log(1.0 + jnp.exp(-jnp.abs(x)))


def _layer_norm(x, g, b):
    mu = jnp.mean(x, axis=-1, keepdims=True)
    xc = x - mu
    var = jnp.mean(xc * xc, axis=-1, keepdims=True)
    return xc * lax.rsqrt(var + NORM_EPS) * g + b


def _chunk_tril_np(tt):
    i = np.arange(tt)
    same = (i[:, None] // CHUNK) == (i[None, :] // CHUNK)
    return (same & (i[None, :] <= i[:, None])).astype(np.float32)


def _const_spec(shape):
    nd = len(shape)
    return pl.BlockSpec(shape, lambda *_: (0,) * nd)


L0_GQ, L0_GK, L0_GV, L0_GR = 0, 256, 512, 1024
L0_RQ, L0_RK, L0_RV, L0_RG = 1536, 1792, 2048, 2560
L0_GA = 3072
L0_COLS = 3200


def _pack_l0_w_in(w):
    offs = np.cumsum([0, GLA_QK, GLA_QK, GLA_V, GLA_GATE_RANK, GLA_V, RET_QK, RET_QK, RET_V, RET_V])
    gq, gk, gv, ga, gr, rq, rk, rv, rg = [w[:, offs[i]:offs[i + 1]] for i in range(9)]
    ga = jnp.pad(ga, ((0, 0), (0, LANES - GLA_GATE_RANK)))
    return jnp.concatenate([gq, gk, gv, gr, rq, rk, rv, rg, ga], axis=1)


def _ret_tables():
    h = np.arange(RET_HEADS, dtype=np.float64)
    log_gamma = np.log(1.0 - 2.0 ** (-5.0 - h))
    pos = np.arange(CHUNK, dtype=np.float64)
    diff = pos[:, None] - pos[None, :]
    dmat = np.where(diff >= 0, np.exp(log_gamma[:, None, None] * np.maximum(diff, 0.0)), 0.0)
    xi = np.exp(log_gamma[None, :] * (pos[:, None] + 1.0))
    zeta = np.exp(log_gamma[None, :] * (CHUNK - 1.0 - pos[:, None]))
    decay = np.exp(log_gamma * CHUNK)
    xi_full = np.repeat(xi, RET_DV, axis=1)
    zeta_full = np.repeat(zeta, RET_DK, axis=1)
    decay_full = np.repeat(decay, RET_DK)[None, :]
    return (dmat.astype(np.float32), xi_full.astype(np.float32), zeta_full.astype(np.float32),
            decay_full.astype(np.float32))


def _rope_tables():
    half = RET_DK // 2
    inv_freq = ROPE_BASE ** (-np.arange(0, RET_DK, 2, dtype=np.float32) / RET_DK)
    per_head = np.concatenate([inv_freq, inv_freq])
    freq_full = np.tile(per_head, RET_HEADS)[None, :].astype(np.float32)
    sign = np.tile(np.concatenate([-np.ones(half), np.ones(half)]), RET_HEADS)[None, :].astype(np.float32)
    return freq_full, sign


def _l0_mixer_kernel(x_ref, pos_ref, w_in_ref, wg2_ref, bg_ref, gnorm_ref, rnorm_ref, w_out_ref,
                     lng_ref, lnb_ref, tril_ref, dmat_ref, xi_ref, zeta_ref, rdecay_ref,
                     freq_ref, sign_ref, o_ref, h_ref, mix_ref, sg_ref, sr_ref, *, tt):
    ti = pl.program_id(1)

    @pl.when(ti == 0)
    def _():
        sg_ref[...] = jnp.zeros_like(sg_ref)
        sr_ref[...] = jnp.zeros_like(sr_ref)

    x = x_ref[...]
    h_ref[...] = _mm(x, w_in_ref[...])

    lane = lax.broadcasted_iota(jnp.int32, (1, LANES), 1)
    lo_half = lane < GLA_DK
    ci = lax.broadcasted_iota(jnp.int32, (CHUNK, CHUNK), 0)
    cj = lax.broadcasted_iota(jnp.int32, (CHUNK, CHUNK), 1)
    causal = cj <= ci

    z = _mm(h_ref[:, L0_GA:L0_GA + LANES], wg2_ref[...]) + bg_ref[...]
    log_a = -_softplus(-z) * (1.0 / GLA_TAU)
    b = _mm_exact_lhs01(tril_ref[...], log_a)
    eb = jnp.exp(b)
    q_dec = h_ref[:, L0_GQ:L0_GQ + GLA_QK] * (GLA_DK ** -0.5) * eb
    k_all = h_ref[:, L0_GK:L0_GK + GLA_QK]
    k_neg = k_all * jnp.exp(-b)

    ang = pos_ref[...] * freq_ref[:, 0:LANES]
    cos = jnp.concatenate([jnp.cos(ang)] * (RET_QK // LANES), axis=1)
    sin = jnp.concatenate([jnp.sin(ang)] * (RET_QK // LANES), axis=1) * sign_ref[...]
    half = RET_DK // 2
    lane256 = lax.broadcasted_iota(jnp.int32, (1, RET_QK), 1)
    first_half = (lane256 & (RET_DK - 1)) < half

    def rope(t):
        swapped = jnp.where(first_half, pltpu.roll(t, RET_QK - half, 1), pltpu.roll(t, half, 1))
        return t * cos + swapped * sin

    rq = rope(h_ref[:, L0_RQ:L0_RQ + RET_QK]) * (RET_DK ** -0.5)
    rk = rope(h_ref[:, L0_RK:L0_RK + RET_QK])

    n_chunks = tt // CHUNK
    heads = range(GLA_HEADS)
    pairs = range(GLA_HEADS // 2)
    pair_lanes = lambda hd: slice((hd // 2) * LANES, (hd // 2 + 1) * LANES)
    head_mask = lambda hd: lo_half if hd % 2 == 0 else jnp.logical_not(lo_half)
    vcol = lambda base, hd: slice(base + hd * GLA_DV, base + (hd + 1) * GLA_DV)
    sg = [sg_ref[p] for p in pairs]
    sr = [sr_ref[p] for p in pairs]
    for c in range(n_chunks):
        rows = slice(c * CHUNK, (c + 1) * CHUNK)
        b_last = b[(c + 1) * CHUNK - 1:(c + 1) * CHUNK, :]
        k_end_c = k_all[rows] * jnp.exp(b_last - b[rows])
        dec_c = jnp.exp(b_last)
        rk_c = rk[rows]
        rk_z = rk_c * zeta_ref[...]
        g_q = [jnp.where(head_mask(hd), q_dec[rows, pair_lanes(hd)], 0.0) for hd in heads]
        r_q = [jnp.where(head_mask(hd), rq[rows, pair_lanes(hd)], 0.0) for hd in heads]
        g_v = [h_ref[rows, vcol(L0_GV, hd)] for hd in heads]
        r_v = [h_ref[rows, vcol(L0_RV, hd)] for hd in heads]
        g_att = [jnp.where(causal, _mm_nt(g_q[hd], k_neg[rows, pair_lanes(hd)]), 0.0) for hd in heads]
        r_att = [_mm_nt(r_q[hd], rk_c[:, pair_lanes(hd)]) * dmat_ref[hd] for hd in heads]
        g_o = [_mm(g_att[hd], g_v[hd]) + _mm_nt(g_q[hd], sg[hd // 2]) for hd in heads]
        r_o = [_mm(r_att[hd], r_v[hd]) + _mm_nt(r_q[hd], sr[hd // 2]) * xi_ref[:, vcol(0, hd)] for hd in heads]
        g_d = [_mm_tn(g_v[hd], k_end_c[:, pair_lanes(hd)]) for hd in heads]
        r_d = [_mm_tn(r_v[hd], rk_z[:, pair_lanes(hd)]) for hd in heads]
        for hd in heads:
            mix_ref[rows, vcol(0, hd)] = g_o[hd]
            mix_ref[rows, vcol(GLA_V, hd)] = r_o[hd]
        sg = [sg[p] * dec_c[:, p * LANES:(p + 1) * LANES] + jnp.where(lo_half, g_d[2 * p], g_d[2 * p + 1])
              for p in pairs]
        sr = [sr[p] * rdecay_ref[:, p * LANES:(p + 1) * LANES] + jnp.where(lo_half, r_d[2 * p], r_d[2 * p + 1])
              for p in pairs]
    for p in pairs:
        sg_ref[p] = sg[p]
        sr_ref[p] = sr[p]

    for hd in range(GLA_HEADS):
        sl = slice(hd * GLA_DV, (hd + 1) * GLA_DV)
        o = mix_ref[:, sl]
        o = o * lax.rsqrt(jnp.mean(o * o, axis=-1, keepdims=True) + NORM_EPS) * gnorm_ref[:, sl]
        mix_ref[:, sl] = o * _silu(h_ref[:, L0_GR + hd * GLA_DV:L0_GR + (hd + 1) * GLA_DV])
    for hd in range(RET_HEADS):
        sl = slice(hd * RET_DV, (hd + 1) * RET_DV)
        o = mix_ref[:, GLA_V + hd * RET_DV:GLA_V + (hd + 1) * RET_DV]
        oc = o - jnp.mean(o, axis=-1, keepdims=True)
        o = oc * lax.rsqrt(jnp.mean(oc * oc, axis=-1, keepdims=True) + NORM_EPS) * rnorm_ref[:, sl]
        mix_ref[:, GLA_V + hd * RET_DV:GLA_V + (hd + 1) * RET_DV] = (
            o * _silu(h_ref[:, L0_RG + hd * RET_DV:L0_RG + (hd + 1) * RET_DV]))

    y = _mm(mix_ref[...], w_out_ref[...])
    o_ref[...] = _layer_norm(DEEPNORM_ALPHA * x + y, lng_ref[...], lnb_ref[...])


def _l0_mixer(x, positions, w_in, w_gate2, b_gate, gla_norm, ret_norm, w_out, ln_g, ln_b, *, tt):
    bsz, seq, d = x.shape
    w_in_p = _pack_l0_w_in(w_in).astype(BF16)
    wg2 = jnp.pad(w_gate2, ((0, LANES - GLA_GATE_RANK), (0, 0))).astype(BF16)
    dmat, xi_full, zeta_full, rdecay = _ret_tables()
    freq_full, sign = _rope_tables()
    pos_f = positions.astype(F32)[..., None]
    consts = [jnp.asarray(_chunk_tril_np(tt), BF16), jnp.asarray(dmat), jnp.asarray(xi_full),
              jnp.asarray(zeta_full), jnp.asarray(rdecay), jnp.asarray(freq_full), jnp.asarray(sign)]
    params = [w_in_p, wg2, b_gate[None, :], gla_norm.reshape(1, GLA_V), ret_norm.reshape(1, RET_V),
              w_out.astype(BF16), ln_g[None, :], ln_b[None, :]]
    tile = lambda w: pl.BlockSpec((None, tt, w), lambda b, t: (b, t, 0))
    return pl.pallas_call(
        functools.partial(_l0_mixer_kernel, tt=tt),
        grid=(bsz, seq // tt),
        in_specs=[tile(d), tile(1)] + [_const_spec(a.shape) for a in params + consts],
        out_specs=tile(d),
        out_shape=jax.ShapeDtypeStruct((bsz, seq, d), F32),
        scratch_shapes=[pltpu.VMEM((tt, L0_COLS), F32), pltpu.VMEM((tt, GLA_V + RET_V), F32),
                        pltpu.VMEM((GLA_HEADS // 2, GLA_DV, LANES), F32),
                        pltpu.VMEM((RET_HEADS // 2, RET_DV, LANES), F32)],
        compiler_params=pltpu.CompilerParams(dimension_semantics=("arbitrary", "arbitrary"),
                                             vmem_limit_bytes=VMEM_LIMIT_BYTES),
        name="l0_mixer",
    )(x, pos_f, *params, *consts)


def _ffn_kernel(x_ref, wg_ref, wu_ref, wd_ref, lng_ref, lnb_ref, o_ref, xb_ref, acc_ref):
    j = pl.program_id(1)

    @pl.when(j == 0)
    def _():
        xb_ref[...] = x_ref[...].astype(BF16)
        acc_ref[...] = jnp.zeros_like(acc_ref)

    xb = xb_ref[...]
    gt = jnp.dot(xb, wg_ref[...], preferred_element_type=F32)
    up = jnp.dot(xb, wu_ref[...], preferred_element_type=F32)
    acc_ref[...] += _mm(_silu(gt) * up, wd_ref[...])

    @pl.when(j == pl.num_programs(1) - 1)
    def _():
        o_ref[...] = _layer_norm(DEEPNORM_ALPHA * x_ref[...] + acc_ref[...], lng_ref[...], lnb_ref[...])


def _ffn(x2d, w_gu, w_down, ln_g, ln_b, *, tm, tf):
    n, d = x2d.shape
    nf = D_FF // tf
    w_gu_b, w_down_b = w_gu.astype(BF16), w_down.astype(BF16)
    return pl.pallas_call(
        _ffn_kernel,
        grid=(n // tm, nf),
        in_specs=[pl.BlockSpec((tm, d), lambda i, j: (i, 0)),
                  pl.BlockSpec((d, tf), lambda i, j: (0, j)),
                  pl.BlockSpec((d, tf), lambda i, j: (0, j + nf)),
                  pl.BlockSpec((tf, d), lambda i, j: (j, 0)),
                  _const_spec((1, d)), _const_spec((1, d))],
        out_specs=pl.BlockSpec((tm, d), lambda i, j: (i, 0)),
        out_shape=jax.ShapeDtypeStruct((n, d), F32),
        scratch_shapes=[pltpu.VMEM((tm, d), BF16), pltpu.VMEM((tm, d), F32)],
        compiler_params=pltpu.CompilerParams(dimension_semantics=("arbitrary", "arbitrary"),
                                             vmem_limit_bytes=VMEM_LIMIT_BYTES),
        name="ffn",
    )(x2d, w_gu_b, w_gu_b, w_down_b, ln_g[None, :], ln_b[None, :])


L1_Q, L1_K, L1_V, L1_GATE, L1_AB = 0, 1024, 2048, 3072, 4096
L1_CONV = 3 * GDN_QK
L1_COLS = 4224
L1_BETA_LANE = GDN_HEADS
CONV_PAD = 8
ROUTE_LANES = LANES


def _pack_l1_w_in(w):
    offs = np.cumsum([0, L1_CONV, GDN_HEADS, GDN_HEADS, GDN_V])
    qkv, a_in, b_in, gate = [w[:, offs[i]:offs[i + 1]] for i in range(4)]
    ab = jnp.pad(jnp.concatenate([a_in, b_in], axis=1), ((0, 0), (0, LANES - 2 * GDN_HEADS)))
    return jnp.concatenate([qkv, gate, ab], axis=1)


def _l1_mixer_kernel(x_ref, w_in_ref, conv_ref, alog_ref, dtb_ref, cnorm_ref, w_out_ref, lng_ref, lnb_ref,
                     tril_ref, wr_hi_ref, wr_lo_ref, br_ref,
                     o_ref, route_ref, ext_ref, h2_ref, qkv_ref, mix_ref, s_ref, u_ref, w_ref, aqk_ref, *, tt):
    ti = pl.program_id(1)

    @pl.when(ti == 0)
    def _():
        s_ref[...] = jnp.zeros_like(s_ref)
        ext_ref[0:CONV_PAD, :] = jnp.zeros((CONV_PAD, L1_CONV), F32)

    x = x_ref[...]
    xb = x.astype(BF16)
    ext_ref[CONV_PAD:CONV_PAD + tt, :] = jnp.dot(xb, w_in_ref[:, 0:L1_CONV], preferred_element_type=F32)
    h2_ref[...] = jnp.dot(xb, w_in_ref[:, L1_CONV:L1_COLS], preferred_element_type=F32)

    conv = ext_ref[CONV_PAD:CONV_PAD + tt, :] * conv_ref[CONV_WIDTH - 1:CONV_WIDTH, :]
    for j in range(CONV_WIDTH - 1):
        back = CONV_WIDTH - 1 - j
        conv = conv + ext_ref[CONV_PAD - back:CONV_PAD - back + tt, :] * conv_ref[j:j + 1, :]
    ext_ref[0:CONV_PAD, :] = ext_ref[tt:tt + CONV_PAD, :]
    qkv_ref[...] = _silu(conv)

    for hd in range(GDN_HEADS):
        for base, scale in ((L1_Q, GDN_DK ** -0.5), (L1_K, 1.0)):
            sl = slice(base + hd * GDN_DK, base + (hd + 1) * GDN_DK)
            t = qkv_ref[:, sl]
            qkv_ref[:, sl] = t * (lax.rsqrt(jnp.sum(t * t, axis=-1, keepdims=True) + L2_EPS) * scale)

    ab = h2_ref[:, L1_AB - L1_CONV:L1_AB - L1_CONV + LANES]
    g_blk = -jnp.exp(alog_ref[...]) * _softplus(ab + dtb_ref[...])
    beta_blk = _sigmoid(ab)
    gc_blk = _mm_exact_lhs01(tril_ref[...], g_blk)
    eg_blk = jnp.exp(gc_blk)

    ci = lax.broadcasted_iota(jnp.int32, (CHUNK, CHUNK), 0)
    cj = lax.broadcasted_iota(jnp.int32, (CHUNK, CHUNK), 1)
    incl = cj <= ci
    strict = cj < ci

    heads = range(GDN_HEADS)
    col = lambda base, hd: slice(base + hd * GDN_DK, base + (hd + 1) * GDN_DK)
    n_chunks = tt // CHUNK

    for c in range(n_chunks):
        rows = slice(c * CHUNK, (c + 1) * CHUNK)
        gc_c = gc_blk[rows]
        gc_t = gc_c.T
        kdec_scale = jnp.exp(gc_c[CHUNK - 1:CHUNK, :] - gc_c)
        q_h = [qkv_ref[rows, col(L1_Q, hd)] for hd in heads]
        k_h = [qkv_ref[rows, col(L1_K, hd)] for hd in heads]
        v_h = [qkv_ref[rows, col(L1_V, hd)] for hd in heads]
        beta = [beta_blk[rows, L1_BETA_LANE + hd:L1_BETA_LANE + hd + 1] for hd in heads]
        eg = [eg_blk[rows, hd:hd + 1] for hd in heads]
        decay = [jnp.exp(jnp.where(incl, gc_c[:, hd:hd + 1] - gc_t[hd:hd + 1, :], NEG_BIG)) for hd in heads]
        kb = [k_h[hd] * beta[hd] for hd in heads]
        low = [jnp.where(strict, _mm_nt(kb[hd], k_h[hd]) * decay[hd], 0.0) for hd in heads]
        a_qk = [_mm_nt(q_h[hd], k_h[hd]) * decay[hd] for hd in heads]
        a_m = [-low[hd] for hd in heads]
        m = [_mm(low[hd], low[hd]) for hd in heads]
        for it in range(5):
            am = [_mm(a_m[hd], m[hd]) for hd in heads]
            a_m = [a_m[hd] + m[hd] + am[hd] for hd in heads]
            if it < 4:
                m = [_mm(m[hd], m[hd]) for hd in heads]
        rhs = [jnp.concatenate([v_h[hd] * beta[hd], kb[hd] * eg[hd]], axis=1) for hd in heads]
        uw = [rhs[hd] + _mm(a_m[hd], rhs[hd]) for hd in heads]
        for hd in heads:
            u_ref[rows, col(0, hd)] = uw[hd][:, 0:GDN_DV]
            w_ref[rows, col(0, hd)] = uw[hd][:, GDN_DV:GDN_DV + GDN_DK]
            aqk_ref[rows, hd * LANES:hd * LANES + CHUNK] = a_qk[hd]
            qkv_ref[rows, col(L1_Q, hd)] = q_h[hd] * eg[hd]
            qkv_ref[rows, col(L1_K, hd)] = k_h[hd] * kdec_scale[:, hd:hd + 1]

    s = [s_ref[hd] for hd in heads]
    for c in range(n_chunks):
        rows = slice(c * CHUNK, (c + 1) * CHUNK)
        e_last = jnp.exp(gc_blk[(c + 1) * CHUNK - 1:(c + 1) * CHUNK, :])
        ws_qs = [_mm(jnp.concatenate([w_ref[rows, col(0, hd)], qkv_ref[rows, col(L1_Q, hd)]], axis=0), s[hd])
                 for hd in heads]
        v_new = [u_ref[rows, col(0, hd)] - ws_qs[hd][0:CHUNK] for hd in heads]
        o_h = [ws_qs[hd][CHUNK:2 * CHUNK] + _mm(aqk_ref[rows, hd * LANES:hd * LANES + CHUNK], v_new[hd])
               for hd in heads]
        s = [s[hd] * e_last[:, hd:hd + 1] + _mm_tn(qkv_ref[rows, col(L1_K, hd)], v_new[hd]) for hd in heads]
        for hd in heads:
            mix_ref[rows, col(0, hd)] = o_h[hd]
    for hd in heads:
        s_ref[hd] = s[hd]

    for hd in range(GDN_HEADS):
        sl = slice(hd * GDN_DV, (hd + 1) * GDN_DV)
        o = mix_ref[:, sl]
        o = o * lax.rsqrt(jnp.mean(o * o, axis=-1, keepdims=True) + NORM_EPS) * cnorm_ref[:, sl]
        mix_ref[:, sl] = o * _silu(h2_ref[:, L1_GATE - L1_CONV + hd * GDN_DV:L1_GATE - L1_CONV + (hd + 1) * GDN_DV])
    y = _mm(mix_ref[...], w_out_ref[...])
    x1 = _layer_norm(DEEPNORM_ALPHA * x + y, lng_ref[...], lnb_ref[...])
    o_ref[...] = x1

    x_hi = x1.astype(BF16)
    x_lo = (x1 - x_hi.astype(F32)).astype(BF16)
    logits = (jnp.dot(x_hi, wr_hi_ref[...], preferred_element_type=F32)
              + jnp.dot(x_hi, wr_lo_ref[...], preferred_element_type=F32)
              + jnp.dot(x_lo, wr_hi_ref[...], preferred_element_type=F32)) + br_ref[...]
    lane = lax.broadcasted_iota(jnp.int32, (tt, ROUTE_LANES), 1)
    lane_f = lane.astype(F32)
    logits = jnp.where(lane < N_EXPERTS, logits, NEG_BIG)
    m1 = jnp.max(logits, axis=-1, keepdims=True)
    i1 = jnp.min(jnp.where(logits == m1, lane_f, float(ROUTE_LANES)), axis=-1, keepdims=True)
    rest = jnp.where(lane_f == i1, NEG_BIG, logits)
    m2 = jnp.max(rest, axis=-1, keepdims=True)
    i2 = jnp.min(jnp.where(rest == m2, lane_f, float(ROUTE_LANES)), axis=-1, keepdims=True)
    e21 = jnp.exp(m2 - m1)
    g1 = 1.0 / (1.0 + e21)
    g2 = e21 * g1
    route_ref[...] = jnp.where(lane == 0, i1, jnp.where(lane == 1, i2, jnp.where(lane == 2, g1, jnp.where(lane == 3, g2, 0.0))))


def _l1_mixer(x, w_in, conv_w, a_log, dt_bias, c_norm, w_out, ln_g, ln_b, w_router, b_router, *, tt):
    bsz, seq, d = x.shape
    w_in_p = _pack_l1_w_in(w_in).astype(BF16)
    lane_pad = lambda v: jnp.pad(v[None, :], ((0, 0), (0, LANES - v.shape[0])))
    wr = jnp.pad(w_router, ((0, 0), (0, ROUTE_LANES - N_EXPERTS)))
    wr_hi = wr.astype(BF16)
    wr_lo = (wr - wr_hi.astype(F32)).astype(BF16)
    params = [w_in_p, conv_w, lane_pad(a_log), lane_pad(dt_bias), c_norm.reshape(1, GDN_V), w_out.astype(BF16),
              ln_g[None, :], ln_b[None, :], jnp.asarray(_chunk_tril_np(tt), BF16), wr_hi, wr_lo, lane_pad(b_router)]
    tile = lambda w: pl.BlockSpec((None, tt, w), lambda b, t: (b, t, 0))
    return pl.pallas_call(
        functools.partial(_l1_mixer_kernel, tt=tt),
        grid=(bsz, seq // tt),
        in_specs=[tile(d)] + [_const_spec(a.shape) for a in params],
        out_specs=[tile(d), tile(ROUTE_LANES)],
        out_shape=[jax.ShapeDtypeStruct((bsz, seq, d), F32), jax.ShapeDtypeStruct((bsz, seq, ROUTE_LANES), F32)],
        scratch_shapes=[pltpu.VMEM((tt + CONV_PAD, L1_CONV), F32), pltpu.VMEM((tt, L1_COLS - L1_CONV), F32),
                        pltpu.VMEM((tt, L1_CONV), F32), pltpu.VMEM((tt, GDN_V), F32),
                        pltpu.VMEM((GDN_HEADS, GDN_DK, GDN_DV), F32),
                        pltpu.VMEM((tt, GDN_V), F32), pltpu.VMEM((tt, GDN_QK), F32),
                        pltpu.VMEM((tt, GDN_HEADS * LANES), F32)],
        compiler_params=pltpu.CompilerParams(dimension_semantics=("arbitrary", "arbitrary"),
                                             vmem_limit_bytes=VMEM_LIMIT_BYTES),
        name="l1_mixer",
    )(x, *params)


ID_ROWS = 4


def _rank_kernel(route_ref, stril_ref, ids_ref, counts_ref, carry_ref):
    @pl.when(pl.program_id(0) == 0)
    def _():
        carry_ref[...] = jnp.zeros_like(carry_ref)

    r = route_ref[...]
    tr = r.shape[0]
    e1, e2 = r[:, 0:1], r[:, 1:2]
    lane = lax.broadcasted_iota(jnp.int32, (tr, ROUTE_LANES), 1)
    lane_f = lane.astype(F32)
    oh1 = (lane_f == e1).astype(F32)
    oh2 = (lane_f == e2).astype(F32)
    both = oh1 + oh2
    before = jnp.dot(stril_ref[...], both.astype(BF16), preferred_element_type=F32) + carry_ref[...]
    rank1 = jnp.sum(oh1 * before, axis=-1, keepdims=True)
    rank2 = jnp.sum(oh2 * before, axis=-1, keepdims=True)
    table = jnp.where(lane == 0, e1, jnp.where(lane == 1, e2, jnp.where(lane == 2, rank1,
                      jnp.where(lane == 3, rank2, 0.0))))
    ids_ref[...] = table.T[0:ID_ROWS, :].astype(jnp.int32)
    carry_ref[...] += jnp.sum(both, axis=0, keepdims=True)
    counts_ref[...] = carry_ref[...]


def _rank(route2d, *, tr):
    n = route2d.shape[0]
    i = np.arange(tr)
    stril = jnp.asarray((i[None, :] < i[:, None]).astype(np.float32), BF16)
    return pl.pallas_call(
        _rank_kernel,
        grid=(n // tr,),
        in_specs=[pl.BlockSpec((tr, ROUTE_LANES), lambda i: (i, 0)), _const_spec((tr, tr))],
        out_specs=[pl.BlockSpec((ID_ROWS, tr), lambda i: (0, i)), _const_spec((1, ROUTE_LANES))],
        out_shape=[jax.ShapeDtypeStruct((ID_ROWS, n), jnp.int32), jax.ShapeDtypeStruct((1, ROUTE_LANES), F32)],
        scratch_shapes=[pltpu.VMEM((1, ROUTE_LANES), F32)],
        compiler_params=pltpu.CompilerParams(dimension_semantics=("arbitrary",)),
        name="moe_rank",
    )(route2d, stril)


ROW_SUB = D_MODEL // LANES


def _to_row_tiles(dst_ref, src, n):
    for lb in range(ROW_SUB):
        dst_ref[pl.ds(lb, n, stride=ROW_SUB), :] = src[:, lb * LANES:(lb + 1) * LANES]


def _from_row_tiles(src_ref, n):
    return [src_ref[pl.ds(lb, n, stride=ROW_SUB), :] for lb in range(ROW_SUB)]


def _tile_copy(src_ref, src_sub, dst_ref, dst_sub, sem):
    src = src_ref.at[pl.ds(pl.multiple_of(src_sub, ROW_SUB), ROW_SUB), :]
    dst = dst_ref.at[pl.ds(pl.multiple_of(dst_sub, ROW_SUB), ROW_SUB), :]
    return pltpu.make_async_copy(src, dst, sem)


def _wait_rows(hbm_ref, buf_ref, sem, n, copies):
    for _ in range(copies):
        pltpu.make_async_copy(hbm_ref.at[pl.ds(0, n * ROW_SUB), :], buf_ref, sem).wait()


def _dispatch_kernel(p0_ref, p1_ref, x_ref, init_hbm, xs_hbm, xt0_ref, xt1_ref, sems, *, td):
    del init_hbm
    p_refs = (p0_ref, p1_ref)
    i = pl.program_id(0)
    last = pl.num_programs(0) - 1

    def step(xt_ref, sem):
        @pl.when(i >= 2)
        def _():
            _wait_rows(xs_hbm, xt_ref, sem, td, TOP_K)

        _to_row_tiles(xt_ref, x_ref[...], td)

        def issue(t, carry):
            for k in range(TOP_K):
                _tile_copy(xt_ref, t * ROW_SUB, xs_hbm, p_refs[k][t], sem).start(priority=k)
            return carry

        lax.fori_loop(0, td, issue, 0, unroll=8)

    for slot, (xt_ref, sem) in enumerate(((xt0_ref, sems.at[0]), (xt1_ref, sems.at[1]))):
        @pl.when(i % 2 == slot)
        def _():
            step(xt_ref, sem)

    @pl.when(i == last)
    def _():
        _wait_rows(xs_hbm, xt0_ref, sems.at[0], td, TOP_K)
        _wait_rows(xs_hbm, xt1_ref, sems.at[1], td, TOP_K)


def _dispatch(pos, x2d, n_rows, *, td):
    n, d = x2d.shape
    init = jnp.zeros((n_rows * ROW_SUB, LANES), F32)
    smem_rows = pl.BlockSpec((td,), lambda i: (i,), memory_space=pltpu.SMEM)
    assert n // td >= 2
    return pl.pallas_call(
        functools.partial(_dispatch_kernel, td=td),
        grid=(n // td,),
        in_specs=[smem_rows, smem_rows, pl.BlockSpec((td, d), lambda i: (i, 0)), pl.BlockSpec(memory_space=pl.ANY)],
        out_specs=pl.BlockSpec(memory_space=pl.ANY),
        scratch_shapes=[pltpu.VMEM((td * ROW_SUB, LANES), F32), pltpu.VMEM((td * ROW_SUB, LANES), F32),
                        pltpu.SemaphoreType.DMA((2,))],
        out_shape=jax.ShapeDtypeStruct((n_rows * ROW_SUB, LANES), F32),
        input_output_aliases={3: 0},
        compiler_params=pltpu.CompilerParams(dimension_semantics=("arbitrary",)),
        name="moe_dispatch",
    )(pos[0], pos[1], x2d, init)


WEIGHT_SLOTS = 2


def _gmm_kernel(te_ref, used_ref, first_ref, x_ref, wgu_hbm, wd_hbm, o_ref,
                xb_ref, acc_ref, wgb_ref, wub_ref, wdb_ref, sg_ref, su_ref, sd_ref, sems, *, tm, tf):
    i = pl.program_id(0)
    n_tiles = pl.num_programs(0)
    nf = D_FF // tf
    active = i < used_ref[0]
    is_first = first_ref[i] == 1

    def chunk_copies(e, j, slot):
        cols = pl.ds(j * tf, tf)
        return (pltpu.make_async_copy(wgu_hbm.at[e, :, cols], sg_ref.at[slot], sems.at[slot]),
                pltpu.make_async_copy(wgu_hbm.at[e, :, pl.ds(D_FF + j * tf, tf)], su_ref.at[slot], sems.at[slot]),
                pltpu.make_async_copy(wd_hbm.at[e, cols, :], sd_ref.at[slot], sems.at[slot]))

    def fetch(e, j):
        for c in chunk_copies(e, j, j % WEIGHT_SLOTS):
            c.start()

    def chunk(j):
        xb = xb_ref[...]
        gt = jnp.dot(xb, wgb_ref[j], preferred_element_type=F32)
        up = jnp.dot(xb, wub_ref[j], preferred_element_type=F32)
        part = jnp.dot((_silu(gt) * up).astype(BF16), wdb_ref[j], preferred_element_type=F32)
        if j == 0:
            acc_ref[...] = part
        else:
            acc_ref[...] += part

    @pl.when(active)
    def _():
        for lb, blk in enumerate(_from_row_tiles(x_ref, tm)):
            xb_ref[:, lb * LANES:(lb + 1) * LANES] = blk.astype(BF16)

    @pl.when(jnp.logical_and(active, is_first))
    def _():
        e = te_ref[i]

        @pl.when(i == 0)
        def _():
            for j in range(WEIGHT_SLOTS):
                fetch(e, j)

        for j in range(nf):
            slot = j % WEIGHT_SLOTS
            for c in chunk_copies(e, j, slot):
                c.wait()
            wgb_ref[j] = sg_ref[slot].astype(BF16)
            wub_ref[j] = su_ref[slot].astype(BF16)
            wdb_ref[j] = sd_ref[slot].astype(BF16)
            if j + WEIGHT_SLOTS < nf:
                fetch(e, j + WEIGHT_SLOTS)
            chunk(j)

    @pl.when(jnp.logical_and(active, jnp.logical_not(is_first)))
    def _():
        for j in range(nf):
            chunk(j)

    @pl.when(active)
    def _():
        _to_row_tiles(o_ref, acc_ref[...], tm)
        nxt = jnp.minimum(i + 1, n_tiles - 1)

        @pl.when(jnp.logical_and(i + 1 < n_tiles, first_ref[nxt] == 1))
        def _():
            for j in range(WEIGHT_SLOTS):
                fetch(te_ref[nxt], j)

    @pl.when(jnp.logical_not(active))
    def _():
        o_ref[...] = jnp.zeros_like(o_ref)


def _gmm(tile_expert, n_used, tile_first, xs, w_gu, w_down, *, tm, tf):
    d = D_MODEL
    n_rows = xs.shape[0] // ROW_SUB
    nf = D_FF // tf
    n_tiles = n_rows // tm
    assert nf >= WEIGHT_SLOTS
    grid_spec = pltpu.PrefetchScalarGridSpec(
        num_scalar_prefetch=3,
        grid=(n_tiles,),
        in_specs=[pl.BlockSpec((tm * ROW_SUB, LANES), lambda i, te, used, first: (jnp.minimum(i, used[0] - 1), 0)),
                  pl.BlockSpec(memory_space=pl.ANY), pl.BlockSpec(memory_space=pl.ANY)],
        out_specs=pl.BlockSpec((tm * ROW_SUB, LANES), lambda i, te, used, first: (i, 0)),
        scratch_shapes=[pltpu.VMEM((tm, d), BF16), pltpu.VMEM((tm, d), F32),
                        pltpu.VMEM((nf, d, tf), BF16), pltpu.VMEM((nf, d, tf), BF16),
                        pltpu.VMEM((nf, tf, d), BF16),
                        pltpu.VMEM((WEIGHT_SLOTS, d, tf), F32), pltpu.VMEM((WEIGHT_SLOTS, d, tf), F32),
                        pltpu.VMEM((WEIGHT_SLOTS, tf, d), F32),
                        pltpu.SemaphoreType.DMA((WEIGHT_SLOTS,))],
    )
    return pl.pallas_call(
        functools.partial(_gmm_kernel, tm=tm, tf=tf),
        grid_spec=grid_spec,
        out_shape=jax.ShapeDtypeStruct((n_rows * ROW_SUB, LANES), F32),
        compiler_params=pltpu.CompilerParams(dimension_semantics=("arbitrary",),
                                             vmem_limit_bytes=VMEM_LIMIT_BYTES),
        name="moe_gmm",
    )(tile_expert, n_used, tile_first, xs, w_gu, w_down)


def _combine_kernel(p0_ref, p1_ref, p0_next_ref, p1_next_ref, x_ref, route_ref, lng_ref, lnb_ref, y_hbm, o_ref,
                    ya0_ref, yb0_ref, ya1_ref, yb1_ref, sems, *, tc):
    i = pl.program_id(0)
    n = pl.num_programs(0)
    slots = ((ya0_ref, yb0_ref, sems.at[0]), (ya1_ref, yb1_ref, sems.at[1]))

    def issue(p_refs, slot):
        ya_ref, yb_ref, sem = slots[slot]

        def body(t, carry):
            for k, (p_ref, buf) in enumerate(zip(p_refs, (ya_ref, yb_ref))):
                _tile_copy(y_hbm, p_ref[t], buf, t * ROW_SUB, sem).start(priority=k)
            return carry

        lax.fori_loop(0, tc, body, 0, unroll=8)

    @pl.when(i == 0)
    def _():
        issue((p0_ref, p1_ref), 0)

    for slot in range(2):
        @pl.when(jnp.logical_and(i + 1 < n, (i + 1) % 2 == slot))
        def _():
            issue((p0_next_ref, p1_next_ref), slot)

    for slot in range(2):
        @pl.when(i % 2 == slot)
        def _():
            ya_ref, yb_ref, sem = slots[slot]
            _wait_rows(y_hbm, ya_ref, sem, tc, TOP_K)
            r = route_ref[...]
            g1, g2 = r[:, 2:3], r[:, 3:4]
            y = jnp.concatenate(
                [g1 * a + g2 * b for a, b in zip(_from_row_tiles(ya_ref, tc), _from_row_tiles(yb_ref, tc))], axis=1)
            o_ref[...] = _layer_norm(DEEPNORM_ALPHA * x_ref[...] + y, lng_ref[...], lnb_ref[...])


def _combine(pos, x2d, route2d, ys, ln_g, ln_b, *, tc):
    n, d = x2d.shape
    steps = n // tc
    smem_rows = pl.BlockSpec((tc,), lambda i: (i,), memory_space=pltpu.SMEM)
    smem_next = pl.BlockSpec((tc,), lambda i: (jnp.minimum(i + 1, steps - 1),), memory_space=pltpu.SMEM)
    return pl.pallas_call(
        functools.partial(_combine_kernel, tc=tc),
        grid=(steps,),
        in_specs=[smem_rows, smem_rows, smem_next, smem_next,
                  pl.BlockSpec((tc, d), lambda i: (i, 0)),
                  pl.BlockSpec((tc, ROUTE_LANES), lambda i: (i, 0)),
                  _const_spec((1, d)), _const_spec((1, d)),
                  pl.BlockSpec(memory_space=pl.ANY)],
        out_specs=pl.BlockSpec((tc, d), lambda i: (i, 0)),
        scratch_shapes=[pltpu.VMEM((tc * ROW_SUB, LANES), F32) for _ in range(2 * TOP_K)]
                       + [pltpu.SemaphoreType.DMA((2,))],
        out_shape=jax.ShapeDtypeStruct((n, d), F32),
        compiler_params=pltpu.CompilerParams(dimension_semantics=("arbitrary",),
                                             vmem_limit_bytes=VMEM_LIMIT_BYTES),
        name="moe_combine",
    )(pos[0], pos[1], pos[0], pos[1], x2d, route2d, ln_g[None, :], ln_b[None, :], ys)


def _moe(x2d, route2d, w_gu, w_down, ln_g, ln_b, *, tm, tf, tr, td, tc):
    n, d = x2d.shape
    ids, counts = _rank(route2d, tr=tr)
    cnt = counts[0, :N_EXPERTS].astype(jnp.int32)
    tiles_per = (cnt + tm - 1) // tm
    tile_end = jnp.cumsum(tiles_per)
    offs = (tile_end - tiles_per) * tm
    n_tiles = (n * TOP_K) // tm + N_EXPERTS
    n_used = tile_end[-1:]
    t_idx = jnp.arange(n_tiles, dtype=jnp.int32)
    tile_expert = jnp.sum(jnp.minimum(t_idx, n_used - 1)[:, None] >= tile_end[None, :], axis=1).astype(jnp.int32)
    tile_start = tile_end - tiles_per
    tile_first = jnp.logical_and(t_idx == tile_start[tile_expert], t_idx < n_used).astype(jnp.int32)
    group_start = sum(jnp.where(ids[0:TOP_K] == e, offs[e], 0) for e in range(N_EXPERTS))
    pos = (group_start + ids[TOP_K:2 * TOP_K]) * ROW_SUB
    xs = _dispatch(pos, x2d, n_tiles * tm, td=td)
    ys = _gmm(tile_expert, n_used, tile_first, xs, w_gu, w_down, tm=tm, tf=tf)
    return _combine(pos, x2d, route2d, ys, ln_g, ln_b, tc=tc)


def kernel(x, positions, ab_w_in, gla_w_gate2, gla_b_gate, gla_norm, ret_norm, ab_w_out, ab_ln1_g, ab_ln1_b, ffn_w_gu, ffn_w_down, ab_ln2_g, ab_ln2_b, c_w_in, c_conv_w, c_a_log, c_dt_bias, c_norm, c_w_out, c_ln1_g, c_ln1_b, moe_w_router, moe_b_router, moe_w_gu, moe_w_down, c_ln2_g, c_ln2_b):
    bsz, seq, d = x.shape
    for layer in range(DEPTH):
        i = layer // 2
        if layer % 2 == 0:
            x = _l0_mixer(x, positions, ab_w_in[i], gla_w_gate2[i], gla_b_gate[i], gla_norm[i], ret_norm[i],
                          ab_w_out[i], ab_ln1_g[i], ab_ln1_b[i], tt=L0_TIME_TILE)
            x = _ffn(x.reshape(bsz * seq, d), ffn_w_gu[i], ffn_w_down[i], ab_ln2_g[i], ab_ln2_b[i],
                     tm=FFN_ROW_TILE, tf=FF_TILE).reshape(bsz, seq, d)
        else:
            x, route = _l1_mixer(x, c_w_in[i], c_conv_w[i], c_a_log[i], c_dt_bias[i], c_norm[i], c_w_out[i],
                                 c_ln1_g[i], c_ln1_b[i], moe_w_router[i], moe_b_router[i], tt=L1_TIME_TILE)
            x = _moe(x.reshape(bsz * seq, d), route.reshape(bsz * seq, ROUTE_LANES), moe_w_gu[i], moe_w_down[i],
                     c_ln2_g[i], c_ln2_b[i], tm=MOE_ROW_TILE, tf=FF_TILE, tr=ROUTE_TILE, td=ROUTE_TILE,
                     tc=ROUTE_TILE).reshape(bsz, seq, d)
    return x
```

```python
import functools

import numpy as np
import jax
import jax.numpy as jnp
from jax import lax
from jax.experimental import pallas as pl
from jax.experimental.pallas import tpu as pltpu

F32 = jnp.float32
BF16 = jnp.bfloat16

D_MODEL = 1024
DEPTH = 2
CHUNK = 64
GLA_HEADS, GLA_DK, GLA_DV, GLA_GATE_RANK, GLA_TAU = 4, 64, 128, 16, 16.0
RET_HEADS, RET_DK, RET_DV = 4, 64, 128
ROPE_BASE = 10000.0
GDN_HEADS, GDN_DK, GDN_DV = 8, 128, 128
CONV_WIDTH = 4
D_FF = 3584
N_EXPERTS = 8
TOP_K = 2
NORM_EPS = 1e-5
L2_EPS = 1e-6
DEEPNORM_ALPHA = (2.0 * DEPTH) ** 0.25

GLA_QK = GLA_HEADS * GLA_DK
GLA_V = GLA_HEADS * GLA_DV
RET_QK = RET_HEADS * RET_DK
RET_V = RET_HEADS * RET_DV
GDN_QK = GDN_HEADS * GDN_DK
GDN_V = GDN_HEADS * GDN_DV

LANES = 128
VMEM_LIMIT_BYTES = 56 * 1024 * 1024
NEG_BIG = -1e30

L0_TIME_TILE = 256
L1_TIME_TILE = 256
FFN_ROW_TILE = 512
MOE_ROW_TILE = 512
FF_TILE = 512
ROUTE_TILE = 512


def _mm(a, b):
    return jnp.dot(a.astype(BF16), b.astype(BF16), preferred_element_type=F32)


def _mm_nt(a, b):
    return lax.dot_general(a.astype(BF16), b.astype(BF16), (((1,), (1,)), ((), ())),
                           preferred_element_type=F32)


def _mm_tn(a, b):
    return lax.dot_general(a.astype(BF16), b.astype(BF16), (((0,), (0,)), ((), ())),
                           preferred_element_type=F32)


def _split3(x):
    hi = x.astype(BF16)
    r1 = x - hi.astype(F32)
    mid = r1.astype(BF16)
    lo = (r1 - mid.astype(F32)).astype(BF16)
    return hi, mid, lo


def _mm_exact_lhs01(m01, x):
    hi, mid, lo = _split3(x)
    return (jnp.dot(m01, hi, preferred_element_type=F32)
            + jnp.dot(m01, mid, preferred_element_type=F32)
            + jnp.dot(m01, lo, preferred_element_type=F32))


def _sigmoid(x):
    return 1.0 / (1.0 + jnp.exp(-x))


def _silu(x):
    return x * _sigmoid(x)


def _softplus(x):
    return jnp.maximum(x, 0.0) + jnp.log(1.0 + jnp.exp(-jnp.abs(x)))


def _layer_norm(x, g, b):
    mu = jnp.mean(x, axis=-1, keepdims=True)
    xc = x - mu
    var = jnp.mean(xc * xc, axis=-1, keepdims=True)
    return xc * lax.rsqrt(var + NORM_EPS) * g + b


def _chunk_tril_np(tt):
    i = np.arange(tt)
    same = (i[:, None] // CHUNK) == (i[None, :] // CHUNK)
    return (same & (i[None, :] <= i[:, None])).astype(np.float32)


def _const_spec(shape):
    nd = len(shape)
    return pl.BlockSpec(shape, lambda *_: (0,) * nd)


L0_GQ, L0_GK, L0_GV, L0_GR = 0, 256, 512, 1024
L0_RQ, L0_RK, L0_RV, L0_RG = 1536, 1792, 2048, 2560
L0_GA = 3072
L0_COLS = 3200
L0_CHUNK_GROUP = 2


def _pack_l0_w_in(w):
    offs = np.cumsum([0, GLA_QK, GLA_QK, GLA_V, GLA_GATE_RANK, GLA_V, RET_QK, RET_QK, RET_V, RET_V])
    gq, gk, gv, ga, gr, rq, rk, rv, rg = [w[:, offs[i]:offs[i + 1]] for i in range(9)]
    ga = jnp.pad(ga, ((0, 0), (0, LANES - GLA_GATE_RANK)))
    return jnp.concatenate([gq, gk, gv, gr, rq, rk, rv, rg, ga], axis=1)


def _ret_tables():
    h = np.arange(RET_HEADS, dtype=np.float64)
    log_gamma = np.log(1.0 - 2.0 ** (-5.0 - h))
    pos = np.arange(CHUNK, dtype=np.float64)
    diff = pos[:, None] - pos[None, :]
    dmat = np.where(diff >= 0, np.exp(log_gamma[:, None, None] * np.maximum(diff, 0.0)), 0.0)
    xi = np.exp(log_gamma[None, :] * (pos[:, None] + 1.0))
    zeta = np.exp(log_gamma[None, :] * (CHUNK - 1.0 - pos[:, None]))
    decay = np.exp(log_gamma * CHUNK)
    xi_full = np.repeat(xi, RET_DV, axis=1)
    zeta_full = np.repeat(zeta, RET_DK, axis=1)
    decay_full = np.repeat(decay, RET_DK)[None, :]
    return (dmat.astype(np.float32), xi_full.astype(np.float32), zeta_full.astype(np.float32),
            decay_full.astype(np.float32))


def _rope_tables():
    half = RET_DK // 2
    inv_freq = ROPE_BASE ** (-np.arange(0, RET_DK, 2, dtype=np.float32) / RET_DK)
    per_head = np.concatenate([inv_freq, inv_freq])
    freq_full = np.tile(per_head, RET_HEADS)[None, :].astype(np.float32)
    sign = np.tile(np.concatenate([-np.ones(half), np.ones(half)]), RET_HEADS)[None, :].astype(np.float32)
    return freq_full, sign


def _l0_mixer_kernel(x_ref, pos_ref, w_in_ref, wg2_ref, bg_ref, gnorm_ref, rnorm_ref, w_out_ref,
                     lng_ref, lnb_ref, tril_ref, dmat_ref, xi_ref, zeta_ref, rdecay_ref,
                     freq_ref, sign_ref, o_ref, h_ref, mix_ref, sg_ref, sr_ref, *, tt):
    ti = pl.program_id(1)

    @pl.when(ti == 0)
    def _():
        sg_ref[...] = jnp.zeros_like(sg_ref)
        sr_ref[...] = jnp.zeros_like(sr_ref)

    x = x_ref[...]
    h_ref[...] = _mm(x, w_in_ref[...])

    lane = lax.broadcasted_iota(jnp.int32, (1, LANES), 1)
    lo_half = lane < GLA_DK
    ci = lax.broadcasted_iota(jnp.int32, (CHUNK, CHUNK), 0)
    cj = lax.broadcasted_iota(jnp.int32, (CHUNK, CHUNK), 1)
    causal = cj <= ci

    z = _mm(h_ref[:, L0_GA:L0_GA + LANES], wg2_ref[...]) + bg_ref[...]
    log_a = -_softplus(-z) * (1.0 / GLA_TAU)
    b = _mm_exact_lhs01(tril_ref[...], log_a)
    eb = jnp.exp(b)
    q_dec = h_ref[:, L0_GQ:L0_GQ + GLA_QK] * (GLA_DK ** -0.5) * eb
    k_all = h_ref[:, L0_GK:L0_GK + GLA_QK]
    k_neg = k_all * jnp.exp(-b)

    ang = pos_ref[...] * freq_ref[:, 0:LANES]
    cos = jnp.concatenate([jnp.cos(ang)] * (RET_QK // LANES), axis=1)
    sin = jnp.concatenate([jnp.sin(ang)] * (RET_QK // LANES), axis=1) * sign_ref[...]
    half = RET_DK // 2
    lane256 = lax.broadcasted_iota(jnp.int32, (1, RET_QK), 1)
    first_half = (lane256 & (RET_DK - 1)) < half

    def rope(t):
        swapped = jnp.where(first_half, pltpu.roll(t, RET_QK - half, 1), pltpu.roll(t, half, 1))
        return t * cos + swapped * sin

    rq = rope(h_ref[:, L0_RQ:L0_RQ + RET_QK]) * (RET_DK ** -0.5)
    rk = rope(h_ref[:, L0_RK:L0_RK + RET_QK])

    n_chunks = tt // CHUNK
    heads = range(GLA_HEADS)
    pairs = range(GLA_HEADS // 2)
    pair_lanes = lambda hd: slice((hd // 2) * LANES, (hd // 2 + 1) * LANES)
    head_mask = lambda hd: lo_half if hd % 2 == 0 else jnp.logical_not(lo_half)
    vcol = lambda base, hd: slice(base + hd * GLA_DV, base + (hd + 1) * GLA_DV)
    sg = [sg_ref[p] for p in pairs]
    sr = [sr_ref[p] for p in pairs]
    for c0 in range(0, n_chunks, L0_CHUNK_GROUP):
        group = range(c0, c0 + L0_CHUNK_GROUP)
        rows = {c: slice(c * CHUNK, (c + 1) * CHUNK) for c in group}
        b_last = {c: b[(c + 1) * CHUNK - 1:(c + 1) * CHUNK, :] for c in group}
        k_end = {c: k_all[rows[c]] * jnp.exp(b_last[c] - b[rows[c]]) for c in group}
        rk_c = {c: rk[rows[c]] for c in group}
        rk_z = {c: rk_c[c] * zeta_ref[...] for c in group}
        inst = [(c, hd) for c in group for hd in heads]
        g_q = {ch: jnp.where(head_mask(ch[1]), q_dec[rows[ch[0]], pair_lanes(ch[1])], 0.0) for ch in inst}
        r_q = {ch: jnp.where(head_mask(ch[1]), rq[rows[ch[0]], pair_lanes(ch[1])], 0.0) for ch in inst}
        g_v = {ch: h_ref[rows[ch[0]], vcol(L0_GV, ch[1])] for ch in inst}
        r_v = {ch: h_ref[rows[ch[0]], vcol(L0_RV, ch[1])] for ch in inst}
        g_att = {(c, hd): jnp.where(causal, _mm_nt(g_q[c, hd], k_neg[rows[c], pair_lanes(hd)]), 0.0)
                 for c, hd in inst}
        r_att = {(c, hd): _mm_nt(r_q[c, hd], rk_c[c][:, pair_lanes(hd)]) * dmat_ref[hd] for c, hd in inst}
        g_d = {(c, hd): _mm_tn(g_v[c, hd], k_end[c][:, pair_lanes(hd)]) for c, hd in inst}
        r_d = {(c, hd): _mm_tn(r_v[c, hd], rk_z[c][:, pair_lanes(hd)]) for c, hd in inst}
        sg_in, sr_in = {}, {}
        for c in group:
            sg_in[c], sr_in[c] = sg, sr
            dec_c = jnp.exp(b_last[c])
            sg = [sg[p] * dec_c[:, p * LANES:(p + 1) * LANES] + jnp.where(lo_half, g_d[c, 2 * p], g_d[c, 2 * p + 1])
                  for p in pairs]
            sr = [sr[p] * rdecay_ref[:, p * LANES:(p + 1) * LANES]
                  + jnp.where(lo_half, r_d[c, 2 * p], r_d[c, 2 * p + 1]) for p in pairs]
        g_o = {(c, hd): _mm(g_att[c, hd], g_v[c, hd]) + _mm_nt(g_q[c, hd], sg_in[c][hd // 2]) for c, hd in inst}
        r_o = {(c, hd): _mm(r_att[c, hd], r_v[c, hd])
               + _mm_nt(r_q[c, hd], sr_in[c][hd // 2]) * xi_ref[:, vcol(0, hd)] for c, hd in inst}
        for c, hd in inst:
            mix_ref[rows[c], vcol(0, hd)] = g_o[c, hd]
            mix_ref[rows[c], vcol(GLA_V, hd)] = r_o[c, hd]
    for p in pairs:
        sg_ref[p] = sg[p]
        sr_ref[p] = sr[p]

    for hd in range(GLA_HEADS):
        sl = slice(hd * GLA_DV, (hd + 1) * GLA_DV)
        o = mix_ref[:, sl]
        o = o * lax.rsqrt(jnp.mean(o * o, axis=-1, keepdims=True) + NORM_EPS) * gnorm_ref[:, sl]
        mix_ref[:, sl] = o * _silu(h_ref[:, L0_GR + hd * GLA_DV:L0_GR + (hd + 1) * GLA_DV])
    for hd in range(RET_HEADS):
        sl = slice(hd * RET_DV, (hd + 1) * RET_DV)
        o = mix_ref[:, GLA_V + hd * RET_DV:GLA_V + (hd + 1) * RET_DV]
        oc = o - jnp.mean(o, axis=-1, keepdims=True)
        o = oc * lax.rsqrt(jnp.mean(oc * oc, axis=-1, keepdims=True) + NORM_EPS) * rnorm_ref[:, sl]
        mix_ref[:, GLA_V + hd * RET_DV:GLA_V + (hd + 1) * RET_DV] = (
            o * _silu(h_ref[:, L0_RG + hd * RET_DV:L0_RG + (hd + 1) * RET_DV]))

    y = _mm(mix_ref[...], w_out_ref[...])
    o_ref[...] = _layer_norm(DEEPNORM_ALPHA * x + y, lng_ref[...], lnb_ref[...])


def _l0_mixer(x, positions, w_in, w_gate2, b_gate, gla_norm, ret_norm, w_out, ln_g, ln_b, *, tt):
    bsz, seq, d = x.shape
    w_in_p = _pack_l0_w_in(w_in).astype(BF16)
    wg2 = jnp.pad(w_gate2, ((0, LANES - GLA_GATE_RANK), (0, 0))).astype(BF16)
    dmat, xi_full, zeta_full, rdecay = _ret_tables()
    freq_full, sign = _rope_tables()
    pos_f = positions.astype(F32)[..., None]
    consts = [jnp.asarray(_chunk_tril_np(tt), BF16), jnp.asarray(dmat), jnp.asarray(xi_full),
              jnp.asarray(zeta_full), jnp.asarray(rdecay), jnp.asarray(freq_full), jnp.asarray(sign)]
    params = [w_in_p, wg2, b_gate[None, :], gla_norm.reshape(1, GLA_V), ret_norm.reshape(1, RET_V),
              w_out.astype(BF16), ln_g[None, :], ln_b[None, :]]
    tile = lambda w: pl.BlockSpec((None, tt, w), lambda b, t: (b, t, 0))
    return pl.pallas_call(
        functools.partial(_l0_mixer_kernel, tt=tt),
        grid=(bsz, seq // tt),
        in_specs=[tile(d), tile(1)] + [_const_spec(a.shape) for a in params + consts],
        out_specs=tile(d),
        out_shape=jax.ShapeDtypeStruct((bsz, seq, d), F32),
        scratch_shapes=[pltpu.VMEM((tt, L0_COLS), F32), pltpu.VMEM((tt, GLA_V + RET_V), F32),
                        pltpu.VMEM((GLA_HEADS // 2, GLA_DV, LANES), F32),
                        pltpu.VMEM((RET_HEADS // 2, RET_DV, LANES), F32)],
        compiler_params=pltpu.CompilerParams(dimension_semantics=("arbitrary", "arbitrary"),
                                             vmem_limit_bytes=VMEM_LIMIT_BYTES),
        name="l0_mixer",
    )(x, pos_f, *params, *consts)


WEIGHT_SLOTS = 2


class _SwigluWeights:
    def __init__(self, wgu_hbm, wd_hbm, resident, staging, sems, tf):
        self.wgu_hbm, self.wd_hbm = wgu_hbm, wd_hbm
        self.wgb, self.wub, self.wdb = resident
        self.sg, self.su, self.sd = staging
        self.sems, self.tf, self.nf = sems, tf, D_FF // tf

    def _copies(self, e, j):
        slot = j % WEIGHT_SLOTS
        cols = pl.ds(j * self.tf, self.tf)
        up_cols = pl.ds(D_FF + j * self.tf, self.tf)
        return (pltpu.make_async_copy(self.wgu_hbm.at[e, :, cols], self.sg.at[slot], self.sems.at[slot]),
                pltpu.make_async_copy(self.wgu_hbm.at[e, :, up_cols], self.su.at[slot], self.sems.at[slot]),
                pltpu.make_async_copy(self.wd_hbm.at[e, cols, :], self.sd.at[slot], self.sems.at[slot]))

    def request(self, e, j):
        for c in self._copies(e, j):
            c.start()

    def request_head(self, e):
        for j in range(WEIGHT_SLOTS):
            self.request(e, j)

    def land(self, e, j):
        slot = j % WEIGHT_SLOTS
        for c in self._copies(e, j):
            c.wait()
        self.wgb[j] = self.sg[slot].astype(BF16)
        self.wub[j] = self.su[slot].astype(BF16)
        self.wdb[j] = self.sd[slot].astype(BF16)
        if j + WEIGHT_SLOTS < self.nf:
            self.request(e, j + WEIGHT_SLOTS)

    def apply_chunk(self, xb_ref, acc_ref, j):
        xb = xb_ref[...]
        gt = jnp.dot(xb, self.wgb[j], preferred_element_type=F32)
        up = jnp.dot(xb, self.wub[j], preferred_element_type=F32)
        part = jnp.dot((_silu(gt) * up).astype(BF16), self.wdb[j], preferred_element_type=F32)
        if j == 0:
            acc_ref[...] = part
        else:
            acc_ref[...] += part

    @staticmethod
    def scratch_shapes(tf):
        nf, d = D_FF // tf, D_MODEL
        return [pltpu.VMEM((nf, d, tf), BF16), pltpu.VMEM((nf, d, tf), BF16), pltpu.VMEM((nf, tf, d), BF16),
                pltpu.VMEM((WEIGHT_SLOTS, d, tf), F32), pltpu.VMEM((WEIGHT_SLOTS, d, tf), F32),
                pltpu.VMEM((WEIGHT_SLOTS, tf, d), F32), pltpu.SemaphoreType.DMA((WEIGHT_SLOTS,))]


def _ffn_kernel(x_ref, wgu_hbm, wd_hbm, lng_ref, lnb_ref, o_ref, xb_ref, acc_ref,
                wgb_ref, wub_ref, wdb_ref, sg_ref, su_ref, sd_ref, sems, *, tf):
    i = pl.program_id(0)
    w = _SwigluWeights(wgu_hbm, wd_hbm, (wgb_ref, wub_ref, wdb_ref), (sg_ref, su_ref, sd_ref), sems, tf)
    xb_ref[...] = x_ref[...].astype(BF16)

    @pl.when(i == 0)
    def _():
        w.request_head(0)
        for j in range(w.nf):
            w.land(0, j)
            w.apply_chunk(xb_ref, acc_ref, j)

    @pl.when(i > 0)
    def _():
        for j in range(w.nf):
            w.apply_chunk(xb_ref, acc_ref, j)

    o_ref[...] = _layer_norm(DEEPNORM_ALPHA * x_ref[...] + acc_ref[...], lng_ref[...], lnb_ref[...])


def _ffn(x2d, w_gu, w_down, ln_g, ln_b, *, tm, tf):
    n, d = x2d.shape
    assert D_FF // tf >= WEIGHT_SLOTS
    return pl.pallas_call(
        functools.partial(_ffn_kernel, tf=tf),
        grid=(n // tm,),
        in_specs=[pl.BlockSpec((tm, d), lambda i: (i, 0)),
                  pl.BlockSpec(memory_space=pl.ANY), pl.BlockSpec(memory_space=pl.ANY),
                  _const_spec((1, d)), _const_spec((1, d))],
        out_specs=pl.BlockSpec((tm, d), lambda i: (i, 0)),
        out_shape=jax.ShapeDtypeStruct((n, d), F32),
        scratch_shapes=[pltpu.VMEM((tm, d), BF16), pltpu.VMEM((tm, d), F32)] + _SwigluWeights.scratch_shapes(tf),
        compiler_params=pltpu.CompilerParams(dimension_semantics=("arbitrary",),
                                             vmem_limit_bytes=VMEM_LIMIT_BYTES),
        name="ffn",
    )(x2d, w_gu[None], w_down[None], ln_g[None, :], ln_b[None, :])


L1_Q, L1_K, L1_V, L1_GATE, L1_AB = 0, 1024, 2048, 3072, 4096
L1_CONV = 3 * GDN_QK
L1_COLS = 4224
L1_BETA_LANE = GDN_HEADS
CONV_PAD = 8
L1_PROJ_BLOCK = 512
L1_CHUNK_GROUP = 2
ROUTE_LANES = LANES


def _pack_l1_w_in(w):
    offs = np.cumsum([0, L1_CONV, GDN_HEADS, GDN_HEADS, GDN_V])
    qkv, a_in, b_in, gate = [w[:, offs[i]:offs[i + 1]] for i in range(4)]
    ab = jnp.pad(jnp.concatenate([a_in, b_in], axis=1), ((0, 0), (0, LANES - 2 * GDN_HEADS)))
    return jnp.concatenate([qkv, gate, ab], axis=1)


def _l1_mixer_kernel(x_ref, w_in_ref, conv_ref, alog_ref, dtb_ref, cnorm_ref, w_out_ref, lng_ref, lnb_ref,
                     tril_ref, wr_hi_ref, wr_lo_ref, br_ref,
                     o_ref, route_ref, h2_ref, qkv_ref, mix_ref, s_ref, u_ref, w_ref, aqk_ref, *ext_refs, tt):
    ti = pl.program_id(1)

    @pl.when(ti == 0)
    def _():
        s_ref[...] = jnp.zeros_like(s_ref)
        for ext_ref in ext_refs:
            ext_ref[0:CONV_PAD, :] = jnp.zeros((CONV_PAD, L1_PROJ_BLOCK), F32)

    x = x_ref[...]
    xb = x.astype(BF16)
    h2_ref[...] = jnp.dot(xb, w_in_ref[:, L1_CONV:L1_COLS], preferred_element_type=F32)

    for blk, ext_ref in enumerate(ext_refs):
        c0 = blk * L1_PROJ_BLOCK
        cs = slice(c0, c0 + L1_PROJ_BLOCK)
        ext_ref[CONV_PAD:CONV_PAD + tt, :] = jnp.dot(xb, w_in_ref[:, cs], preferred_element_type=F32)
        conv = ext_ref[CONV_PAD:CONV_PAD + tt, :] * conv_ref[CONV_WIDTH - 1:CONV_WIDTH, cs]
        for j in range(CONV_WIDTH - 1):
            back = CONV_WIDTH - 1 - j
            conv = conv + ext_ref[CONV_PAD - back:CONV_PAD - back + tt, :] * conv_ref[j:j + 1, cs]
        ext_ref[0:CONV_PAD, :] = ext_ref[tt:tt + CONV_PAD, :]
        act = _silu(conv)
        if c0 >= L1_V:
            qkv_ref[:, cs] = act
        else:
            scale = GDN_DK ** -0.5 if c0 < L1_K else 1.0
            for h in range(L1_PROJ_BLOCK // GDN_DK):
                t = act[:, h * GDN_DK:(h + 1) * GDN_DK]
                qkv_ref[:, c0 + h * GDN_DK:c0 + (h + 1) * GDN_DK] = t * (
                    lax.rsqrt(jnp.sum(t * t, axis=-1, keepdims=True) + L2_EPS) * scale)

    ab = h2_ref[:, L1_AB - L1_CONV:L1_AB - L1_CONV + LANES]
    g_blk = -jnp.exp(alog_ref[...]) * _softplus(ab + dtb_ref[...])
    beta_blk = _sigmoid(ab)
    gc_blk = _mm_exact_lhs01(tril_ref[...], g_blk)
    eg_blk = jnp.exp(gc_blk)

    ci = lax.broadcasted_iota(jnp.int32, (CHUNK, CHUNK), 0)
    cj = lax.broadcasted_iota(jnp.int32, (CHUNK, CHUNK), 1)
    incl = cj <= ci
    strict = cj < ci

    heads = range(GDN_HEADS)
    col = lambda base, hd: slice(base + hd * GDN_DK, base + (hd + 1) * GDN_DK)
    n_chunks = tt // CHUNK

    for c0 in range(0, n_chunks, L1_CHUNK_GROUP):
        inst = [(c, hd) for c in range(c0, c0 + L1_CHUNK_GROUP) for hd in heads]
        n_i = range(len(inst))
        rows = [slice(c * CHUNK, (c + 1) * CHUNK) for c, _ in inst]
        gc_c = {c: gc_blk[c * CHUNK:(c + 1) * CHUNK] for c in range(c0, c0 + L1_CHUNK_GROUP)}
        gc_t = {c: gc_c[c].T for c in gc_c}
        kdec_scale = {c: jnp.exp(gc_c[c][CHUNK - 1:CHUNK, :] - gc_c[c]) for c in gc_c}
        q_h = [qkv_ref[rows[i], col(L1_Q, hd)] for i, (c, hd) in enumerate(inst)]
        k_h = [qkv_ref[rows[i], col(L1_K, hd)] for i, (c, hd) in enumerate(inst)]
        v_h = [qkv_ref[rows[i], col(L1_V, hd)] for i, (c, hd) in enumerate(inst)]
        beta = [beta_blk[rows[i], L1_BETA_LANE + hd:L1_BETA_LANE + hd + 1] for i, (c, hd) in enumerate(inst)]
        eg = [eg_blk[rows[i], hd:hd + 1] for i, (c, hd) in enumerate(inst)]
        decay = [jnp.exp(jnp.where(incl, gc_c[c][:, hd:hd + 1] - gc_t[c][hd:hd + 1, :], NEG_BIG)) for c, hd in inst]
        kb = [k_h[i] * beta[i] for i in n_i]
        low = [jnp.where(strict, _mm_nt(kb[i], k_h[i]) * decay[i], 0.0) for i in n_i]
        a_qk = [_mm_nt(q_h[i], k_h[i]) * decay[i] for i in n_i]
        a_m = [-low[i] for i in n_i]
        m = [_mm(low[i], low[i]) for i in n_i]
        for it in range(5):
            am = [_mm(a_m[i], m[i]) for i in n_i]
            a_m = [a_m[i] + m[i] + am[i] for i in n_i]
            if it < 4:
                m = [_mm(m[i], m[i]) for i in n_i]
        rhs = [jnp.concatenate([v_h[i] * beta[i], kb[i] * eg[i]], axis=1) for i in n_i]
        uw = [rhs[i] + _mm(a_m[i], rhs[i]) for i in n_i]
        for i, (c, hd) in enumerate(inst):
            u_ref[rows[i], col(0, hd)] = uw[i][:, 0:GDN_DV]
            w_ref[rows[i], col(0, hd)] = uw[i][:, GDN_DV:GDN_DV + GDN_DK]
            aqk_ref[rows[i], hd * LANES:hd * LANES + CHUNK] = a_qk[i]
            qkv_ref[rows[i], col(L1_Q, hd)] = q_h[i] * eg[i]
            qkv_ref[rows[i], col(L1_K, hd)] = k_h[i] * kdec_scale[c][:, hd:hd + 1]

    s = [s_ref[hd] for hd in heads]
    for c in range(n_chunks):
        rows = slice(c * CHUNK, (c + 1) * CHUNK)
        e_last = jnp.exp(gc_blk[(c + 1) * CHUNK - 1:(c + 1) * CHUNK, :])
        ws_qs = [_mm(jnp.concatenate([w_ref[rows, col(0, hd)], qkv_ref[rows, col(L1_Q, hd)]], axis=0), s[hd])
                 for hd in heads]
        v_new = [u_ref[rows, col(0, hd)] - ws_qs[hd][0:CHUNK] for hd in heads]
        o_h = [ws_qs[hd][CHUNK:2 * CHUNK] + _mm(aqk_ref[rows, hd * LANES:hd * LANES + CHUNK], v_new[hd])
               for hd in heads]
        s = [s[hd] * e_last[:, hd:hd + 1] + _mm_tn(qkv_ref[rows, col(L1_K, hd)], v_new[hd]) for hd in heads]
        for hd in heads:
            mix_ref[rows, col(0, hd)] = o_h[hd]
    for hd in heads:
        s_ref[hd] = s[hd]

    for hd in range(GDN_HEADS):
        sl = slice(hd * GDN_DV, (hd + 1) * GDN_DV)
        o = mix_ref[:, sl]
        o = o * lax.rsqrt(jnp.mean(o * o, axis=-1, keepdims=True) + NORM_EPS) * cnorm_ref[:, sl]
        mix_ref[:, sl] = o * _silu(h2_ref[:, L1_GATE - L1_CONV + hd * GDN_DV:L1_GATE - L1_CONV + (hd + 1) * GDN_DV])
    y = _mm(mix_ref[...], w_out_ref[...])
    x1 = _layer_norm(DEEPNORM_ALPHA * x + y, lng_ref[...], lnb_ref[...])
    o_ref[...] = x1

    x_hi = x1.astype(BF16)
    x_lo = (x1 - x_hi.astype(F32)).astype(BF16)
    logits = (jnp.dot(x_hi, wr_hi_ref[...], preferred_element_type=F32)
              + jnp.dot(x_hi, wr_lo_ref[...], preferred_element_type=F32)
              + jnp.dot(x_lo, wr_hi_ref[...], preferred_element_type=F32)) + br_ref[...]
    lane = lax.broadcasted_iota(jnp.int32, (tt, ROUTE_LANES), 1)
    lane_f = lane.astype(F32)
    logits = jnp.where(lane < N_EXPERTS, logits, NEG_BIG)
    m1 = jnp.max(logits, axis=-1, keepdims=True)
    i1 = jnp.min(jnp.where(logits == m1, lane_f, float(ROUTE_LANES)), axis=-1, keepdims=True)
    rest = jnp.where(lane_f == i1, NEG_BIG, logits)
    m2 = jnp.max(rest, axis=-1, keepdims=True)
    i2 = jnp.min(jnp.where(rest == m2, lane_f, float(ROUTE_LANES)), axis=-1, keepdims=True)
    e21 = jnp.exp(m2 - m1)
    g1 = 1.0 / (1.0 + e21)
    g2 = e21 * g1
    route_ref[...] = jnp.where(lane == 0, i1, jnp.where(lane == 1, i2, jnp.where(lane == 2, g1, jnp.where(lane == 3, g2, 0.0))))


def _l1_mixer(x, w_in, conv_w, a_log, dt_bias, c_norm, w_out, ln_g, ln_b, w_router, b_router, *, tt):
    bsz, seq, d = x.shape
    w_in_p = _pack_l1_w_in(w_in).astype(BF16)
    lane_pad = lambda v: jnp.pad(v[None, :], ((0, 0), (0, LANES - v.shape[0])))
    wr = jnp.pad(w_router, ((0, 0), (0, ROUTE_LANES - N_EXPERTS)))
    wr_hi = wr.astype(BF16)
    wr_lo = (wr - wr_hi.astype(F32)).astype(BF16)
    params = [w_in_p, conv_w, lane_pad(a_log), lane_pad(dt_bias), c_norm.reshape(1, GDN_V), w_out.astype(BF16),
              ln_g[None, :], ln_b[None, :], jnp.asarray(_chunk_tril_np(tt), BF16), wr_hi, wr_lo, lane_pad(b_router)]
    tile = lambda w: pl.BlockSpec((None, tt, w), lambda b, t: (b, t, 0))
    return pl.pallas_call(
        functools.partial(_l1_mixer_kernel, tt=tt),
        grid=(bsz, seq // tt),
        in_specs=[tile(d)] + [_const_spec(a.shape) for a in params],
        out_specs=[tile(d), tile(ROUTE_LANES)],
        out_shape=[jax.ShapeDtypeStruct((bsz, seq, d), F32), jax.ShapeDtypeStruct((bsz, seq, ROUTE_LANES), F32)],
        scratch_shapes=[pltpu.VMEM((tt, L1_COLS - L1_CONV), F32),
                        pltpu.VMEM((tt, L1_CONV), F32), pltpu.VMEM((tt, GDN_V), F32),
                        pltpu.VMEM((GDN_HEADS, GDN_DK, GDN_DV), F32),
                        pltpu.VMEM((tt, GDN_V), F32), pltpu.VMEM((tt, GDN_QK), F32),
                        pltpu.VMEM((tt, GDN_HEADS * LANES), F32)]
                       + [pltpu.VMEM((tt + CONV_PAD, L1_PROJ_BLOCK), F32)] * (L1_CONV // L1_PROJ_BLOCK),
        compiler_params=pltpu.CompilerParams(dimension_semantics=("arbitrary", "arbitrary"),
                                             vmem_limit_bytes=VMEM_LIMIT_BYTES),
        name="l1_mixer",
    )(x, *params)


ID_ROWS = 4


def _rank_kernel(route_ref, stril_ref, ids_ref, counts_ref, carry_ref):
    @pl.when(pl.program_id(0) == 0)
    def _():
        carry_ref[...] = jnp.zeros_like(carry_ref)

    r = route_ref[...]
    tr = r.shape[0]
    e1, e2 = r[:, 0:1], r[:, 1:2]
    lane = lax.broadcasted_iota(jnp.int32, (tr, ROUTE_LANES), 1)
    lane_f = lane.astype(F32)
    oh1 = (lane_f == e1).astype(F32)
    oh2 = (lane_f == e2).astype(F32)
    both = oh1 + oh2
    before = jnp.dot(stril_ref[...], both.astype(BF16), preferred_element_type=F32) + carry_ref[...]
    rank1 = jnp.sum(oh1 * before, axis=-1, keepdims=True)
    rank2 = jnp.sum(oh2 * before, axis=-1, keepdims=True)
    table = jnp.where(lane == 0, e1, jnp.where(lane == 1, e2, jnp.where(lane == 2, rank1,
                      jnp.where(lane == 3, rank2, 0.0))))
    ids_ref[...] = table.T[0:ID_ROWS, :].astype(jnp.int32)
    carry_ref[...] += jnp.sum(both, axis=0, keepdims=True)
    counts_ref[...] = carry_ref[...]


def _rank(route2d, *, tr):
    n = route2d.shape[0]
    i = np.arange(tr)
    stril = jnp.asarray((i[None, :] < i[:, None]).astype(np.float32), BF16)
    return pl.pallas_call(
        _rank_kernel,
        grid=(n // tr,),
        in_specs=[pl.BlockSpec((tr, ROUTE_LANES), lambda i: (i, 0)), _const_spec((tr, tr))],
        out_specs=[pl.BlockSpec((ID_ROWS, tr), lambda i: (0, i)), _const_spec((1, ROUTE_LANES))],
        out_shape=[jax.ShapeDtypeStruct((ID_ROWS, n), jnp.int32), jax.ShapeDtypeStruct((1, ROUTE_LANES), F32)],
        scratch_shapes=[pltpu.VMEM((1, ROUTE_LANES), F32)],
        compiler_params=pltpu.CompilerParams(dimension_semantics=("arbitrary",)),
        name="moe_rank",
    )(route2d, stril)


ROW_SUB = D_MODEL // LANES


def _to_row_tiles(dst_ref, src, n):
    for lb in range(ROW_SUB):
        dst_ref[pl.ds(lb, n, stride=ROW_SUB), :] = src[:, lb * LANES:(lb + 1) * LANES]


def _from_row_tiles(src_ref, n):
    return [src_ref[pl.ds(lb, n, stride=ROW_SUB), :] for lb in range(ROW_SUB)]


def _tile_copy(src_ref, src_sub, dst_ref, dst_sub, sem):
    src = src_ref.at[pl.ds(pl.multiple_of(src_sub, ROW_SUB), ROW_SUB), :]
    dst = dst_ref.at[pl.ds(pl.multiple_of(dst_sub, ROW_SUB), ROW_SUB), :]
    return pltpu.make_async_copy(src, dst, sem)


def _wait_rows(hbm_ref, buf_ref, sem, n, copies):
    for _ in range(copies):
        pltpu.make_async_copy(hbm_ref.at[pl.ds(0, n * ROW_SUB), :], buf_ref, sem).wait()


def _dispatch_kernel(p0_ref, p1_ref, x_ref, init_hbm, xs_hbm, xt0_ref, xt1_ref, sems, *, td):
    del init_hbm
    p_refs = (p0_ref, p1_ref)
    i = pl.program_id(0)
    last = pl.num_programs(0) - 1

    def step(xt_ref, sem):
        @pl.when(i >= 2)
        def _():
            _wait_rows(xs_hbm, xt_ref, sem, td, TOP_K)

        _to_row_tiles(xt_ref, x_ref[...], td)

        def issue(t, carry):
            for k in range(TOP_K):
                _tile_copy(xt_ref, t * ROW_SUB, xs_hbm, p_refs[k][t], sem).start(priority=k)
            return carry

        lax.fori_loop(0, td, issue, 0, unroll=8)

    for slot, (xt_ref, sem) in enumerate(((xt0_ref, sems.at[0]), (xt1_ref, sems.at[1]))):
        @pl.when(i % 2 == slot)
        def _():
            step(xt_ref, sem)

    @pl.when(i == last)
    def _():
        _wait_rows(xs_hbm, xt0_ref, sems.at[0], td, TOP_K)
        _wait_rows(xs_hbm, xt1_ref, sems.at[1], td, TOP_K)


def _dispatch(pos, x2d, n_rows, *, td):
    n, d = x2d.shape
    init = jnp.zeros((n_rows * ROW_SUB, LANES), F32)
    smem_rows = pl.BlockSpec((td,), lambda i: (i,), memory_space=pltpu.SMEM)
    assert n // td >= 2
    return pl.pallas_call(
        functools.partial(_dispatch_kernel, td=td),
        grid=(n // td,),
        in_specs=[smem_rows, smem_rows, pl.BlockSpec((td, d), lambda i: (i, 0)), pl.BlockSpec(memory_space=pl.ANY)],
        out_specs=pl.BlockSpec(memory_space=pl.ANY),
        scratch_shapes=[pltpu.VMEM((td * ROW_SUB, LANES), F32), pltpu.VMEM((td * ROW_SUB, LANES), F32),
                        pltpu.SemaphoreType.DMA((2,))],
        out_shape=jax.ShapeDtypeStruct((n_rows * ROW_SUB, LANES), F32),
        input_output_aliases={3: 0},
        compiler_params=pltpu.CompilerParams(dimension_semantics=("arbitrary",)),
        name="moe_dispatch",
    )(pos[0], pos[1], x2d, init)


def _gmm_kernel(te_ref, used_ref, first_ref, x_ref, wgu_hbm, wd_hbm, o_ref,
                xb_ref, acc_ref, wgb_ref, wub_ref, wdb_ref, sg_ref, su_ref, sd_ref, sems, *, tm, tf):
    i = pl.program_id(0)
    n_tiles = pl.num_programs(0)
    w = _SwigluWeights(wgu_hbm, wd_hbm, (wgb_ref, wub_ref, wdb_ref), (sg_ref, su_ref, sd_ref), sems, tf)
    active = i < used_ref[0]
    is_first = first_ref[i] == 1

    @pl.when(active)
    def _():
        for lb, blk in enumerate(_from_row_tiles(x_ref, tm)):
            xb_ref[:, lb * LANES:(lb + 1) * LANES] = blk.astype(BF16)

    @pl.when(jnp.logical_and(active, is_first))
    def _():
        e = te_ref[i]

        @pl.when(i == 0)
        def _():
            w.request_head(e)

        for j in range(w.nf):
            w.land(e, j)
            w.apply_chunk(xb_ref, acc_ref, j)

    @pl.when(jnp.logical_and(active, jnp.logical_not(is_first)))
    def _():
        for j in range(w.nf):
            w.apply_chunk(xb_ref, acc_ref, j)

    @pl.when(active)
    def _():
        _to_row_tiles(o_ref, acc_ref[...], tm)
        nxt = jnp.minimum(i + 1, n_tiles - 1)

        @pl.when(jnp.logical_and(i + 1 < n_tiles, first_ref[nxt] == 1))
        def _():
            w.request_head(te_ref[nxt])

    @pl.when(jnp.logical_not(active))
    def _():
        o_ref[...] = jnp.zeros_like(o_ref)


def _gmm(tile_expert, n_used, tile_first, xs, w_gu, w_down, *, tm, tf):
    d = D_MODEL
    n_rows = xs.shape[0] // ROW_SUB
    nf = D_FF // tf
    n_tiles = n_rows // tm
    assert nf >= WEIGHT_SLOTS
    grid_spec = pltpu.PrefetchScalarGridSpec(
        num_scalar_prefetch=3,
        grid=(n_tiles,),
        in_specs=[pl.BlockSpec((tm * ROW_SUB, LANES), lambda i, te, used, first: (jnp.minimum(i, used[0] - 1), 0)),
                  pl.BlockSpec(memory_space=pl.ANY), pl.BlockSpec(memory_space=pl.ANY)],
        out_specs=pl.BlockSpec((tm * ROW_SUB, LANES), lambda i, te, used, first: (i, 0)),
        scratch_shapes=[pltpu.VMEM((tm, d), BF16), pltpu.VMEM((tm, d), F32)] + _SwigluWeights.scratch_shapes(tf),
    )
    return pl.pallas_call(
        functools.partial(_gmm_kernel, tm=tm, tf=tf),
        grid_spec=grid_spec,
        out_shape=jax.ShapeDtypeStruct((n_rows * ROW_SUB, LANES), F32),
        compiler_params=pltpu.CompilerParams(dimension_semantics=("arbitrary",),
                                             vmem_limit_bytes=VMEM_LIMIT_BYTES),
        name="moe_gmm",
    )(tile_expert, n_used, tile_first, xs, w_gu, w_down)


def _combine_kernel(p0_ref, p1_ref, p0_next_ref, p1_next_ref, x_ref, route_ref, lng_ref, lnb_ref, y_hbm, o_ref,
                    ya0_ref, yb0_ref, ya1_ref, yb1_ref, sems, *, tc):
    i = pl.program_id(0)
    n = pl.num_programs(0)
    slots = ((ya0_ref, yb0_ref, sems.at[0]), (ya1_ref, yb1_ref, sems.at[1]))

    def issue(p_refs, slot):
        ya_ref, yb_ref, sem = slots[slot]

        def body(t, carry):
            for k, (p_ref, buf) in enumerate(zip(p_refs, (ya_ref, yb_ref))):
                _tile_copy(y_hbm, p_ref[t], buf, t * ROW_SUB, sem).start(priority=k)
            return carry

        lax.fori_loop(0, tc, body, 0, unroll=8)

    @pl.when(i == 0)
    def _():
        issue((p0_ref, p1_ref), 0)

    for slot in range(2):
        @pl.when(jnp.logical_and(i + 1 < n, (i + 1) % 2 == slot))
        def _():
            issue((p0_next_ref, p1_next_ref), slot)

    for slot in range(2):
        @pl.when(i % 2 == slot)
        def _():
            ya_ref, yb_ref, sem = slots[slot]
            _wait_rows(y_hbm, ya_ref, sem, tc, TOP_K)
            r = route_ref[...]
            g1, g2 = r[:, 2:3], r[:, 3:4]
            y = jnp.concatenate(
                [g1 * a + g2 * b for a, b in zip(_from_row_tiles(ya_ref, tc), _from_row_tiles(yb_ref, tc))], axis=1)
            o_ref[...] = _layer_norm(DEEPNORM_ALPHA * x_ref[...] + y, lng_ref[...], lnb_ref[...])


def _combine(pos, x2d, route2d, ys, ln_g, ln_b, *, tc):
    n, d = x2d.shape
    steps = n // tc
    smem_rows = pl.BlockSpec((tc,), lambda i: (i,), memory_space=pltpu.SMEM)
    smem_next = pl.BlockSpec((tc,), lambda i: (jnp.minimum(i + 1, steps - 1),), memory_space=pltpu.SMEM)
    return pl.pallas_call(
        functools.partial(_combine_kernel, tc=tc),
        grid=(steps,),
        in_specs=[smem_rows, smem_rows, smem_next, smem_next,
                  pl.BlockSpec((tc, d), lambda i: (i, 0)),
                  pl.BlockSpec((tc, ROUTE_LANES), lambda i: (i, 0)),
                  _const_spec((1, d)), _const_spec((1, d)),
                  pl.BlockSpec(memory_space=pl.ANY)],
        out_specs=pl.BlockSpec((tc, d), lambda i: (i, 0)),
        scratch_shapes=[pltpu.VMEM((tc * ROW_SUB, LANES), F32) for _ in range(2 * TOP_K)]
                       + [pltpu.SemaphoreType.DMA((2,))],
        out_shape=jax.ShapeDtypeStruct((n, d), F32),
        compiler_params=pltpu.CompilerParams(dimension_semantics=("arbitrary",),
                                             vmem_limit_bytes=VMEM_LIMIT_BYTES),
        name="moe_combine",
    )(pos[0], pos[1], pos[0], pos[1], x2d, route2d, ln_g[None, :], ln_b[None, :], ys)


def _moe(x2d, route2d, w_gu, w_down, ln_g, ln_b, *, tm, tf, tr, td, tc):
    n, d = x2d.shape
    ids, counts = _rank(route2d, tr=tr)
    cnt = counts[0, :N_EXPERTS].astype(jnp.int32)
    tiles_per = (cnt + tm - 1) // tm
    tile_end = jnp.cumsum(tiles_per)
    offs = (tile_end - tiles_per) * tm
    n_tiles = (n * TOP_K) // tm + N_EXPERTS
    n_used = tile_end[-1:]
    t_idx = jnp.arange(n_tiles, dtype=jnp.int32)
    tile_expert = jnp.sum(jnp.minimum(t_idx, n_used - 1)[:, None] >= tile_end[None, :], axis=1).astype(jnp.int32)
    tile_start = tile_end - tiles_per
    tile_first = jnp.logical_and(t_idx == tile_start[tile_expert], t_idx < n_used).astype(jnp.int32)
    group_start = sum(jnp.where(ids[0:TOP_K] == e, offs[e], 0) for e in range(N_EXPERTS))
    pos = (group_start + ids[TOP_K:2 * TOP_K]) * ROW_SUB
    xs = _dispatch(pos, x2d, n_tiles * tm, td=td)
    ys = _gmm(tile_expert, n_used, tile_first, xs, w_gu, w_down, tm=tm, tf=tf)
    return _combine(pos, x2d, route2d, ys, ln_g, ln_b, tc=tc)


def kernel(x, positions, ab_w_in, gla_w_gate2, gla_b_gate, gla_norm, ret_norm, ab_w_out, ab_ln1_g, ab_ln1_b, ffn_w_gu, ffn_w_down, ab_ln2_g, ab_ln2_b, c_w_in, c_conv_w, c_a_log, c_dt_bias, c_norm, c_w_out, c_ln1_g, c_ln1_b, moe_w_router, moe_b_router, moe_w_gu, moe_w_down, c_ln2_g, c_ln2_b):
    bsz, seq, d = x.shape
    for layer in range(DEPTH):
        i = layer // 2
        if layer % 2 == 0:
            x = _l0_mixer(x, positions, ab_w_in[i], gla_w_gate2[i], gla_b_gate[i], gla_norm[i], ret_norm[i],
                          ab_w_out[i], ab_ln1_g[i], ab_ln1_b[i], tt=L0_TIME_TILE)
            x = _ffn(x.reshape(bsz * seq, d), ffn_w_gu[i], ffn_w_down[i], ab_ln2_g[i], ab_ln2_b[i],
                     tm=FFN_ROW_TILE, tf=FF_TILE).reshape(bsz, seq, d)
        else:
            x, route = _l1_mixer(x, c_w_in[i], c_conv_w[i], c_a_log[i], c_dt_bias[i], c_norm[i], c_w_out[i],
                                 c_ln1_g[i], c_ln1_b[i], moe_w_router[i], moe_b_router[i], tt=L1_TIME_TILE)
            x = _moe(x.reshape(bsz * seq, d), route.reshape(bsz * seq, ROUTE_LANES), moe_w_gu[i], moe_w_down[i],
                     c_ln2_g[i], c_ln2_b[i], tm=MOE_ROW_TILE, tf=FF_TILE, tr=ROUTE_TILE, td=ROUTE_TILE,
                     tc=ROUTE_TILE).reshape(bsz, seq, d)
    return x
```

```python
import functools

import numpy as np
import jax
import jax.numpy as jnp
from jax import lax
from jax.experimental import pallas as pl
from jax.experimental.pallas import tpu as pltpu

F32 = jnp.float32
BF16 = jnp.bfloat16

D_MODEL = 1024
DEPTH = 2
CHUNK = 64
GLA_HEADS, GLA_DK, GLA_DV, GLA_GATE_RANK, GLA_TAU = 4, 64, 128, 16, 16.0
RET_HEADS, RET_DK, RET_DV = 4, 64, 128
ROPE_BASE = 10000.0
GDN_HEADS, GDN_DK, GDN_DV = 8, 128, 128
CONV_WIDTH = 4
D_FF = 3584
N_EXPERTS = 8
TOP_K = 2
NORM_EPS = 1e-5
L2_EPS = 1e-6
DEEPNORM_ALPHA = (2.0 * DEPTH) ** 0.25

GLA_QK = GLA_HEADS * GLA_DK
GLA_V = GLA_HEADS * GLA_DV
RET_QK = RET_HEADS * RET_DK
RET_V = RET_HEADS * RET_DV
GDN_QK = GDN_HEADS * GDN_DK
GDN_V = GDN_HEADS * GDN_DV

LANES = 128
VMEM_LIMIT_BYTES = 56 * 1024 * 1024
NEG_BIG = -1e30

L0_TIME_TILE = 256
L1_TIME_TILE = 256
FFN_ROW_TILE = 512
MOE_ROW_TILE = 512
FF_TILE = 512
ROUTE_TILE = 512


def _mm(a, b):
    return jnp.dot(a.astype(BF16), b.astype(BF16), preferred_element_type=F32)


def _mm_nt(a, b):
    return lax.dot_general(a.astype(BF16), b.astype(BF16), (((1,), (1,)), ((), ())),
                           preferred_element_type=F32)


def _mm_tn(a, b):
    return lax.dot_general(a.astype(BF16), b.astype(BF16), (((0,), (0,)), ((), ())),
                           preferred_element_type=F32)


def _split3(x):
    hi = x.astype(BF16)
    r1 = x - hi.astype(F32)
    mid = r1.astype(BF16)
    lo = (r1 - mid.astype(F32)).astype(BF16)
    return hi, mid, lo


def _mm_exact_lhs01(m01, x):
    hi, mid, lo = _split3(x)
    return (jnp.dot(m01, hi, preferred_element_type=F32)
            + jnp.dot(m01, mid, preferred_element_type=F32)
            + jnp.dot(m01, lo, preferred_element_type=F32))


def _sigmoid(x):
    return 1.0 / (1.0 + jnp.exp(-x))


def _silu(x):
    return x * _sigmoid(x)


def _softplus(x):
    return jnp.maximum(x, 0.0) + jnp.log(1.0 + jnp.exp(-jnp.abs(x)))


def _layer_norm(x, g, b):
    mu = jnp.mean(x, axis=-1, keepdims=True)
    xc = x - mu
    var = jnp.mean(xc * xc, axis=-1, keepdims=True)
    return xc * lax.rsqrt(var + NORM_EPS) * g + b


def _chunk_tril_np(tt):
    i = np.arange(tt)
    same = (i[:, None] // CHUNK) == (i[None, :] // CHUNK)
    return (same & (i[None, :] <= i[:, None])).astype(np.float32)


def _const_spec(shape):
    nd = len(shape)
    return pl.BlockSpec(shape, lambda *_: (0,) * nd)


L0_GQ, L0_GK, L0_GV, L0_GR = 0, 256, 512, 1024
L0_RQ, L0_RK, L0_RV, L0_RG = 1536, 1792, 2048, 2560
L0_GA = 3072
L0_COLS = 3200
L0_CHUNK_GROUP = 1


def _pack_l0_w_in(w):
    offs = np.cumsum([0, GLA_QK, GLA_QK, GLA_V, GLA_GATE_RANK, GLA_V, RET_QK, RET_QK, RET_V, RET_V])
    gq, gk, gv, ga, gr, rq, rk, rv, rg = [w[:, offs[i]:offs[i + 1]] for i in range(9)]
    ga = jnp.pad(ga, ((0, 0), (0, LANES - GLA_GATE_RANK)))
    return jnp.concatenate([gq, gk, gv, gr, rq, rk, rv, rg, ga], axis=1)


def _ret_tables():
    h = np.arange(RET_HEADS, dtype=np.float64)
    log_gamma = np.log(1.0 - 2.0 ** (-5.0 - h))
    pos = np.arange(CHUNK, dtype=np.float64)
    diff = pos[:, None] - pos[None, :]
    dmat = np.where(diff >= 0, np.exp(log_gamma[:, None, None] * np.maximum(diff, 0.0)), 0.0)
    xi = np.exp(log_gamma[None, :] * (pos[:, None] + 1.0))
    zeta = np.exp(log_gamma[None, :] * (CHUNK - 1.0 - pos[:, None]))
    decay = np.exp(log_gamma * CHUNK)
    xi_full = np.repeat(xi, RET_DV, axis=1)
    zeta_full = np.repeat(zeta, RET_DK, axis=1)
    decay_full = np.repeat(decay, RET_DK)[None, :]
    return (dmat.astype(np.float32), xi_full.astype(np.float32), zeta_full.astype(np.float32),
            decay_full.astype(np.float32))


def _rope_tables():
    half = RET_DK // 2
    inv_freq = ROPE_BASE ** (-np.arange(0, RET_DK, 2, dtype=np.float32) / RET_DK)
    per_head = np.concatenate([inv_freq, inv_freq])
    freq_full = np.tile(per_head, RET_HEADS)[None, :].astype(np.float32)
    sign = np.tile(np.concatenate([-np.ones(half), np.ones(half)]), RET_HEADS)[None, :].astype(np.float32)
    return freq_full, sign


def _l0_mixer_kernel(x_ref, pos_ref, w_in_ref, wg2_ref, bg_ref, gnorm_ref, rnorm_ref, w_out_ref,
                     lng_ref, lnb_ref, tril_ref, dmat_ref, xi_ref, zeta_ref, rdecay_ref,
                     freq_ref, sign_ref, o_ref, h_ref, mix_ref, sg_ref, sr_ref, *, tt):
    ti = pl.program_id(1)

    @pl.when(ti == 0)
    def _():
        sg_ref[...] = jnp.zeros_like(sg_ref)
        sr_ref[...] = jnp.zeros_like(sr_ref)

    x = x_ref[...]
    h_ref[...] = _mm(x, w_in_ref[...])

    lane = lax.broadcasted_iota(jnp.int32, (1, LANES), 1)
    lo_half = lane < GLA_DK
    ci = lax.broadcasted_iota(jnp.int32, (CHUNK, CHUNK), 0)
    cj = lax.broadcasted_iota(jnp.int32, (CHUNK, CHUNK), 1)
    causal = cj <= ci

    z = _mm(h_ref[:, L0_GA:L0_GA + LANES], wg2_ref[...]) + bg_ref[...]
    log_a = -_softplus(-z) * (1.0 / GLA_TAU)
    b = _mm_exact_lhs01(tril_ref[...], log_a)
    eb = jnp.exp(b)
    q_dec = h_ref[:, L0_GQ:L0_GQ + GLA_QK] * (GLA_DK ** -0.5) * eb
    k_all = h_ref[:, L0_GK:L0_GK + GLA_QK]
    k_neg = k_all * jnp.exp(-b)

    ang = pos_ref[...] * freq_ref[:, 0:LANES]
    cos = jnp.concatenate([jnp.cos(ang)] * (RET_QK // LANES), axis=1)
    sin = jnp.concatenate([jnp.sin(ang)] * (RET_QK // LANES), axis=1) * sign_ref[...]
    half = RET_DK // 2
    lane256 = lax.broadcasted_iota(jnp.int32, (1, RET_QK), 1)
    first_half = (lane256 & (RET_DK - 1)) < half

    def rope(t):
        swapped = jnp.where(first_half, pltpu.roll(t, RET_QK - half, 1), pltpu.roll(t, half, 1))
        return t * cos + swapped * sin

    rq = rope(h_ref[:, L0_RQ:L0_RQ + RET_QK]) * (RET_DK ** -0.5)
    rk = rope(h_ref[:, L0_RK:L0_RK + RET_QK])

    n_chunks = tt // CHUNK
    heads = range(GLA_HEADS)
    pairs = range(GLA_HEADS // 2)
    pair_lanes = lambda hd: slice((hd // 2) * LANES, (hd // 2 + 1) * LANES)
    head_mask = lambda hd: lo_half if hd % 2 == 0 else jnp.logical_not(lo_half)
    vcol = lambda base, hd: slice(base + hd * GLA_DV, base + (hd + 1) * GLA_DV)
    sg = [sg_ref[p] for p in pairs]
    sr = [sr_ref[p] for p in pairs]
    for c0 in range(0, n_chunks, L0_CHUNK_GROUP):
        group = range(c0, c0 + L0_CHUNK_GROUP)
        rows = {c: slice(c * CHUNK, (c + 1) * CHUNK) for c in group}
        b_last = {c: b[(c + 1) * CHUNK - 1:(c + 1) * CHUNK, :] for c in group}
        k_end = {c: k_all[rows[c]] * jnp.exp(b_last[c] - b[rows[c]]) for c in group}
        rk_c = {c: rk[rows[c]] for c in group}
        rk_z = {c: rk_c[c] * zeta_ref[...] for c in group}
        inst = [(c, hd) for c in group for hd in heads]
        g_q = {ch: jnp.where(head_mask(ch[1]), q_dec[rows[ch[0]], pair_lanes(ch[1])], 0.0) for ch in inst}
        r_q = {ch: jnp.where(head_mask(ch[1]), rq[rows[ch[0]], pair_lanes(ch[1])], 0.0) for ch in inst}
        g_v = {ch: h_ref[rows[ch[0]], vcol(L0_GV, ch[1])] for ch in inst}
        r_v = {ch: h_ref[rows[ch[0]], vcol(L0_RV, ch[1])] for ch in inst}
        g_att = {(c, hd): jnp.where(causal, _mm_nt(g_q[c, hd], k_neg[rows[c], pair_lanes(hd)]), 0.0)
                 for c, hd in inst}
        r_att = {(c, hd): _mm_nt(r_q[c, hd], rk_c[c][:, pair_lanes(hd)]) * dmat_ref[hd] for c, hd in inst}
        g_d = {(c, hd): _mm_tn(g_v[c, hd], k_end[c][:, pair_lanes(hd)]) for c, hd in inst}
        r_d = {(c, hd): _mm_tn(r_v[c, hd], rk_z[c][:, pair_lanes(hd)]) for c, hd in inst}
        sg_in, sr_in = {}, {}
        for c in group:
            sg_in[c], sr_in[c] = sg, sr
            dec_c = jnp.exp(b_last[c])
            sg = [sg[p] * dec_c[:, p * LANES:(p + 1) * LANES] + jnp.where(lo_half, g_d[c, 2 * p], g_d[c, 2 * p + 1])
                  for p in pairs]
            sr = [sr[p] * rdecay_ref[:, p * LANES:(p + 1) * LANES]
                  + jnp.where(lo_half, r_d[c, 2 * p], r_d[c, 2 * p + 1]) for p in pairs]
        g_o = {(c, hd): _mm(g_att[c, hd], g_v[c, hd]) + _mm_nt(g_q[c, hd], sg_in[c][hd // 2]) for c, hd in inst}
        r_o = {(c, hd): _mm(r_att[c, hd], r_v[c, hd])
               + _mm_nt(r_q[c, hd], sr_in[c][hd // 2]) * xi_ref[:, vcol(0, hd)] for c, hd in inst}
        for c, hd in inst:
            mix_ref[rows[c], vcol(0, hd)] = g_o[c, hd]
            mix_ref[rows[c], vcol(GLA_V, hd)] = r_o[c, hd]
    for p in pairs:
        sg_ref[p] = sg[p]
        sr_ref[p] = sr[p]

    for hd in range(GLA_HEADS):
        sl = slice(hd * GLA_DV, (hd + 1) * GLA_DV)
        o = mix_ref[:, sl]
        o = o * lax.rsqrt(jnp.mean(o * o, axis=-1, keepdims=True) + NORM_EPS) * gnorm_ref[:, sl]
        mix_ref[:, sl] = o * _silu(h_ref[:, L0_GR + hd * GLA_DV:L0_GR + (hd + 1) * GLA_DV])
    for hd in range(RET_HEADS):
        sl = slice(hd * RET_DV, (hd + 1) * RET_DV)
        o = mix_ref[:, GLA_V + hd * RET_DV:GLA_V + (hd + 1) * RET_DV]
        oc = o - jnp.mean(o, axis=-1, keepdims=True)
        o = oc * lax.rsqrt(jnp.mean(oc * oc, axis=-1, keepdims=True) + NORM_EPS) * rnorm_ref[:, sl]
        mix_ref[:, GLA_V + hd * RET_DV:GLA_V + (hd + 1) * RET_DV] = (
            o * _silu(h_ref[:, L0_RG + hd * RET_DV:L0_RG + (hd + 1) * RET_DV]))

    y = _mm(mix_ref[...], w_out_ref[...])
    o_ref[...] = _layer_norm(DEEPNORM_ALPHA * x + y, lng_ref[...], lnb_ref[...])


def _l0_mixer(x, positions, w_in, w_gate2, b_gate, gla_norm, ret_norm, w_out, ln_g, ln_b, *, tt):
    bsz, seq, d = x.shape
    w_in_p = _pack_l0_w_in(w_in).astype(BF16)
    wg2 = jnp.pad(w_gate2, ((0, LANES - GLA_GATE_RANK), (0, 0))).astype(BF16)
    dmat, xi_full, zeta_full, rdecay = _ret_tables()
    freq_full, sign = _rope_tables()
    pos_f = positions.astype(F32)[..., None]
    consts = [jnp.asarray(_chunk_tril_np(tt), BF16), jnp.asarray(dmat), jnp.asarray(xi_full),
              jnp.asarray(zeta_full), jnp.asarray(rdecay), jnp.asarray(freq_full), jnp.asarray(sign)]
    params = [w_in_p, wg2, b_gate[None, :], gla_norm.reshape(1, GLA_V), ret_norm.reshape(1, RET_V),
              w_out.astype(BF16), ln_g[None, :], ln_b[None, :]]
    tile = lambda w: pl.BlockSpec((None, tt, w), lambda b, t: (b, t, 0))
    return pl.pallas_call(
        functools.partial(_l0_mixer_kernel, tt=tt),
        grid=(bsz, seq // tt),
        in_specs=[tile(d), tile(1)] + [_const_spec(a.shape) for a in params + consts],
        out_specs=tile(d),
        out_shape=jax.ShapeDtypeStruct((bsz, seq, d), F32),
        scratch_shapes=[pltpu.VMEM((tt, L0_COLS), F32), pltpu.VMEM((tt, GLA_V + RET_V), F32),
                        pltpu.VMEM((GLA_HEADS // 2, GLA_DV, LANES), F32),
                        pltpu.VMEM((RET_HEADS // 2, RET_DV, LANES), F32)],
        compiler_params=pltpu.CompilerParams(dimension_semantics=("arbitrary", "arbitrary"),
                                             vmem_limit_bytes=VMEM_LIMIT_BYTES),
        name="l0_mixer",
    )(x, pos_f, *params, *consts)


WEIGHT_SLOTS = 2


class _SwigluWeights:
    def __init__(self, wgu_hbm, wd_hbm, resident, staging, sems, tf):
        self.wgu_hbm, self.wd_hbm = wgu_hbm, wd_hbm
        self.wgb, self.wub, self.wdb = resident
        self.sg, self.su, self.sd = staging
        self.sems, self.tf, self.nf = sems, tf, D_FF // tf

    def _copies(self, e, j):
        slot = j % WEIGHT_SLOTS
        cols = pl.ds(j * self.tf, self.tf)
        up_cols = pl.ds(D_FF + j * self.tf, self.tf)
        return (pltpu.make_async_copy(self.wgu_hbm.at[e, :, cols], self.sg.at[slot], self.sems.at[slot]),
                pltpu.make_async_copy(self.wgu_hbm.at[e, :, up_cols], self.su.at[slot], self.sems.at[slot]),
                pltpu.make_async_copy(self.wd_hbm.at[e, cols, :], self.sd.at[slot], self.sems.at[slot]))

    def request(self, e, j):
        for c in self._copies(e, j):
            c.start()

    def request_head(self, e):
        for j in range(WEIGHT_SLOTS):
            self.request(e, j)

    def land(self, e, j):
        slot = j % WEIGHT_SLOTS
        for c in self._copies(e, j):
            c.wait()
        self.wgb[j] = self.sg[slot].astype(BF16)
        self.wub[j] = self.su[slot].astype(BF16)
        self.wdb[j] = self.sd[slot].astype(BF16)
        if j + WEIGHT_SLOTS < self.nf:
            self.request(e, j + WEIGHT_SLOTS)

    def apply_chunk(self, xb_ref, acc_ref, j):
        xb = xb_ref[...]
        gt = jnp.dot(xb, self.wgb[j], preferred_element_type=F32)
        up = jnp.dot(xb, self.wub[j], preferred_element_type=F32)
        part = jnp.dot((_silu(gt) * up).astype(BF16), self.wdb[j], preferred_element_type=F32)
        if j == 0:
            acc_ref[...] = part
        else:
            acc_ref[...] += part

    @staticmethod
    def scratch_shapes(tf):
        nf, d = D_FF // tf, D_MODEL
        return [pltpu.VMEM((nf, d, tf), BF16), pltpu.VMEM((nf, d, tf), BF16), pltpu.VMEM((nf, tf, d), BF16),
                pltpu.VMEM((WEIGHT_SLOTS, d, tf), F32), pltpu.VMEM((WEIGHT_SLOTS, d, tf), F32),
                pltpu.VMEM((WEIGHT_SLOTS, tf, d), F32), pltpu.SemaphoreType.DMA((WEIGHT_SLOTS,))]


def _ffn_kernel(x_ref, wgu_hbm, wd_hbm, lng_ref, lnb_ref, o_ref, xb_ref, acc_ref,
                wgb_ref, wub_ref, wdb_ref, sg_ref, su_ref, sd_ref, sems, *, tf):
    i = pl.program_id(0)
    w = _SwigluWeights(wgu_hbm, wd_hbm, (wgb_ref, wub_ref, wdb_ref), (sg_ref, su_ref, sd_ref), sems, tf)
    xb_ref[...] = x_ref[...].astype(BF16)

    @pl.when(i == 0)
    def _():
        w.request_head(0)
        for j in range(w.nf):
            w.land(0, j)
            w.apply_chunk(xb_ref, acc_ref, j)

    @pl.when(i > 0)
    def _():
        for j in range(w.nf):
            w.apply_chunk(xb_ref, acc_ref, j)

    o_ref[...] = _layer_norm(DEEPNORM_ALPHA * x_ref[...] + acc_ref[...], lng_ref[...], lnb_ref[...])


def _ffn(x2d, w_gu, w_down, ln_g, ln_b, *, tm, tf):
    n, d = x2d.shape
    assert D_FF // tf >= WEIGHT_SLOTS
    return pl.pallas_call(
        functools.partial(_ffn_kernel, tf=tf),
        grid=(n // tm,),
        in_specs=[pl.BlockSpec((tm, d), lambda i: (i, 0)),
                  pl.BlockSpec(memory_space=pl.ANY), pl.BlockSpec(memory_space=pl.ANY),
                  _const_spec((1, d)), _const_spec((1, d))],
        out_specs=pl.BlockSpec((tm, d), lambda i: (i, 0)),
        out_shape=jax.ShapeDtypeStruct((n, d), F32),
        scratch_shapes=[pltpu.VMEM((tm, d), BF16), pltpu.VMEM((tm, d), F32)] + _SwigluWeights.scratch_shapes(tf),
        compiler_params=pltpu.CompilerParams(dimension_semantics=("arbitrary",),
                                             vmem_limit_bytes=VMEM_LIMIT_BYTES),
        name="ffn",
    )(x2d, w_gu[None], w_down[None], ln_g[None, :], ln_b[None, :])


L1_Q, L1_K, L1_V, L1_GATE, L1_AB = 0, 1024, 2048, 3072, 4096
L1_CONV = 3 * GDN_QK
L1_COLS = 4224
L1_BETA_LANE = GDN_HEADS
CONV_PAD = 8
L1_PROJ_BLOCK = 512
L1_CHUNK_GROUP = 2
ROUTE_LANES = LANES


def _pack_l1_w_in(w):
    offs = np.cumsum([0, L1_CONV, GDN_HEADS, GDN_HEADS, GDN_V])
    qkv, a_in, b_in, gate = [w[:, offs[i]:offs[i + 1]] for i in range(4)]
    ab = jnp.pad(jnp.concatenate([a_in, b_in], axis=1), ((0, 0), (0, LANES - 2 * GDN_HEADS)))
    return jnp.concatenate([qkv, gate, ab], axis=1)


def _l1_mixer_kernel(x_ref, w_in_ref, conv_ref, alog_ref, dtb_ref, cnorm_ref, w_out_ref, lng_ref, lnb_ref,
                     tril_ref, wr_hi_ref, wr_lo_ref, br_ref,
                     o_ref, route_ref, h2_ref, qkv_ref, mix_ref, s_ref, u_ref, w_ref, aqk_ref, ext_ref, *, tt):
    ti = pl.program_id(1)

    @pl.when(ti == 0)
    def _():
        s_ref[...] = jnp.zeros_like(s_ref)
        ext_ref[:, 0:CONV_PAD, :] = jnp.zeros((L1_CONV // LANES, CONV_PAD, LANES), F32)

    x = x_ref[...]
    xb = x.astype(BF16)
    h2_ref[...] = jnp.dot(xb, w_in_ref[:, L1_CONV:L1_COLS], preferred_element_type=F32)

    half = tt // 2
    lanes_per_block = L1_PROJ_BLOCK // LANES
    for blk in range(L1_CONV // L1_PROJ_BLOCK):
        h_blk = jnp.dot(xb, w_in_ref[:, blk * L1_PROJ_BLOCK:(blk + 1) * L1_PROJ_BLOCK], preferred_element_type=F32)
        for l in range(lanes_per_block):
            ext_ref[blk * lanes_per_block + l, CONV_PAD:CONV_PAD + tt, :] = h_blk[:, l * LANES:(l + 1) * LANES]
        for l in range(lanes_per_block):
            lb = blk * lanes_per_block + l
            lanes = slice(lb * LANES, (lb + 1) * LANES)
            for parity in range(2):
                conv = None
                for j in range(CONV_WIDTH):
                    first_row = CONV_PAD - (CONV_WIDTH - 1 - j) + parity
                    term = ext_ref[lb, pl.ds(first_row, half, stride=2), :] * conv_ref[j:j + 1, lanes]
                    conv = term if conv is None else conv + term
                act = _silu(conv)
                if lb * LANES < L1_V:
                    scale = GDN_DK ** -0.5 if lb * LANES < L1_K else 1.0
                    act = act * (lax.rsqrt(jnp.sum(act * act, axis=-1, keepdims=True) + L2_EPS) * scale)
                qkv_ref[lb, pl.ds(parity, half, stride=2), :] = act
            ext_ref[lb, 0:CONV_PAD, :] = ext_ref[lb, tt:tt + CONV_PAD, :]

    ab = h2_ref[:, L1_AB - L1_CONV:L1_AB - L1_CONV + LANES]
    g_blk = -jnp.exp(alog_ref[...]) * _softplus(ab + dtb_ref[...])
    beta_blk = _sigmoid(ab)
    gc_blk = _mm_exact_lhs01(tril_ref[...], g_blk)
    eg_blk = jnp.exp(gc_blk)

    ci = lax.broadcasted_iota(jnp.int32, (CHUNK, CHUNK), 0)
    cj = lax.broadcasted_iota(jnp.int32, (CHUNK, CHUNK), 1)
    incl = cj <= ci
    strict = cj < ci

    heads = range(GDN_HEADS)
    col = lambda base, hd: slice(base + hd * GDN_DK, base + (hd + 1) * GDN_DK)
    blk_of = lambda base, hd: base // LANES + hd
    n_chunks = tt // CHUNK

    s = [s_ref[hd] for hd in heads]
    for c0 in range(0, n_chunks, L1_CHUNK_GROUP):
        inst = [(c, hd) for c in range(c0, c0 + L1_CHUNK_GROUP) for hd in heads]
        n_i = range(len(inst))
        rows = [slice(c * CHUNK, (c + 1) * CHUNK) for c, _ in inst]
        gc_c = {c: gc_blk[c * CHUNK:(c + 1) * CHUNK] for c in range(c0, c0 + L1_CHUNK_GROUP)}
        gc_t = {c: gc_c[c].T for c in gc_c}
        kdec_scale = {c: jnp.exp(gc_c[c][CHUNK - 1:CHUNK, :] - gc_c[c]) for c in gc_c}
        q_h = [qkv_ref[blk_of(L1_Q, hd), rows[i], :] for i, (c, hd) in enumerate(inst)]
        k_h = [qkv_ref[blk_of(L1_K, hd), rows[i], :] for i, (c, hd) in enumerate(inst)]
        v_h = [qkv_ref[blk_of(L1_V, hd), rows[i], :] for i, (c, hd) in enumerate(inst)]
        beta = [beta_blk[rows[i], L1_BETA_LANE + hd:L1_BETA_LANE + hd + 1] for i, (c, hd) in enumerate(inst)]
        eg = [eg_blk[rows[i], hd:hd + 1] for i, (c, hd) in enumerate(inst)]
        decay = [jnp.exp(jnp.where(incl, gc_c[c][:, hd:hd + 1] - gc_t[c][hd:hd + 1, :], NEG_BIG)) for c, hd in inst]
        kb = [k_h[i] * beta[i] for i in n_i]
        low = [jnp.where(strict, _mm_nt(kb[i], k_h[i]) * decay[i], 0.0) for i in n_i]
        a_qk = [_mm_nt(q_h[i], k_h[i]) * decay[i] for i in n_i]
        a_m = [-low[i] for i in n_i]
        m = [_mm(low[i], low[i]) for i in n_i]
        for it in range(5):
            am = [_mm(a_m[i], m[i]) for i in n_i]
            a_m = [a_m[i] + m[i] + am[i] for i in n_i]
            if it < 4:
                m = [_mm(m[i], m[i]) for i in n_i]
        rhs = [jnp.concatenate([v_h[i] * beta[i], kb[i] * eg[i]], axis=1) for i in n_i]
        uw = [rhs[i] + _mm(a_m[i], rhs[i]) for i in n_i]
        for i, (c, hd) in enumerate(inst):
            u_ref[rows[i], col(0, hd)] = uw[i][:, 0:GDN_DV]
            w_ref[rows[i], col(0, hd)] = uw[i][:, GDN_DV:GDN_DV + GDN_DK]
            aqk_ref[rows[i], hd * LANES:hd * LANES + CHUNK] = a_qk[i]
            qkv_ref[blk_of(L1_Q, hd), rows[i], :] = q_h[i] * eg[i]
            qkv_ref[blk_of(L1_K, hd), rows[i], :] = k_h[i] * kdec_scale[c][:, hd:hd + 1]

        for c in range(c0, c0 + L1_CHUNK_GROUP):
            crow = slice(c * CHUNK, (c + 1) * CHUNK)
            e_last = jnp.exp(gc_blk[(c + 1) * CHUNK - 1:(c + 1) * CHUNK, :])
            ws_qs = [_mm(jnp.concatenate([w_ref[crow, col(0, hd)], qkv_ref[blk_of(L1_Q, hd), crow, :]], axis=0), s[hd])
                     for hd in heads]
            v_new = [u_ref[crow, col(0, hd)] - ws_qs[hd][0:CHUNK] for hd in heads]
            o_h = [ws_qs[hd][CHUNK:2 * CHUNK] + _mm(aqk_ref[crow, hd * LANES:hd * LANES + CHUNK], v_new[hd])
                   for hd in heads]
            s = [s[hd] * e_last[:, hd:hd + 1] + _mm_tn(qkv_ref[blk_of(L1_K, hd), crow, :], v_new[hd]) for hd in heads]
            for hd in heads:
                mix_ref[crow, col(0, hd)] = o_h[hd]
    for hd in heads:
        s_ref[hd] = s[hd]

    for hd in range(GDN_HEADS):
        sl = slice(hd * GDN_DV, (hd + 1) * GDN_DV)
        o = mix_ref[:, sl]
        o = o * lax.rsqrt(jnp.mean(o * o, axis=-1, keepdims=True) + NORM_EPS) * cnorm_ref[:, sl]
        mix_ref[:, sl] = o * _silu(h2_ref[:, L1_GATE - L1_CONV + hd * GDN_DV:L1_GATE - L1_CONV + (hd + 1) * GDN_DV])
    y = _mm(mix_ref[...], w_out_ref[...])
    x1 = _layer_norm(DEEPNORM_ALPHA * x + y, lng_ref[...], lnb_ref[...])
    o_ref[...] = x1

    x_hi = x1.astype(BF16)
    x_lo = (x1 - x_hi.astype(F32)).astype(BF16)
    logits = (jnp.dot(x_hi, wr_hi_ref[...], preferred_element_type=F32)
              + jnp.dot(x_hi, wr_lo_ref[...], preferred_element_type=F32)
              + jnp.dot(x_lo, wr_hi_ref[...], preferred_element_type=F32)) + br_ref[...]
    lane = lax.broadcasted_iota(jnp.int32, (tt, ROUTE_LANES), 1)
    lane_f = lane.astype(F32)
    logits = jnp.where(lane < N_EXPERTS, logits, NEG_BIG)
    m1 = jnp.max(logits, axis=-1, keepdims=True)
    i1 = jnp.min(jnp.where(logits == m1, lane_f, float(ROUTE_LANES)), axis=-1, keepdims=True)
    rest = jnp.where(lane_f == i1, NEG_BIG, logits)
    m2 = jnp.max(rest, axis=-1, keepdims=True)
    i2 = jnp.min(jnp.where(rest == m2, lane_f, float(ROUTE_LANES)), axis=-1, keepdims=True)
    e21 = jnp.exp(m2 - m1)
    g1 = 1.0 / (1.0 + e21)
    g2 = e21 * g1
    route_ref[...] = jnp.where(lane == 0, i1, jnp.where(lane == 1, i2, jnp.where(lane == 2, g1, jnp.where(lane == 3, g2, 0.0))))


def _l1_mixer(x, w_in, conv_w, a_log, dt_bias, c_norm, w_out, ln_g, ln_b, w_router, b_router, *, tt):
    bsz, seq, d = x.shape
    w_in_p = _pack_l1_w_in(w_in).astype(BF16)
    lane_pad = lambda v: jnp.pad(v[None, :], ((0, 0), (0, LANES - v.shape[0])))
    wr = jnp.pad(w_router, ((0, 0), (0, ROUTE_LANES - N_EXPERTS)))
    wr_hi = wr.astype(BF16)
    wr_lo = (wr - wr_hi.astype(F32)).astype(BF16)
    params = [w_in_p, conv_w, lane_pad(a_log), lane_pad(dt_bias), c_norm.reshape(1, GDN_V), w_out.astype(BF16),
              ln_g[None, :], ln_b[None, :], jnp.asarray(_chunk_tril_np(tt), BF16), wr_hi, wr_lo, lane_pad(b_router)]
    tile = lambda w: pl.BlockSpec((None, tt, w), lambda b, t: (b, t, 0))
    return pl.pallas_call(
        functools.partial(_l1_mixer_kernel, tt=tt),
        grid=(bsz, seq // tt),
        in_specs=[tile(d)] + [_const_spec(a.shape) for a in params],
        out_specs=[tile(d), tile(ROUTE_LANES)],
        out_shape=[jax.ShapeDtypeStruct((bsz, seq, d), F32), jax.ShapeDtypeStruct((bsz, seq, ROUTE_LANES), F32)],
        scratch_shapes=[pltpu.VMEM((tt, L1_COLS - L1_CONV), F32),
                        pltpu.VMEM((L1_CONV // LANES, tt, LANES), F32), pltpu.VMEM((tt, GDN_V), F32),
                        pltpu.VMEM((GDN_HEADS, GDN_DK, GDN_DV), F32),
                        pltpu.VMEM((tt, GDN_V), F32), pltpu.VMEM((tt, GDN_QK), F32),
                        pltpu.VMEM((tt, GDN_HEADS * LANES), F32),
                        pltpu.VMEM((L1_CONV // LANES, tt + CONV_PAD, LANES), F32)],
        compiler_params=pltpu.CompilerParams(dimension_semantics=("arbitrary", "arbitrary"),
                                             vmem_limit_bytes=VMEM_LIMIT_BYTES),
        name="l1_mixer",
    )(x, *params)


ID_ROWS = 4


def _rank_kernel(route_ref, stril_ref, ids_ref, counts_ref, carry_ref):
    @pl.when(pl.program_id(0) == 0)
    def _():
        carry_ref[...] = jnp.zeros_like(carry_ref)

    r = route_ref[...]
    tr = r.shape[0]
    e1, e2 = r[:, 0:1], r[:, 1:2]
    lane = lax.broadcasted_iota(jnp.int32, (tr, ROUTE_LANES), 1)
    lane_f = lane.astype(F32)
    oh1 = (lane_f == e1).astype(F32)
    oh2 = (lane_f == e2).astype(F32)
    both = oh1 + oh2
    before = jnp.dot(stril_ref[...], both.astype(BF16), preferred_element_type=F32) + carry_ref[...]
    rank1 = jnp.sum(oh1 * before, axis=-1, keepdims=True)
    rank2 = jnp.sum(oh2 * before, axis=-1, keepdims=True)
    table = jnp.where(lane == 0, e1, jnp.where(lane == 1, e2, jnp.where(lane == 2, rank1,
                      jnp.where(lane == 3, rank2, 0.0))))
    ids_ref[...] = table.T[0:ID_ROWS, :].astype(jnp.int32)
    carry_ref[...] += jnp.sum(both, axis=0, keepdims=True)
    counts_ref[...] = carry_ref[...]


def _rank(route2d, *, tr):
    n = route2d.shape[0]
    i = np.arange(tr)
    stril = jnp.asarray((i[None, :] < i[:, None]).astype(np.float32), BF16)
    return pl.pallas_call(
        _rank_kernel,
        grid=(n // tr,),
        in_specs=[pl.BlockSpec((tr, ROUTE_LANES), lambda i: (i, 0)), _const_spec((tr, tr))],
        out_specs=[pl.BlockSpec((ID_ROWS, tr), lambda i: (0, i)), _const_spec((1, ROUTE_LANES))],
        out_shape=[jax.ShapeDtypeStruct((ID_ROWS, n), jnp.int32), jax.ShapeDtypeStruct((1, ROUTE_LANES), F32)],
        scratch_shapes=[pltpu.VMEM((1, ROUTE_LANES), F32)],
        compiler_params=pltpu.CompilerParams(dimension_semantics=("arbitrary",)),
        name="moe_rank",
    )(route2d, stril)


ROW_SUB = D_MODEL // LANES


def _to_row_tiles(dst_ref, src, n):
    for lb in range(ROW_SUB):
        dst_ref[pl.ds(lb, n, stride=ROW_SUB), :] = src[:, lb * LANES:(lb + 1) * LANES]


def _from_row_tiles(src_ref, n):
    return [src_ref[pl.ds(lb, n, stride=ROW_SUB), :] for lb in range(ROW_SUB)]


def _tile_copy(src_ref, src_sub, dst_ref, dst_sub, sem):
    src = src_ref.at[pl.ds(pl.multiple_of(src_sub, ROW_SUB), ROW_SUB), :]
    dst = dst_ref.at[pl.ds(pl.multiple_of(dst_sub, ROW_SUB), ROW_SUB), :]
    return pltpu.make_async_copy(src, dst, sem)


def _wait_rows(hbm_ref, buf_ref, sem, n, copies):
    for _ in range(copies):
        pltpu.make_async_copy(hbm_ref.at[pl.ds(0, n * ROW_SUB), :], buf_ref, sem).wait()


ZERO_ROWS = 64


def _dispatch_kernel(zfill_ref, p0_ref, p1_ref, x_ref, xs_hbm, xt0_ref, xt1_ref, zero_ref, sems, zsem, *, td):
    p_refs = (p0_ref, p1_ref)
    i = pl.program_id(0)
    last = pl.num_programs(0) - 1

    @pl.when(i == 0)
    def _():
        zero_ref[...] = jnp.zeros_like(zero_ref)
        n_ranges = zfill_ref.shape[0] // 2

        def zero_copy(start_sub, r):
            dst = xs_hbm.at[pl.ds(pl.multiple_of(start_sub + r * (ZERO_ROWS * ROW_SUB), ROW_SUB),
                                  ZERO_ROWS * ROW_SUB), :]
            return pltpu.make_async_copy(zero_ref, dst, zsem)

        def fill(e, wait):
            def body(r, carry):
                copy = zero_copy(zfill_ref[e], r)
                if wait:
                    copy.wait()
                else:
                    copy.start()
                return carry

            lax.fori_loop(0, zfill_ref[n_ranges + e], body, 0)

        for e in range(n_ranges):
            fill(e, wait=False)
        for e in range(n_ranges):
            fill(e, wait=True)

    def step(xt_ref, sem):
        @pl.when(i >= 2)
        def _():
            _wait_rows(xs_hbm, xt_ref, sem, td, TOP_K)

        _to_row_tiles(xt_ref, x_ref[...], td)

        def issue(t, carry):
            for k in range(TOP_K):
                _tile_copy(xt_ref, t * ROW_SUB, xs_hbm, p_refs[k][t], sem).start(priority=k)
            return carry

        lax.fori_loop(0, td, issue, 0, unroll=8)

    for slot, (xt_ref, sem) in enumerate(((xt0_ref, sems.at[0]), (xt1_ref, sems.at[1]))):
        @pl.when(i % 2 == slot)
        def _():
            step(xt_ref, sem)

    @pl.when(i == last)
    def _():
        _wait_rows(xs_hbm, xt0_ref, sems.at[0], td, TOP_K)
        _wait_rows(xs_hbm, xt1_ref, sems.at[1], td, TOP_K)


def _dispatch(zfill, pos, x2d, n_rows, *, td):
    n, d = x2d.shape
    smem_rows = pl.BlockSpec((td,), lambda i, zf: (i,), memory_space=pltpu.SMEM)
    assert n // td >= 2
    grid_spec = pltpu.PrefetchScalarGridSpec(
        num_scalar_prefetch=1,
        grid=(n // td,),
        in_specs=[smem_rows, smem_rows, pl.BlockSpec((td, d), lambda i, zf: (i, 0))],
        out_specs=pl.BlockSpec(memory_space=pl.ANY),
        scratch_shapes=[pltpu.VMEM((td * ROW_SUB, LANES), F32), pltpu.VMEM((td * ROW_SUB, LANES), F32),
                        pltpu.VMEM((ZERO_ROWS * ROW_SUB, LANES), F32),
                        pltpu.SemaphoreType.DMA((2,)), pltpu.SemaphoreType.DMA(())],
    )
    return pl.pallas_call(
        functools.partial(_dispatch_kernel, td=td),
        grid_spec=grid_spec,
        out_shape=jax.ShapeDtypeStruct((n_rows * ROW_SUB, LANES), F32),
        compiler_params=pltpu.CompilerParams(dimension_semantics=("arbitrary",)),
        name="moe_dispatch",
    )(zfill, pos[0], pos[1], x2d)


def _gmm_kernel(te_ref, used_ref, first_ref, x_ref, wgu_hbm, wd_hbm, o_ref,
                xb_ref, acc_ref, wgb_ref, wub_ref, wdb_ref, sg_ref, su_ref, sd_ref, sems, *, tm, tf):
    i = pl.program_id(0)
    n_tiles = pl.num_programs(0)
    w = _SwigluWeights(wgu_hbm, wd_hbm, (wgb_ref, wub_ref, wdb_ref), (sg_ref, su_ref, sd_ref), sems, tf)
    active = i < used_ref[0]
    is_first = first_ref[i] == 1

    @pl.when(active)
    def _():
        for lb, blk in enumerate(_from_row_tiles(x_ref, tm)):
            xb_ref[:, lb * LANES:(lb + 1) * LANES] = blk.astype(BF16)

    @pl.when(jnp.logical_and(active, is_first))
    def _():
        e = te_ref[i]

        @pl.when(i == 0)
        def _():
            w.request_head(e)

        for j in range(w.nf):
            w.land(e, j)
            w.apply_chunk(xb_ref, acc_ref, j)

    @pl.when(jnp.logical_and(active, jnp.logical_not(is_first)))
    def _():
        for j in range(w.nf):
            w.apply_chunk(xb_ref, acc_ref, j)

    @pl.when(active)
    def _():
        _to_row_tiles(o_ref, acc_ref[...], tm)
        nxt = jnp.minimum(i + 1, n_tiles - 1)

        @pl.when(jnp.logical_and(i + 1 < n_tiles, first_ref[nxt] == 1))
        def _():
            w.request_head(te_ref[nxt])

    @pl.when(jnp.logical_not(active))
    def _():
        o_ref[...] = jnp.zeros_like(o_ref)


def _gmm(tile_expert, n_used, tile_first, xs, w_gu, w_down, *, tm, tf):
    d = D_MODEL
    n_rows = xs.shape[0] // ROW_SUB
    nf = D_FF // tf
    n_tiles = n_rows // tm
    assert nf >= WEIGHT_SLOTS
    grid_spec = pltpu.PrefetchScalarGridSpec(
        num_scalar_prefetch=3,
        grid=(n_tiles,),
        in_specs=[pl.BlockSpec((tm * ROW_SUB, LANES), lambda i, te, used, first: (jnp.minimum(i, used[0] - 1), 0)),
                  pl.BlockSpec(memory_space=pl.ANY), pl.BlockSpec(memory_space=pl.ANY)],
        out_specs=pl.BlockSpec((tm * ROW_SUB, LANES), lambda i, te, used, first: (i, 0)),
        scratch_shapes=[pltpu.VMEM((tm, d), BF16), pltpu.VMEM((tm, d), F32)] + _SwigluWeights.scratch_shapes(tf),
    )
    return pl.pallas_call(
        functools.partial(_gmm_kernel, tm=tm, tf=tf),
        grid_spec=grid_spec,
        out_shape=jax.ShapeDtypeStruct((n_rows * ROW_SUB, LANES), F32),
        compiler_params=pltpu.CompilerParams(dimension_semantics=("arbitrary",),
                                             vmem_limit_bytes=VMEM_LIMIT_BYTES),
        name="moe_gmm",
    )(tile_expert, n_used, tile_first, xs, w_gu, w_down)


def _combine_kernel(p0_ref, p1_ref, p0_next_ref, p1_next_ref, x_ref, route_ref, lng_ref, lnb_ref, y_hbm, o_ref,
                    ya0_ref, yb0_ref, ya1_ref, yb1_ref, sems, *, tc):
    i = pl.program_id(0)
    n = pl.num_programs(0)
    slots = ((ya0_ref, yb0_ref, sems.at[0]), (ya1_ref, yb1_ref, sems.at[1]))

    def issue(p_refs, slot):
        ya_ref, yb_ref, sem = slots[slot]

        def body(t, carry):
            for k, (p_ref, buf) in enumerate(zip(p_refs, (ya_ref, yb_ref))):
                _tile_copy(y_hbm, p_ref[t], buf, t * ROW_SUB, sem).start(priority=k)
            return carry

        lax.fori_loop(0, tc, body, 0, unroll=8)

    @pl.when(i == 0)
    def _():
        issue((p0_ref, p1_ref), 0)

    for slot in range(2):
        @pl.when(jnp.logical_and(i + 1 < n, (i + 1) % 2 == slot))
        def _():
            issue((p0_next_ref, p1_next_ref), slot)

    for slot in range(2):
        @pl.when(i % 2 == slot)
        def _():
            ya_ref, yb_ref, sem = slots[slot]
            _wait_rows(y_hbm, ya_ref, sem, tc, TOP_K)
            r = route_ref[...]
            g1, g2 = r[:, 2:3], r[:, 3:4]
            y = jnp.concatenate(
                [g1 * a + g2 * b for a, b in zip(_from_row_tiles(ya_ref, tc), _from_row_tiles(yb_ref, tc))], axis=1)
            o_ref[...] = _layer_norm(DEEPNORM_ALPHA * x_ref[...] + y, lng_ref[...], lnb_ref[...])


def _combine(pos, x2d, route2d, ys, ln_g, ln_b, *, tc):
    n, d = x2d.shape
    steps = n // tc
    smem_rows = pl.BlockSpec((tc,), lambda i: (i,), memory_space=pltpu.SMEM)
    smem_next = pl.BlockSpec((tc,), lambda i: (jnp.minimum(i + 1, steps - 1),), memory_space=pltpu.SMEM)
    return pl.pallas_call(
        functools.partial(_combine_kernel, tc=tc),
        grid=(steps,),
        in_specs=[smem_rows, smem_rows, smem_next, smem_next,
                  pl.BlockSpec((tc, d), lambda i: (i, 0)),
                  pl.BlockSpec((tc, ROUTE_LANES), lambda i: (i, 0)),
                  _const_spec((1, d)), _const_spec((1, d)),
                  pl.BlockSpec(memory_space=pl.ANY)],
        out_specs=pl.BlockSpec((tc, d), lambda i: (i, 0)),
        scratch_shapes=[pltpu.VMEM((tc * ROW_SUB, LANES), F32) for _ in range(2 * TOP_K)]
                       + [pltpu.SemaphoreType.DMA((2,))],
        out_shape=jax.ShapeDtypeStruct((n, d), F32),
        compiler_params=pltpu.CompilerParams(dimension_semantics=("arbitrary",),
                                             vmem_limit_bytes=VMEM_LIMIT_BYTES),
        name="moe_combine",
    )(pos[0], pos[1], pos[0], pos[1], x2d, route2d, ln_g[None, :], ln_b[None, :], ys)


def _moe(x2d, route2d, w_gu, w_down, ln_g, ln_b, *, tm, tf, tr, td, tc):
    n, d = x2d.shape
    ids, counts = _rank(route2d, tr=tr)
    cnt = counts[0, :N_EXPERTS].astype(jnp.int32)
    tiles_per = (cnt + tm - 1) // tm
    tile_end = jnp.cumsum(tiles_per)
    offs = (tile_end - tiles_per) * tm
    n_tiles = (n * TOP_K) // tm + N_EXPERTS
    n_used = tile_end[-1:]
    t_idx = jnp.arange(n_tiles, dtype=jnp.int32)
    tile_expert = jnp.sum(jnp.minimum(t_idx, n_used - 1)[:, None] >= tile_end[None, :], axis=1).astype(jnp.int32)
    tile_start = tile_end - tiles_per
    tile_first = jnp.logical_and(t_idx == tile_start[tile_expert], t_idx < n_used).astype(jnp.int32)
    group_start = sum(jnp.where(ids[0:TOP_K] == e, offs[e], 0) for e in range(N_EXPERTS))
    pos = (group_start + ids[TOP_K:2 * TOP_K]) * ROW_SUB
    pad_start = ((offs + cnt) // ZERO_ROWS) * ZERO_ROWS
    pad_blocks = (offs + tiles_per * tm - pad_start) // ZERO_ROWS
    tail_start = n_used * tm
    tail_blocks = (n_tiles - n_used) * (tm // ZERO_ROWS)
    zfill = jnp.concatenate([pad_start * ROW_SUB, tail_start * ROW_SUB, pad_blocks, tail_blocks]).astype(jnp.int32)
    xs = _dispatch(zfill, pos, x2d, n_tiles * tm, td=td)
    ys = _gmm(tile_expert, n_used, tile_first, xs, w_gu, w_down, tm=tm, tf=tf)
    return _combine(pos, x2d, route2d, ys, ln_g, ln_b, tc=tc)


def kernel(x, positions, ab_w_in, gla_w_gate2, gla_b_gate, gla_norm, ret_norm, ab_w_out, ab_ln1_g, ab_ln1_b, ffn_w_gu, ffn_w_down, ab_ln2_g, ab_ln2_b, c_w_in, c_conv_w, c_a_log, c_dt_bias, c_norm, c_w_out, c_ln1_g, c_ln1_b, moe_w_router, moe_b_router, moe_w_gu, moe_w_down, c_ln2_g, c_ln2_b):
    bsz, seq, d = x.shape
    for layer in range(DEPTH):
        i = layer // 2
        if layer % 2 == 0:
            x = _l0_mixer(x, positions, ab_w_in[i], gla_w_gate2[i], gla_b_gate[i], gla_norm[i], ret_norm[i],
                          ab_w_out[i], ab_ln1_g[i], ab_ln1_b[i], tt=L0_TIME_TILE)
            x = _ffn(x.reshape(bsz * seq, d), ffn_w_gu[i], ffn_w_down[i], ab_ln2_g[i], ab_ln2_b[i],
                     tm=FFN_ROW_TILE, tf=FF_TILE).reshape(bsz, seq, d)
        else:
            x, route = _l1_mixer(x, c_w_in[i], c_conv_w[i], c_a_log[i], c_dt_bias[i], c_norm[i], c_w_out[i],
                                 c_ln1_g[i], c_ln1_b[i], moe_w_router[i], moe_b_router[i], tt=L1_TIME_TILE)
            x = _moe(x.reshape(bsz * seq, d), route.reshape(bsz * seq, ROUTE_LANES), moe_w_gu[i], moe_w_down[i],
                     c_ln2_g[i], c_ln2_b[i], tm=MOE_ROW_TILE, tf=FF_TILE, tr=ROUTE_TILE, td=ROUTE_TILE,
                     tc=ROUTE_TILE).reshape(bsz, seq, d)
    return x
```

```python
import functools

import numpy as np
import jax
import jax.numpy as jnp
from jax import lax
from jax.experimental import pallas as pl
from jax.experimental.pallas import tpu as pltpu

F32 = jnp.float32
BF16 = jnp.bfloat16

D_MODEL = 1024
DEPTH = 2
CHUNK = 64
GLA_HEADS, GLA_DK, GLA_DV, GLA_GATE_RANK, GLA_TAU = 4, 64, 128, 16, 16.0
RET_HEADS, RET_DK, RET_DV = 4, 64, 128
ROPE_BASE = 10000.0
GDN_HEADS, GDN_DK, GDN_DV = 8, 128, 128
CONV_WIDTH = 4
D_FF = 3584
N_EXPERTS = 8
TOP_K = 2
NORM_EPS = 1e-5
L2_EPS = 1e-6
DEEPNORM_ALPHA = (2.0 * DEPTH) ** 0.25

GLA_QK = GLA_HEADS * GLA_DK
GLA_V = GLA_HEADS * GLA_DV
RET_QK = RET_HEADS * RET_DK
RET_V = RET_HEADS * RET_DV
GDN_QK = GDN_HEADS * GDN_DK
GDN_V = GDN_HEADS * GDN_DV

LANES = 128
VMEM_LIMIT_BYTES = 56 * 1024 * 1024
NEG_BIG = -1e30

L0_TIME_TILE = 256
L1_TIME_TILE = 256
FFN_ROW_TILE = 512
MOE_ROW_TILE = 512
FF_TILE = 512
ROUTE_TILE = 1024


def _mm(a, b):
    return jnp.dot(a.astype(BF16), b.astype(BF16), preferred_element_type=F32)


def _mm_nt(a, b):
    return lax.dot_general(a.astype(BF16), b.astype(BF16), (((1,), (1,)), ((), ())),
                           preferred_element_type=F32)


def _mm_tn(a, b):
    return lax.dot_general(a.astype(BF16), b.astype(BF16), (((0,), (0,)), ((), ())),
                           preferred_element_type=F32)


def _split3(x):
    hi = x.astype(BF16)
    r1 = x - hi.astype(F32)
    mid = r1.astype(BF16)
    lo = (r1 - mid.astype(F32)).astype(BF16)
    return hi, mid, lo


def _mm_exact_lhs01(m01, x):
    hi, mid, lo = _split3(x)
    return (jnp.dot(m01, hi, preferred_element_type=F32)
            + jnp.dot(m01, mid, preferred_element_type=F32)
            + jnp.dot(m01, lo, preferred_element_type=F32))


def _sigmoid(x):
    return 1.0 / (1.0 + jnp.exp(-x))


def _silu(x):
    return x * _sigmoid(x)


def _softplus(x):
    return jnp.maximum(x, 0.0) + jnp.log(1.0 + jnp.exp(-jnp.abs(x)))


def _layer_norm(x, g, b):
    mu = jnp.mean(x, axis=-1, keepdims=True)
    xc = x - mu
    var = jnp.mean(xc * xc, axis=-1, keepdims=True)
    return xc * lax.rsqrt(var + NORM_EPS) * g + b


def _chunk_tril_np(tt):
    i = np.arange(tt)
    same = (i[:, None] // CHUNK) == (i[None, :] // CHUNK)
    return (same & (i[None, :] <= i[:, None])).astype(np.float32)


def _const_spec(shape):
    nd = len(shape)
    return pl.BlockSpec(shape, lambda *_: (0,) * nd)


L0_GQ, L0_GK, L0_GV, L0_GR = 0, 256, 512, 1024
L0_RQ, L0_RK, L0_RV, L0_RG = 1536, 1792, 2048, 2560
L0_GA = 3072
L0_COLS = 3200
L0_CHUNK_GROUP = 1


def _pack_l0_w_in(w):
    offs = np.cumsum([0, GLA_QK, GLA_QK, GLA_V, GLA_GATE_RANK, GLA_V, RET_QK, RET_QK, RET_V, RET_V])
    gq, gk, gv, ga, gr, rq, rk, rv, rg = [w[:, offs[i]:offs[i + 1]] for i in range(9)]
    ga = jnp.pad(ga, ((0, 0), (0, LANES - GLA_GATE_RANK)))
    return jnp.concatenate([gq, gk, gv, gr, rq, rk, rv, rg, ga], axis=1)


def _ret_tables():
    h = np.arange(RET_HEADS, dtype=np.float64)
    log_gamma = np.log(1.0 - 2.0 ** (-5.0 - h))
    pos = np.arange(CHUNK, dtype=np.float64)
    diff = pos[:, None] - pos[None, :]
    dmat = np.where(diff >= 0, np.exp(log_gamma[:, None, None] * np.maximum(diff, 0.0)), 0.0)
    xi = np.exp(log_gamma[None, :] * (pos[:, None] + 1.0))
    zeta = np.exp(log_gamma[None, :] * (CHUNK - 1.0 - pos[:, None]))
    decay = np.exp(log_gamma * CHUNK)
    xi_full = np.repeat(xi, RET_DV, axis=1)
    zeta_full = np.repeat(zeta, RET_DK, axis=1)
    decay_full = np.repeat(decay, RET_DK)[None, :]
    return (dmat.astype(np.float32), xi_full.astype(np.float32), zeta_full.astype(np.float32),
            decay_full.astype(np.float32))


def _rope_tables():
    half = RET_DK // 2
    inv_freq = ROPE_BASE ** (-np.arange(0, RET_DK, 2, dtype=np.float32) / RET_DK)
    per_head = np.concatenate([inv_freq, inv_freq])
    freq_full = np.tile(per_head, RET_HEADS)[None, :].astype(np.float32)
    sign = np.tile(np.concatenate([-np.ones(half), np.ones(half)]), RET_HEADS)[None, :].astype(np.float32)
    return freq_full, sign


def _l0_mixer_kernel(x_ref, pos_ref, w_in_ref, wg2_ref, bg_ref, gnorm_ref, rnorm_ref, w_out_ref,
                     lng_ref, lnb_ref, tril_ref, dmat_ref, xi_ref, zeta_ref, rdecay_ref,
                     freq_ref, sign_ref, o_ref, h_ref, mix_ref, sg_ref, sr_ref, *, tt):
    ti = pl.program_id(1)

    @pl.when(ti == 0)
    def _():
        sg_ref[...] = jnp.zeros_like(sg_ref)
        sr_ref[...] = jnp.zeros_like(sr_ref)

    ang = pos_ref[...] * freq_ref[:, 0:LANES]
    cos = jnp.concatenate([jnp.cos(ang)] * (RET_QK // LANES), axis=1)
    sin = jnp.concatenate([jnp.sin(ang)] * (RET_QK // LANES), axis=1) * sign_ref[...]

    x = x_ref[...]
    h_ref[...] = _mm(x, w_in_ref[...])

    lane = lax.broadcasted_iota(jnp.int32, (1, LANES), 1)
    lo_half = lane < GLA_DK
    ci = lax.broadcasted_iota(jnp.int32, (CHUNK, CHUNK), 0)
    cj = lax.broadcasted_iota(jnp.int32, (CHUNK, CHUNK), 1)
    causal = cj <= ci

    z = _mm(h_ref[:, L0_GA:L0_GA + LANES], wg2_ref[...]) + bg_ref[...]
    log_a = -_softplus(-z) * (1.0 / GLA_TAU)
    b = _mm_exact_lhs01(tril_ref[...], log_a)
    eb = jnp.exp(b)
    q_dec = h_ref[:, L0_GQ:L0_GQ + GLA_QK] * (GLA_DK ** -0.5) * eb
    k_all = h_ref[:, L0_GK:L0_GK + GLA_QK]
    k_neg = k_all * jnp.exp(-b)

    half = RET_DK // 2
    lane256 = lax.broadcasted_iota(jnp.int32, (1, RET_QK), 1)
    first_half = (lane256 & (RET_DK - 1)) < half

    def rope(t):
        swapped = jnp.where(first_half, pltpu.roll(t, RET_QK - half, 1), pltpu.roll(t, half, 1))
        return t * cos + swapped * sin

    rq = rope(h_ref[:, L0_RQ:L0_RQ + RET_QK]) * (RET_DK ** -0.5)
    rk = rope(h_ref[:, L0_RK:L0_RK + RET_QK])

    n_chunks = tt // CHUNK
    heads = range(GLA_HEADS)
    pairs = range(GLA_HEADS // 2)
    pair_lanes = lambda hd: slice((hd // 2) * LANES, (hd // 2 + 1) * LANES)
    head_mask = lambda hd: lo_half if hd % 2 == 0 else jnp.logical_not(lo_half)
    vcol = lambda base, hd: slice(base + hd * GLA_DV, base + (hd + 1) * GLA_DV)
    sg = [sg_ref[p] for p in pairs]
    sr = [sr_ref[p] for p in pairs]
    for c0 in range(0, n_chunks, L0_CHUNK_GROUP):
        group = range(c0, c0 + L0_CHUNK_GROUP)
        rows = {c: slice(c * CHUNK, (c + 1) * CHUNK) for c in group}
        b_last = {c: b[(c + 1) * CHUNK - 1:(c + 1) * CHUNK, :] for c in group}
        k_end = {c: k_all[rows[c]] * jnp.exp(b_last[c] - b[rows[c]]) for c in group}
        rk_c = {c: rk[rows[c]] for c in group}
        rk_z = {c: rk_c[c] * zeta_ref[...] for c in group}
        inst = [(c, hd) for c in group for hd in heads]
        g_q = {ch: jnp.where(head_mask(ch[1]), q_dec[rows[ch[0]], pair_lanes(ch[1])], 0.0) for ch in inst}
        r_q = {ch: jnp.where(head_mask(ch[1]), rq[rows[ch[0]], pair_lanes(ch[1])], 0.0) for ch in inst}
        g_v = {ch: h_ref[rows[ch[0]], vcol(L0_GV, ch[1])] for ch in inst}
        r_v = {ch: h_ref[rows[ch[0]], vcol(L0_RV, ch[1])] for ch in inst}
        g_att = {(c, hd): jnp.where(causal, _mm_nt(g_q[c, hd], k_neg[rows[c], pair_lanes(hd)]), 0.0)
                 for c, hd in inst}
        r_att = {(c, hd): _mm_nt(r_q[c, hd], rk_c[c][:, pair_lanes(hd)]) * dmat_ref[hd] for c, hd in inst}
        g_d = {(c, hd): _mm_tn(g_v[c, hd], k_end[c][:, pair_lanes(hd)]) for c, hd in inst}
        r_d = {(c, hd): _mm_tn(r_v[c, hd], rk_z[c][:, pair_lanes(hd)]) for c, hd in inst}
        sg_in, sr_in = {}, {}
        for c in group:
            sg_in[c], sr_in[c] = sg, sr
            dec_c = jnp.exp(b_last[c])
            sg = [sg[p] * dec_c[:, p * LANES:(p + 1) * LANES] + jnp.where(lo_half, g_d[c, 2 * p], g_d[c, 2 * p + 1])
                  for p in pairs]
            sr = [sr[p] * rdecay_ref[:, p * LANES:(p + 1) * LANES]
                  + jnp.where(lo_half, r_d[c, 2 * p], r_d[c, 2 * p + 1]) for p in pairs]
        g_o = {(c, hd): _mm(g_att[c, hd], g_v[c, hd]) + _mm_nt(g_q[c, hd], sg_in[c][hd // 2]) for c, hd in inst}
        r_o = {(c, hd): _mm(r_att[c, hd], r_v[c, hd])
               + _mm_nt(r_q[c, hd], sr_in[c][hd // 2]) * xi_ref[:, vcol(0, hd)] for c, hd in inst}
        for c, hd in inst:
            mix_ref[rows[c], vcol(0, hd)] = g_o[c, hd]
            mix_ref[rows[c], vcol(GLA_V, hd)] = r_o[c, hd]
    for p in pairs:
        sg_ref[p] = sg[p]
        sr_ref[p] = sr[p]

    for hd in range(GLA_HEADS):
        sl = slice(hd * GLA_DV, (hd + 1) * GLA_DV)
        o = mix_ref[:, sl]
        o = o * lax.rsqrt(jnp.mean(o * o, axis=-1, keepdims=True) + NORM_EPS) * gnorm_ref[:, sl]
        mix_ref[:, sl] = o * _silu(h_ref[:, L0_GR + hd * GLA_DV:L0_GR + (hd + 1) * GLA_DV])
    for hd in range(RET_HEADS):
        sl = slice(hd * RET_DV, (hd + 1) * RET_DV)
        o = mix_ref[:, GLA_V + hd * RET_DV:GLA_V + (hd + 1) * RET_DV]
        oc = o - jnp.mean(o, axis=-1, keepdims=True)
        o = oc * lax.rsqrt(jnp.mean(oc * oc, axis=-1, keepdims=True) + NORM_EPS) * rnorm_ref[:, sl]
        mix_ref[:, GLA_V + hd * RET_DV:GLA_V + (hd + 1) * RET_DV] = (
            o * _silu(h_ref[:, L0_RG + hd * RET_DV:L0_RG + (hd + 1) * RET_DV]))

    y = _mm(mix_ref[...], w_out_ref[...])
    o_ref[...] = _layer_norm(DEEPNORM_ALPHA * x + y, lng_ref[...], lnb_ref[...])


def _l0_mixer(x, positions, w_in, w_gate2, b_gate, gla_norm, ret_norm, w_out, ln_g, ln_b, *, tt):
    bsz, seq, d = x.shape
    w_in_p = _pack_l0_w_in(w_in).astype(BF16)
    wg2 = jnp.pad(w_gate2, ((0, LANES - GLA_GATE_RANK), (0, 0))).astype(BF16)
    dmat, xi_full, zeta_full, rdecay = _ret_tables()
    freq_full, sign = _rope_tables()
    pos_f = positions.astype(F32)[..., None]
    consts = [jnp.asarray(_chunk_tril_np(tt), BF16), jnp.asarray(dmat), jnp.asarray(xi_full),
              jnp.asarray(zeta_full), jnp.asarray(rdecay), jnp.asarray(freq_full), jnp.asarray(sign)]
    params = [w_in_p, wg2, b_gate[None, :], gla_norm.reshape(1, GLA_V), ret_norm.reshape(1, RET_V),
              w_out.astype(BF16), ln_g[None, :], ln_b[None, :]]
    tile = lambda w: pl.BlockSpec((None, tt, w), lambda b, t: (b, t, 0))
    return pl.pallas_call(
        functools.partial(_l0_mixer_kernel, tt=tt),
        grid=(bsz, seq // tt),
        in_specs=[tile(d), tile(1)] + [_const_spec(a.shape) for a in params + consts],
        out_specs=tile(d),
        out_shape=jax.ShapeDtypeStruct((bsz, seq, d), F32),
        scratch_shapes=[pltpu.VMEM((tt, L0_COLS), F32), pltpu.VMEM((tt, GLA_V + RET_V), F32),
                        pltpu.VMEM((GLA_HEADS // 2, GLA_DV, LANES), F32),
                        pltpu.VMEM((RET_HEADS // 2, RET_DV, LANES), F32)],
        compiler_params=pltpu.CompilerParams(dimension_semantics=("arbitrary", "arbitrary"),
                                             vmem_limit_bytes=VMEM_LIMIT_BYTES),
        name="l0_mixer",
    )(x, pos_f, *params, *consts)


WEIGHT_SLOTS = 2


class _SwigluWeights:
    def __init__(self, wgu_hbm, wd_hbm, resident, staging, sems, tf):
        self.wgu_hbm, self.wd_hbm = wgu_hbm, wd_hbm
        self.wgb, self.wub, self.wdb = resident
        self.sg, self.su, self.sd = staging
        self.sems, self.tf, self.nf = sems, tf, D_FF // tf

    def _copies(self, e, j):
        slot = j % WEIGHT_SLOTS
        cols = pl.ds(j * self.tf, self.tf)
        up_cols = pl.ds(D_FF + j * self.tf, self.tf)
        return (pltpu.make_async_copy(self.wgu_hbm.at[e, :, cols], self.sg.at[slot], self.sems.at[slot]),
                pltpu.make_async_copy(self.wgu_hbm.at[e, :, up_cols], self.su.at[slot], self.sems.at[slot]),
                pltpu.make_async_copy(self.wd_hbm.at[e, cols, :], self.sd.at[slot], self.sems.at[slot]))

    def request(self, e, j):
        for c in self._copies(e, j):
            c.start()

    def request_head(self, e):
        for j in range(WEIGHT_SLOTS):
            self.request(e, j)

    def land(self, e, j):
        slot = j % WEIGHT_SLOTS
        for c in self._copies(e, j):
            c.wait()
        self.wgb[j] = self.sg[slot].astype(BF16)
        self.wub[j] = self.su[slot].astype(BF16)
        self.wdb[j] = self.sd[slot].astype(BF16)
        if j + WEIGHT_SLOTS < self.nf:
            self.request(e, j + WEIGHT_SLOTS)

    def apply_chunk(self, xb_ref, acc_ref, j):
        xb = xb_ref[...]
        gt = jnp.dot(xb, self.wgb[j], preferred_element_type=F32)
        up = jnp.dot(xb, self.wub[j], preferred_element_type=F32)
        part = jnp.dot((_silu(gt) * up).astype(BF16), self.wdb[j], preferred_element_type=F32)
        if j == 0:
            acc_ref[...] = part
        else:
            acc_ref[...] += part

    @staticmethod
    def scratch_shapes(tf):
        nf, d = D_FF // tf, D_MODEL
        return [pltpu.VMEM((nf, d, tf), BF16), pltpu.VMEM((nf, d, tf), BF16), pltpu.VMEM((nf, tf, d), BF16),
                pltpu.VMEM((WEIGHT_SLOTS, d, tf), F32), pltpu.VMEM((WEIGHT_SLOTS, d, tf), F32),
                pltpu.VMEM((WEIGHT_SLOTS, tf, d), F32), pltpu.SemaphoreType.DMA((WEIGHT_SLOTS,))]


def _ffn_kernel(x_ref, wgu_hbm, wd_hbm, lng_ref, lnb_ref, o_ref, xb_ref, acc_ref,
                wgb_ref, wub_ref, wdb_ref, sg_ref, su_ref, sd_ref, sems, *, tf):
    i = pl.program_id(0)
    w = _SwigluWeights(wgu_hbm, wd_hbm, (wgb_ref, wub_ref, wdb_ref), (sg_ref, su_ref, sd_ref), sems, tf)
    xb_ref[...] = x_ref[...].astype(BF16)

    @pl.when(i == 0)
    def _():
        w.request_head(0)
        for j in range(w.nf):
            w.land(0, j)
            w.apply_chunk(xb_ref, acc_ref, j)

    @pl.when(i > 0)
    def _():
        for j in range(w.nf):
            w.apply_chunk(xb_ref, acc_ref, j)

    o_ref[...] = _layer_norm(DEEPNORM_ALPHA * x_ref[...] + acc_ref[...], lng_ref[...], lnb_ref[...])


def _ffn(x2d, w_gu, w_down, ln_g, ln_b, *, tm, tf):
    n, d = x2d.shape
    assert D_FF // tf >= WEIGHT_SLOTS
    return pl.pallas_call(
        functools.partial(_ffn_kernel, tf=tf),
        grid=(n // tm,),
        in_specs=[pl.BlockSpec((tm, d), lambda i: (i, 0)),
                  pl.BlockSpec(memory_space=pl.ANY), pl.BlockSpec(memory_space=pl.ANY),
                  _const_spec((1, d)), _const_spec((1, d))],
        out_specs=pl.BlockSpec((tm, d), lambda i: (i, 0)),
        out_shape=jax.ShapeDtypeStruct((n, d), F32),
        scratch_shapes=[pltpu.VMEM((tm, d), BF16), pltpu.VMEM((tm, d), F32)] + _SwigluWeights.scratch_shapes(tf),
        compiler_params=pltpu.CompilerParams(dimension_semantics=("arbitrary",),
                                             vmem_limit_bytes=VMEM_LIMIT_BYTES),
        name="ffn",
    )(x2d, w_gu[None], w_down[None], ln_g[None, :], ln_b[None, :])


L1_Q, L1_K, L1_V, L1_GATE, L1_AB = 0, 1024, 2048, 3072, 4096
L1_CONV = 3 * GDN_QK
L1_COLS = 4224
L1_BETA_LANE = GDN_HEADS
CONV_PAD = 8
L1_PROJ_BLOCK = 512
L1_CHUNK_GROUP = 2
ROUTE_LANES = LANES


def _pack_l1_w_in(w):
    offs = np.cumsum([0, L1_CONV, GDN_HEADS, GDN_HEADS, GDN_V])
    qkv, a_in, b_in, gate = [w[:, offs[i]:offs[i + 1]] for i in range(4)]
    ab = jnp.pad(jnp.concatenate([a_in, b_in], axis=1), ((0, 0), (0, LANES - 2 * GDN_HEADS)))
    return jnp.concatenate([qkv, gate, ab], axis=1)


def _l1_mixer_kernel(x_ref, w_in_ref, conv_ref, alog_ref, dtb_ref, cnorm_ref, w_out_ref, lng_ref, lnb_ref,
                     tril_ref, wr_hi_ref, wr_lo_ref, br_ref,
                     o_ref, route_ref, h2_ref, qkv_ref, mix_ref, s_ref, u_ref, w_ref, aqk_ref, ext_ref, *, tt):
    ti = pl.program_id(1)

    @pl.when(ti == 0)
    def _():
        s_ref[...] = jnp.zeros_like(s_ref)
        ext_ref[:, 0:CONV_PAD, :] = jnp.zeros((L1_CONV // LANES, CONV_PAD, LANES), F32)

    x = x_ref[...]
    xb = x.astype(BF16)
    h2_ref[...] = jnp.dot(xb, w_in_ref[:, L1_GATE:L1_GATE + GDN_V], preferred_element_type=F32)

    half = tt // 2
    lanes_per_block = L1_PROJ_BLOCK // LANES
    for blk in range(L1_CONV // L1_PROJ_BLOCK):
        h_blk = jnp.dot(xb, w_in_ref[:, blk * L1_PROJ_BLOCK:(blk + 1) * L1_PROJ_BLOCK], preferred_element_type=F32)
        for l in range(lanes_per_block):
            ext_ref[blk * lanes_per_block + l, CONV_PAD:CONV_PAD + tt, :] = h_blk[:, l * LANES:(l + 1) * LANES]
        for l in range(lanes_per_block):
            lb = blk * lanes_per_block + l
            lanes = slice(lb * LANES, (lb + 1) * LANES)
            for parity in range(2):
                conv = None
                for j in range(CONV_WIDTH):
                    first_row = CONV_PAD - (CONV_WIDTH - 1 - j) + parity
                    term = ext_ref[lb, pl.ds(first_row, half, stride=2), :] * conv_ref[j:j + 1, lanes]
                    conv = term if conv is None else conv + term
                act = _silu(conv)
                if lb * LANES < L1_V:
                    scale = GDN_DK ** -0.5 if lb * LANES < L1_K else 1.0
                    act = act * (lax.rsqrt(jnp.sum(act * act, axis=-1, keepdims=True) + L2_EPS) * scale)
                qkv_ref[lb, pl.ds(parity, half, stride=2), :] = act
            ext_ref[lb, 0:CONV_PAD, :] = ext_ref[lb, tt:tt + CONV_PAD, :]

    ab = jnp.dot(xb, w_in_ref[:, L1_AB:L1_AB + LANES], preferred_element_type=F32)
    g_blk = -jnp.exp(alog_ref[...]) * _softplus(ab + dtb_ref[...])
    beta_blk = _sigmoid(ab)
    gc_blk = _mm_exact_lhs01(tril_ref[...], g_blk)
    eg_blk = jnp.exp(gc_blk)

    ci = lax.broadcasted_iota(jnp.int32, (CHUNK, CHUNK), 0)
    cj = lax.broadcasted_iota(jnp.int32, (CHUNK, CHUNK), 1)
    incl = cj <= ci
    strict = cj < ci

    heads = range(GDN_HEADS)
    col = lambda base, hd: slice(base + hd * GDN_DK, base + (hd + 1) * GDN_DK)
    blk_of = lambda base, hd: base // LANES + hd
    n_chunks = tt // CHUNK

    s = [s_ref[hd] for hd in heads]
    for c0 in range(0, n_chunks, L1_CHUNK_GROUP):
        inst = [(c, hd) for c in range(c0, c0 + L1_CHUNK_GROUP) for hd in heads]
        n_i = range(len(inst))
        rows = [slice(c * CHUNK, (c + 1) * CHUNK) for c, _ in inst]
        gc_c = {c: gc_blk[c * CHUNK:(c + 1) * CHUNK] for c in range(c0, c0 + L1_CHUNK_GROUP)}
        gc_t = {c: gc_c[c].T for c in gc_c}
        kdec_scale = {c: jnp.exp(gc_c[c][CHUNK - 1:CHUNK, :] - gc_c[c]) for c in gc_c}
        q_h = [qkv_ref[blk_of(L1_Q, hd), rows[i], :] for i, (c, hd) in enumerate(inst)]
        k_h = [qkv_ref[blk_of(L1_K, hd), rows[i], :] for i, (c, hd) in enumerate(inst)]
        v_h = [qkv_ref[blk_of(L1_V, hd), rows[i], :] for i, (c, hd) in enumerate(inst)]
        beta = [beta_blk[rows[i], L1_BETA_LANE + hd:L1_BETA_LANE + hd + 1] for i, (c, hd) in enumerate(inst)]
        eg = [eg_blk[rows[i], hd:hd + 1] for i, (c, hd) in enumerate(inst)]
        decay = [jnp.exp(jnp.where(incl, gc_c[c][:, hd:hd + 1] - gc_t[c][hd:hd + 1, :], NEG_BIG)) for c, hd in inst]
        kb = [k_h[i] * beta[i] for i in n_i]
        low = [jnp.where(strict, _mm_nt(kb[i], k_h[i]) * decay[i], 0.0) for i in n_i]
        a_qk = [_mm_nt(q_h[i], k_h[i]) * decay[i] for i in n_i]
        a_m = [-low[i] for i in n_i]
        m = [_mm(low[i], low[i]) for i in n_i]
        for it in range(5):
            am = [_mm(a_m[i], m[i]) for i in n_i]
            a_m = [a_m[i] + m[i] + am[i] for i in n_i]
            if it < 4:
                m = [_mm(m[i], m[i]) for i in n_i]
        rhs = [jnp.concatenate([v_h[i] * beta[i], kb[i] * eg[i]], axis=1) for i in n_i]
        uw = [rhs[i] + _mm(a_m[i], rhs[i]) for i in n_i]
        for i, (c, hd) in enumerate(inst):
            u_ref[rows[i], col(0, hd)] = uw[i][:, 0:GDN_DV]
            w_ref[rows[i], col(0, hd)] = uw[i][:, GDN_DV:GDN_DV + GDN_DK]
            aqk_ref[rows[i], hd * LANES:hd * LANES + CHUNK] = a_qk[i]
            qkv_ref[blk_of(L1_Q, hd), rows[i], :] = q_h[i] * eg[i]
            qkv_ref[blk_of(L1_K, hd), rows[i], :] = k_h[i] * kdec_scale[c][:, hd:hd + 1]

        for c in range(c0, c0 + L1_CHUNK_GROUP):
            crow = slice(c * CHUNK, (c + 1) * CHUNK)
            e_last = jnp.exp(gc_blk[(c + 1) * CHUNK - 1:(c + 1) * CHUNK, :])
            ws_qs = [_mm(jnp.concatenate([w_ref[crow, col(0, hd)], qkv_ref[blk_of(L1_Q, hd), crow, :]], axis=0), s[hd])
                     for hd in heads]
            v_new = [u_ref[crow, col(0, hd)] - ws_qs[hd][0:CHUNK] for hd in heads]
            o_h = [ws_qs[hd][CHUNK:2 * CHUNK] + _mm(aqk_ref[crow, hd * LANES:hd * LANES + CHUNK], v_new[hd])
                   for hd in heads]
            s = [s[hd] * e_last[:, hd:hd + 1] + _mm_tn(qkv_ref[blk_of(L1_K, hd), crow, :], v_new[hd]) for hd in heads]
            for hd in heads:
                mix_ref[crow, col(0, hd)] = o_h[hd]
    for hd in heads:
        s_ref[hd] = s[hd]

    for hd in range(GDN_HEADS):
        sl = slice(hd * GDN_DV, (hd + 1) * GDN_DV)
        o = mix_ref[:, sl]
        o = o * lax.rsqrt(jnp.mean(o * o, axis=-1, keepdims=True) + NORM_EPS) * cnorm_ref[:, sl]
        mix_ref[:, sl] = o * _silu(h2_ref[:, sl])
    y = _mm(mix_ref[...], w_out_ref[...])
    x1 = _layer_norm(DEEPNORM_ALPHA * x + y, lng_ref[...], lnb_ref[...])
    o_ref[...] = x1

    x_hi = x1.astype(BF16)
    x_lo = (x1 - x_hi.astype(F32)).astype(BF16)
    logits = (jnp.dot(x_hi, wr_hi_ref[...], preferred_element_type=F32)
              + jnp.dot(x_hi, wr_lo_ref[...], preferred_element_type=F32)
              + jnp.dot(x_lo, wr_hi_ref[...], preferred_element_type=F32)) + br_ref[...]
    lane = lax.broadcasted_iota(jnp.int32, (tt, ROUTE_LANES), 1)
    lane_f = lane.astype(F32)
    logits = jnp.where(lane < N_EXPERTS, logits, NEG_BIG)
    m1 = jnp.max(logits, axis=-1, keepdims=True)
    i1 = jnp.min(jnp.where(logits == m1, lane_f, float(ROUTE_LANES)), axis=-1, keepdims=True)
    rest = jnp.where(lane_f == i1, NEG_BIG, logits)
    m2 = jnp.max(rest, axis=-1, keepdims=True)
    i2 = jnp.min(jnp.where(rest == m2, lane_f, float(ROUTE_LANES)), axis=-1, keepdims=True)
    e21 = jnp.exp(m2 - m1)
    g1 = 1.0 / (1.0 + e21)
    g2 = e21 * g1
    route_ref[...] = jnp.where(lane == 0, i1, jnp.where(lane == 1, i2, jnp.where(lane == 2, g1, jnp.where(lane == 3, g2, 0.0))))


def _l1_mixer(x, w_in, conv_w, a_log, dt_bias, c_norm, w_out, ln_g, ln_b, w_router, b_router, *, tt):
    bsz, seq, d = x.shape
    w_in_p = _pack_l1_w_in(w_in).astype(BF16)
    lane_pad = lambda v: jnp.pad(v[None, :], ((0, 0), (0, LANES - v.shape[0])))
    wr = jnp.pad(w_router, ((0, 0), (0, ROUTE_LANES - N_EXPERTS)))
    wr_hi = wr.astype(BF16)
    wr_lo = (wr - wr_hi.astype(F32)).astype(BF16)
    params = [w_in_p, conv_w, lane_pad(a_log), lane_pad(dt_bias), c_norm.reshape(1, GDN_V), w_out.astype(BF16),
              ln_g[None, :], ln_b[None, :], jnp.asarray(_chunk_tril_np(tt), BF16), wr_hi, wr_lo, lane_pad(b_router)]
    tile = lambda w: pl.BlockSpec((None, tt, w), lambda b, t: (b, t, 0))
    return pl.pallas_call(
        functools.partial(_l1_mixer_kernel, tt=tt),
        grid=(bsz, seq // tt),
        in_specs=[tile(d)] + [_const_spec(a.shape) for a in params],
        out_specs=[tile(d), tile(ROUTE_LANES)],
        out_shape=[jax.ShapeDtypeStruct((bsz, seq, d), F32), jax.ShapeDtypeStruct((bsz, seq, ROUTE_LANES), F32)],
        scratch_shapes=[pltpu.VMEM((tt, GDN_V), F32),
                        pltpu.VMEM((L1_CONV // LANES, tt, LANES), F32), pltpu.VMEM((tt, GDN_V), F32),
                        pltpu.VMEM((GDN_HEADS, GDN_DK, GDN_DV), F32),
                        pltpu.VMEM((tt, GDN_V), F32), pltpu.VMEM((tt, GDN_QK), F32),
                        pltpu.VMEM((tt, GDN_HEADS * LANES), F32),
                        pltpu.VMEM((L1_CONV // LANES, tt + CONV_PAD, LANES), F32)],
        compiler_params=pltpu.CompilerParams(dimension_semantics=("arbitrary", "arbitrary"),
                                             vmem_limit_bytes=VMEM_LIMIT_BYTES),
        name="l1_mixer",
    )(x, *params)


ID_ROWS = 4


def _rank_kernel(route_ref, stril_ref, ids_ref, counts_ref, carry_ref):
    @pl.when(pl.program_id(0) == 0)
    def _():
        carry_ref[...] = jnp.zeros_like(carry_ref)

    r = route_ref[...]
    tr = r.shape[0]
    e1, e2 = r[:, 0:1], r[:, 1:2]
    lane = lax.broadcasted_iota(jnp.int32, (tr, ROUTE_LANES), 1)
    lane_f = lane.astype(F32)
    oh1 = (lane_f == e1).astype(F32)
    oh2 = (lane_f == e2).astype(F32)
    both = oh1 + oh2
    before = jnp.dot(stril_ref[...], both.astype(BF16), preferred_element_type=F32) + carry_ref[...]
    rank1 = jnp.sum(oh1 * before, axis=-1, keepdims=True)
    rank2 = jnp.sum(oh2 * before, axis=-1, keepdims=True)
    table = jnp.where(lane == 0, e1, jnp.where(lane == 1, e2, jnp.where(lane == 2, rank1,
                      jnp.where(lane == 3, rank2, 0.0))))
    ids_ref[...] = table.T[0:ID_ROWS, :].astype(jnp.int32)
    carry_ref[...] += jnp.sum(both, axis=0, keepdims=True)
    counts_ref[...] = carry_ref[...]


def _rank(route2d, *, tr):
    n = route2d.shape[0]
    i = np.arange(tr)
    stril = jnp.asarray((i[None, :] < i[:, None]).astype(np.float32), BF16)
    return pl.pallas_call(
        _rank_kernel,
        grid=(n // tr,),
        in_specs=[pl.BlockSpec((tr, ROUTE_LANES), lambda i: (i, 0)), _const_spec((tr, tr))],
        out_specs=[pl.BlockSpec((ID_ROWS, tr), lambda i: (0, i)), _const_spec((1, ROUTE_LANES))],
        out_shape=[jax.ShapeDtypeStruct((ID_ROWS, n), jnp.int32), jax.ShapeDtypeStruct((1, ROUTE_LANES), F32)],
        scratch_shapes=[pltpu.VMEM((1, ROUTE_LANES), F32)],
        compiler_params=pltpu.CompilerParams(dimension_semantics=("arbitrary",)),
        name="moe_rank",
    )(route2d, stril)


ROW_SUB = D_MODEL // LANES


def _to_row_tiles(dst_ref, src, n):
    for lb in range(ROW_SUB):
        dst_ref[pl.ds(lb, n, stride=ROW_SUB), :] = src[:, lb * LANES:(lb + 1) * LANES]


def _from_row_tiles(src_ref, n):
    return [src_ref[pl.ds(lb, n, stride=ROW_SUB), :] for lb in range(ROW_SUB)]


def _tile_copy(src_ref, src_sub, dst_ref, dst_sub, sem):
    src = src_ref.at[pl.ds(pl.multiple_of(src_sub, ROW_SUB), ROW_SUB), :]
    dst = dst_ref.at[pl.ds(pl.multiple_of(dst_sub, ROW_SUB), ROW_SUB), :]
    return pltpu.make_async_copy(src, dst, sem)


def _wait_rows(hbm_ref, buf_ref, sem, n, copies):
    for _ in range(copies):
        pltpu.make_async_copy(hbm_ref.at[pl.ds(0, n * ROW_SUB), :], buf_ref, sem).wait()


ZERO_ROWS = 64


def _dispatch_kernel(zfill_ref, p0_ref, p1_ref, x_ref, xs_hbm, xt0_ref, xt1_ref, zero_ref, sems, zsem, *, td):
    p_refs = (p0_ref, p1_ref)
    i = pl.program_id(0)
    last = pl.num_programs(0) - 1

    @pl.when(i == 0)
    def _():
        zero_ref[...] = jnp.zeros_like(zero_ref)
        n_ranges = zfill_ref.shape[0] // 2

        def zero_copy(start_sub, r):
            dst = xs_hbm.at[pl.ds(pl.multiple_of(start_sub + r * (ZERO_ROWS * ROW_SUB), ROW_SUB),
                                  ZERO_ROWS * ROW_SUB), :]
            return pltpu.make_async_copy(zero_ref, dst, zsem)

        def fill(e, wait):
            def body(r, carry):
                copy = zero_copy(zfill_ref[e], r)
                if wait:
                    copy.wait()
                else:
                    copy.start()
                return carry

            lax.fori_loop(0, zfill_ref[n_ranges + e], body, 0)

        for e in range(n_ranges):
            fill(e, wait=False)
        for e in range(n_ranges):
            fill(e, wait=True)

    def step(xt_ref, sem):
        @pl.when(i >= 2)
        def _():
            _wait_rows(xs_hbm, xt_ref, sem, td, TOP_K)

        _to_row_tiles(xt_ref, x_ref[...], td)

        def issue(t, carry):
            for k in range(TOP_K):
                _tile_copy(xt_ref, t * ROW_SUB, xs_hbm, p_refs[k][t], sem).start(priority=k)
            return carry

        lax.fori_loop(0, td, issue, 0, unroll=8)

    for slot, (xt_ref, sem) in enumerate(((xt0_ref, sems.at[0]), (xt1_ref, sems.at[1]))):
        @pl.when(i % 2 == slot)
        def _():
            step(xt_ref, sem)

    @pl.when(i == last)
    def _():
        _wait_rows(xs_hbm, xt0_ref, sems.at[0], td, TOP_K)
        _wait_rows(xs_hbm, xt1_ref, sems.at[1], td, TOP_K)


def _dispatch(zfill, pos, x2d, n_rows, *, td):
    n, d = x2d.shape
    smem_rows = pl.BlockSpec((td,), lambda i, zf: (i,), memory_space=pltpu.SMEM)
    assert n // td >= 2
    grid_spec = pltpu.PrefetchScalarGridSpec(
        num_scalar_prefetch=1,
        grid=(n // td,),
        in_specs=[smem_rows, smem_rows, pl.BlockSpec((td, d), lambda i, zf: (i, 0))],
        out_specs=pl.BlockSpec(memory_space=pl.ANY),
        scratch_shapes=[pltpu.VMEM((td * ROW_SUB, LANES), F32), pltpu.VMEM((td * ROW_SUB, LANES), F32),
                        pltpu.VMEM((ZERO_ROWS * ROW_SUB, LANES), F32),
                        pltpu.SemaphoreType.DMA((2,)), pltpu.SemaphoreType.DMA(())],
    )
    return pl.pallas_call(
        functools.partial(_dispatch_kernel, td=td),
        grid_spec=grid_spec,
        out_shape=jax.ShapeDtypeStruct((n_rows * ROW_SUB, LANES), F32),
        compiler_params=pltpu.CompilerParams(dimension_semantics=("arbitrary",)),
        name="moe_dispatch",
    )(zfill, pos[0], pos[1], x2d)


def _gmm_kernel(te_ref, used_ref, first_ref, x_ref, wgu_hbm, wd_hbm, o_ref,
                xb_ref, acc_ref, wgb_ref, wub_ref, wdb_ref, sg_ref, su_ref, sd_ref, sems, *, tm, tf):
    i = pl.program_id(0)
    n_tiles = pl.num_programs(0)
    w = _SwigluWeights(wgu_hbm, wd_hbm, (wgb_ref, wub_ref, wdb_ref), (sg_ref, su_ref, sd_ref), sems, tf)
    active = i < used_ref[0]
    is_first = first_ref[i] == 1

    @pl.when(active)
    def _():
        for lb, blk in enumerate(_from_row_tiles(x_ref, tm)):
            xb_ref[:, lb * LANES:(lb + 1) * LANES] = blk.astype(BF16)

    @pl.when(jnp.logical_and(active, is_first))
    def _():
        e = te_ref[i]

        @pl.when(i == 0)
        def _():
            w.request_head(e)

        for j in range(w.nf):
            w.land(e, j)
            w.apply_chunk(xb_ref, acc_ref, j)

    @pl.when(jnp.logical_and(active, jnp.logical_not(is_first)))
    def _():
        for j in range(w.nf):
            w.apply_chunk(xb_ref, acc_ref, j)

    @pl.when(active)
    def _():
        _to_row_tiles(o_ref, acc_ref[...], tm)
        nxt = jnp.minimum(i + 1, n_tiles - 1)

        @pl.when(jnp.logical_and(i + 1 < n_tiles, first_ref[nxt] == 1))
        def _():
            w.request_head(te_ref[nxt])

    @pl.when(jnp.logical_not(active))
    def _():
        o_ref[...] = jnp.zeros_like(o_ref)


def _gmm(tile_expert, n_used, tile_first, xs, w_gu, w_down, *, tm, tf):
    d = D_MODEL
    n_rows = xs.shape[0] // ROW_SUB
    nf = D_FF // tf
    n_tiles = n_rows // tm
    assert nf >= WEIGHT_SLOTS
    grid_spec = pltpu.PrefetchScalarGridSpec(
        num_scalar_prefetch=3,
        grid=(n_tiles,),
        in_specs=[pl.BlockSpec((tm * ROW_SUB, LANES), lambda i, te, used, first: (jnp.minimum(i, used[0] - 1), 0)),
                  pl.BlockSpec(memory_space=pl.ANY), pl.BlockSpec(memory_space=pl.ANY)],
        out_specs=pl.BlockSpec((tm * ROW_SUB, LANES), lambda i, te, used, first: (i, 0)),
        scratch_shapes=[pltpu.VMEM((tm, d), BF16), pltpu.VMEM((tm, d), F32)] + _SwigluWeights.scratch_shapes(tf),
    )
    return pl.pallas_call(
        functools.partial(_gmm_kernel, tm=tm, tf=tf),
        grid_spec=grid_spec,
        out_shape=jax.ShapeDtypeStruct((n_rows * ROW_SUB, LANES), F32),
        compiler_params=pltpu.CompilerParams(dimension_semantics=("arbitrary",),
                                             vmem_limit_bytes=VMEM_LIMIT_BYTES),
        name="moe_gmm",
    )(tile_expert, n_used, tile_first, xs, w_gu, w_down)


def _combine_kernel(p0_ref, p1_ref, p0_next_ref, p1_next_ref, x_ref, route_ref, lng_ref, lnb_ref, y_hbm, o_ref,
                    ya0_ref, yb0_ref, ya1_ref, yb1_ref, sems, *, tc):
    i = pl.program_id(0)
    n = pl.num_programs(0)
    slots = ((ya0_ref, yb0_ref, sems.at[0]), (ya1_ref, yb1_ref, sems.at[1]))

    def issue(p_refs, slot):
        ya_ref, yb_ref, sem = slots[slot]

        def body(t, carry):
            for k, (p_ref, buf) in enumerate(zip(p_refs, (ya_ref, yb_ref))):
                _tile_copy(y_hbm, p_ref[t], buf, t * ROW_SUB, sem).start(priority=k)
            return carry

        lax.fori_loop(0, tc, body, 0, unroll=8)

    @pl.when(i == 0)
    def _():
        issue((p0_ref, p1_ref), 0)

    for slot in range(2):
        @pl.when(jnp.logical_and(i + 1 < n, (i + 1) % 2 == slot))
        def _():
            issue((p0_next_ref, p1_next_ref), slot)

    for slot in range(2):
        @pl.when(i % 2 == slot)
        def _():
            ya_ref, yb_ref, sem = slots[slot]
            _wait_rows(y_hbm, ya_ref, sem, tc, TOP_K)
            r = route_ref[...]
            g1, g2 = r[:, 2:3], r[:, 3:4]
            y = jnp.concatenate(
                [g1 * a + g2 * b for a, b in zip(_from_row_tiles(ya_ref, tc), _from_row_tiles(yb_ref, tc))], axis=1)
            o_ref[...] = _layer_norm(DEEPNORM_ALPHA * x_ref[...] + y, lng_ref[...], lnb_ref[...])


def _combine(pos, x2d, route2d, ys, ln_g, ln_b, *, tc):
    n, d = x2d.shape
    steps = n // tc
    smem_rows = pl.BlockSpec((tc,), lambda i: (i,), memory_space=pltpu.SMEM)
    smem_next = pl.BlockSpec((tc,), lambda i: (jnp.minimum(i + 1, steps - 1),), memory_space=pltpu.SMEM)
    return pl.pallas_call(
        functools.partial(_combine_kernel, tc=tc),
        grid=(steps,),
        in_specs=[smem_rows, smem_rows, smem_next, smem_next,
                  pl.BlockSpec((tc, d), lambda i: (i, 0)),
                  pl.BlockSpec((tc, ROUTE_LANES), lambda i: (i, 0)),
                  _const_spec((1, d)), _const_spec((1, d)),
                  pl.BlockSpec(memory_space=pl.ANY)],
        out_specs=pl.BlockSpec((tc, d), lambda i: (i, 0)),
        scratch_shapes=[pltpu.VMEM((tc * ROW_SUB, LANES), F32) for _ in range(2 * TOP_K)]
                       + [pltpu.SemaphoreType.DMA((2,))],
        out_shape=jax.ShapeDtypeStruct((n, d), F32),
        compiler_params=pltpu.CompilerParams(dimension_semantics=("arbitrary",),
                                             vmem_limit_bytes=VMEM_LIMIT_BYTES),
        name="moe_combine",
    )(pos[0], pos[1], pos[0], pos[1], x2d, route2d, ln_g[None, :], ln_b[None, :], ys)


def _moe(x2d, route2d, w_gu, w_down, ln_g, ln_b, *, tm, tf, tr, td, tc):
    n, d = x2d.shape
    ids, counts = _rank(route2d, tr=tr)
    cnt = counts[0, :N_EXPERTS].astype(jnp.int32)
    tiles_per = (cnt + tm - 1) // tm
    tile_end = jnp.cumsum(tiles_per)
    offs = (tile_end - tiles_per) * tm
    n_tiles = (n * TOP_K) // tm + N_EXPERTS
    n_used = tile_end[-1:]
    t_idx = jnp.arange(n_tiles, dtype=jnp.int32)
    tile_expert = jnp.sum(jnp.minimum(t_idx, n_used - 1)[:, None] >= tile_end[None, :], axis=1).astype(jnp.int32)
    tile_start = tile_end - tiles_per
    tile_first = jnp.logical_and(t_idx == tile_start[tile_expert], t_idx < n_used).astype(jnp.int32)
    group_start = sum(jnp.where(ids[0:TOP_K] == e, offs[e], 0) for e in range(N_EXPERTS))
    pos = (group_start + ids[TOP_K:2 * TOP_K]) * ROW_SUB
    pad_start = ((offs + cnt) // ZERO_ROWS) * ZERO_ROWS
    pad_blocks = (offs + tiles_per * tm - pad_start) // ZERO_ROWS
    tail_start = n_used * tm
    tail_blocks = (n_tiles - n_used) * (tm // ZERO_ROWS)
    zfill = jnp.concatenate([pad_start * ROW_SUB, tail_start * ROW_SUB, pad_blocks, tail_blocks]).astype(jnp.int32)
    xs = _dispatch(zfill, pos, x2d, n_tiles * tm, td=td)
    ys = _gmm(tile_expert, n_used, tile_first, xs, w_gu, w_down, tm=tm, tf=tf)
    return _combine(pos, x2d, route2d, ys, ln_g, ln_b, tc=tc)


def kernel(x, positions, ab_w_in, gla_w_gate2, gla_b_gate, gla_norm, ret_norm, ab_w_out, ab_ln1_g, ab_ln1_b, ffn_w_gu, ffn_w_down, ab_ln2_g, ab_ln2_b, c_w_in, c_conv_w, c_a_log, c_dt_bias, c_norm, c_w_out, c_ln1_g, c_ln1_b, moe_w_router, moe_b_router, moe_w_gu, moe_w_down, c_ln2_g, c_ln2_b):
    bsz, seq, d = x.shape
    for layer in range(DEPTH):
        i = layer // 2
        if layer % 2 == 0:
            x = _l0_mixer(x, positions, ab_w_in[i], gla_w_gate2[i], gla_b_gate[i], gla_norm[i], ret_norm[i],
                          ab_w_out[i], ab_ln1_g[i], ab_ln1_b[i], tt=L0_TIME_TILE)
            x = _ffn(x.reshape(bsz * seq, d), ffn_w_gu[i], ffn_w_down[i], ab_ln2_g[i], ab_ln2_b[i],
                     tm=FFN_ROW_TILE, tf=FF_TILE).reshape(bsz, seq, d)
        else:
            x, route = _l1_mixer(x, c_w_in[i], c_conv_w[i], c_a_log[i], c_dt_bias[i], c_norm[i], c_w_out[i],
                                 c_ln1_g[i], c_ln1_b[i], moe_w_router[i], moe_b_router[i], tt=L1_TIME_TILE)
            x = _moe(x.reshape(bsz * seq, d), route.reshape(bsz * seq, ROUTE_LANES), moe_w_gu[i], moe_w_down[i],
                     c_ln2_g[i], c_ln2_b[i], tm=MOE_ROW_TILE, tf=FF_TILE, tr=ROUTE_TILE, td=ROUTE_TILE,
                     tc=ROUTE_TILE).reshape(bsz, seq, d)
    return x
```

```python
import functools

import numpy as np
import jax
import jax.numpy as jnp
from jax import lax
from jax.experimental import pallas as pl
from jax.experimental.pallas import tpu as pltpu

F32 = jnp.float32
BF16 = jnp.bfloat16

D_MODEL = 1024
DEPTH = 2
CHUNK = 64
GLA_HEADS, GLA_DK, GLA_DV, GLA_GATE_RANK, GLA_TAU = 4, 64, 128, 16, 16.0
RET_HEADS, RET_DK, RET_DV = 4, 64, 128
ROPE_BASE = 10000.0
GDN_HEADS, GDN_DK, GDN_DV = 8, 128, 128
CONV_WIDTH = 4
D_FF = 3584
N_EXPERTS = 8
TOP_K = 2
NORM_EPS = 1e-5
L2_EPS = 1e-6
DEEPNORM_ALPHA = (2.0 * DEPTH) ** 0.25

GLA_QK = GLA_HEADS * GLA_DK
GLA_V = GLA_HEADS * GLA_DV
RET_QK = RET_HEADS * RET_DK
RET_V = RET_HEADS * RET_DV
GDN_QK = GDN_HEADS * GDN_DK
GDN_V = GDN_HEADS * GDN_DV

LANES = 128
VMEM_LIMIT_BYTES = 56 * 1024 * 1024
NEG_BIG = -1e30

L0_TIME_TILE = 256
L1_TIME_TILE = 256
FFN_ROW_TILE = 512
MOE_ROW_TILE = 512
FF_TILE = 512
ROUTE_TILE = 512


def _mm(a, b):
    return jnp.dot(a.astype(BF16), b.astype(BF16), preferred_element_type=F32)


def _mm_nt(a, b):
    return lax.dot_general(a.astype(BF16), b.astype(BF16), (((1,), (1,)), ((), ())),
                           preferred_element_type=F32)


def _mm_tn(a, b):
    return lax.dot_general(a.astype(BF16), b.astype(BF16), (((0,), (0,)), ((), ())),
                           preferred_element_type=F32)


def _split3(x):
    hi = x.astype(BF16)
    r1 = x - hi.astype(F32)
    mid = r1.astype(BF16)
    lo = (r1 - mid.astype(F32)).astype(BF16)
    return hi, mid, lo


def _mm_exact_lhs01(m01, x):
    hi, mid, lo = _split3(x)
    return (jnp.dot(m01, hi, preferred_element_type=F32)
            + jnp.dot(m01, mid, preferred_element_type=F32)
            + jnp.dot(m01, lo, preferred_element_type=F32))


def _sigmoid(x):
    return 0.5 * jnp.tanh(0.5 * x) + 0.5


def _silu(x):
    return x * _sigmoid(x)


def _softplus(x):
    return jnp.maximum(x, 0.0) + jnp.log(1.0 + jnp.exp(-jnp.abs(x)))


def _layer_norm(x, g, b):
    mu = jnp.mean(x, axis=-1, keepdims=True)
    xc = x - mu
    var = jnp.mean(xc * xc, axis=-1, keepdims=True)
    return xc * lax.rsqrt(var + NORM_EPS) * g + b


def _chunk_tril_np(tt):
    i = np.arange(tt)
    same = (i[:, None] // CHUNK) == (i[None, :] // CHUNK)
    return (same & (i[None, :] <= i[:, None])).astype(np.float32)


def _const_spec(shape):
    nd = len(shape)
    return pl.BlockSpec(shape, lambda *_: (0,) * nd)


L0_GQ, L0_GK, L0_GV, L0_GR = 0, 256, 512, 1024
L0_RQ, L0_RK, L0_RV, L0_RG = 1536, 1792, 2048, 2560
L0_GA = 3072
L0_COLS = 3200
L0_CHUNK_GROUP = 1


def _pack_l0_w_in(w):
    offs = np.cumsum([0, GLA_QK, GLA_QK, GLA_V, GLA_GATE_RANK, GLA_V, RET_QK, RET_QK, RET_V, RET_V])
    gq, gk, gv, ga, gr, rq, rk, rv, rg = [w[:, offs[i]:offs[i + 1]] for i in range(9)]
    ga = jnp.pad(ga, ((0, 0), (0, LANES - GLA_GATE_RANK)))
    return jnp.concatenate([gq, gk, gv, gr, rq, rk, rv, rg, ga], axis=1)


def _ret_tables():
    h = np.arange(RET_HEADS, dtype=np.float64)
    log_gamma = np.log(1.0 - 2.0 ** (-5.0 - h))
    pos = np.arange(CHUNK, dtype=np.float64)
    diff = pos[:, None] - pos[None, :]
    dmat = np.where(diff >= 0, np.exp(log_gamma[:, None, None] * np.maximum(diff, 0.0)), 0.0)
    xi = np.exp(log_gamma[None, :] * (pos[:, None] + 1.0))
    zeta = np.exp(log_gamma[None, :] * (CHUNK - 1.0 - pos[:, None]))
    decay = np.exp(log_gamma * CHUNK)
    xi_full = np.repeat(xi, RET_DV, axis=1)
    zeta_full = np.repeat(zeta, RET_DK, axis=1)
    decay_full = np.repeat(decay, RET_DK)[None, :]
    return (dmat.astype(np.float32), xi_full.astype(np.float32), zeta_full.astype(np.float32),
            decay_full.astype(np.float32))


def _rope_tables():
    half = RET_DK // 2
    inv_freq = ROPE_BASE ** (-np.arange(0, RET_DK, 2, dtype=np.float32) / RET_DK)
    per_head = np.concatenate([inv_freq, inv_freq])
    freq_full = np.tile(per_head, RET_HEADS)[None, :].astype(np.float32)
    sign = np.tile(np.concatenate([-np.ones(half), np.ones(half)]), RET_HEADS)[None, :].astype(np.float32)
    return freq_full, sign


def _l0_mixer_kernel(x_ref, pos_ref, w_in_ref, wg2_ref, bg_ref, gnorm_ref, rnorm_ref, w_out_ref,
                     lng_ref, lnb_ref, tril_ref, dmat_ref, xi_ref, zeta_ref, rdecay_ref,
                     freq_ref, sign_ref, o_ref, h_ref, mix_ref, sg_ref, sr_ref, *, tt):
    ti = pl.program_id(1)

    @pl.when(ti == 0)
    def _():
        sg_ref[...] = jnp.zeros_like(sg_ref)
        sr_ref[...] = jnp.zeros_like(sr_ref)

    x = x_ref[...]
    h_ref[...] = _mm(x, w_in_ref[...])

    lane = lax.broadcasted_iota(jnp.int32, (1, LANES), 1)
    lo_half = lane < GLA_DK
    ci = lax.broadcasted_iota(jnp.int32, (CHUNK, CHUNK), 0)
    cj = lax.broadcasted_iota(jnp.int32, (CHUNK, CHUNK), 1)
    causal = cj <= ci

    z = _mm(h_ref[:, L0_GA:L0_GA + LANES], wg2_ref[...]) + bg_ref[...]
    log_a = -_softplus(-z) * (1.0 / GLA_TAU)
    b = _mm_exact_lhs01(tril_ref[...], log_a)
    eb = jnp.exp(b)
    q_dec = h_ref[:, L0_GQ:L0_GQ + GLA_QK] * (GLA_DK ** -0.5) * eb
    k_all = h_ref[:, L0_GK:L0_GK + GLA_QK]
    k_neg = k_all * jnp.exp(-b)

    ang = pos_ref[...] * freq_ref[:, 0:LANES]
    cos = jnp.concatenate([jnp.cos(ang)] * (RET_QK // LANES), axis=1)
    sin = jnp.concatenate([jnp.sin(ang)] * (RET_QK // LANES), axis=1) * sign_ref[...]
    half = RET_DK // 2
    lane256 = lax.broadcasted_iota(jnp.int32, (1, RET_QK), 1)
    first_half = (lane256 & (RET_DK - 1)) < half

    def rope(t):
        swapped = jnp.where(first_half, pltpu.roll(t, RET_QK - half, 1), pltpu.roll(t, half, 1))
        return t * cos + swapped * sin

    rq = rope(h_ref[:, L0_RQ:L0_RQ + RET_QK]) * (RET_DK ** -0.5)
    rk = rope(h_ref[:, L0_RK:L0_RK + RET_QK])

    n_chunks = tt // CHUNK
    heads = range(GLA_HEADS)
    pairs = range(GLA_HEADS // 2)
    pair_lanes = lambda hd: slice((hd // 2) * LANES, (hd // 2 + 1) * LANES)
    head_mask = lambda hd: lo_half if hd % 2 == 0 else jnp.logical_not(lo_half)
    vcol = lambda base, hd: slice(base + hd * GLA_DV, base + (hd + 1) * GLA_DV)
    sg = [sg_ref[p] for p in pairs]
    sr = [sr_ref[p] for p in pairs]
    for c0 in range(0, n_chunks, L0_CHUNK_GROUP):
        group = range(c0, c0 + L0_CHUNK_GROUP)
        rows = {c: slice(c * CHUNK, (c + 1) * CHUNK) for c in group}
        b_last = {c: b[(c + 1) * CHUNK - 1:(c + 1) * CHUNK, :] for c in group}
        k_end = {c: k_all[rows[c]] * jnp.exp(b_last[c] - b[rows[c]]) for c in group}
        rk_c = {c: rk[rows[c]] for c in group}
        rk_z = {c: rk_c[c] * zeta_ref[...] for c in group}
        inst = [(c, hd) for c in group for hd in heads]
        g_q = {ch: jnp.where(head_mask(ch[1]), q_dec[rows[ch[0]], pair_lanes(ch[1])], 0.0) for ch in inst}
        r_q = {ch: jnp.where(head_mask(ch[1]), rq[rows[ch[0]], pair_lanes(ch[1])], 0.0) for ch in inst}
        g_v = {ch: h_ref[rows[ch[0]], vcol(L0_GV, ch[1])] for ch in inst}
        r_v = {ch: h_ref[rows[ch[0]], vcol(L0_RV, ch[1])] for ch in inst}
        g_att = {(c, hd): jnp.where(causal, _mm_nt(g_q[c, hd], k_neg[rows[c], pair_lanes(hd)]), 0.0)
                 for c, hd in inst}
        r_att = {(c, hd): _mm_nt(r_q[c, hd], rk_c[c][:, pair_lanes(hd)]) * dmat_ref[hd] for c, hd in inst}
        g_d = {(c, hd): _mm_tn(g_v[c, hd], k_end[c][:, pair_lanes(hd)]) for c, hd in inst}
        r_d = {(c, hd): _mm_tn(r_v[c, hd], rk_z[c][:, pair_lanes(hd)]) for c, hd in inst}
        sg_in, sr_in = {}, {}
        for c in group:
            sg_in[c], sr_in[c] = sg, sr
            dec_c = jnp.exp(b_last[c])
            sg = [sg[p] * dec_c[:, p * LANES:(p + 1) * LANES] + jnp.where(lo_half, g_d[c, 2 * p], g_d[c, 2 * p + 1])
                  for p in pairs]
            sr = [sr[p] * rdecay_ref[:, p * LANES:(p + 1) * LANES]
                  + jnp.where(lo_half, r_d[c, 2 * p], r_d[c, 2 * p + 1]) for p in pairs]
        g_o = {(c, hd): _mm(g_att[c, hd], g_v[c, hd]) + _mm_nt(g_q[c, hd], sg_in[c][hd // 2]) for c, hd in inst}
        r_o = {(c, hd): _mm(r_att[c, hd], r_v[c, hd])
               + _mm_nt(r_q[c, hd], sr_in[c][hd // 2]) * xi_ref[:, vcol(0, hd)] for c, hd in inst}
        for c, hd in inst:
            mix_ref[rows[c], vcol(0, hd)] = g_o[c, hd]
            mix_ref[rows[c], vcol(GLA_V, hd)] = r_o[c, hd]
    for p in pairs:
        sg_ref[p] = sg[p]
        sr_ref[p] = sr[p]

    for hd in range(GLA_HEADS):
        sl = slice(hd * GLA_DV, (hd + 1) * GLA_DV)
        o = mix_ref[:, sl]
        o = o * lax.rsqrt(jnp.mean(o * o, axis=-1, keepdims=True) + NORM_EPS) * gnorm_ref[:, sl]
        mix_ref[:, sl] = o * _silu(h_ref[:, L0_GR + hd * GLA_DV:L0_GR + (hd + 1) * GLA_DV])
    for hd in range(RET_HEADS):
        sl = slice(hd * RET_DV, (hd + 1) * RET_DV)
        o = mix_ref[:, GLA_V + hd * RET_DV:GLA_V + (hd + 1) * RET_DV]
        oc = o - jnp.mean(o, axis=-1, keepdims=True)
        o = oc * lax.rsqrt(jnp.mean(oc * oc, axis=-1, keepdims=True) + NORM_EPS) * rnorm_ref[:, sl]
        mix_ref[:, GLA_V + hd * RET_DV:GLA_V + (hd + 1) * RET_DV] = (
            o * _silu(h_ref[:, L0_RG + hd * RET_DV:L0_RG + (hd + 1) * RET_DV]))

    y = _mm(mix_ref[...], w_out_ref[...])
    o_ref[...] = _layer_norm(DEEPNORM_ALPHA * x + y, lng_ref[...], lnb_ref[...])


def _l0_mixer(x, positions, w_in, w_gate2, b_gate, gla_norm, ret_norm, w_out, ln_g, ln_b, *, tt):
    bsz, seq, d = x.shape
    w_in_p = _pack_l0_w_in(w_in).astype(BF16)
    wg2 = jnp.pad(w_gate2, ((0, LANES - GLA_GATE_RANK), (0, 0))).astype(BF16)
    dmat, xi_full, zeta_full, rdecay = _ret_tables()
    freq_full, sign = _rope_tables()
    pos_f = positions.astype(F32)[..., None]
    consts = [jnp.asarray(_chunk_tril_np(tt), BF16), jnp.asarray(dmat), jnp.asarray(xi_full),
              jnp.asarray(zeta_full), jnp.asarray(rdecay), jnp.asarray(freq_full), jnp.asarray(sign)]
    params = [w_in_p, wg2, b_gate[None, :], gla_norm.reshape(1, GLA_V), ret_norm.reshape(1, RET_V),
              w_out.astype(BF16), ln_g[None, :], ln_b[None, :]]
    tile = lambda w: pl.BlockSpec((None, tt, w), lambda b, t: (b, t, 0))
    return pl.pallas_call(
        functools.partial(_l0_mixer_kernel, tt=tt),
        grid=(bsz, seq // tt),
        in_specs=[tile(d), tile(1)] + [_const_spec(a.shape) for a in params + consts],
        out_specs=tile(d),
        out_shape=jax.ShapeDtypeStruct((bsz, seq, d), F32),
        scratch_shapes=[pltpu.VMEM((tt, L0_COLS), F32), pltpu.VMEM((tt, GLA_V + RET_V), F32),
                        pltpu.VMEM((GLA_HEADS // 2, GLA_DV, LANES), F32),
                        pltpu.VMEM((RET_HEADS // 2, RET_DV, LANES), F32)],
        compiler_params=pltpu.CompilerParams(dimension_semantics=("arbitrary", "arbitrary"),
                                             vmem_limit_bytes=VMEM_LIMIT_BYTES),
        name="l0_mixer",
    )(x, pos_f, *params, *consts)


WEIGHT_SLOTS = 2


class _SwigluWeights:
    def __init__(self, wgu_hbm, wd_hbm, resident, staging, sems, tf):
        self.wgu_hbm, self.wd_hbm = wgu_hbm, wd_hbm
        self.wgb, self.wub, self.wdb = resident
        self.sg, self.su, self.sd = staging
        self.sems, self.tf, self.nf = sems, tf, D_FF // tf

    def _copies(self, e, j):
        slot = j % WEIGHT_SLOTS
        cols = pl.ds(j * self.tf, self.tf)
        up_cols = pl.ds(D_FF + j * self.tf, self.tf)
        return (pltpu.make_async_copy(self.wgu_hbm.at[e, :, cols], self.sg.at[slot], self.sems.at[slot]),
                pltpu.make_async_copy(self.wgu_hbm.at[e, :, up_cols], self.su.at[slot], self.sems.at[slot]),
                pltpu.make_async_copy(self.wd_hbm.at[e, cols, :], self.sd.at[slot], self.sems.at[slot]))

    def request(self, e, j):
        for c in self._copies(e, j):
            c.start()

    def request_head(self, e):
        for j in range(WEIGHT_SLOTS):
            self.request(e, j)

    def land(self, e, j):
        slot = j % WEIGHT_SLOTS
        for c in self._copies(e, j):
            c.wait()
        self.wgb[j] = self.sg[slot].astype(BF16)
        self.wub[j] = self.su[slot].astype(BF16)
        self.wdb[j] = self.sd[slot].astype(BF16)
        if j + WEIGHT_SLOTS < self.nf:
            self.request(e, j + WEIGHT_SLOTS)

    def apply_chunk(self, xb_ref, acc_ref, j):
        xb = xb_ref[...]
        gt = jnp.dot(xb, self.wgb[j], preferred_element_type=F32)
        up = jnp.dot(xb, self.wub[j], preferred_element_type=F32)
        part = jnp.dot((_silu(gt) * up).astype(BF16), self.wdb[j], preferred_element_type=F32)
        if j == 0:
            acc_ref[...] = part
        else:
            acc_ref[...] += part

    @staticmethod
    def scratch_shapes(tf):
        nf, d = D_FF // tf, D_MODEL
        return [pltpu.VMEM((nf, d, tf), BF16), pltpu.VMEM((nf, d, tf), BF16), pltpu.VMEM((nf, tf, d), BF16),
                pltpu.VMEM((WEIGHT_SLOTS, d, tf), F32), pltpu.VMEM((WEIGHT_SLOTS, d, tf), F32),
                pltpu.VMEM((WEIGHT_SLOTS, tf, d), F32), pltpu.SemaphoreType.DMA((WEIGHT_SLOTS,))]


def _ffn_kernel(x_ref, wgu_hbm, wd_hbm, lng_ref, lnb_ref, o_ref, xb_ref, acc_ref,
                wgb_ref, wub_ref, wdb_ref, sg_ref, su_ref, sd_ref, sems, *, tf):
    i = pl.program_id(0)
    w = _SwigluWeights(wgu_hbm, wd_hbm, (wgb_ref, wub_ref, wdb_ref), (sg_ref, su_ref, sd_ref), sems, tf)
    xb_ref[...] = x_ref[...].astype(BF16)

    @pl.when(i == 0)
    def _():
        w.request_head(0)
        for j in range(w.nf):
            w.land(0, j)
            w.apply_chunk(xb_ref, acc_ref, j)

    @pl.when(i > 0)
    def _():
        for j in range(w.nf):
            w.apply_chunk(xb_ref, acc_ref, j)

    o_ref[...] = _layer_norm(DEEPNORM_ALPHA * x_ref[...] + acc_ref[...], lng_ref[...], lnb_ref[...])


def _ffn(x2d, w_gu, w_down, ln_g, ln_b, *, tm, tf):
    n, d = x2d.shape
    assert D_FF // tf >= WEIGHT_SLOTS
    return pl.pallas_call(
        functools.partial(_ffn_kernel, tf=tf),
        grid=(n // tm,),
        in_specs=[pl.BlockSpec((tm, d), lambda i: (i, 0)),
                  pl.BlockSpec(memory_space=pl.ANY), pl.BlockSpec(memory_space=pl.ANY),
                  _const_spec((1, d)), _const_spec((1, d))],
        out_specs=pl.BlockSpec((tm, d), lambda i: (i, 0)),
        out_shape=jax.ShapeDtypeStruct((n, d), F32),
        scratch_shapes=[pltpu.VMEM((tm, d), BF16), pltpu.VMEM((tm, d), F32)] + _SwigluWeights.scratch_shapes(tf),
        compiler_params=pltpu.CompilerParams(dimension_semantics=("arbitrary",),
                                             vmem_limit_bytes=VMEM_LIMIT_BYTES),
        name="ffn",
    )(x2d, w_gu[None], w_down[None], ln_g[None, :], ln_b[None, :])


L1_Q, L1_K, L1_V, L1_GATE, L1_AB = 0, 1024, 2048, 3072, 4096
L1_CONV = 3 * GDN_QK
L1_COLS = 4224
L1_BETA_LANE = GDN_HEADS
CONV_PAD = 8
L1_PROJ_BLOCK = 512
L1_CHUNK_GROUP = 2
ROUTE_LANES = LANES


def _pack_l1_w_in(w):
    offs = np.cumsum([0, L1_CONV, GDN_HEADS, GDN_HEADS, GDN_V])
    qkv, a_in, b_in, gate = [w[:, offs[i]:offs[i + 1]] for i in range(4)]
    ab = jnp.pad(jnp.concatenate([a_in, b_in], axis=1), ((0, 0), (0, LANES - 2 * GDN_HEADS)))
    return jnp.concatenate([qkv, gate, ab], axis=1)


def _l1_mixer_kernel(x_ref, w_in_ref, conv_ref, alog_ref, dtb_ref, cnorm_ref, w_out_ref, lng_ref, lnb_ref,
                     tril_ref, wr_hi_ref, wr_lo_ref, br_ref,
                     o_ref, route_ref, h2_ref, qkv_ref, mix_ref, s_ref, u_ref, w_ref, aqk_ref, ext_ref, *, tt):
    ti = pl.program_id(1)

    @pl.when(ti == 0)
    def _():
        s_ref[...] = jnp.zeros_like(s_ref)
        ext_ref[:, 0:CONV_PAD, :] = jnp.zeros((L1_CONV // LANES, CONV_PAD, LANES), F32)

    x = x_ref[...]
    xb = x.astype(BF16)
    h2_ref[...] = jnp.dot(xb, w_in_ref[:, L1_CONV:L1_COLS], preferred_element_type=F32)

    half = tt // 2
    lanes_per_block = L1_PROJ_BLOCK // LANES
    for blk in range(L1_CONV // L1_PROJ_BLOCK):
        h_blk = jnp.dot(xb, w_in_ref[:, blk * L1_PROJ_BLOCK:(blk + 1) * L1_PROJ_BLOCK], preferred_element_type=F32)
        for l in range(lanes_per_block):
            ext_ref[blk * lanes_per_block + l, CONV_PAD:CONV_PAD + tt, :] = h_blk[:, l * LANES:(l + 1) * LANES]
        for l in range(lanes_per_block):
            lb = blk * lanes_per_block + l
            lanes = slice(lb * LANES, (lb + 1) * LANES)
            for parity in range(2):
                conv = None
                for j in range(CONV_WIDTH):
                    first_row = CONV_PAD - (CONV_WIDTH - 1 - j) + parity
                    term = ext_ref[lb, pl.ds(first_row, half, stride=2), :] * conv_ref[j:j + 1, lanes]
                    conv = term if conv is None else conv + term
                act = _silu(conv)
                if lb * LANES < L1_V:
                    scale = GDN_DK ** -0.5 if lb * LANES < L1_K else 1.0
                    act = act * (lax.rsqrt(jnp.sum(act * act, axis=-1, keepdims=True) + L2_EPS) * scale)
                qkv_ref[lb, pl.ds(parity, half, stride=2), :] = act
            ext_ref[lb, 0:CONV_PAD, :] = ext_ref[lb, tt:tt + CONV_PAD, :]

    ab = h2_ref[:, L1_AB - L1_CONV:L1_AB - L1_CONV + LANES]
    g_blk = -jnp.exp(alog_ref[...]) * _softplus(ab + dtb_ref[...])
    beta_blk = _sigmoid(ab)
    gc_blk = _mm_exact_lhs01(tril_ref[...], g_blk)
    eg_blk = jnp.exp(gc_blk)

    ci = lax.broadcasted_iota(jnp.int32, (CHUNK, CHUNK), 0)
    cj = lax.broadcasted_iota(jnp.int32, (CHUNK, CHUNK), 1)
    incl = cj <= ci
    strict = cj < ci

    heads = range(GDN_HEADS)
    col = lambda base, hd: slice(base + hd * GDN_DK, base + (hd + 1) * GDN_DK)
    blk_of = lambda base, hd: base // LANES + hd
    n_chunks = tt // CHUNK

    s = [s_ref[hd] for hd in heads]
    for c0 in range(0, n_chunks, L1_CHUNK_GROUP):
        inst = [(c, hd) for c in range(c0, c0 + L1_CHUNK_GROUP) for hd in heads]
        n_i = range(len(inst))
        rows = [slice(c * CHUNK, (c + 1) * CHUNK) for c, _ in inst]
        gc_c = {c: gc_blk[c * CHUNK:(c + 1) * CHUNK] for c in range(c0, c0 + L1_CHUNK_GROUP)}
        gc_t = {c: gc_c[c].T for c in gc_c}
        kdec_scale = {c: jnp.exp(gc_c[c][CHUNK - 1:CHUNK, :] - gc_c[c]) for c in gc_c}
        q_h = [qkv_ref[blk_of(L1_Q, hd), rows[i], :] for i, (c, hd) in enumerate(inst)]
        k_h = [qkv_ref[blk_of(L1_K, hd), rows[i], :] for i, (c, hd) in enumerate(inst)]
        v_h = [qkv_ref[blk_of(L1_V, hd), rows[i], :] for i, (c, hd) in enumerate(inst)]
        beta = [beta_blk[rows[i], L1_BETA_LANE + hd:L1_BETA_LANE + hd + 1] for i, (c, hd) in enumerate(inst)]
        eg = [eg_blk[rows[i], hd:hd + 1] for i, (c, hd) in enumerate(inst)]
        decay = [jnp.exp(jnp.where(incl, gc_c[c][:, hd:hd + 1] - gc_t[c][hd:hd + 1, :], NEG_BIG)) for c, hd in inst]
        kb = [k_h[i] * beta[i] for i in n_i]
        low = [jnp.where(strict, _mm_nt(kb[i], k_h[i]) * decay[i], 0.0) for i in n_i]
        a_qk = [_mm_nt(q_h[i], k_h[i]) * decay[i] for i in n_i]
        a_m = [-low[i] for i in n_i]
        m = [_mm(low[i], low[i]) for i in n_i]
        for it in range(5):
            am = [_mm(a_m[i], m[i]) for i in n_i]
            a_m = [a_m[i] + m[i] + am[i] for i in n_i]
            if it < 4:
                m = [_mm(m[i], m[i]) for i in n_i]
        rhs = [jnp.concatenate([v_h[i] * beta[i], kb[i] * eg[i]], axis=1) for i in n_i]
        uw = [rhs[i] + _mm(a_m[i], rhs[i]) for i in n_i]
        for i, (c, hd) in enumerate(inst):
            u_ref[rows[i], col(0, hd)] = uw[i][:, 0:GDN_DV]
            w_ref[rows[i], col(0, hd)] = uw[i][:, GDN_DV:GDN_DV + GDN_DK]
            aqk_ref[rows[i], hd * LANES:hd * LANES + CHUNK] = a_qk[i]
            qkv_ref[blk_of(L1_Q, hd), rows[i], :] = q_h[i] * eg[i]
            qkv_ref[blk_of(L1_K, hd), rows[i], :] = k_h[i] * kdec_scale[c][:, hd:hd + 1]

        for c in range(c0, c0 + L1_CHUNK_GROUP):
            crow = slice(c * CHUNK, (c + 1) * CHUNK)
            e_last = jnp.exp(gc_blk[(c + 1) * CHUNK - 1:(c + 1) * CHUNK, :])
            ws_qs = [_mm(jnp.concatenate([w_ref[crow, col(0, hd)], qkv_ref[blk_of(L1_Q, hd), crow, :]], axis=0), s[hd])
                     for hd in heads]
            v_new = [u_ref[crow, col(0, hd)] - ws_qs[hd][0:CHUNK] for hd in heads]
            o_h = [ws_qs[hd][CHUNK:2 * CHUNK] + _mm(aqk_ref[crow, hd * LANES:hd * LANES + CHUNK], v_new[hd])
                   for hd in heads]
            s = [s[hd] * e_last[:, hd:hd + 1] + _mm_tn(qkv_ref[blk_of(L1_K, hd), crow, :], v_new[hd]) for hd in heads]
            for hd in heads:
                mix_ref[crow, col(0, hd)] = o_h[hd]
    for hd in heads:
        s_ref[hd] = s[hd]

    for hd in range(GDN_HEADS):
        sl = slice(hd * GDN_DV, (hd + 1) * GDN_DV)
        o = mix_ref[:, sl]
        o = o * lax.rsqrt(jnp.mean(o * o, axis=-1, keepdims=True) + NORM_EPS) * cnorm_ref[:, sl]
        mix_ref[:, sl] = o * _silu(h2_ref[:, L1_GATE - L1_CONV + hd * GDN_DV:L1_GATE - L1_CONV + (hd + 1) * GDN_DV])
    y = _mm(mix_ref[...], w_out_ref[...])
    x1 = _layer_norm(DEEPNORM_ALPHA * x + y, lng_ref[...], lnb_ref[...])
    o_ref[...] = x1

    x_hi = x1.astype(BF16)
    x_lo = (x1 - x_hi.astype(F32)).astype(BF16)
    logits = (jnp.dot(x_hi, wr_hi_ref[...], preferred_element_type=F32)
              + jnp.dot(x_hi, wr_lo_ref[...], preferred_element_type=F32)
              + jnp.dot(x_lo, wr_hi_ref[...], preferred_element_type=F32)) + br_ref[...]
    lane = lax.broadcasted_iota(jnp.int32, (tt, ROUTE_LANES), 1)
    lane_f = lane.astype(F32)
    logits = jnp.where(lane < N_EXPERTS, logits, NEG_BIG)
    m1 = jnp.max(logits, axis=-1, keepdims=True)
    i1 = jnp.min(jnp.where(logits == m1, lane_f, float(ROUTE_LANES)), axis=-1, keepdims=True)
    rest = jnp.where(lane_f == i1, NEG_BIG, logits)
    m2 = jnp.max(rest, axis=-1, keepdims=True)
    i2 = jnp.min(jnp.where(rest == m2, lane_f, float(ROUTE_LANES)), axis=-1, keepdims=True)
    e21 = jnp.exp(m2 - m1)
    g1 = 1.0 / (1.0 + e21)
    g2 = e21 * g1
    route_ref[...] = jnp.where(lane == 0, i1, jnp.where(lane == 1, i2, jnp.where(lane == 2, g1, jnp.where(lane == 3, g2, 0.0))))


def _l1_mixer(x, w_in, conv_w, a_log, dt_bias, c_norm, w_out, ln_g, ln_b, w_router, b_router, *, tt):
    bsz, seq, d = x.shape
    w_in_p = _pack_l1_w_in(w_in).astype(BF16)
    lane_pad = lambda v: jnp.pad(v[None, :], ((0, 0), (0, LANES - v.shape[0])))
    wr = jnp.pad(w_router, ((0, 0), (0, ROUTE_LANES - N_EXPERTS)))
    wr_hi = wr.astype(BF16)
    wr_lo = (wr - wr_hi.astype(F32)).astype(BF16)
    params = [w_in_p, conv_w, lane_pad(a_log), lane_pad(dt_bias), c_norm.reshape(1, GDN_V), w_out.astype(BF16),
              ln_g[None, :], ln_b[None, :], jnp.asarray(_chunk_tril_np(tt), BF16), wr_hi, wr_lo, lane_pad(b_router)]
    tile = lambda w: pl.BlockSpec((None, tt, w), lambda b, t: (b, t, 0))
    return pl.pallas_call(
        functools.partial(_l1_mixer_kernel, tt=tt),
        grid=(bsz, seq // tt),
        in_specs=[tile(d)] + [_const_spec(a.shape) for a in params],
        out_specs=[tile(d), tile(ROUTE_LANES)],
        out_shape=[jax.ShapeDtypeStruct((bsz, seq, d), F32), jax.ShapeDtypeStruct((bsz, seq, ROUTE_LANES), F32)],
        scratch_shapes=[pltpu.VMEM((tt, L1_COLS - L1_CONV), F32),
                        pltpu.VMEM((L1_CONV // LANES, tt, LANES), F32), pltpu.VMEM((tt, GDN_V), F32),
                        pltpu.VMEM((GDN_HEADS, GDN_DK, GDN_DV), F32),
                        pltpu.VMEM((tt, GDN_V), F32), pltpu.VMEM((tt, GDN_QK), F32),
                        pltpu.VMEM((tt, GDN_HEADS * LANES), F32),
                        pltpu.VMEM((L1_CONV // LANES, tt + CONV_PAD, LANES), F32)],
        compiler_params=pltpu.CompilerParams(dimension_semantics=("arbitrary", "arbitrary"),
                                             vmem_limit_bytes=VMEM_LIMIT_BYTES),
        name="l1_mixer",
    )(x, *params)


ID_ROWS = 4


def _rank_kernel(route_ref, stril_ref, ids_ref, counts_ref, carry_ref):
    @pl.when(pl.program_id(0) == 0)
    def _():
        carry_ref[...] = jnp.zeros_like(carry_ref)

    r = route_ref[...]
    tr = r.shape[0]
    e1, e2 = r[:, 0:1], r[:, 1:2]
    lane = lax.broadcasted_iota(jnp.int32, (tr, ROUTE_LANES), 1)
    lane_f = lane.astype(F32)
    oh1 = (lane_f == e1).astype(F32)
    oh2 = (lane_f == e2).astype(F32)
    both = oh1 + oh2
    before = jnp.dot(stril_ref[...], both.astype(BF16), preferred_element_type=F32) + carry_ref[...]
    rank1 = jnp.sum(oh1 * before, axis=-1, keepdims=True)
    rank2 = jnp.sum(oh2 * before, axis=-1, keepdims=True)
    table = jnp.where(lane == 0, e1, jnp.where(lane == 1, e2, jnp.where(lane == 2, rank1,
                      jnp.where(lane == 3, rank2, 0.0))))
    ids_ref[...] = table.T[0:ID_ROWS, :].astype(jnp.int32)
    carry_ref[...] += jnp.sum(both, axis=0, keepdims=True)
    counts_ref[...] = carry_ref[...]


def _rank(route2d, *, tr):
    n = route2d.shape[0]
    i = np.arange(tr)
    stril = jnp.asarray((i[None, :] < i[:, None]).astype(np.float32), BF16)
    return pl.pallas_call(
        _rank_kernel,
        grid=(n // tr,),
        in_specs=[pl.BlockSpec((tr, ROUTE_LANES), lambda i: (i, 0)), _const_spec((tr, tr))],
        out_specs=[pl.BlockSpec((ID_ROWS, tr), lambda i: (0, i)), _const_spec((1, ROUTE_LANES))],
        out_shape=[jax.ShapeDtypeStruct((ID_ROWS, n), jnp.int32), jax.ShapeDtypeStruct((1, ROUTE_LANES), F32)],
        scratch_shapes=[pltpu.VMEM((1, ROUTE_LANES), F32)],
        compiler_params=pltpu.CompilerParams(dimension_semantics=("arbitrary",)),
        name="moe_rank",
    )(route2d, stril)


ROW_SUB = D_MODEL // LANES


def _to_row_tiles(dst_ref, src, n):
    for lb in range(ROW_SUB):
        dst_ref[pl.ds(lb, n, stride=ROW_SUB), :] = src[:, lb * LANES:(lb + 1) * LANES]


def _from_row_tiles(src_ref, n):
    return [src_ref[pl.ds(lb, n, stride=ROW_SUB), :] for lb in range(ROW_SUB)]


def _tile_copy(src_ref, src_sub, dst_ref, dst_sub, sem):
    src = src_ref.at[pl.ds(pl.multiple_of(src_sub, ROW_SUB), ROW_SUB), :]
    dst = dst_ref.at[pl.ds(pl.multiple_of(dst_sub, ROW_SUB), ROW_SUB), :]
    return pltpu.make_async_copy(src, dst, sem)


def _wait_rows(hbm_ref, buf_ref, sem, n, copies):
    for _ in range(copies):
        pltpu.make_async_copy(hbm_ref.at[pl.ds(0, n * ROW_SUB), :], buf_ref, sem).wait()


ZERO_ROWS = 64


def _dispatch_kernel(zfill_ref, p0_ref, p1_ref, x_ref, xs_hbm, xt0_ref, xt1_ref, zero_ref, sems, zsem, *, td):
    p_refs = (p0_ref, p1_ref)
    i = pl.program_id(0)
    last = pl.num_programs(0) - 1

    @pl.when(i == 0)
    def _():
        zero_ref[...] = jnp.zeros_like(zero_ref)
        n_ranges = zfill_ref.shape[0] // 2

        def zero_copy(start_sub, r):
            dst = xs_hbm.at[pl.ds(pl.multiple_of(start_sub + r * (ZERO_ROWS * ROW_SUB), ROW_SUB),
                                  ZERO_ROWS * ROW_SUB), :]
            return pltpu.make_async_copy(zero_ref, dst, zsem)

        def fill(e, wait):
            def body(r, carry):
                copy = zero_copy(zfill_ref[e], r)
                if wait:
                    copy.wait()
                else:
                    copy.start()
                return carry

            lax.fori_loop(0, zfill_ref[n_ranges + e], body, 0)

        for e in range(n_ranges):
            fill(e, wait=False)
        for e in range(n_ranges):
            fill(e, wait=True)

    def step(xt_ref, sem):
        @pl.when(i >= 2)
        def _():
            _wait_rows(xs_hbm, xt_ref, sem, td, TOP_K)

        _to_row_tiles(xt_ref, x_ref[...], td)

        def issue(t, carry):
            for k in range(TOP_K):
                _tile_copy(xt_ref, t * ROW_SUB, xs_hbm, p_refs[k][t], sem).start(priority=k)
            return carry

        lax.fori_loop(0, td, issue, 0, unroll=8)

    for slot, (xt_ref, sem) in enumerate(((xt0_ref, sems.at[0]), (xt1_ref, sems.at[1]))):
        @pl.when(i % 2 == slot)
        def _():
            step(xt_ref, sem)

    @pl.when(i == last)
    def _():
        _wait_rows(xs_hbm, xt0_ref, sems.at[0], td, TOP_K)
        _wait_rows(xs_hbm, xt1_ref, sems.at[1], td, TOP_K)


def _dispatch(zfill, pos, x2d, n_rows, *, td):
    n, d = x2d.shape
    smem_rows = pl.BlockSpec((td,), lambda i, zf: (i,), memory_space=pltpu.SMEM)
    assert n // td >= 2
    grid_spec = pltpu.PrefetchScalarGridSpec(
        num_scalar_prefetch=1,
        grid=(n // td,),
        in_specs=[smem_rows, smem_rows, pl.BlockSpec((td, d), lambda i, zf: (i, 0))],
        out_specs=pl.BlockSpec(memory_space=pl.ANY),
        scratch_shapes=[pltpu.VMEM((td * ROW_SUB, LANES), F32), pltpu.VMEM((td * ROW_SUB, LANES), F32),
                        pltpu.VMEM((ZERO_ROWS * ROW_SUB, LANES), F32),
                        pltpu.SemaphoreType.DMA((2,)), pltpu.SemaphoreType.DMA(())],
    )
    return pl.pallas_call(
        functools.partial(_dispatch_kernel, td=td),
        grid_spec=grid_spec,
        out_shape=jax.ShapeDtypeStruct((n_rows * ROW_SUB, LANES), F32),
        compiler_params=pltpu.CompilerParams(dimension_semantics=("arbitrary",)),
        name="moe_dispatch",
    )(zfill, pos[0], pos[1], x2d)


def _gmm_kernel(te_ref, used_ref, first_ref, x_ref, wgu_hbm, wd_hbm, o_ref,
                xb_ref, acc_ref, wgb_ref, wub_ref, wdb_ref, sg_ref, su_ref, sd_ref, sems, *, tm, tf):
    i = pl.program_id(0)
    n_tiles = pl.num_programs(0)
    w = _SwigluWeights(wgu_hbm, wd_hbm, (wgb_ref, wub_ref, wdb_ref), (sg_ref, su_ref, sd_ref), sems, tf)
    active = i < used_ref[0]
    is_first = first_ref[i] == 1

    @pl.when(active)
    def _():
        for lb, blk in enumerate(_from_row_tiles(x_ref, tm)):
            xb_ref[:, lb * LANES:(lb + 1) * LANES] = blk.astype(BF16)

    @pl.when(jnp.logical_and(active, is_first))
    def _():
        e = te_ref[i]

        @pl.when(i == 0)
        def _():
            w.request_head(e)

        for j in range(w.nf):
            w.land(e, j)
            w.apply_chunk(xb_ref, acc_ref, j)

    @pl.when(jnp.logical_and(active, jnp.logical_not(is_first)))
    def _():
        for j in range(w.nf):
            w.apply_chunk(xb_ref, acc_ref, j)

    @pl.when(active)
    def _():
        _to_row_tiles(o_ref, acc_ref[...], tm)
        nxt = jnp.minimum(i + 1, n_tiles - 1)

        @pl.when(jnp.logical_and(i + 1 < n_tiles, first_ref[nxt] == 1))
        def _():
            w.request_head(te_ref[nxt])

    @pl.when(jnp.logical_not(active))
    def _():
        o_ref[...] = jnp.zeros_like(o_ref)


def _gmm(tile_expert, n_used, tile_first, xs, w_gu, w_down, *, tm, tf):
    d = D_MODEL
    n_rows = xs.shape[0] // ROW_SUB
    nf = D_FF // tf
    n_tiles = n_rows // tm
    assert nf >= WEIGHT_SLOTS
    grid_spec = pltpu.PrefetchScalarGridSpec(
        num_scalar_prefetch=3,
        grid=(n_tiles,),
        in_specs=[pl.BlockSpec((tm * ROW_SUB, LANES), lambda i, te, used, first: (jnp.minimum(i, used[0] - 1), 0)),
                  pl.BlockSpec(memory_space=pl.ANY), pl.BlockSpec(memory_space=pl.ANY)],
        out_specs=pl.BlockSpec((tm * ROW_SUB, LANES), lambda i, te, used, first: (i, 0)),
        scratch_shapes=[pltpu.VMEM((tm, d), BF16), pltpu.VMEM((tm, d), F32)] + _SwigluWeights.scratch_shapes(tf),
    )
    return pl.pallas_call(
        functools.partial(_gmm_kernel, tm=tm, tf=tf),
        grid_spec=grid_spec,
        out_shape=jax.ShapeDtypeStruct((n_rows * ROW_SUB, LANES), F32),
        compiler_params=pltpu.CompilerParams(dimension_semantics=("arbitrary",),
                                             vmem_limit_bytes=VMEM_LIMIT_BYTES),
        name="moe_gmm",
    )(tile_expert, n_used, tile_first, xs, w_gu, w_down)


def _combine_kernel(p0_ref, p1_ref, p0_next_ref, p1_next_ref, x_ref, route_ref, lng_ref, lnb_ref, y_hbm, o_ref,
                    ya0_ref, yb0_ref, ya1_ref, yb1_ref, sems, *, tc):
    i = pl.program_id(0)
    n = pl.num_programs(0)
    slots = ((ya0_ref, yb0_ref, sems.at[0]), (ya1_ref, yb1_ref, sems.at[1]))

    def issue(p_refs, slot):
        ya_ref, yb_ref, sem = slots[slot]

        def body(t, carry):
            for k, (p_ref, buf) in enumerate(zip(p_refs, (ya_ref, yb_ref))):
                _tile_copy(y_hbm, p_ref[t], buf, t * ROW_SUB, sem).start(priority=k)
            return carry

        lax.fori_loop(0, tc, body, 0, unroll=8)

    @pl.when(i == 0)
    def _():
        issue((p0_ref, p1_ref), 0)

    for slot in range(2):
        @pl.when(jnp.logical_and(i + 1 < n, (i + 1) % 2 == slot))
        def _():
            issue((p0_next_ref, p1_next_ref), slot)

    for slot in range(2):
        @pl.when(i % 2 == slot)
        def _():
            ya_ref, yb_ref, sem = slots[slot]
            _wait_rows(y_hbm, ya_ref, sem, tc, TOP_K)
            r = route_ref[...]
            g1, g2 = r[:, 2:3], r[:, 3:4]
            y = jnp.concatenate(
                [g1 * a + g2 * b for a, b in zip(_from_row_tiles(ya_ref, tc), _from_row_tiles(yb_ref, tc))], axis=1)
            o_ref[...] = _layer_norm(DEEPNORM_ALPHA * x_ref[...] + y, lng_ref[...], lnb_ref[...])


def _combine(pos, x2d, route2d, ys, ln_g, ln_b, *, tc):
    n, d = x2d.shape
    steps = n // tc
    smem_rows = pl.BlockSpec((tc,), lambda i: (i,), memory_space=pltpu.SMEM)
    smem_next = pl.BlockSpec((tc,), lambda i: (jnp.minimum(i + 1, steps - 1),), memory_space=pltpu.SMEM)
    return pl.pallas_call(
        functools.partial(_combine_kernel, tc=tc),
        grid=(steps,),
        in_specs=[smem_rows, smem_rows, smem_next, smem_next,
                  pl.BlockSpec((tc, d), lambda i: (i, 0)),
                  pl.BlockSpec((tc, ROUTE_LANES), lambda i: (i, 0)),
                  _const_spec((1, d)), _const_spec((1, d)),
                  pl.BlockSpec(memory_space=pl.ANY)],
        out_specs=pl.BlockSpec((tc, d), lambda i: (i, 0)),
        scratch_shapes=[pltpu.VMEM((tc * ROW_SUB, LANES), F32) for _ in range(2 * TOP_K)]
                       + [pltpu.SemaphoreType.DMA((2,))],
        out_shape=jax.ShapeDtypeStruct((n, d), F32),
        compiler_params=pltpu.CompilerParams(dimension_semantics=("arbitrary",),
                                             vmem_limit_bytes=VMEM_LIMIT_BYTES),
        name="moe_combine",
    )(pos[0], pos[1], pos[0], pos[1], x2d, route2d, ln_g[None, :], ln_b[None, :], ys)


def _moe(x2d, route2d, w_gu, w_down, ln_g, ln_b, *, tm, tf, tr, td, tc):
    n, d = x2d.shape
    ids, counts = _rank(route2d, tr=tr)
    cnt = counts[0, :N_EXPERTS].astype(jnp.int32)
    tiles_per = (cnt + tm - 1) // tm
    tile_end = jnp.cumsum(tiles_per)
    offs = (tile_end - tiles_per) * tm
    n_tiles = (n * TOP_K) // tm + N_EXPERTS
    n_used = tile_end[-1:]
    t_idx = jnp.arange(n_tiles, dtype=jnp.int32)
    tile_expert = jnp.sum(jnp.minimum(t_idx, n_used - 1)[:, None] >= tile_end[None, :], axis=1).astype(jnp.int32)
    tile_start = tile_end - tiles_per
    tile_first = jnp.logical_and(t_idx == tile_start[tile_expert], t_idx < n_used).astype(jnp.int32)
    group_start = sum(jnp.where(ids[0:TOP_K] == e, offs[e], 0) for e in range(N_EXPERTS))
    pos = (group_start + ids[TOP_K:2 * TOP_K]) * ROW_SUB
    pad_start = ((offs + cnt) // ZERO_ROWS) * ZERO_ROWS
    pad_blocks = (offs + tiles_per * tm - pad_start) // ZERO_ROWS
    tail_start = n_used * tm
    tail_blocks = (n_tiles - n_used) * (tm // ZERO_ROWS)
    zfill = jnp.concatenate([pad_start * ROW_SUB, tail_start * ROW_SUB, pad_blocks, tail_blocks]).astype(jnp.int32)
    xs = _dispatch(zfill, pos, x2d, n_tiles * tm, td=td)
    ys = _gmm(tile_expert, n_used, tile_first, xs, w_gu, w_down, tm=tm, tf=tf)
    return _combine(pos, x2d, route2d, ys, ln_g, ln_b, tc=tc)


def kernel(x, positions, ab_w_in, gla_w_gate2, gla_b_gate, gla_norm, ret_norm, ab_w_out, ab_ln1_g, ab_ln1_b, ffn_w_gu, ffn_w_down, ab_ln2_g, ab_ln2_b, c_w_in, c_conv_w, c_a_log, c_dt_bias, c_norm, c_w_out, c_ln1_g, c_ln1_b, moe_w_router, moe_b_router, moe_w_gu, moe_w_down, c_ln2_g, c_ln2_b):
    bsz, seq, d = x.shape
    for layer in range(DEPTH):
        i = layer // 2
        if layer % 2 == 0:
            x = _l0_mixer(x, positions, ab_w_in[i], gla_w_gate2[i], gla_b_gate[i], gla_norm[i], ret_norm[i],
                          ab_w_out[i], ab_ln1_g[i], ab_ln1_b[i], tt=L0_TIME_TILE)
            x = _ffn(x.reshape(bsz * seq, d), ffn_w_gu[i], ffn_w_down[i], ab_ln2_g[i], ab_ln2_b[i],
                     tm=FFN_ROW_TILE, tf=FF_TILE).reshape(bsz, seq, d)
        else:
            x, route = _l1_mixer(x, c_w_in[i], c_conv_w[i], c_a_log[i], c_dt_bias[i], c_norm[i], c_w_out[i],
                                 c_ln1_g[i], c_ln1_b[i], moe_w_router[i], moe_b_router[i], tt=L1_TIME_TILE)
            x = _moe(x.reshape(bsz * seq, d), route.reshape(bsz * seq, ROUTE_LANES), moe_w_gu[i], moe_w_down[i],
                     c_ln2_g[i], c_ln2_b[i], tm=MOE_ROW_TILE, tf=FF_TILE, tr=ROUTE_TILE, td=ROUTE_TILE,
                     tc=ROUTE_TILE).reshape(bsz, seq, d)
    return x
```

```python
import functools

import numpy as np
import jax
import jax.numpy as jnp
from jax import lax
from jax.experimental import pallas as pl
from jax.experimental.pallas import tpu as pltpu

F32 = jnp.float32
BF16 = jnp.bfloat16

D_MODEL = 1024
DEPTH = 2
CHUNK = 64
GLA_HEADS, GLA_DK, GLA_DV, GLA_GATE_RANK, GLA_TAU = 4, 64, 128, 16, 16.0
RET_HEADS, RET_DK, RET_DV = 4, 64, 128
ROPE_BASE = 10000.0
GDN_HEADS, GDN_DK, GDN_DV = 8, 128, 128
CONV_WIDTH = 4
D_FF = 3584
N_EXPERTS = 8
TOP_K = 2
NORM_EPS = 1e-5
L2_EPS = 1e-6
DEEPNORM_ALPHA = (2.0 * DEPTH) ** 0.25

GLA_QK = GLA_HEADS * GLA_DK
GLA_V = GLA_HEADS * GLA_DV
RET_QK = RET_HEADS * RET_DK
RET_V = RET_HEADS * RET_DV
GDN_QK = GDN_HEADS * GDN_DK
GDN_V = GDN_HEADS * GDN_DV

LANES = 128
VMEM_LIMIT_BYTES = 56 * 1024 * 1024
NEG_BIG = -1e30

L0_TIME_TILE = 256
L1_TIME_TILE = 256
FFN_ROW_TILE = 512
MOE_ROW_TILE = 512
FF_TILE = 512
ROUTE_TILE = 512


def _mm(a, b):
    return jnp.dot(a.astype(BF16), b.astype(BF16), preferred_element_type=F32)


def _mm_nt(a, b):
    return lax.dot_general(a.astype(BF16), b.astype(BF16), (((1,), (1,)), ((), ())),
                           preferred_element_type=F32)


def _mm_tn(a, b):
    return lax.dot_general(a.astype(BF16), b.astype(BF16), (((0,), (0,)), ((), ())),
                           preferred_element_type=F32)


def _split3(x):
    hi = x.astype(BF16)
    r1 = x - hi.astype(F32)
    mid = r1.astype(BF16)
    lo = (r1 - mid.astype(F32)).astype(BF16)
    return hi, mid, lo


def _mm_exact_lhs01(m01, x):
    hi, mid, lo = _split3(x)
    return (jnp.dot(m01, hi, preferred_element_type=F32)
            + jnp.dot(m01, mid, preferred_element_type=F32)
            + jnp.dot(m01, lo, preferred_element_type=F32))


def _sigmoid(x):
    return 0.5 * jnp.tanh(0.5 * x) + 0.5


def _silu(x):
    a = 0.5 * x
    return a * (jnp.tanh(a) + 1.0)


def _softplus(x):
    return jnp.maximum(x, 0.0) + jnp.log(1.0 + jnp.exp(-jnp.abs(x)))


def _layer_norm(x, g, b):
    mu = jnp.mean(x, axis=-1, keepdims=True)
    xc = x - mu
    var = jnp.mean(xc * xc, axis=-1, keepdims=True)
    return xc * lax.rsqrt(var + NORM_EPS) * g + b


def _chunk_tril_np(tt):
    i = np.arange(tt)
    same = (i[:, None] // CHUNK) == (i[None, :] // CHUNK)
    return (same & (i[None, :] <= i[:, None])).astype(np.float32)


def _const_spec(shape):
    nd = len(shape)
    return pl.BlockSpec(shape, lambda *_: (0,) * nd)


L0_GQ, L0_GK, L0_GV, L0_GR = 0, 256, 512, 1024
L0_RQ, L0_RK, L0_RV, L0_RG = 1536, 1792, 2048, 2560
L0_GA = 3072
L0_COLS = 3200
L0_CHUNK_GROUP = 1


def _pack_l0_w_in(w):
    offs = np.cumsum([0, GLA_QK, GLA_QK, GLA_V, GLA_GATE_RANK, GLA_V, RET_QK, RET_QK, RET_V, RET_V])
    gq, gk, gv, ga, gr, rq, rk, rv, rg = [w[:, offs[i]:offs[i + 1]] for i in range(9)]
    ga = jnp.pad(ga, ((0, 0), (0, LANES - GLA_GATE_RANK)))
    return jnp.concatenate([gq, gk, gv, gr, rq, rk, rv, rg, ga], axis=1)


def _ret_tables():
    h = np.arange(RET_HEADS, dtype=np.float64)
    log_gamma = np.log(1.0 - 2.0 ** (-5.0 - h))
    pos = np.arange(CHUNK, dtype=np.float64)
    diff = pos[:, None] - pos[None, :]
    dmat = np.where(diff >= 0, np.exp(log_gamma[:, None, None] * np.maximum(diff, 0.0)), 0.0)
    xi = np.exp(log_gamma[None, :] * (pos[:, None] + 1.0))
    zeta = np.exp(log_gamma[None, :] * (CHUNK - 1.0 - pos[:, None]))
    decay = np.exp(log_gamma * CHUNK)
    xi_full = np.repeat(xi, RET_DV, axis=1)
    zeta_full = np.repeat(zeta, RET_DK, axis=1)
    decay_full = np.repeat(decay, RET_DK)[None, :]
    return (dmat.astype(np.float32), xi_full.astype(np.float32), zeta_full.astype(np.float32),
            decay_full.astype(np.float32))


def _rope_tables():
    half = RET_DK // 2
    inv_freq = ROPE_BASE ** (-np.arange(0, RET_DK, 2, dtype=np.float32) / RET_DK)
    per_head = np.concatenate([inv_freq, inv_freq])
    freq_full = np.tile(per_head, RET_HEADS)[None, :].astype(np.float32)
    sign = np.tile(np.concatenate([-np.ones(half), np.ones(half)]), RET_HEADS)[None, :].astype(np.float32)
    return freq_full, sign


def _l0_mixer_kernel(x_ref, pos_ref, w_in_ref, wg2_ref, bg_ref, gnorm_ref, rnorm_ref, w_out_ref,
                     lng_ref, lnb_ref, tril_ref, dmat_ref, xi_ref, zeta_ref, rdecay_ref,
                     freq_ref, sign_ref, o_ref, h_ref, mix_ref, sg_ref, sr_ref, *, tt):
    ti = pl.program_id(1)

    @pl.when(ti == 0)
    def _():
        sg_ref[...] = jnp.zeros_like(sg_ref)
        sr_ref[...] = jnp.zeros_like(sr_ref)

    x = x_ref[...]
    h_ref[...] = _mm(x, w_in_ref[...])

    lane = lax.broadcasted_iota(jnp.int32, (1, LANES), 1)
    lo_half = lane < GLA_DK
    ci = lax.broadcasted_iota(jnp.int32, (CHUNK, CHUNK), 0)
    cj = lax.broadcasted_iota(jnp.int32, (CHUNK, CHUNK), 1)
    causal = cj <= ci

    z = _mm(h_ref[:, L0_GA:L0_GA + LANES], wg2_ref[...]) + bg_ref[...]
    log_a = -_softplus(-z) * (1.0 / GLA_TAU)
    b = _mm_exact_lhs01(tril_ref[...], log_a)
    eb = jnp.exp(b)
    q_dec = h_ref[:, L0_GQ:L0_GQ + GLA_QK] * (GLA_DK ** -0.5) * eb
    k_all = h_ref[:, L0_GK:L0_GK + GLA_QK]
    k_neg = k_all * jnp.exp(-b)

    ang = pos_ref[...] * freq_ref[:, 0:LANES]
    cos = jnp.concatenate([jnp.cos(ang)] * (RET_QK // LANES), axis=1)
    sin = jnp.concatenate([jnp.sin(ang)] * (RET_QK // LANES), axis=1) * sign_ref[...]
    half = RET_DK // 2
    lane256 = lax.broadcasted_iota(jnp.int32, (1, RET_QK), 1)
    first_half = (lane256 & (RET_DK - 1)) < half

    def rope(t):
        swapped = jnp.where(first_half, pltpu.roll(t, RET_QK - half, 1), pltpu.roll(t, half, 1))
        return t * cos + swapped * sin

    rq = rope(h_ref[:, L0_RQ:L0_RQ + RET_QK]) * (RET_DK ** -0.5)
    rk = rope(h_ref[:, L0_RK:L0_RK + RET_QK])

    n_chunks = tt // CHUNK
    heads = range(GLA_HEADS)
    pairs = range(GLA_HEADS // 2)
    pair_lanes = lambda hd: slice((hd // 2) * LANES, (hd // 2 + 1) * LANES)
    head_mask = lambda hd: lo_half if hd % 2 == 0 else jnp.logical_not(lo_half)
    vcol = lambda base, hd: slice(base + hd * GLA_DV, base + (hd + 1) * GLA_DV)
    sg = [sg_ref[p] for p in pairs]
    sr = [sr_ref[p] for p in pairs]
    for c0 in range(0, n_chunks, L0_CHUNK_GROUP):
        group = range(c0, c0 + L0_CHUNK_GROUP)
        rows = {c: slice(c * CHUNK, (c + 1) * CHUNK) for c in group}
        b_last = {c: b[(c + 1) * CHUNK - 1:(c + 1) * CHUNK, :] for c in group}
        k_end = {c: k_all[rows[c]] * jnp.exp(b_last[c] - b[rows[c]]) for c in group}
        rk_c = {c: rk[rows[c]] for c in group}
        rk_z = {c: rk_c[c] * zeta_ref[...] for c in group}
        inst = [(c, hd) for c in group for hd in heads]
        g_q = {ch: jnp.where(head_mask(ch[1]), q_dec[rows[ch[0]], pair_lanes(ch[1])], 0.0) for ch in inst}
        r_q = {ch: jnp.where(head_mask(ch[1]), rq[rows[ch[0]], pair_lanes(ch[1])], 0.0) for ch in inst}
        g_v = {ch: h_ref[rows[ch[0]], vcol(L0_GV, ch[1])] for ch in inst}
        r_v = {ch: h_ref[rows[ch[0]], vcol(L0_RV, ch[1])] for ch in inst}
        g_att = {(c, hd): jnp.where(causal, _mm_nt(g_q[c, hd], k_neg[rows[c], pair_lanes(hd)]), 0.0)
                 for c, hd in inst}
        r_att = {(c, hd): _mm_nt(r_q[c, hd], rk_c[c][:, pair_lanes(hd)]) * dmat_ref[hd] for c, hd in inst}
        g_d = {(c, hd): _mm_tn(g_v[c, hd], k_end[c][:, pair_lanes(hd)]) for c, hd in inst}
        r_d = {(c, hd): _mm_tn(r_v[c, hd], rk_z[c][:, pair_lanes(hd)]) for c, hd in inst}
        sg_in, sr_in = {}, {}
        for c in group:
            sg_in[c], sr_in[c] = sg, sr
            dec_c = jnp.exp(b_last[c])
            sg = [sg[p] * dec_c[:, p * LANES:(p + 1) * LANES] + jnp.where(lo_half, g_d[c, 2 * p], g_d[c, 2 * p + 1])
                  for p in pairs]
            sr = [sr[p] * rdecay_ref[:, p * LANES:(p + 1) * LANES]
                  + jnp.where(lo_half, r_d[c, 2 * p], r_d[c, 2 * p + 1]) for p in pairs]
        g_o = {(c, hd): _mm(g_att[c, hd], g_v[c, hd]) + _mm_nt(g_q[c, hd], sg_in[c][hd // 2]) for c, hd in inst}
        r_o = {(c, hd): _mm(r_att[c, hd], r_v[c, hd])
               + _mm_nt(r_q[c, hd], sr_in[c][hd // 2]) * xi_ref[:, vcol(0, hd)] for c, hd in inst}
        for c, hd in inst:
            mix_ref[rows[c], vcol(0, hd)] = g_o[c, hd]
            mix_ref[rows[c], vcol(GLA_V, hd)] = r_o[c, hd]
    for p in pairs:
        sg_ref[p] = sg[p]
        sr_ref[p] = sr[p]

    for hd in range(GLA_HEADS):
        sl = slice(hd * GLA_DV, (hd + 1) * GLA_DV)
        o = mix_ref[:, sl]
        o = o * lax.rsqrt(jnp.mean(o * o, axis=-1, keepdims=True) + NORM_EPS) * gnorm_ref[:, sl]
        mix_ref[:, sl] = o * _silu(h_ref[:, L0_GR + hd * GLA_DV:L0_GR + (hd + 1) * GLA_DV])
    for hd in range(RET_HEADS):
        sl = slice(hd * RET_DV, (hd + 1) * RET_DV)
        o = mix_ref[:, GLA_V + hd * RET_DV:GLA_V + (hd + 1) * RET_DV]
        oc = o - jnp.mean(o, axis=-1, keepdims=True)
        o = oc * lax.rsqrt(jnp.mean(oc * oc, axis=-1, keepdims=True) + NORM_EPS) * rnorm_ref[:, sl]
        mix_ref[:, GLA_V + hd * RET_DV:GLA_V + (hd + 1) * RET_DV] = (
            o * _silu(h_ref[:, L0_RG + hd * RET_DV:L0_RG + (hd + 1) * RET_DV]))

    y = _mm(mix_ref[...], w_out_ref[...])
    o_ref[...] = _layer_norm(DEEPNORM_ALPHA * x + y, lng_ref[...], lnb_ref[...])


def _l0_mixer(x, positions, w_in, w_gate2, b_gate, gla_norm, ret_norm, w_out, ln_g, ln_b, *, tt):
    bsz, seq, d = x.shape
    w_in_p = _pack_l0_w_in(w_in).astype(BF16)
    wg2 = jnp.pad(w_gate2, ((0, LANES - GLA_GATE_RANK), (0, 0))).astype(BF16)
    dmat, xi_full, zeta_full, rdecay = _ret_tables()
    freq_full, sign = _rope_tables()
    pos_f = positions.astype(F32)[..., None]
    consts = [jnp.asarray(_chunk_tril_np(tt), BF16), jnp.asarray(dmat), jnp.asarray(xi_full),
              jnp.asarray(zeta_full), jnp.asarray(rdecay), jnp.asarray(freq_full), jnp.asarray(sign)]
    params = [w_in_p, wg2, b_gate[None, :], gla_norm.reshape(1, GLA_V), ret_norm.reshape(1, RET_V),
              w_out.astype(BF16), ln_g[None, :], ln_b[None, :]]
    tile = lambda w: pl.BlockSpec((None, tt, w), lambda b, t: (b, t, 0))
    return pl.pallas_call(
        functools.partial(_l0_mixer_kernel, tt=tt),
        grid=(bsz, seq // tt),
        in_specs=[tile(d), tile(1)] + [_const_spec(a.shape) for a in params + consts],
        out_specs=tile(d),
        out_shape=jax.ShapeDtypeStruct((bsz, seq, d), F32),
        scratch_shapes=[pltpu.VMEM((tt, L0_COLS), F32), pltpu.VMEM((tt, GLA_V + RET_V), F32),
                        pltpu.VMEM((GLA_HEADS // 2, GLA_DV, LANES), F32),
                        pltpu.VMEM((RET_HEADS // 2, RET_DV, LANES), F32)],
        compiler_params=pltpu.CompilerParams(dimension_semantics=("arbitrary", "arbitrary"),
                                             vmem_limit_bytes=VMEM_LIMIT_BYTES),
        name="l0_mixer",
    )(x, pos_f, *params, *consts)


WEIGHT_SLOTS = 3


class _SwigluWeights:
    def __init__(self, wgu_hbm, wd_hbm, resident, staging, sems, tf):
        self.wgu_hbm, self.wd_hbm = wgu_hbm, wd_hbm
        self.wgb, self.wub, self.wdb = resident
        self.sg, self.su, self.sd = staging
        self.sems, self.tf, self.nf = sems, tf, D_FF // tf

    def _copies(self, e, j):
        slot = j % WEIGHT_SLOTS
        cols = pl.ds(j * self.tf, self.tf)
        up_cols = pl.ds(D_FF + j * self.tf, self.tf)
        return (pltpu.make_async_copy(self.wgu_hbm.at[e, :, cols], self.sg.at[slot], self.sems.at[slot]),
                pltpu.make_async_copy(self.wgu_hbm.at[e, :, up_cols], self.su.at[slot], self.sems.at[slot]),
                pltpu.make_async_copy(self.wd_hbm.at[e, cols, :], self.sd.at[slot], self.sems.at[slot]))

    def request(self, e, j):
        for c in self._copies(e, j):
            c.start()

    def request_head(self, e):
        for j in range(WEIGHT_SLOTS):
            self.request(e, j)

    def land(self, e, j):
        slot = j % WEIGHT_SLOTS
        for c in self._copies(e, j):
            c.wait()
        self.wgb[j] = self.sg[slot].astype(BF16)
        self.wub[j] = self.su[slot].astype(BF16)
        self.wdb[j] = self.sd[slot].astype(BF16)
        if j + WEIGHT_SLOTS < self.nf:
            self.request(e, j + WEIGHT_SLOTS)

    def apply_chunk(self, xb_ref, acc_ref, j):
        xb = xb_ref[...]
        gt = jnp.dot(xb, self.wgb[j], preferred_element_type=F32)
        up = jnp.dot(xb, self.wub[j], preferred_element_type=F32)
        part = jnp.dot((_silu(gt) * up).astype(BF16), self.wdb[j], preferred_element_type=F32)
        if j == 0:
            acc_ref[...] = part
        else:
            acc_ref[...] += part

    @staticmethod
    def scratch_shapes(tf):
        nf, d = D_FF // tf, D_MODEL
        return [pltpu.VMEM((nf, d, tf), BF16), pltpu.VMEM((nf, d, tf), BF16), pltpu.VMEM((nf, tf, d), BF16),
                pltpu.VMEM((WEIGHT_SLOTS, d, tf), F32), pltpu.VMEM((WEIGHT_SLOTS, d, tf), F32),
                pltpu.VMEM((WEIGHT_SLOTS, tf, d), F32), pltpu.SemaphoreType.DMA((WEIGHT_SLOTS,))]


def _ffn_kernel(x_ref, wgu_hbm, wd_hbm, lng_ref, lnb_ref, o_ref, xb_ref, acc_ref,
                wgb_ref, wub_ref, wdb_ref, sg_ref, su_ref, sd_ref, sems, *, tf):
    i = pl.program_id(0)
    w = _SwigluWeights(wgu_hbm, wd_hbm, (wgb_ref, wub_ref, wdb_ref), (sg_ref, su_ref, sd_ref), sems, tf)
    xb_ref[...] = x_ref[...].astype(BF16)

    @pl.when(i == 0)
    def _():
        w.request_head(0)
        for j in range(w.nf):
            w.land(0, j)
            w.apply_chunk(xb_ref, acc_ref, j)

    @pl.when(i > 0)
    def _():
        for j in range(w.nf):
            w.apply_chunk(xb_ref, acc_ref, j)

    o_ref[...] = _layer_norm(DEEPNORM_ALPHA * x_ref[...] + acc_ref[...], lng_ref[...], lnb_ref[...])


def _ffn(x2d, w_gu, w_down, ln_g, ln_b, *, tm, tf):
    n, d = x2d.shape
    assert D_FF // tf >= WEIGHT_SLOTS
    return pl.pallas_call(
        functools.partial(_ffn_kernel, tf=tf),
        grid=(n // tm,),
        in_specs=[pl.BlockSpec((tm, d), lambda i: (i, 0)),
                  pl.BlockSpec(memory_space=pl.ANY), pl.BlockSpec(memory_space=pl.ANY),
                  _const_spec((1, d)), _const_spec((1, d))],
        out_specs=pl.BlockSpec((tm, d), lambda i: (i, 0)),
        out_shape=jax.ShapeDtypeStruct((n, d), F32),
        scratch_shapes=[pltpu.VMEM((tm, d), BF16), pltpu.VMEM((tm, d), F32)] + _SwigluWeights.scratch_shapes(tf),
        compiler_params=pltpu.CompilerParams(dimension_semantics=("arbitrary",),
                                             vmem_limit_bytes=VMEM_LIMIT_BYTES),
        name="ffn",
    )(x2d, w_gu[None], w_down[None], ln_g[None, :], ln_b[None, :])


L1_Q, L1_K, L1_V, L1_GATE, L1_AB = 0, 1024, 2048, 3072, 4096
L1_CONV = 3 * GDN_QK
L1_COLS = 4224
L1_BETA_LANE = GDN_HEADS
CONV_PAD = 8
L1_PROJ_BLOCK = 512
L1_CHUNK_GROUP = 2
ROUTE_LANES = LANES


def _pack_l1_w_in(w):
    offs = np.cumsum([0, L1_CONV, GDN_HEADS, GDN_HEADS, GDN_V])
    qkv, a_in, b_in, gate = [w[:, offs[i]:offs[i + 1]] for i in range(4)]
    ab = jnp.pad(jnp.concatenate([a_in, b_in], axis=1), ((0, 0), (0, LANES - 2 * GDN_HEADS)))
    return jnp.concatenate([qkv, gate, ab], axis=1)


def _l1_mixer_kernel(x_ref, w_in_ref, conv_ref, alog_ref, dtb_ref, cnorm_ref, w_out_ref, lng_ref, lnb_ref,
                     tril_ref, wr_hi_ref, wr_lo_ref, br_ref,
                     o_ref, route_ref, h2_ref, qkv_ref, mix_ref, s_ref, u_ref, w_ref, aqk_ref, ext_ref, *, tt):
    ti = pl.program_id(1)

    @pl.when(ti == 0)
    def _():
        s_ref[...] = jnp.zeros_like(s_ref)
        ext_ref[:, 0:CONV_PAD, :] = jnp.zeros((L1_CONV // LANES, CONV_PAD, LANES), F32)

    x = x_ref[...]
    xb = x.astype(BF16)
    h2_ref[...] = jnp.dot(xb, w_in_ref[:, L1_CONV:L1_COLS], preferred_element_type=F32)

    half = tt // 2
    lanes_per_block = L1_PROJ_BLOCK // LANES
    for blk in range(L1_CONV // L1_PROJ_BLOCK):
        h_blk = jnp.dot(xb, w_in_ref[:, blk * L1_PROJ_BLOCK:(blk + 1) * L1_PROJ_BLOCK], preferred_element_type=F32)
        for l in range(lanes_per_block):
            ext_ref[blk * lanes_per_block + l, CONV_PAD:CONV_PAD + tt, :] = h_blk[:, l * LANES:(l + 1) * LANES]
        for l in range(lanes_per_block):
            lb = blk * lanes_per_block + l
            lanes = slice(lb * LANES, (lb + 1) * LANES)
            for parity in range(2):
                conv = None
                for j in range(CONV_WIDTH):
                    first_row = CONV_PAD - (CONV_WIDTH - 1 - j) + parity
                    term = ext_ref[lb, pl.ds(first_row, half, stride=2), :] * conv_ref[j:j + 1, lanes]
                    conv = term if conv is None else conv + term
                act = _silu(conv)
                if lb * LANES < L1_V:
                    scale = GDN_DK ** -0.5 if lb * LANES < L1_K else 1.0
                    act = act * (lax.rsqrt(jnp.sum(act * act, axis=-1, keepdims=True) + L2_EPS) * scale)
                qkv_ref[lb, pl.ds(parity, half, stride=2), :] = act
            ext_ref[lb, 0:CONV_PAD, :] = ext_ref[lb, tt:tt + CONV_PAD, :]

    ab = h2_ref[:, L1_AB - L1_CONV:L1_AB - L1_CONV + LANES]
    g_blk = -jnp.exp(alog_ref[...]) * _softplus(ab + dtb_ref[...])
    beta_blk = _sigmoid(ab)
    gc_blk = _mm_exact_lhs01(tril_ref[...], g_blk)
    eg_blk = jnp.exp(gc_blk)

    ci = lax.broadcasted_iota(jnp.int32, (CHUNK, CHUNK), 0)
    cj = lax.broadcasted_iota(jnp.int32, (CHUNK, CHUNK), 1)
    incl = cj <= ci
    strict = cj < ci

    heads = range(GDN_HEADS)
    col = lambda base, hd: slice(base + hd * GDN_DK, base + (hd + 1) * GDN_DK)
    blk_of = lambda base, hd: base // LANES + hd
    n_chunks = tt // CHUNK

    s = [s_ref[hd] for hd in heads]
    for c0 in range(0, n_chunks, L1_CHUNK_GROUP):
        inst = [(c, hd) for c in range(c0, c0 + L1_CHUNK_GROUP) for hd in heads]
        n_i = range(len(inst))
        rows = [slice(c * CHUNK, (c + 1) * CHUNK) for c, _ in inst]
        gc_c = {c: gc_blk[c * CHUNK:(c + 1) * CHUNK] for c in range(c0, c0 + L1_CHUNK_GROUP)}
        gc_t = {c: gc_c[c].T for c in gc_c}
        kdec_scale = {c: jnp.exp(gc_c[c][CHUNK - 1:CHUNK, :] - gc_c[c]) for c in gc_c}
        q_h = [qkv_ref[blk_of(L1_Q, hd), rows[i], :] for i, (c, hd) in enumerate(inst)]
        k_h = [qkv_ref[blk_of(L1_K, hd), rows[i], :] for i, (c, hd) in enumerate(inst)]
        v_h = [qkv_ref[blk_of(L1_V, hd), rows[i], :] for i, (c, hd) in enumerate(inst)]
        beta = [beta_blk[rows[i], L1_BETA_LANE + hd:L1_BETA_LANE + hd + 1] for i, (c, hd) in enumerate(inst)]
        eg = [eg_blk[rows[i], hd:hd + 1] for i, (c, hd) in enumerate(inst)]
        decay = [jnp.exp(jnp.where(incl, gc_c[c][:, hd:hd + 1] - gc_t[c][hd:hd + 1, :], NEG_BIG)) for c, hd in inst]
        kb = [k_h[i] * beta[i] for i in n_i]
        low = [jnp.where(strict, _mm_nt(kb[i], k_h[i]) * decay[i], 0.0) for i in n_i]
        a_qk = [_mm_nt(q_h[i], k_h[i]) * decay[i] for i in n_i]
        a_m = [-low[i] for i in n_i]
        m = [_mm(low[i], low[i]) for i in n_i]
        for it in range(5):
            am = [_mm(a_m[i], m[i]) for i in n_i]
            a_m = [a_m[i] + m[i] + am[i] for i in n_i]
            if it < 4:
                m = [_mm(m[i], m[i]) for i in n_i]
        rhs = [jnp.concatenate([v_h[i] * beta[i], kb[i] * eg[i]], axis=1) for i in n_i]
        uw = [rhs[i] + _mm(a_m[i], rhs[i]) for i in n_i]
        for i, (c, hd) in enumerate(inst):
            u_ref[rows[i], col(0, hd)] = uw[i][:, 0:GDN_DV]
            w_ref[rows[i], col(0, hd)] = uw[i][:, GDN_DV:GDN_DV + GDN_DK]
            aqk_ref[rows[i], hd * LANES:hd * LANES + CHUNK] = a_qk[i]
            qkv_ref[blk_of(L1_Q, hd), rows[i], :] = q_h[i] * eg[i]
            qkv_ref[blk_of(L1_K, hd), rows[i], :] = k_h[i] * kdec_scale[c][:, hd:hd + 1]

        for c in range(c0, c0 + L1_CHUNK_GROUP):
            crow = slice(c * CHUNK, (c + 1) * CHUNK)
            e_last = jnp.exp(gc_blk[(c + 1) * CHUNK - 1:(c + 1) * CHUNK, :])
            ws_qs = [_mm(jnp.concatenate([w_ref[crow, col(0, hd)], qkv_ref[blk_of(L1_Q, hd), crow, :]], axis=0), s[hd])
                     for hd in heads]
            v_new = [u_ref[crow, col(0, hd)] - ws_qs[hd][0:CHUNK] for hd in heads]
            o_h = [ws_qs[hd][CHUNK:2 * CHUNK] + _mm(aqk_ref[crow, hd * LANES:hd * LANES + CHUNK], v_new[hd])
                   for hd in heads]
            s = [s[hd] * e_last[:, hd:hd + 1] + _mm_tn(qkv_ref[blk_of(L1_K, hd), crow, :], v_new[hd]) for hd in heads]
            for hd in heads:
                mix_ref[crow, col(0, hd)] = o_h[hd]
    for hd in heads:
        s_ref[hd] = s[hd]

    for hd in range(GDN_HEADS):
        sl = slice(hd * GDN_DV, (hd + 1) * GDN_DV)
        o = mix_ref[:, sl]
        o = o * lax.rsqrt(jnp.mean(o * o, axis=-1, keepdims=True) + NORM_EPS) * cnorm_ref[:, sl]
        mix_ref[:, sl] = o * _silu(h2_ref[:, L1_GATE - L1_CONV + hd * GDN_DV:L1_GATE - L1_CONV + (hd + 1) * GDN_DV])
    y = _mm(mix_ref[...], w_out_ref[...])
    x1 = _layer_norm(DEEPNORM_ALPHA * x + y, lng_ref[...], lnb_ref[...])
    o_ref[...] = x1

    x_hi = x1.astype(BF16)
    x_lo = (x1 - x_hi.astype(F32)).astype(BF16)
    logits = (jnp.dot(x_hi, wr_hi_ref[...], preferred_element_type=F32)
              + jnp.dot(x_hi, wr_lo_ref[...], preferred_element_type=F32)
              + jnp.dot(x_lo, wr_hi_ref[...], preferred_element_type=F32)) + br_ref[...]
    lane = lax.broadcasted_iota(jnp.int32, (tt, ROUTE_LANES), 1)
    lane_f = lane.astype(F32)
    logits = jnp.where(lane < N_EXPERTS, logits, NEG_BIG)
    m1 = jnp.max(logits, axis=-1, keepdims=True)
    i1 = jnp.min(jnp.where(logits == m1, lane_f, float(ROUTE_LANES)), axis=-1, keepdims=True)
    rest = jnp.where(lane_f == i1, NEG_BIG, logits)
    m2 = jnp.max(rest, axis=-1, keepdims=True)
    i2 = jnp.min(jnp.where(rest == m2, lane_f, float(ROUTE_LANES)), axis=-1, keepdims=True)
    e21 = jnp.exp(m2 - m1)
    g1 = 1.0 / (1.0 + e21)
    g2 = e21 * g1
    route_ref[...] = jnp.where(lane == 0, i1, jnp.where(lane == 1, i2, jnp.where(lane == 2, g1, jnp.where(lane == 3, g2, 0.0))))


def _l1_mixer(x, w_in, conv_w, a_log, dt_bias, c_norm, w_out, ln_g, ln_b, w_router, b_router, *, tt):
    bsz, seq, d = x.shape
    w_in_p = _pack_l1_w_in(w_in).astype(BF16)
    lane_pad = lambda v: jnp.pad(v[None, :], ((0, 0), (0, LANES - v.shape[0])))
    wr = jnp.pad(w_router, ((0, 0), (0, ROUTE_LANES - N_EXPERTS)))
    wr_hi = wr.astype(BF16)
    wr_lo = (wr - wr_hi.astype(F32)).astype(BF16)
    params = [w_in_p, conv_w, lane_pad(a_log), lane_pad(dt_bias), c_norm.reshape(1, GDN_V), w_out.astype(BF16),
              ln_g[None, :], ln_b[None, :], jnp.asarray(_chunk_tril_np(tt), BF16), wr_hi, wr_lo, lane_pad(b_router)]
    tile = lambda w: pl.BlockSpec((None, tt, w), lambda b, t: (b, t, 0))
    return pl.pallas_call(
        functools.partial(_l1_mixer_kernel, tt=tt),
        grid=(bsz, seq // tt),
        in_specs=[tile(d)] + [_const_spec(a.shape) for a in params],
        out_specs=[tile(d), tile(ROUTE_LANES)],
        out_shape=[jax.ShapeDtypeStruct((bsz, seq, d), F32), jax.ShapeDtypeStruct((bsz, seq, ROUTE_LANES), F32)],
        scratch_shapes=[pltpu.VMEM((tt, L1_COLS - L1_CONV), F32),
                        pltpu.VMEM((L1_CONV // LANES, tt, LANES), F32), pltpu.VMEM((tt, GDN_V), F32),
                        pltpu.VMEM((GDN_HEADS, GDN_DK, GDN_DV), F32),
                        pltpu.VMEM((tt, GDN_V), F32), pltpu.VMEM((tt, GDN_QK), F32),
                        pltpu.VMEM((tt, GDN_HEADS * LANES), F32),
                        pltpu.VMEM((L1_CONV // LANES, tt + CONV_PAD, LANES), F32)],
        compiler_params=pltpu.CompilerParams(dimension_semantics=("arbitrary", "arbitrary"),
                                             vmem_limit_bytes=VMEM_LIMIT_BYTES),
        name="l1_mixer",
    )(x, *params)


ID_ROWS = 4


def _rank_kernel(route_ref, stril_ref, ids_ref, counts_ref, carry_ref):
    @pl.when(pl.program_id(0) == 0)
    def _():
        carry_ref[...] = jnp.zeros_like(carry_ref)

    r = route_ref[...]
    tr = r.shape[0]
    e1, e2 = r[:, 0:1], r[:, 1:2]
    lane = lax.broadcasted_iota(jnp.int32, (tr, ROUTE_LANES), 1)
    lane_f = lane.astype(F32)
    oh1 = (lane_f == e1).astype(F32)
    oh2 = (lane_f == e2).astype(F32)
    both = oh1 + oh2
    before = jnp.dot(stril_ref[...], both.astype(BF16), preferred_element_type=F32) + carry_ref[...]
    rank1 = jnp.sum(oh1 * before, axis=-1, keepdims=True)
    rank2 = jnp.sum(oh2 * before, axis=-1, keepdims=True)
    table = jnp.where(lane == 0, e1, jnp.where(lane == 1, e2, jnp.where(lane == 2, rank1,
                      jnp.where(lane == 3, rank2, 0.0))))
    ids_ref[...] = table.T[0:ID_ROWS, :].astype(jnp.int32)
    carry_ref[...] += jnp.sum(both, axis=0, keepdims=True)
    counts_ref[...] = carry_ref[...]


def _rank(route2d, *, tr):
    n = route2d.shape[0]
    i = np.arange(tr)
    stril = jnp.asarray((i[None, :] < i[:, None]).astype(np.float32), BF16)
    return pl.pallas_call(
        _rank_kernel,
        grid=(n // tr,),
        in_specs=[pl.BlockSpec((tr, ROUTE_LANES), lambda i: (i, 0)), _const_spec((tr, tr))],
        out_specs=[pl.BlockSpec((ID_ROWS, tr), lambda i: (0, i)), _const_spec((1, ROUTE_LANES))],
        out_shape=[jax.ShapeDtypeStruct((ID_ROWS, n), jnp.int32), jax.ShapeDtypeStruct((1, ROUTE_LANES), F32)],
        scratch_shapes=[pltpu.VMEM((1, ROUTE_LANES), F32)],
        compiler_params=pltpu.CompilerParams(dimension_semantics=("arbitrary",)),
        name="moe_rank",
    )(route2d, stril)


ROW_SUB = D_MODEL // LANES


def _to_row_tiles(dst_ref, src, n):
    for lb in range(ROW_SUB):
        dst_ref[pl.ds(lb, n, stride=ROW_SUB), :] = src[:, lb * LANES:(lb + 1) * LANES]


def _from_row_tiles(src_ref, n):
    return [src_ref[pl.ds(lb, n, stride=ROW_SUB), :] for lb in range(ROW_SUB)]


def _tile_copy(src_ref, src_sub, dst_ref, dst_sub, sem):
    src = src_ref.at[pl.ds(pl.multiple_of(src_sub, ROW_SUB), ROW_SUB), :]
    dst = dst_ref.at[pl.ds(pl.multiple_of(dst_sub, ROW_SUB), ROW_SUB), :]
    return pltpu.make_async_copy(src, dst, sem)


def _wait_rows(hbm_ref, buf_ref, sem, n, copies):
    for _ in range(copies):
        pltpu.make_async_copy(hbm_ref.at[pl.ds(0, n * ROW_SUB), :], buf_ref, sem).wait()


ZERO_ROWS = 64


def _dispatch_kernel(zfill_ref, p0_ref, p1_ref, x_ref, xs_hbm, xt0_ref, xt1_ref, zero_ref, sems, zsem, *, td):
    p_refs = (p0_ref, p1_ref)
    i = pl.program_id(0)
    last = pl.num_programs(0) - 1

    @pl.when(i == 0)
    def _():
        zero_ref[...] = jnp.zeros_like(zero_ref)
        n_ranges = zfill_ref.shape[0] // 2

        def zero_copy(start_sub, r):
            dst = xs_hbm.at[pl.ds(pl.multiple_of(start_sub + r * (ZERO_ROWS * ROW_SUB), ROW_SUB),
                                  ZERO_ROWS * ROW_SUB), :]
            return pltpu.make_async_copy(zero_ref, dst, zsem)

        def fill(e, wait):
            def body(r, carry):
                copy = zero_copy(zfill_ref[e], r)
                if wait:
                    copy.wait()
                else:
                    copy.start()
                return carry

            lax.fori_loop(0, zfill_ref[n_ranges + e], body, 0)

        for e in range(n_ranges):
            fill(e, wait=False)
        for e in range(n_ranges):
            fill(e, wait=True)

    def step(xt_ref, sem):
        @pl.when(i >= 2)
        def _():
            _wait_rows(xs_hbm, xt_ref, sem, td, TOP_K)

        _to_row_tiles(xt_ref, x_ref[...], td)

        def issue(t, carry):
            for k in range(TOP_K):
                _tile_copy(xt_ref, t * ROW_SUB, xs_hbm, p_refs[k][t], sem).start(priority=k)
            return carry

        lax.fori_loop(0, td, issue, 0, unroll=8)

    for slot, (xt_ref, sem) in enumerate(((xt0_ref, sems.at[0]), (xt1_ref, sems.at[1]))):
        @pl.when(i % 2 == slot)
        def _():
            step(xt_ref, sem)

    @pl.when(i == last)
    def _():
        _wait_rows(xs_hbm, xt0_ref, sems.at[0], td, TOP_K)
        _wait_rows(xs_hbm, xt1_ref, sems.at[1], td, TOP_K)


def _dispatch(zfill, pos, x2d, n_rows, *, td):
    n, d = x2d.shape
    smem_rows = pl.BlockSpec((td,), lambda i, zf: (i,), memory_space=pltpu.SMEM)
    assert n // td >= 2
    grid_spec = pltpu.PrefetchScalarGridSpec(
        num_scalar_prefetch=1,
        grid=(n // td,),
        in_specs=[smem_rows, smem_rows, pl.BlockSpec((td, d), lambda i, zf: (i, 0))],
        out_specs=pl.BlockSpec(memory_space=pl.ANY),
        scratch_shapes=[pltpu.VMEM((td * ROW_SUB, LANES), F32), pltpu.VMEM((td * ROW_SUB, LANES), F32),
                        pltpu.VMEM((ZERO_ROWS * ROW_SUB, LANES), F32),
                        pltpu.SemaphoreType.DMA((2,)), pltpu.SemaphoreType.DMA(())],
    )
    return pl.pallas_call(
        functools.partial(_dispatch_kernel, td=td),
        grid_spec=grid_spec,
        out_shape=jax.ShapeDtypeStruct((n_rows * ROW_SUB, LANES), F32),
        compiler_params=pltpu.CompilerParams(dimension_semantics=("arbitrary",)),
        name="moe_dispatch",
    )(zfill, pos[0], pos[1], x2d)


def _gmm_kernel(te_ref, used_ref, first_ref, x_ref, wgu_hbm, wd_hbm, o_ref,
                xb_ref, acc_ref, wgb_ref, wub_ref, wdb_ref, sg_ref, su_ref, sd_ref, sems, *, tm, tf):
    i = pl.program_id(0)
    n_tiles = pl.num_programs(0)
    w = _SwigluWeights(wgu_hbm, wd_hbm, (wgb_ref, wub_ref, wdb_ref), (sg_ref, su_ref, sd_ref), sems, tf)
    active = i < used_ref[0]
    is_first = first_ref[i] == 1

    @pl.when(active)
    def _():
        for lb, blk in enumerate(_from_row_tiles(x_ref, tm)):
            xb_ref[:, lb * LANES:(lb + 1) * LANES] = blk.astype(BF16)

    @pl.when(jnp.logical_and(active, is_first))
    def _():
        e = te_ref[i]

        @pl.when(i == 0)
        def _():
            w.request_head(e)

        for j in range(w.nf):
            w.land(e, j)
            w.apply_chunk(xb_ref, acc_ref, j)

    @pl.when(jnp.logical_and(active, jnp.logical_not(is_first)))
    def _():
        for j in range(w.nf):
            w.apply_chunk(xb_ref, acc_ref, j)

    @pl.when(active)
    def _():
        _to_row_tiles(o_ref, acc_ref[...], tm)
        nxt = jnp.minimum(i + 1, n_tiles - 1)

        @pl.when(jnp.logical_and(i + 1 < n_tiles, first_ref[nxt] == 1))
        def _():
            w.request_head(te_ref[nxt])

    @pl.when(jnp.logical_not(active))
    def _():
        o_ref[...] = jnp.zeros_like(o_ref)


def _gmm(tile_expert, n_used, tile_first, xs, w_gu, w_down, *, tm, tf):
    d = D_MODEL
    n_rows = xs.shape[0] // ROW_SUB
    nf = D_FF // tf
    n_tiles = n_rows // tm
    assert nf >= WEIGHT_SLOTS
    grid_spec = pltpu.PrefetchScalarGridSpec(
        num_scalar_prefetch=3,
        grid=(n_tiles,),
        in_specs=[pl.BlockSpec((tm * ROW_SUB, LANES), lambda i, te, used, first: (jnp.minimum(i, used[0] - 1), 0)),
                  pl.BlockSpec(memory_space=pl.ANY), pl.BlockSpec(memory_space=pl.ANY)],
        out_specs=pl.BlockSpec((tm * ROW_SUB, LANES), lambda i, te, used, first: (i, 0)),
        scratch_shapes=[pltpu.VMEM((tm, d), BF16), pltpu.VMEM((tm, d), F32)] + _SwigluWeights.scratch_shapes(tf),
    )
    return pl.pallas_call(
        functools.partial(_gmm_kernel, tm=tm, tf=tf),
        grid_spec=grid_spec,
        out_shape=jax.ShapeDtypeStruct((n_rows * ROW_SUB, LANES), F32),
        compiler_params=pltpu.CompilerParams(dimension_semantics=("arbitrary",),
                                             vmem_limit_bytes=VMEM_LIMIT_BYTES),
        name="moe_gmm",
    )(tile_expert, n_used, tile_first, xs, w_gu, w_down)


def _combine_kernel(p0_ref, p1_ref, p0_next_ref, p1_next_ref, x_ref, route_ref, lng_ref, lnb_ref, y_hbm, o_ref,
                    ya0_ref, yb0_ref, ya1_ref, yb1_ref, sems, *, tc):
    i = pl.program_id(0)
    n = pl.num_programs(0)
    slots = ((ya0_ref, yb0_ref, sems.at[0]), (ya1_ref, yb1_ref, sems.at[1]))

    def issue(p_refs, slot):
        ya_ref, yb_ref, sem = slots[slot]

        def body(t, carry):
            for k, (p_ref, buf) in enumerate(zip(p_refs, (ya_ref, yb_ref))):
                _tile_copy(y_hbm, p_ref[t], buf, t * ROW_SUB, sem).start(priority=k)
            return carry

        lax.fori_loop(0, tc, body, 0, unroll=8)

    @pl.when(i == 0)
    def _():
        issue((p0_ref, p1_ref), 0)

    for slot in range(2):
        @pl.when(jnp.logical_and(i + 1 < n, (i + 1) % 2 == slot))
        def _():
            issue((p0_next_ref, p1_next_ref), slot)

    for slot in range(2):
        @pl.when(i % 2 == slot)
        def _():
            ya_ref, yb_ref, sem = slots[slot]
            _wait_rows(y_hbm, ya_ref, sem, tc, TOP_K)
            r = route_ref[...]
            g1, g2 = r[:, 2:3], r[:, 3:4]
            y = jnp.concatenate(
                [g1 * a + g2 * b for a, b in zip(_from_row_tiles(ya_ref, tc), _from_row_tiles(yb_ref, tc))], axis=1)
            o_ref[...] = _layer_norm(DEEPNORM_ALPHA * x_ref[...] + y, lng_ref[...], lnb_ref[...])


def _combine(pos, x2d, route2d, ys, ln_g, ln_b, *, tc):
    n, d = x2d.shape
    steps = n // tc
    smem_rows = pl.BlockSpec((tc,), lambda i: (i,), memory_space=pltpu.SMEM)
    smem_next = pl.BlockSpec((tc,), lambda i: (jnp.minimum(i + 1, steps - 1),), memory_space=pltpu.SMEM)
    return pl.pallas_call(
        functools.partial(_combine_kernel, tc=tc),
        grid=(steps,),
        in_specs=[smem_rows, smem_rows, smem_next, smem_next,
                  pl.BlockSpec((tc, d), lambda i: (i, 0)),
                  pl.BlockSpec((tc, ROUTE_LANES), lambda i: (i, 0)),
                  _const_spec((1, d)), _const_spec((1, d)),
                  pl.BlockSpec(memory_space=pl.ANY)],
        out_specs=pl.BlockSpec((tc, d), lambda i: (i, 0)),
        scratch_shapes=[pltpu.VMEM((tc * ROW_SUB, LANES), F32) for _ in range(2 * TOP_K)]
                       + [pltpu.SemaphoreType.DMA((2,))],
        out_shape=jax.ShapeDtypeStruct((n, d), F32),
        compiler_params=pltpu.CompilerParams(dimension_semantics=("arbitrary",),
                                             vmem_limit_bytes=VMEM_LIMIT_BYTES),
        name="moe_combine",
    )(pos[0], pos[1], pos[0], pos[1], x2d, route2d, ln_g[None, :], ln_b[None, :], ys)


def _moe(x2d, route2d, w_gu, w_down, ln_g, ln_b, *, tm, tf, tr, td, tc):
    n, d = x2d.shape
    ids, counts = _rank(route2d, tr=tr)
    cnt = counts[0, :N_EXPERTS].astype(jnp.int32)
    tiles_per = (cnt + tm - 1) // tm
    tile_end = jnp.cumsum(tiles_per)
    offs = (tile_end - tiles_per) * tm
    n_tiles = (n * TOP_K) // tm + N_EXPERTS
    n_used = tile_end[-1:]
    t_idx = jnp.arange(n_tiles, dtype=jnp.int32)
    tile_expert = jnp.sum(jnp.minimum(t_idx, n_used - 1)[:, None] >= tile_end[None, :], axis=1).astype(jnp.int32)
    tile_start = tile_end - tiles_per
    tile_first = jnp.logical_and(t_idx == tile_start[tile_expert], t_idx < n_used).astype(jnp.int32)
    group_start = sum(jnp.where(ids[0:TOP_K] == e, offs[e], 0) for e in range(N_EXPERTS))
    pos = (group_start + ids[TOP_K:2 * TOP_K]) * ROW_SUB
    pad_start = ((offs + cnt) // ZERO_ROWS) * ZERO_ROWS
    pad_blocks = (offs + tiles_per * tm - pad_start) // ZERO_ROWS
    tail_start = n_used * tm
    tail_blocks = (n_tiles - n_used) * (tm // ZERO_ROWS)
    zfill = jnp.concatenate([pad_start * ROW_SUB, tail_start * ROW_SUB, pad_blocks, tail_blocks]).astype(jnp.int32)
    xs = _dispatch(zfill, pos, x2d, n_tiles * tm, td=td)
    ys = _gmm(tile_expert, n_used, tile_first, xs, w_gu, w_down, tm=tm, tf=tf)
    return _combine(pos, x2d, route2d, ys, ln_g, ln_b, tc=tc)


def kernel(x, positions, ab_w_in, gla_w_gate2, gla_b_gate, gla_norm, ret_norm, ab_w_out, ab_ln1_g, ab_ln1_b, ffn_w_gu, ffn_w_down, ab_ln2_g, ab_ln2_b, c_w_in, c_conv_w, c_a_log, c_dt_bias, c_norm, c_w_out, c_ln1_g, c_ln1_b, moe_w_router, moe_b_router, moe_w_gu, moe_w_down, c_ln2_g, c_ln2_b):
    bsz, seq, d = x.shape
    for layer in range(DEPTH):
        i = layer // 2
        if layer % 2 == 0:
            x = _l0_mixer(x, positions, ab_w_in[i], gla_w_gate2[i], gla_b_gate[i], gla_norm[i], ret_norm[i],
                          ab_w_out[i], ab_ln1_g[i], ab_ln1_b[i], tt=L0_TIME_TILE)
            x = _ffn(x.reshape(bsz * seq, d), ffn_w_gu[i], ffn_w_down[i], ab_ln2_g[i], ab_ln2_b[i],
                     tm=FFN_ROW_TILE, tf=FF_TILE).reshape(bsz, seq, d)
        else:
            x, route = _l1_mixer(x, c_w_in[i], c_conv_w[i], c_a_log[i], c_dt_bias[i], c_norm[i], c_w_out[i],
                                 c_ln1_g[i], c_ln1_b[i], moe_w_router[i], moe_b_router[i], tt=L1_TIME_TILE)
            x = _moe(x.reshape(bsz * seq, d), route.reshape(bsz * seq, ROUTE_LANES), moe_w_gu[i], moe_w_down[i],
                     c_ln2_g[i], c_ln2_b[i], tm=MOE_ROW_TILE, tf=FF_TILE, tr=ROUTE_TILE, td=ROUTE_TILE,
                     tc=ROUTE_TILE).reshape(bsz, seq, d)
    return x
```

```python
import functools

import numpy as np
import jax
import jax.numpy as jnp
from jax import lax
from jax.experimental import pallas as pl
from jax.experimental.pallas import tpu as pltpu

F32 = jnp.float32
BF16 = jnp.bfloat16

D_MODEL = 1024
DEPTH = 2
CHUNK = 64
GLA_HEADS, GLA_DK, GLA_DV, GLA_GATE_RANK, GLA_TAU = 4, 64, 128, 16, 16.0
RET_HEADS, RET_DK, RET_DV = 4, 64, 128
ROPE_BASE = 10000.0
GDN_HEADS, GDN_DK, GDN_DV = 8, 128, 128
CONV_WIDTH = 4
D_FF = 3584
N_EXPERTS = 8
TOP_K = 2
NORM_EPS = 1e-5
L2_EPS = 1e-6
DEEPNORM_ALPHA = (2.0 * DEPTH) ** 0.25

GLA_QK = GLA_HEADS * GLA_DK
GLA_V = GLA_HEADS * GLA_DV
RET_QK = RET_HEADS * RET_DK
RET_V = RET_HEADS * RET_DV
GDN_QK = GDN_HEADS * GDN_DK
GDN_V = GDN_HEADS * GDN_DV

LANES = 128
VMEM_LIMIT_BYTES = 56 * 1024 * 1024
NEG_BIG = -1e30

L0_TIME_TILE = 256
L1_TIME_TILE = 256
FFN_ROW_TILE = 512
MOE_ROW_TILE = 512
FF_TILE = 512
ROUTE_TILE = 512


def _mm(a, b):
    return jnp.dot(a.astype(BF16), b.astype(BF16), preferred_element_type=F32)


def _mm_nt(a, b):
    return lax.dot_general(a.astype(BF16), b.astype(BF16), (((1,), (1,)), ((), ())),
                           preferred_element_type=F32)


def _mm_tn(a, b):
    return lax.dot_general(a.astype(BF16), b.astype(BF16), (((0,), (0,)), ((), ())),
                           preferred_element_type=F32)


def _split3(x):
    hi = x.astype(BF16)
    r1 = x - hi.astype(F32)
    mid = r1.astype(BF16)
    lo = (r1 - mid.astype(F32)).astype(BF16)
    return hi, mid, lo


def _mm_exact_lhs01(m01, x):
    hi, mid, lo = _split3(x)
    return (jnp.dot(m01, hi, preferred_element_type=F32)
            + jnp.dot(m01, mid, preferred_element_type=F32)
            + jnp.dot(m01, lo, preferred_element_type=F32))


def _sigmoid(x):
    return 0.5 * jnp.tanh(0.5 * x) + 0.5


def _silu(x):
    return x * _sigmoid(x)


def _softplus(x):
    return jnp.maximum(x, 0.0) + jnp.log(1.0 + jnp.exp(-jnp.abs(x)))


def _layer_norm(x, g, b):
    mu = jnp.mean(x, axis=-1, keepdims=True)
    xc = x - mu
    var = jnp.mean(xc * xc, axis=-1, keepdims=True)
    return xc * lax.rsqrt(var + NORM_EPS) * g + b


def _chunk_tril_np(tt):
    i = np.arange(tt)
    same = (i[:, None] // CHUNK) == (i[None, :] // CHUNK)
    return (same & (i[None, :] <= i[:, None])).astype(np.float32)


def _const_spec(shape):
    nd = len(shape)
    return pl.BlockSpec(shape, lambda *_: (0,) * nd)


L0_GQ, L0_GK, L0_GV, L0_GR = 0, 256, 512, 1024
L0_RQ, L0_RK, L0_RV, L0_RG = 1536, 1792, 2048, 2560
L0_GA = 3072
L0_COLS = 3200
L0_CHUNK_GROUP = 1


def _pack_l0_w_in(w):
    offs = np.cumsum([0, GLA_QK, GLA_QK, GLA_V, GLA_GATE_RANK, GLA_V, RET_QK, RET_QK, RET_V, RET_V])
    gq, gk, gv, ga, gr, rq, rk, rv, rg = [w[:, offs[i]:offs[i + 1]] for i in range(9)]
    ga = jnp.pad(ga, ((0, 0), (0, LANES - GLA_GATE_RANK)))
    return jnp.concatenate([gq, gk, gv, gr, rq, rk, rv, rg, ga], axis=1)


def _ret_tables():
    h = np.arange(RET_HEADS, dtype=np.float64)
    log_gamma = np.log(1.0 - 2.0 ** (-5.0 - h))
    pos = np.arange(CHUNK, dtype=np.float64)
    diff = pos[:, None] - pos[None, :]
    dmat = np.where(diff >= 0, np.exp(log_gamma[:, None, None] * np.maximum(diff, 0.0)), 0.0)
    xi = np.exp(log_gamma[None, :] * (pos[:, None] + 1.0))
    zeta = np.exp(log_gamma[None, :] * (CHUNK - 1.0 - pos[:, None]))
    decay = np.exp(log_gamma * CHUNK)
    xi_full = np.repeat(xi, RET_DV, axis=1)
    zeta_full = np.repeat(zeta, RET_DK, axis=1)
    decay_full = np.repeat(decay, RET_DK)[None, :]
    return (dmat.astype(np.float32), xi_full.astype(np.float32), zeta_full.astype(np.float32),
            decay_full.astype(np.float32))


def _rope_tables():
    half = RET_DK // 2
    inv_freq = ROPE_BASE ** (-np.arange(0, RET_DK, 2, dtype=np.float32) / RET_DK)
    per_head = np.concatenate([inv_freq, inv_freq])
    freq_full = np.tile(per_head, RET_HEADS)[None, :].astype(np.float32)
    sign = np.tile(np.concatenate([-np.ones(half), np.ones(half)]), RET_HEADS)[None, :].astype(np.float32)
    return freq_full, sign


def _l0_mixer_kernel(x_ref, pos_ref, w_in_ref, wg2_ref, bg_ref, gnorm_ref, rnorm_ref, w_out_ref,
                     lng_ref, lnb_ref, tril_ref, dmat_ref, xi_ref, zeta_ref, rdecay_ref,
                     freq_ref, sign_ref, o_ref, h_ref, mix_ref, sg_ref, sr_ref, *, tt):
    ti = pl.program_id(1)

    @pl.when(ti == 0)
    def _():
        sg_ref[...] = jnp.zeros_like(sg_ref)
        sr_ref[...] = jnp.zeros_like(sr_ref)

    x = x_ref[...]
    h_ref[...] = _mm(x, w_in_ref[...])

    lane = lax.broadcasted_iota(jnp.int32, (1, LANES), 1)
    lo_half = lane < GLA_DK
    ci = lax.broadcasted_iota(jnp.int32, (CHUNK, CHUNK), 0)
    cj = lax.broadcasted_iota(jnp.int32, (CHUNK, CHUNK), 1)
    causal = cj <= ci

    z = _mm(h_ref[:, L0_GA:L0_GA + LANES], wg2_ref[...]) + bg_ref[...]
    log_a = -_softplus(-z) * (1.0 / GLA_TAU)
    b = _mm_exact_lhs01(tril_ref[...], log_a)
    eb = jnp.exp(b)
    q_dec = h_ref[:, L0_GQ:L0_GQ + GLA_QK] * (GLA_DK ** -0.5) * eb
    k_all = h_ref[:, L0_GK:L0_GK + GLA_QK]
    k_neg = k_all * jnp.exp(-b)

    ang = pos_ref[...] * freq_ref[:, 0:LANES]
    cos = jnp.concatenate([jnp.cos(ang)] * (RET_QK // LANES), axis=1)
    sin = jnp.concatenate([jnp.sin(ang)] * (RET_QK // LANES), axis=1) * sign_ref[...]
    half = RET_DK // 2
    lane256 = lax.broadcasted_iota(jnp.int32, (1, RET_QK), 1)
    first_half = (lane256 & (RET_DK - 1)) < half

    def rope(t):
        swapped = jnp.where(first_half, pltpu.roll(t, RET_QK - half, 1), pltpu.roll(t, half, 1))
        return t * cos + swapped * sin

    rq = rope(h_ref[:, L0_RQ:L0_RQ + RET_QK]) * (RET_DK ** -0.5)
    rk = rope(h_ref[:, L0_RK:L0_RK + RET_QK])

    n_chunks = tt // CHUNK
    heads = range(GLA_HEADS)
    pairs = range(GLA_HEADS // 2)
    pair_lanes = lambda hd: slice((hd // 2) * LANES, (hd // 2 + 1) * LANES)
    head_mask = lambda hd: lo_half if hd % 2 == 0 else jnp.logical_not(lo_half)
    vcol = lambda base, hd: slice(base + hd * GLA_DV, base + (hd + 1) * GLA_DV)
    sg = [sg_ref[p] for p in pairs]
    sr = [sr_ref[p] for p in pairs]
    for c0 in range(0, n_chunks, L0_CHUNK_GROUP):
        group = range(c0, c0 + L0_CHUNK_GROUP)
        rows = {c: slice(c * CHUNK, (c + 1) * CHUNK) for c in group}
        b_last = {c: b[(c + 1) * CHUNK - 1:(c + 1) * CHUNK, :] for c in group}
        k_end = {c: k_all[rows[c]] * jnp.exp(b_last[c] - b[rows[c]]) for c in group}
        rk_c = {c: rk[rows[c]] for c in group}
        rk_z = {c: rk_c[c] * zeta_ref[...] for c in group}
        inst = [(c, hd) for c in group for hd in heads]
        g_q = {ch: jnp.where(head_mask(ch[1]), q_dec[rows[ch[0]], pair_lanes(ch[1])], 0.0) for ch in inst}
        r_q = {ch: jnp.where(head_mask(ch[1]), rq[rows[ch[0]], pair_lanes(ch[1])], 0.0) for ch in inst}
        g_v = {ch: h_ref[rows[ch[0]], vcol(L0_GV, ch[1])] for ch in inst}
        r_v = {ch: h_ref[rows[ch[0]], vcol(L0_RV, ch[1])] for ch in inst}
        g_att = {(c, hd): jnp.where(causal, _mm_nt(g_q[c, hd], k_neg[rows[c], pair_lanes(hd)]), 0.0)
                 for c, hd in inst}
        r_att = {(c, hd): _mm_nt(r_q[c, hd], rk_c[c][:, pair_lanes(hd)]) * dmat_ref[hd] for c, hd in inst}
        g_d = {(c, hd): _mm_tn(g_v[c, hd], k_end[c][:, pair_lanes(hd)]) for c, hd in inst}
        r_d = {(c, hd): _mm_tn(r_v[c, hd], rk_z[c][:, pair_lanes(hd)]) for c, hd in inst}
        sg_in, sr_in = {}, {}
        for c in group:
            sg_in[c], sr_in[c] = sg, sr
            dec_c = jnp.exp(b_last[c])
            sg = [sg[p] * dec_c[:, p * LANES:(p + 1) * LANES] + jnp.where(lo_half, g_d[c, 2 * p], g_d[c, 2 * p + 1])
                  for p in pairs]
            sr = [sr[p] * rdecay_ref[:, p * LANES:(p + 1) * LANES]
                  + jnp.where(lo_half, r_d[c, 2 * p], r_d[c, 2 * p + 1]) for p in pairs]
        g_o = {(c, hd): _mm(g_att[c, hd], g_v[c, hd]) + _mm_nt(g_q[c, hd], sg_in[c][hd // 2]) for c, hd in inst}
        r_o = {(c, hd): _mm(r_att[c, hd], r_v[c, hd])
               + _mm_nt(r_q[c, hd], sr_in[c][hd // 2]) * xi_ref[:, vcol(0, hd)] for c, hd in inst}
        for c, hd in inst:
            mix_ref[rows[c], vcol(0, hd)] = g_o[c, hd]
            mix_ref[rows[c], vcol(GLA_V, hd)] = r_o[c, hd]
    for p in pairs:
        sg_ref[p] = sg[p]
        sr_ref[p] = sr[p]

    for hd in range(GLA_HEADS):
        sl = slice(hd * GLA_DV, (hd + 1) * GLA_DV)
        o = mix_ref[:, sl]
        o = o * lax.rsqrt(jnp.mean(o * o, axis=-1, keepdims=True) + NORM_EPS) * gnorm_ref[:, sl]
        mix_ref[:, sl] = o * _silu(h_ref[:, L0_GR + hd * GLA_DV:L0_GR + (hd + 1) * GLA_DV])
    for hd in range(RET_HEADS):
        sl = slice(hd * RET_DV, (hd + 1) * RET_DV)
        o = mix_ref[:, GLA_V + hd * RET_DV:GLA_V + (hd + 1) * RET_DV]
        oc = o - jnp.mean(o, axis=-1, keepdims=True)
        o = oc * lax.rsqrt(jnp.mean(oc * oc, axis=-1, keepdims=True) + NORM_EPS) * rnorm_ref[:, sl]
        mix_ref[:, GLA_V + hd * RET_DV:GLA_V + (hd + 1) * RET_DV] = (
            o * _silu(h_ref[:, L0_RG + hd * RET_DV:L0_RG + (hd + 1) * RET_DV]))

    y = _mm(mix_ref[...], w_out_ref[...])
    o_ref[...] = _layer_norm(DEEPNORM_ALPHA * x + y, lng_ref[...], lnb_ref[...])


def _l0_mixer(x, positions, w_in, w_gate2, b_gate, gla_norm, ret_norm, w_out, ln_g, ln_b, *, tt):
    bsz, seq, d = x.shape
    w_in_p = _pack_l0_w_in(w_in).astype(BF16)
    wg2 = jnp.pad(w_gate2, ((0, LANES - GLA_GATE_RANK), (0, 0))).astype(BF16)
    dmat, xi_full, zeta_full, rdecay = _ret_tables()
    freq_full, sign = _rope_tables()
    pos_f = positions.astype(F32)[..., None]
    consts = [jnp.asarray(_chunk_tril_np(tt), BF16), jnp.asarray(dmat), jnp.asarray(xi_full),
              jnp.asarray(zeta_full), jnp.asarray(rdecay), jnp.asarray(freq_full), jnp.asarray(sign)]
    params = [w_in_p, wg2, b_gate[None, :], gla_norm.reshape(1, GLA_V), ret_norm.reshape(1, RET_V),
              w_out.astype(BF16), ln_g[None, :], ln_b[None, :]]
    tile = lambda w: pl.BlockSpec((None, tt, w), lambda b, t: (b, t, 0))
    return pl.pallas_call(
        functools.partial(_l0_mixer_kernel, tt=tt),
        grid=(bsz, seq // tt),
        in_specs=[tile(d), tile(1)] + [_const_spec(a.shape) for a in params + consts],
        out_specs=tile(d),
        out_shape=jax.ShapeDtypeStruct((bsz, seq, d), F32),
        scratch_shapes=[pltpu.VMEM((tt, L0_COLS), F32), pltpu.VMEM((tt, GLA_V + RET_V), F32),
                        pltpu.VMEM((GLA_HEADS // 2, GLA_DV, LANES), F32),
                        pltpu.VMEM((RET_HEADS // 2, RET_DV, LANES), F32)],
        compiler_params=pltpu.CompilerParams(dimension_semantics=("arbitrary", "arbitrary"),
                                             vmem_limit_bytes=VMEM_LIMIT_BYTES),
        name="l0_mixer",
    )(x, pos_f, *params, *consts)


WEIGHT_SLOTS = 2


class _SwigluWeights:
    def __init__(self, wgu_hbm, wd_hbm, resident, staging, sems, tf):
        self.wgu_hbm, self.wd_hbm = wgu_hbm, wd_hbm
        self.wgb, self.wub, self.wdb = resident
        self.sg, self.su, self.sd = staging
        self.sems, self.tf, self.nf = sems, tf, D_FF // tf

    def _copies(self, e, j):
        slot = j % WEIGHT_SLOTS
        cols = pl.ds(j * self.tf, self.tf)
        up_cols = pl.ds(D_FF + j * self.tf, self.tf)
        return (pltpu.make_async_copy(self.wgu_hbm.at[e, :, cols], self.sg.at[slot], self.sems.at[slot]),
                pltpu.make_async_copy(self.wgu_hbm.at[e, :, up_cols], self.su.at[slot], self.sems.at[slot]),
                pltpu.make_async_copy(self.wd_hbm.at[e, cols, :], self.sd.at[slot], self.sems.at[slot]))

    def request(self, e, j):
        for c in self._copies(e, j):
            c.start()

    def request_head(self, e):
        for j in range(WEIGHT_SLOTS):
            self.request(e, j)

    def land(self, e, j):
        slot = j % WEIGHT_SLOTS
        for c in self._copies(e, j):
            c.wait()
        self.wgb[j] = self.sg[slot].astype(BF16)
        self.wub[j] = self.su[slot].astype(BF16)
        self.wdb[j] = self.sd[slot].astype(BF16)
        if j + WEIGHT_SLOTS < self.nf:
            self.request(e, j + WEIGHT_SLOTS)

    def apply_chunk(self, xb_ref, acc_ref, j):
        xb = xb_ref[...]
        gt = jnp.dot(xb, self.wgb[j], preferred_element_type=F32)
        up = jnp.dot(xb, self.wub[j], preferred_element_type=F32)
        part = jnp.dot((_silu(gt) * up).astype(BF16), self.wdb[j], preferred_element_type=F32)
        if j == 0:
            acc_ref[...] = part
        else:
            acc_ref[...] += part

    @staticmethod
    def scratch_shapes(tf):
        nf, d = D_FF // tf, D_MODEL
        return [pltpu.VMEM((nf, d, tf), BF16), pltpu.VMEM((nf, d, tf), BF16), pltpu.VMEM((nf, tf, d), BF16),
                pltpu.VMEM((WEIGHT_SLOTS, d, tf), F32), pltpu.VMEM((WEIGHT_SLOTS, d, tf), F32),
                pltpu.VMEM((WEIGHT_SLOTS, tf, d), F32), pltpu.SemaphoreType.DMA((WEIGHT_SLOTS,))]


def _ffn_kernel(x_ref, wgu_hbm, wd_hbm, lng_ref, lnb_ref, o_ref, xb_ref, acc_ref,
                wgb_ref, wub_ref, wdb_ref, sg_ref, su_ref, sd_ref, sems, *, tf):
    i = pl.program_id(0)
    w = _SwigluWeights(wgu_hbm, wd_hbm, (wgb_ref, wub_ref, wdb_ref), (sg_ref, su_ref, sd_ref), sems, tf)
    xb_ref[...] = x_ref[...].astype(BF16)

    @pl.when(i == 0)
    def _():
        w.request_head(0)
        for j in range(w.nf):
            w.land(0, j)
            w.apply_chunk(xb_ref, acc_ref, j)

    @pl.when(i > 0)
    def _():
        for j in range(w.nf):
            w.apply_chunk(xb_ref, acc_ref, j)

    o_ref[...] = _layer_norm(DEEPNORM_ALPHA * x_ref[...] + acc_ref[...], lng_ref[...], lnb_ref[...])


def _ffn(x2d, w_gu, w_down, ln_g, ln_b, *, tm, tf):
    n, d = x2d.shape
    assert D_FF // tf >= WEIGHT_SLOTS
    return pl.pallas_call(
        functools.partial(_ffn_kernel, tf=tf),
        grid=(n // tm,),
        in_specs=[pl.BlockSpec((tm, d), lambda i: (i, 0)),
                  pl.BlockSpec(memory_space=pl.ANY), pl.BlockSpec(memory_space=pl.ANY),
                  _const_spec((1, d)), _const_spec((1, d))],
        out_specs=pl.BlockSpec((tm, d), lambda i: (i, 0)),
        out_shape=jax.ShapeDtypeStruct((n, d), F32),
        scratch_shapes=[pltpu.VMEM((tm, d), BF16), pltpu.VMEM((tm, d), F32)] + _SwigluWeights.scratch_shapes(tf),
        compiler_params=pltpu.CompilerParams(dimension_semantics=("arbitrary",),
                                             vmem_limit_bytes=VMEM_LIMIT_BYTES),
        name="ffn",
    )(x2d, w_gu[None], w_down[None], ln_g[None, :], ln_b[None, :])


L1_Q, L1_K, L1_V, L1_GATE, L1_AB = 0, 1024, 2048, 3072, 4096
L1_CONV = 3 * GDN_QK
L1_COLS = 4224
L1_BETA_LANE = GDN_HEADS
CONV_PAD = 8
L1_PROJ_BLOCK = 512
L1_CHUNK_GROUP = 2
ROUTE_LANES = LANES


def _pack_l1_w_in(w):
    offs = np.cumsum([0, L1_CONV, GDN_HEADS, GDN_HEADS, GDN_V])
    qkv, a_in, b_in, gate = [w[:, offs[i]:offs[i + 1]] for i in range(4)]
    ab = jnp.pad(jnp.concatenate([a_in, b_in], axis=1), ((0, 0), (0, LANES - 2 * GDN_HEADS)))
    return jnp.concatenate([qkv, gate, ab], axis=1)


def _l1_mixer_kernel(x_ref, w_in_ref, conv_ref, alog_ref, dtb_ref, cnorm_ref, w_out_ref, lng_ref, lnb_ref,
                     tril_ref, wr_hi_ref, wr_lo_ref, br_ref,
                     o_ref, route_ref, h2_ref, qkv_ref, mix_ref, s_ref, u_ref, w_ref, aqk_ref, ext_ref, *, tt):
    ti = pl.program_id(1)

    @pl.when(ti == 0)
    def _():
        s_ref[...] = jnp.zeros_like(s_ref)
        ext_ref[:, 0:CONV_PAD, :] = jnp.zeros((L1_CONV // LANES, CONV_PAD, LANES), F32)

    x = x_ref[...]
    xb = x.astype(BF16)
    h2_ref[...] = jnp.dot(xb, w_in_ref[:, L1_CONV:L1_COLS], preferred_element_type=F32)

    half = tt // 2
    lanes_per_block = L1_PROJ_BLOCK // LANES
    for blk in range(L1_CONV // L1_PROJ_BLOCK):
        h_blk = jnp.dot(xb, w_in_ref[:, blk * L1_PROJ_BLOCK:(blk + 1) * L1_PROJ_BLOCK], preferred_element_type=F32)
        for l in range(lanes_per_block):
            ext_ref[blk * lanes_per_block + l, CONV_PAD:CONV_PAD + tt, :] = h_blk[:, l * LANES:(l + 1) * LANES]
        for l in range(lanes_per_block):
            lb = blk * lanes_per_block + l
            lanes = slice(lb * LANES, (lb + 1) * LANES)
            for parity in range(2):
                conv = None
                for j in range(CONV_WIDTH):
                    first_row = CONV_PAD - (CONV_WIDTH - 1 - j) + parity
                    term = ext_ref[lb, pl.ds(first_row, half, stride=2), :] * conv_ref[j:j + 1, lanes]
                    conv = term if conv is None else conv + term
                act = _silu(conv)
                if lb * LANES < L1_V:
                    scale = GDN_DK ** -0.5 if lb * LANES < L1_K else 1.0
                    act = act * (lax.rsqrt(jnp.sum(act * act, axis=-1, keepdims=True) + L2_EPS) * scale)
                qkv_ref[lb, pl.ds(parity, half, stride=2), :] = act
            ext_ref[lb, 0:CONV_PAD, :] = ext_ref[lb, tt:tt + CONV_PAD, :]

    ab = h2_ref[:, L1_AB - L1_CONV:L1_AB - L1_CONV + LANES]
    g_blk = -jnp.exp(alog_ref[...]) * _softplus(ab + dtb_ref[...])
    beta_blk = _sigmoid(ab)
    gc_blk = _mm_exact_lhs01(tril_ref[...], g_blk)
    eg_blk = jnp.exp(gc_blk)

    ci = lax.broadcasted_iota(jnp.int32, (CHUNK, CHUNK), 0)
    cj = lax.broadcasted_iota(jnp.int32, (CHUNK, CHUNK), 1)
    incl = cj <= ci
    strict = cj < ci

    heads = range(GDN_HEADS)
    col = lambda base, hd: slice(base + hd * GDN_DK, base + (hd + 1) * GDN_DK)
    blk_of = lambda base, hd: base // LANES + hd
    n_chunks = tt // CHUNK

    s = [s_ref[hd] for hd in heads]
    for c0 in range(0, n_chunks, L1_CHUNK_GROUP):
        inst = [(c, hd) for c in range(c0, c0 + L1_CHUNK_GROUP) for hd in heads]
        n_i = range(len(inst))
        rows = [slice(c * CHUNK, (c + 1) * CHUNK) for c, _ in inst]
        gc_c = {c: gc_blk[c * CHUNK:(c + 1) * CHUNK] for c in range(c0, c0 + L1_CHUNK_GROUP)}
        gc_t = {c: gc_c[c].T for c in gc_c}
        kdec_scale = {c: jnp.exp(gc_c[c][CHUNK - 1:CHUNK, :] - gc_c[c]) for c in gc_c}
        q_h = [qkv_ref[blk_of(L1_Q, hd), rows[i], :] for i, (c, hd) in enumerate(inst)]
        k_h = [qkv_ref[blk_of(L1_K, hd), rows[i], :] for i, (c, hd) in enumerate(inst)]
        v_h = [qkv_ref[blk_of(L1_V, hd), rows[i], :] for i, (c, hd) in enumerate(inst)]
        beta = [beta_blk[rows[i], L1_BETA_LANE + hd:L1_BETA_LANE + hd + 1] for i, (c, hd) in enumerate(inst)]
        eg = [eg_blk[rows[i], hd:hd + 1] for i, (c, hd) in enumerate(inst)]
        decay = [jnp.exp(jnp.where(incl, gc_c[c][:, hd:hd + 1] - gc_t[c][hd:hd + 1, :], NEG_BIG)) for c, hd in inst]
        kb = [k_h[i] * beta[i] for i in n_i]
        low = [jnp.where(strict, _mm_nt(kb[i], k_h[i]) * decay[i], 0.0) for i in n_i]
        a_qk = [_mm_nt(q_h[i], k_h[i]) * decay[i] for i in n_i]
        a_m = [-low[i] for i in n_i]
        m = [_mm(low[i], low[i]) for i in n_i]
        for it in range(5):
            am = [_mm(a_m[i], m[i]) for i in n_i]
            a_m = [a_m[i] + m[i] + am[i] for i in n_i]
            if it < 4:
                m = [_mm(m[i], m[i]) for i in n_i]
        rhs = [jnp.concatenate([v_h[i] * beta[i], kb[i] * eg[i]], axis=1) for i in n_i]
        uw = [rhs[i] + _mm(a_m[i], rhs[i]) for i in n_i]
        for i, (c, hd) in enumerate(inst):
            u_ref[rows[i], col(0, hd)] = uw[i][:, 0:GDN_DV]
            w_ref[rows[i], col(0, hd)] = uw[i][:, GDN_DV:GDN_DV + GDN_DK]
            aqk_ref[rows[i], hd * LANES:hd * LANES + CHUNK] = a_qk[i]
            qkv_ref[blk_of(L1_Q, hd), rows[i], :] = q_h[i] * eg[i]
            qkv_ref[blk_of(L1_K, hd), rows[i], :] = k_h[i] * kdec_scale[c][:, hd:hd + 1]

        for c in range(c0, c0 + L1_CHUNK_GROUP):
            crow = slice(c * CHUNK, (c + 1) * CHUNK)
            e_last = jnp.exp(gc_blk[(c + 1) * CHUNK - 1:(c + 1) * CHUNK, :])
            ws_qs = [_mm(jnp.concatenate([w_ref[crow, col(0, hd)], qkv_ref[blk_of(L1_Q, hd), crow, :]], axis=0), s[hd])
                     for hd in heads]
            v_new = [u_ref[crow, col(0, hd)] - ws_qs[hd][0:CHUNK] for hd in heads]
            o_h = [ws_qs[hd][CHUNK:2 * CHUNK] + _mm(aqk_ref[crow, hd * LANES:hd * LANES + CHUNK], v_new[hd])
                   for hd in heads]
            s = [s[hd] * e_last[:, hd:hd + 1] + _mm_tn(qkv_ref[blk_of(L1_K, hd), crow, :], v_new[hd]) for hd in heads]
            for hd in heads:
                mix_ref[crow, col(0, hd)] = o_h[hd]
    for hd in heads:
        s_ref[hd] = s[hd]

    for hd in range(GDN_HEADS):
        sl = slice(hd * GDN_DV, (hd + 1) * GDN_DV)
        o = mix_ref[:, sl]
        o = o * lax.rsqrt(jnp.mean(o * o, axis=-1, keepdims=True) + NORM_EPS) * cnorm_ref[:, sl]
        mix_ref[:, sl] = o * _silu(h2_ref[:, L1_GATE - L1_CONV + hd * GDN_DV:L1_GATE - L1_CONV + (hd + 1) * GDN_DV])
    y = _mm(mix_ref[...], w_out_ref[...])
    x1 = _layer_norm(DEEPNORM_ALPHA * x + y, lng_ref[...], lnb_ref[...])
    o_ref[...] = x1

    x_hi = x1.astype(BF16)
    x_lo = (x1 - x_hi.astype(F32)).astype(BF16)
    logits = (jnp.dot(x_hi, wr_hi_ref[...], preferred_element_type=F32)
              + jnp.dot(x_hi, wr_lo_ref[...], preferred_element_type=F32)
              + jnp.dot(x_lo, wr_hi_ref[...], preferred_element_type=F32)) + br_ref[...]
    lane = lax.broadcasted_iota(jnp.int32, (tt, ROUTE_LANES), 1)
    lane_f = lane.astype(F32)
    logits = jnp.where(lane < N_EXPERTS, logits, NEG_BIG)
    m1 = jnp.max(logits, axis=-1, keepdims=True)
    i1 = jnp.min(jnp.where(logits == m1, lane_f, float(ROUTE_LANES)), axis=-1, keepdims=True)
    rest = jnp.where(lane_f == i1, NEG_BIG, logits)
    m2 = jnp.max(rest, axis=-1, keepdims=True)
    i2 = jnp.min(jnp.where(rest == m2, lane_f, float(ROUTE_LANES)), axis=-1, keepdims=True)
    e21 = jnp.exp(m2 - m1)
    g1 = 1.0 / (1.0 + e21)
    g2 = e21 * g1
    route_ref[...] = jnp.where(lane == 0, i1, jnp.where(lane == 1, i2, jnp.where(lane == 2, g1, jnp.where(lane == 3, g2, 0.0))))


def _l1_mixer(x, w_in, conv_w, a_log, dt_bias, c_norm, w_out, ln_g, ln_b, w_router, b_router, *, tt):
    bsz, seq, d = x.shape
    w_in_p = _pack_l1_w_in(w_in).astype(BF16)
    lane_pad = lambda v: jnp.pad(v[None, :], ((0, 0), (0, LANES - v.shape[0])))
    wr = jnp.pad(w_router, ((0, 0), (0, ROUTE_LANES - N_EXPERTS)))
    wr_hi = wr.astype(BF16)
    wr_lo = (wr - wr_hi.astype(F32)).astype(BF16)
    params = [w_in_p, conv_w, lane_pad(a_log), lane_pad(dt_bias), c_norm.reshape(1, GDN_V), w_out.astype(BF16),
              ln_g[None, :], ln_b[None, :], jnp.asarray(_chunk_tril_np(tt), BF16), wr_hi, wr_lo, lane_pad(b_router)]
    tile = lambda w: pl.BlockSpec((None, tt, w), lambda b, t: (b, t, 0))
    return pl.pallas_call(
        functools.partial(_l1_mixer_kernel, tt=tt),
        grid=(bsz, seq // tt),
        in_specs=[tile(d)] + [_const_spec(a.shape) for a in params],
        out_specs=[tile(d), tile(ROUTE_LANES)],
        out_shape=[jax.ShapeDtypeStruct((bsz, seq, d), F32), jax.ShapeDtypeStruct((bsz, seq, ROUTE_LANES), F32)],
        scratch_shapes=[pltpu.VMEM((tt, L1_COLS - L1_CONV), F32),
                        pltpu.VMEM((L1_CONV // LANES, tt, LANES), F32), pltpu.VMEM((tt, GDN_V), F32),
                        pltpu.VMEM((GDN_HEADS, GDN_DK, GDN_DV), F32),
                        pltpu.VMEM((tt, GDN_V), F32), pltpu.VMEM((tt, GDN_QK), F32),
                        pltpu.VMEM((tt, GDN_HEADS * LANES), F32),
                        pltpu.VMEM((L1_CONV // LANES, tt + CONV_PAD, LANES), F32)],
        compiler_params=pltpu.CompilerParams(dimension_semantics=("arbitrary", "arbitrary"),
                                             vmem_limit_bytes=VMEM_LIMIT_BYTES),
        name="l1_mixer",
    )(x, *params)


ID_ROWS = 4


def _rank_kernel(route_ref, stril_ref, ids_ref, counts_ref, carry_ref):
    @pl.when(pl.program_id(0) == 0)
    def _():
        carry_ref[...] = jnp.zeros_like(carry_ref)

    r = route_ref[...]
    tr = r.shape[0]
    e1, e2 = r[:, 0:1], r[:, 1:2]
    lane = lax.broadcasted_iota(jnp.int32, (tr, ROUTE_LANES), 1)
    lane_f = lane.astype(F32)
    oh1 = (lane_f == e1).astype(F32)
    oh2 = (lane_f == e2).astype(F32)
    both = oh1 + oh2
    before = jnp.dot(stril_ref[...], both.astype(BF16), preferred_element_type=F32) + carry_ref[...]
    rank1 = jnp.sum(oh1 * before, axis=-1, keepdims=True)
    rank2 = jnp.sum(oh2 * before, axis=-1, keepdims=True)
    table = jnp.where(lane == 0, e1, jnp.where(lane == 1, e2, jnp.where(lane == 2, rank1,
                      jnp.where(lane == 3, rank2, 0.0))))
    ids_ref[...] = table.T[0:ID_ROWS, :].astype(jnp.int32)
    carry_ref[...] += jnp.sum(both, axis=0, keepdims=True)
    counts_ref[...] = carry_ref[...]


def _rank(route2d, *, tr):
    n = route2d.shape[0]
    i = np.arange(tr)
    stril = jnp.asarray((i[None, :] < i[:, None]).astype(np.float32), BF16)
    return pl.pallas_call(
        _rank_kernel,
        grid=(n // tr,),
        in_specs=[pl.BlockSpec((tr, ROUTE_LANES), lambda i: (i, 0)), _const_spec((tr, tr))],
        out_specs=[pl.BlockSpec((ID_ROWS, tr), lambda i: (0, i)), _const_spec((1, ROUTE_LANES))],
        out_shape=[jax.ShapeDtypeStruct((ID_ROWS, n), jnp.int32), jax.ShapeDtypeStruct((1, ROUTE_LANES), F32)],
        scratch_shapes=[pltpu.VMEM((1, ROUTE_LANES), F32)],
        compiler_params=pltpu.CompilerParams(dimension_semantics=("arbitrary",)),
        name="moe_rank",
    )(route2d, stril)


ROW_SUB = D_MODEL // LANES


def _to_row_tiles(dst_ref, src, n):
    for lb in range(ROW_SUB):
        dst_ref[pl.ds(lb, n, stride=ROW_SUB), :] = src[:, lb * LANES:(lb + 1) * LANES]


def _from_row_tiles(src_ref, n):
    return [src_ref[pl.ds(lb, n, stride=ROW_SUB), :] for lb in range(ROW_SUB)]


def _tile_copy(src_ref, src_sub, dst_ref, dst_sub, sem):
    src = src_ref.at[pl.ds(pl.multiple_of(src_sub, ROW_SUB), ROW_SUB), :]
    dst = dst_ref.at[pl.ds(pl.multiple_of(dst_sub, ROW_SUB), ROW_SUB), :]
    return pltpu.make_async_copy(src, dst, sem)


def _wait_rows(hbm_ref, buf_ref, sem, n, copies):
    for _ in range(copies):
        pltpu.make_async_copy(hbm_ref.at[pl.ds(0, n * ROW_SUB), :], buf_ref, sem).wait()


ZERO_ROWS = 64


def _dispatch_kernel(zfill_ref, p0_ref, p1_ref, x_ref, xs_hbm, xt0_ref, xt1_ref, zero_ref, sems, zsem, *, td):
    p_refs = (p0_ref, p1_ref)
    i = pl.program_id(0)
    last = pl.num_programs(0) - 1

    @pl.when(i == 0)
    def _():
        zero_ref[...] = jnp.zeros_like(zero_ref)
        n_ranges = zfill_ref.shape[0] // 2

        def zero_copy(start_sub, r):
            dst = xs_hbm.at[pl.ds(pl.multiple_of(start_sub + r * (ZERO_ROWS * ROW_SUB), ROW_SUB),
                                  ZERO_ROWS * ROW_SUB), :]
            return pltpu.make_async_copy(zero_ref, dst, zsem)

        def fill(e, wait):
            def body(r, carry):
                copy = zero_copy(zfill_ref[e], r)
                if wait:
                    copy.wait()
                else:
                    copy.start()
                return carry

            lax.fori_loop(0, zfill_ref[n_ranges + e], body, 0)

        for e in range(n_ranges):
            fill(e, wait=False)
        for e in range(n_ranges):
            fill(e, wait=True)

    def step(xt_ref, sem):
        @pl.when(i >= 2)
        def _():
            _wait_rows(xs_hbm, xt_ref, sem, td, TOP_K)

        _to_row_tiles(xt_ref, x_ref[...], td)

        def issue(t, carry):
            for k in range(TOP_K):
                _tile_copy(xt_ref, t * ROW_SUB, xs_hbm, p_refs[k][t], sem).start(priority=k)
            return carry

        lax.fori_loop(0, td, issue, 0, unroll=8)

    for slot, (xt_ref, sem) in enumerate(((xt0_ref, sems.at[0]), (xt1_ref, sems.at[1]))):
        @pl.when(i % 2 == slot)
        def _():
            step(xt_ref, sem)

    @pl.when(i == last)
    def _():
        _wait_rows(xs_hbm, xt0_ref, sems.at[0], td, TOP_K)
        _wait_rows(xs_hbm, xt1_ref, sems.at[1], td, TOP_K)


def _dispatch(zfill, pos, x2d, n_rows, *, td):
    n, d = x2d.shape
    smem_rows = pl.BlockSpec((td,), lambda i, zf: (i,), memory_space=pltpu.SMEM)
    assert n // td >= 2
    grid_spec = pltpu.PrefetchScalarGridSpec(
        num_scalar_prefetch=1,
        grid=(n // td,),
        in_specs=[smem_rows, smem_rows, pl.BlockSpec((td, d), lambda i, zf: (i, 0))],
        out_specs=pl.BlockSpec(memory_space=pl.ANY),
        scratch_shapes=[pltpu.VMEM((td * ROW_SUB, LANES), F32), pltpu.VMEM((td * ROW_SUB, LANES), F32),
                        pltpu.VMEM((ZERO_ROWS * ROW_SUB, LANES), F32),
                        pltpu.SemaphoreType.DMA((2,)), pltpu.SemaphoreType.DMA(())],
    )
    return pl.pallas_call(
        functools.partial(_dispatch_kernel, td=td),
        grid_spec=grid_spec,
        out_shape=jax.ShapeDtypeStruct((n_rows * ROW_SUB, LANES), F32),
        compiler_params=pltpu.CompilerParams(dimension_semantics=("arbitrary",)),
        name="moe_dispatch",
    )(zfill, pos[0], pos[1], x2d)


def _gmm_kernel(te_ref, used_ref, first_ref, x_ref, wgu_hbm, wd_hbm, o_ref,
                xb_ref, acc_ref, wgb_ref, wub_ref, wdb_ref, sg_ref, su_ref, sd_ref, sems, *, tm, tf):
    i = pl.program_id(0)
    n_tiles = pl.num_programs(0)
    w = _SwigluWeights(wgu_hbm, wd_hbm, (wgb_ref, wub_ref, wdb_ref), (sg_ref, su_ref, sd_ref), sems, tf)
    active = i < used_ref[0]
    is_first = first_ref[i] == 1

    @pl.when(active)
    def _():
        for lb, blk in enumerate(_from_row_tiles(x_ref, tm)):
            xb_ref[:, lb * LANES:(lb + 1) * LANES] = blk.astype(BF16)

    @pl.when(jnp.logical_and(active, is_first))
    def _():
        e = te_ref[i]

        @pl.when(i == 0)
        def _():
            w.request_head(e)

        for j in range(w.nf):
            w.land(e, j)
            w.apply_chunk(xb_ref, acc_ref, j)

    @pl.when(jnp.logical_and(active, jnp.logical_not(is_first)))
    def _():
        for j in range(w.nf):
            w.apply_chunk(xb_ref, acc_ref, j)

    @pl.when(active)
    def _():
        _to_row_tiles(o_ref, acc_ref[...], tm)
        nxt = jnp.minimum(i + 1, n_tiles - 1)

        @pl.when(jnp.logical_and(i + 1 < n_tiles, first_ref[nxt] == 1))
        def _():
            w.request_head(te_ref[nxt])

    @pl.when(jnp.logical_not(active))
    def _():
        o_ref[...] = jnp.zeros_like(o_ref)


def _gmm(tile_expert, n_used, tile_first, xs, w_gu, w_down, *, tm, tf):
    d = D_MODEL
    n_rows = xs.shape[0] // ROW_SUB
    nf = D_FF // tf
    n_tiles = n_rows // tm
    assert nf >= WEIGHT_SLOTS
    grid_spec = pltpu.PrefetchScalarGridSpec(
        num_scalar_prefetch=3,
        grid=(n_tiles,),
        in_specs=[pl.BlockSpec((tm * ROW_SUB, LANES), lambda i, te, used, first: (jnp.minimum(i, used[0] - 1), 0)),
                  pl.BlockSpec(memory_space=pl.ANY), pl.BlockSpec(memory_space=pl.ANY)],
        out_specs=pl.BlockSpec((tm * ROW_SUB, LANES), lambda i, te, used, first: (i, 0)),
        scratch_shapes=[pltpu.VMEM((tm, d), BF16), pltpu.VMEM((tm, d), F32)] + _SwigluWeights.scratch_shapes(tf),
    )
    return pl.pallas_call(
        functools.partial(_gmm_kernel, tm=tm, tf=tf),
        grid_spec=grid_spec,
        out_shape=jax.ShapeDtypeStruct((n_rows * ROW_SUB, LANES), F32),
        compiler_params=pltpu.CompilerParams(dimension_semantics=("arbitrary",),
                                             vmem_limit_bytes=VMEM_LIMIT_BYTES),
        name="moe_gmm",
    )(tile_expert, n_used, tile_first, xs, w_gu, w_down)


def _combine_kernel(p0_ref, p1_ref, p0_next_ref, p1_next_ref, x_ref, route_ref, lng_ref, lnb_ref, y_hbm, o_ref,
                    ya0_ref, yb0_ref, ya1_ref, yb1_ref, sems, *, tc):
    i = pl.program_id(0)
    n = pl.num_programs(0)
    slots = ((ya0_ref, yb0_ref, sems.at[0]), (ya1_ref, yb1_ref, sems.at[1]))

    def issue(p_refs, slot):
        ya_ref, yb_ref, sem = slots[slot]

        def body(t, carry):
            for k, (p_ref, buf) in enumerate(zip(p_refs, (ya_ref, yb_ref))):
                _tile_copy(y_hbm, p_ref[t], buf, t * ROW_SUB, sem).start(priority=k)
            return carry

        lax.fori_loop(0, tc, body, 0, unroll=8)

    @pl.when(i == 0)
    def _():
        issue((p0_ref, p1_ref), 0)

    group = tc // ROW_SUB

    for slot in range(2):
        @pl.when(i % 2 == slot)
        def _():
            ya_ref, yb_ref, sem = slots[slot]
            nya_ref, nyb_ref, nsem = slots[1 - slot]
            _wait_rows(y_hbm, ya_ref, sem, tc, TOP_K)
            r = route_ref[...]
            g1, g2 = r[:, 2:3], r[:, 3:4]
            z = []
            for lb in range(ROW_SUB):
                a = ya_ref[pl.ds(lb, tc, stride=ROW_SUB), :]
                b = yb_ref[pl.ds(lb, tc, stride=ROW_SUB), :]
                z.append(DEEPNORM_ALPHA * x_ref[:, lb * LANES:(lb + 1) * LANES] + (g1 * a + g2 * b))
                for t in range(lb * group, (lb + 1) * group):
                    for k, (p_ref, buf) in enumerate(zip((p0_next_ref, p1_next_ref), (nya_ref, nyb_ref))):
                        _tile_copy(y_hbm, p_ref[t], buf, t * ROW_SUB, nsem).start(priority=k)
            o_ref[...] = _layer_norm(jnp.concatenate(z, axis=1), lng_ref[...], lnb_ref[...])

            @pl.when(i == n - 1)
            def _():
                _wait_rows(y_hbm, nya_ref, nsem, tc, TOP_K)


def _combine(pos, x2d, route2d, ys, ln_g, ln_b, *, tc):
    n, d = x2d.shape
    steps = n // tc
    smem_rows = pl.BlockSpec((tc,), lambda i: (i,), memory_space=pltpu.SMEM)
    smem_next = pl.BlockSpec((tc,), lambda i: (jnp.minimum(i + 1, steps - 1),), memory_space=pltpu.SMEM)
    return pl.pallas_call(
        functools.partial(_combine_kernel, tc=tc),
        grid=(steps,),
        in_specs=[smem_rows, smem_rows, smem_next, smem_next,
                  pl.BlockSpec((tc, d), lambda i: (i, 0)),
                  pl.BlockSpec((tc, ROUTE_LANES), lambda i: (i, 0)),
                  _const_spec((1, d)), _const_spec((1, d)),
                  pl.BlockSpec(memory_space=pl.ANY)],
        out_specs=pl.BlockSpec((tc, d), lambda i: (i, 0)),
        scratch_shapes=[pltpu.VMEM((tc * ROW_SUB, LANES), F32) for _ in range(2 * TOP_K)]
                       + [pltpu.SemaphoreType.DMA((2,))],
        out_shape=jax.ShapeDtypeStruct((n, d), F32),
        compiler_params=pltpu.CompilerParams(dimension_semantics=("arbitrary",),
                                             vmem_limit_bytes=VMEM_LIMIT_BYTES),
        name="moe_combine",
    )(pos[0], pos[1], pos[0], pos[1], x2d, route2d, ln_g[None, :], ln_b[None, :], ys)


def _moe(x2d, route2d, w_gu, w_down, ln_g, ln_b, *, tm, tf, tr, td, tc):
    n, d = x2d.shape
    ids, counts = _rank(route2d, tr=tr)
    cnt = counts[0, :N_EXPERTS].astype(jnp.int32)
    tiles_per = (cnt + tm - 1) // tm
    tile_end = jnp.cumsum(tiles_per)
    offs = (tile_end - tiles_per) * tm
    n_tiles = (n * TOP_K) // tm + N_EXPERTS
    n_used = tile_end[-1:]
    t_idx = jnp.arange(n_tiles, dtype=jnp.int32)
    tile_expert = jnp.sum(jnp.minimum(t_idx, n_used - 1)[:, None] >= tile_end[None, :], axis=1).astype(jnp.int32)
    tile_start = tile_end - tiles_per
    tile_first = jnp.logical_and(t_idx == tile_start[tile_expert], t_idx < n_used).astype(jnp.int32)
    group_start = sum(jnp.where(ids[0:TOP_K] == e, offs[e], 0) for e in range(N_EXPERTS))
    pos = (group_start + ids[TOP_K:2 * TOP_K]) * ROW_SUB
    pad_start = ((offs + cnt) // ZERO_ROWS) * ZERO_ROWS
    pad_blocks = (offs + tiles_per * tm - pad_start) // ZERO_ROWS
    tail_start = n_used * tm
    tail_blocks = (n_tiles - n_used) * (tm // ZERO_ROWS)
    zfill = jnp.concatenate([pad_start * ROW_SUB, tail_start * ROW_SUB, pad_blocks, tail_blocks]).astype(jnp.int32)
    xs = _dispatch(zfill, pos, x2d, n_tiles * tm, td=td)
    ys = _gmm(tile_expert, n_used, tile_first, xs, w_gu, w_down, tm=tm, tf=tf)
    return _combine(pos, x2d, route2d, ys, ln_g, ln_b, tc=tc)


def kernel(x, positions, ab_w_in, gla_w_gate2, gla_b_gate, gla_norm, ret_norm, ab_w_out, ab_ln1_g, ab_ln1_b, ffn_w_gu, ffn_w_down, ab_ln2_g, ab_ln2_b, c_w_in, c_conv_w, c_a_log, c_dt_bias, c_norm, c_w_out, c_ln1_g, c_ln1_b, moe_w_router, moe_b_router, moe_w_gu, moe_w_down, c_ln2_g, c_ln2_b):
    bsz, seq, d = x.shape
    for layer in range(DEPTH):
        i = layer // 2
        if layer % 2 == 0:
            x = _l0_mixer(x, positions, ab_w_in[i], gla_w_gate2[i], gla_b_gate[i], gla_norm[i], ret_norm[i],
                          ab_w_out[i], ab_ln1_g[i], ab_ln1_b[i], tt=L0_TIME_TILE)
            x = _ffn(x.reshape(bsz * seq, d), ffn_w_gu[i], ffn_w_down[i], ab_ln2_g[i], ab_ln2_b[i],
                     tm=FFN_ROW_TILE, tf=FF_TILE).reshape(bsz, seq, d)
        else:
            x, route = _l1_mixer(x, c_w_in[i], c_conv_w[i], c_a_log[i], c_dt_bias[i], c_norm[i], c_w_out[i],
                                 c_ln1_g[i], c_ln1_b[i], moe_w_router[i], moe_b_router[i], tt=L1_TIME_TILE)
            x = _moe(x.reshape(bsz * seq, d), route.reshape(bsz * seq, ROUTE_LANES), moe_w_gu[i], moe_w_down[i],
                     c_ln2_g[i], c_ln2_b[i], tm=MOE_ROW_TILE, tf=FF_TILE, tr=ROUTE_TILE, td=ROUTE_TILE,
                     tc=ROUTE_TILE).reshape(bsz, seq, d)
    return x
```

```python
import functools

import numpy as np
import jax
import jax.numpy as jnp
from jax import lax
from jax.experimental import pallas as pl
from jax.experimental.pallas import tpu as pltpu

F32 = jnp.float32
BF16 = jnp.bfloat16

D_MODEL = 1024
DEPTH = 2
CHUNK = 64
GLA_HEADS, GLA_DK, GLA_DV, GLA_GATE_RANK, GLA_TAU = 4, 64, 128, 16, 16.0
RET_HEADS, RET_DK, RET_DV = 4, 64, 128
ROPE_BASE = 10000.0
GDN_HEADS, GDN_DK, GDN_DV = 8, 128, 128
CONV_WIDTH = 4
D_FF = 3584
N_EXPERTS = 8
TOP_K = 2
NORM_EPS = 1e-5
L2_EPS = 1e-6
DEEPNORM_ALPHA = (2.0 * DEPTH) ** 0.25

GLA_QK = GLA_HEADS * GLA_DK
GLA_V = GLA_HEADS * GLA_DV
RET_QK = RET_HEADS * RET_DK
RET_V = RET_HEADS * RET_DV
GDN_QK = GDN_HEADS * GDN_DK
GDN_V = GDN_HEADS * GDN_DV

LANES = 128
VMEM_LIMIT_BYTES = 56 * 1024 * 1024
NEG_BIG = -1e30

L0_TIME_TILE = 256
L1_TIME_TILE = 256
FFN_ROW_TILE = 512
MOE_ROW_TILE = 512
FF_TILE = 512
ROUTE_TILE = 512


def _mm(a, b):
    return jnp.dot(a.astype(BF16), b.astype(BF16), preferred_element_type=F32)


def _mm_nt(a, b):
    return lax.dot_general(a.astype(BF16), b.astype(BF16), (((1,), (1,)), ((), ())),
                           preferred_element_type=F32)


def _mm_tn(a, b):
    return lax.dot_general(a.astype(BF16), b.astype(BF16), (((0,), (0,)), ((), ())),
                           preferred_element_type=F32)


def _split3(x):
    hi = x.astype(BF16)
    r1 = x - hi.astype(F32)
    mid = r1.astype(BF16)
    lo = (r1 - mid.astype(F32)).astype(BF16)
    return hi, mid, lo


def _mm_exact_lhs01(m01, x):
    hi, mid, lo = _split3(x)
    return (jnp.dot(m01, hi, preferred_element_type=F32)
            + jnp.dot(m01, mid, preferred_element_type=F32)
            + jnp.dot(m01, lo, preferred_element_type=F32))


def _sigmoid(x):
    return 0.5 * jnp.tanh(0.5 * x) + 0.5


def _silu(x):
    return x * _sigmoid(x)


def _softplus(x):
    return jnp.maximum(x, 0.0) + jnp.log(1.0 + jnp.exp(-jnp.abs(x)))


def _layer_norm(x, g, b):
    mu = jnp.mean(x, axis=-1, keepdims=True)
    xc = x - mu
    var = jnp.mean(xc * xc, axis=-1, keepdims=True)
    return xc * lax.rsqrt(var + NORM_EPS) * g + b


def _chunk_tril_np(tt):
    i = np.arange(tt)
    same = (i[:, None] // CHUNK) == (i[None, :] // CHUNK)
    return (same & (i[None, :] <= i[:, None])).astype(np.float32)


def _const_spec(shape):
    nd = len(shape)
    return pl.BlockSpec(shape, lambda *_: (0,) * nd)


L0_GQ, L0_GK, L0_GV, L0_GR = 0, 256, 512, 1024
L0_RQ, L0_RK, L0_RV, L0_RG = 1536, 1792, 2048, 2560
L0_GA = 3072
L0_COLS = 3200
L0_CHUNK_GROUP = 1


def _pack_l0_w_in(w):
    offs = np.cumsum([0, GLA_QK, GLA_QK, GLA_V, GLA_GATE_RANK, GLA_V, RET_QK, RET_QK, RET_V, RET_V])
    gq, gk, gv, ga, gr, rq, rk, rv, rg = [w[:, offs[i]:offs[i + 1]] for i in range(9)]
    ga = jnp.pad(ga, ((0, 0), (0, LANES - GLA_GATE_RANK)))
    return jnp.concatenate([gq, gk, gv, gr, rq, rk, rv, rg, ga], axis=1)


def _ret_tables():
    h = np.arange(RET_HEADS, dtype=np.float64)
    log_gamma = np.log(1.0 - 2.0 ** (-5.0 - h))
    pos = np.arange(CHUNK, dtype=np.float64)
    diff = pos[:, None] - pos[None, :]
    dmat = np.where(diff >= 0, np.exp(log_gamma[:, None, None] * np.maximum(diff, 0.0)), 0.0)
    xi = np.exp(log_gamma[None, :] * (pos[:, None] + 1.0))
    zeta = np.exp(log_gamma[None, :] * (CHUNK - 1.0 - pos[:, None]))
    decay = np.exp(log_gamma * CHUNK)
    xi_full = np.repeat(xi, RET_DV, axis=1)
    zeta_full = np.repeat(zeta, RET_DK, axis=1)
    decay_full = np.repeat(decay, RET_DK)[None, :]
    return (dmat.astype(np.float32), xi_full.astype(np.float32), zeta_full.astype(np.float32),
            decay_full.astype(np.float32))


def _rope_tables():
    half = RET_DK // 2
    inv_freq = ROPE_BASE ** (-np.arange(0, RET_DK, 2, dtype=np.float32) / RET_DK)
    per_head = np.concatenate([inv_freq, inv_freq])
    freq_full = np.tile(per_head, RET_HEADS)[None, :].astype(np.float32)
    sign = np.tile(np.concatenate([-np.ones(half), np.ones(half)]), RET_HEADS)[None, :].astype(np.float32)
    return freq_full, sign


def _l0_mixer_kernel(x_ref, pos_ref, w_in_ref, wg2_ref, bg_ref, gnorm_ref, rnorm_ref, w_out_ref,
                     lng_ref, lnb_ref, tril_ref, dmat_ref, xi_ref, zeta_ref, rdecay_ref,
                     freq_ref, sign_ref, o_ref, h_ref, mix_ref, sg_ref, sr_ref, *, tt):
    ti = pl.program_id(1)

    @pl.when(ti == 0)
    def _():
        sg_ref[...] = jnp.zeros_like(sg_ref)
        sr_ref[...] = jnp.zeros_like(sr_ref)

    x = x_ref[...]
    h_ref[...] = _mm(x, w_in_ref[...])

    lane = lax.broadcasted_iota(jnp.int32, (1, LANES), 1)
    lo_half = lane < GLA_DK
    ci = lax.broadcasted_iota(jnp.int32, (CHUNK, CHUNK), 0)
    cj = lax.broadcasted_iota(jnp.int32, (CHUNK, CHUNK), 1)
    causal = cj <= ci

    z = _mm(h_ref[:, L0_GA:L0_GA + LANES], wg2_ref[...]) + bg_ref[...]
    log_a = -_softplus(-z) * (1.0 / GLA_TAU)
    b = _mm_exact_lhs01(tril_ref[...], log_a)
    eb = jnp.exp(b)
    q_dec = h_ref[:, L0_GQ:L0_GQ + GLA_QK] * (GLA_DK ** -0.5) * eb
    k_all = h_ref[:, L0_GK:L0_GK + GLA_QK]
    k_neg = k_all * jnp.exp(-b)

    ang = pos_ref[...] * freq_ref[:, 0:LANES]
    cos = jnp.concatenate([jnp.cos(ang)] * (RET_QK // LANES), axis=1)
    sin = jnp.concatenate([jnp.sin(ang)] * (RET_QK // LANES), axis=1) * sign_ref[...]
    half = RET_DK // 2
    lane256 = lax.broadcasted_iota(jnp.int32, (1, RET_QK), 1)
    first_half = (lane256 & (RET_DK - 1)) < half

    def rope(t):
        swapped = jnp.where(first_half, pltpu.roll(t, RET_QK - half, 1), pltpu.roll(t, half, 1))
        return t * cos + swapped * sin

    rq = rope(h_ref[:, L0_RQ:L0_RQ + RET_QK]) * (RET_DK ** -0.5)
    rk = rope(h_ref[:, L0_RK:L0_RK + RET_QK])

    n_chunks = tt // CHUNK
    heads = range(GLA_HEADS)
    pairs = range(GLA_HEADS // 2)
    pair_lanes = lambda hd: slice((hd // 2) * LANES, (hd // 2 + 1) * LANES)
    head_mask = lambda hd: lo_half if hd % 2 == 0 else jnp.logical_not(lo_half)
    vcol = lambda base, hd: slice(base + hd * GLA_DV, base + (hd + 1) * GLA_DV)
    sg = [sg_ref[p] for p in pairs]
    sr = [sr_ref[p] for p in pairs]
    for c0 in range(0, n_chunks, L0_CHUNK_GROUP):
        group = range(c0, c0 + L0_CHUNK_GROUP)
        rows = {c: slice(c * CHUNK, (c + 1) * CHUNK) for c in group}
        b_last = {c: b[(c + 1) * CHUNK - 1:(c + 1) * CHUNK, :] for c in group}
        k_end = {c: k_all[rows[c]] * jnp.exp(b_last[c] - b[rows[c]]) for c in group}
        rk_c = {c: rk[rows[c]] for c in group}
        rk_z = {c: rk_c[c] * zeta_ref[...] for c in group}
        inst = [(c, hd) for c in group for hd in heads]
        g_q = {ch: jnp.where(head_mask(ch[1]), q_dec[rows[ch[0]], pair_lanes(ch[1])], 0.0) for ch in inst}
        r_q = {ch: jnp.where(head_mask(ch[1]), rq[rows[ch[0]], pair_lanes(ch[1])], 0.0) for ch in inst}
        g_v = {ch: h_ref[rows[ch[0]], vcol(L0_GV, ch[1])] for ch in inst}
        r_v = {ch: h_ref[rows[ch[0]], vcol(L0_RV, ch[1])] for ch in inst}
        g_att = {(c, hd): jnp.where(causal, _mm_nt(g_q[c, hd], k_neg[rows[c], pair_lanes(hd)]), 0.0)
                 for c, hd in inst}
        r_att = {(c, hd): _mm_nt(r_q[c, hd], rk_c[c][:, pair_lanes(hd)]) * dmat_ref[hd] for c, hd in inst}
        g_d = {(c, hd): _mm_tn(g_v[c, hd], k_end[c][:, pair_lanes(hd)]) for c, hd in inst}
        r_d = {(c, hd): _mm_tn(r_v[c, hd], rk_z[c][:, pair_lanes(hd)]) for c, hd in inst}
        sg_in, sr_in = {}, {}
        for c in group:
            sg_in[c], sr_in[c] = sg, sr
            dec_c = jnp.exp(b_last[c])
            sg = [sg[p] * dec_c[:, p * LANES:(p + 1) * LANES] + jnp.where(lo_half, g_d[c, 2 * p], g_d[c, 2 * p + 1])
                  for p in pairs]
            sr = [sr[p] * rdecay_ref[:, p * LANES:(p + 1) * LANES]
                  + jnp.where(lo_half, r_d[c, 2 * p], r_d[c, 2 * p + 1]) for p in pairs]
        g_o = {(c, hd): _mm(g_att[c, hd], g_v[c, hd]) + _mm_nt(g_q[c, hd], sg_in[c][hd // 2]) for c, hd in inst}
        r_o = {(c, hd): _mm(r_att[c, hd], r_v[c, hd])
               + _mm_nt(r_q[c, hd], sr_in[c][hd // 2]) * xi_ref[:, vcol(0, hd)] for c, hd in inst}
        for c, hd in inst:
            mix_ref[rows[c], vcol(0, hd)] = g_o[c, hd]
            mix_ref[rows[c], vcol(GLA_V, hd)] = r_o[c, hd]
    for p in pairs:
        sg_ref[p] = sg[p]
        sr_ref[p] = sr[p]

    for hd in range(GLA_HEADS):
        sl = slice(hd * GLA_DV, (hd + 1) * GLA_DV)
        o = mix_ref[:, sl]
        o = o * lax.rsqrt(jnp.mean(o * o, axis=-1, keepdims=True) + NORM_EPS) * gnorm_ref[:, sl]
        mix_ref[:, sl] = o * _silu(h_ref[:, L0_GR + hd * GLA_DV:L0_GR + (hd + 1) * GLA_DV])
    for hd in range(RET_HEADS):
        sl = slice(hd * RET_DV, (hd + 1) * RET_DV)
        o = mix_ref[:, GLA_V + hd * RET_DV:GLA_V + (hd + 1) * RET_DV]
        oc = o - jnp.mean(o, axis=-1, keepdims=True)
        o = oc * lax.rsqrt(jnp.mean(oc * oc, axis=-1, keepdims=True) + NORM_EPS) * rnorm_ref[:, sl]
        mix_ref[:, GLA_V + hd * RET_DV:GLA_V + (hd + 1) * RET_DV] = (
            o * _silu(h_ref[:, L0_RG + hd * RET_DV:L0_RG + (hd + 1) * RET_DV]))

    y = _mm(mix_ref[...], w_out_ref[...])
    o_ref[...] = _layer_norm(DEEPNORM_ALPHA * x + y, lng_ref[...], lnb_ref[...])


def _l0_mixer(x, positions, w_in, w_gate2, b_gate, gla_norm, ret_norm, w_out, ln_g, ln_b, *, tt):
    bsz, seq, d = x.shape
    w_in_p = _pack_l0_w_in(w_in).astype(BF16)
    wg2 = jnp.pad(w_gate2, ((0, LANES - GLA_GATE_RANK), (0, 0))).astype(BF16)
    dmat, xi_full, zeta_full, rdecay = _ret_tables()
    freq_full, sign = _rope_tables()
    pos_f = positions.astype(F32)[..., None]
    consts = [jnp.asarray(_chunk_tril_np(tt), BF16), jnp.asarray(dmat), jnp.asarray(xi_full),
              jnp.asarray(zeta_full), jnp.asarray(rdecay), jnp.asarray(freq_full), jnp.asarray(sign)]
    params = [w_in_p, wg2, b_gate[None, :], gla_norm.reshape(1, GLA_V), ret_norm.reshape(1, RET_V),
              w_out.astype(BF16), ln_g[None, :], ln_b[None, :]]
    tile = lambda w: pl.BlockSpec((None, tt, w), lambda b, t: (b, t, 0))
    return pl.pallas_call(
        functools.partial(_l0_mixer_kernel, tt=tt),
        grid=(bsz, seq // tt),
        in_specs=[tile(d), tile(1)] + [_const_spec(a.shape) for a in params + consts],
        out_specs=tile(d),
        out_shape=jax.ShapeDtypeStruct((bsz, seq, d), F32),
        scratch_shapes=[pltpu.VMEM((tt, L0_COLS), F32), pltpu.VMEM((tt, GLA_V + RET_V), F32),
                        pltpu.VMEM((GLA_HEADS // 2, GLA_DV, LANES), F32),
                        pltpu.VMEM((RET_HEADS // 2, RET_DV, LANES), F32)],
        compiler_params=pltpu.CompilerParams(dimension_semantics=("arbitrary", "arbitrary"),
                                             vmem_limit_bytes=VMEM_LIMIT_BYTES),
        name="l0_mixer",
    )(x, pos_f, *params, *consts)


WEIGHT_SLOTS = 2


class _SwigluWeights:
    def __init__(self, wgu_hbm, wd_hbm, resident, staging, sems, tf):
        self.wgu_hbm, self.wd_hbm = wgu_hbm, wd_hbm
        self.wgb, self.wub, self.wdb = resident
        self.sg, self.su, self.sd = staging
        self.sems, self.tf, self.nf = sems, tf, D_FF // tf

    def _copies(self, e, j):
        slot = j % WEIGHT_SLOTS
        cols = pl.ds(j * self.tf, self.tf)
        up_cols = pl.ds(D_FF + j * self.tf, self.tf)
        return (pltpu.make_async_copy(self.wgu_hbm.at[e, :, cols], self.sg.at[slot], self.sems.at[slot]),
                pltpu.make_async_copy(self.wgu_hbm.at[e, :, up_cols], self.su.at[slot], self.sems.at[slot]),
                pltpu.make_async_copy(self.wd_hbm.at[e, cols, :], self.sd.at[slot], self.sems.at[slot]))

    def request(self, e, j):
        for c in self._copies(e, j):
            c.start()

    def request_head(self, e):
        for j in range(WEIGHT_SLOTS):
            self.request(e, j)

    def land(self, e, j):
        slot = j % WEIGHT_SLOTS
        for c in self._copies(e, j):
            c.wait()
        self.wgb[j] = self.sg[slot].astype(BF16)
        self.wub[j] = self.su[slot].astype(BF16)
        self.wdb[j] = self.sd[slot].astype(BF16)
        if j + WEIGHT_SLOTS < self.nf:
            self.request(e, j + WEIGHT_SLOTS)

    def apply_chunk(self, xb_ref, acc_ref, j):
        xb = xb_ref[...]
        gt = jnp.dot(xb, self.wgb[j], preferred_element_type=F32)
        up = jnp.dot(xb, self.wub[j], preferred_element_type=F32)
        part = jnp.dot((_silu(gt) * up).astype(BF16), self.wdb[j], preferred_element_type=F32)
        if j == 0:
            acc_ref[...] = part
        else:
            acc_ref[...] += part

    @staticmethod
    def scratch_shapes(tf):
        nf, d = D_FF // tf, D_MODEL
        return [pltpu.VMEM((nf, d, tf), BF16), pltpu.VMEM((nf, d, tf), BF16), pltpu.VMEM((nf, tf, d), BF16),
                pltpu.VMEM((WEIGHT_SLOTS, d, tf), F32), pltpu.VMEM((WEIGHT_SLOTS, d, tf), F32),
                pltpu.VMEM((WEIGHT_SLOTS, tf, d), F32), pltpu.SemaphoreType.DMA((WEIGHT_SLOTS,))]


def _ffn_kernel(x_ref, wgu_hbm, wd_hbm, lng_ref, lnb_ref, o_ref, xb_ref, acc_ref,
                wgb_ref, wub_ref, wdb_ref, sg_ref, su_ref, sd_ref, sems, *, tf):
    i = pl.program_id(0)
    w = _SwigluWeights(wgu_hbm, wd_hbm, (wgb_ref, wub_ref, wdb_ref), (sg_ref, su_ref, sd_ref), sems, tf)
    xb_ref[...] = x_ref[...].astype(BF16)

    @pl.when(i == 0)
    def _():
        w.request_head(0)
        for j in range(w.nf):
            w.land(0, j)
            w.apply_chunk(xb_ref, acc_ref, j)

    @pl.when(i > 0)
    def _():
        for j in range(w.nf):
            w.apply_chunk(xb_ref, acc_ref, j)

    o_ref[...] = _layer_norm(DEEPNORM_ALPHA * x_ref[...] + acc_ref[...], lng_ref[...], lnb_ref[...])


def _ffn(x2d, w_gu, w_down, ln_g, ln_b, *, tm, tf):
    n, d = x2d.shape
    assert D_FF // tf >= WEIGHT_SLOTS
    return pl.pallas_call(
        functools.partial(_ffn_kernel, tf=tf),
        grid=(n // tm,),
        in_specs=[pl.BlockSpec((tm, d), lambda i: (i, 0)),
                  pl.BlockSpec(memory_space=pl.ANY), pl.BlockSpec(memory_space=pl.ANY),
                  _const_spec((1, d)), _const_spec((1, d))],
        out_specs=pl.BlockSpec((tm, d), lambda i: (i, 0)),
        out_shape=jax.ShapeDtypeStruct((n, d), F32),
        scratch_shapes=[pltpu.VMEM((tm, d), BF16), pltpu.VMEM((tm, d), F32)] + _SwigluWeights.scratch_shapes(tf),
        compiler_params=pltpu.CompilerParams(dimension_semantics=("arbitrary",),
                                             vmem_limit_bytes=VMEM_LIMIT_BYTES),
        name="ffn",
    )(x2d, w_gu[None], w_down[None], ln_g[None, :], ln_b[None, :])


L1_Q, L1_K, L1_V, L1_GATE, L1_AB = 0, 1024, 2048, 3072, 4096
L1_CONV = 3 * GDN_QK
L1_COLS = 4224
L1_BETA_LANE = GDN_HEADS
CONV_PAD = 8
L1_PROJ_BLOCK = 512
L1_CHUNK_GROUP = 2
ROUTE_LANES = LANES


def _pack_l1_w_in(w):
    offs = np.cumsum([0, L1_CONV, GDN_HEADS, GDN_HEADS, GDN_V])
    qkv, a_in, b_in, gate = [w[:, offs[i]:offs[i + 1]] for i in range(4)]
    ab = jnp.pad(jnp.concatenate([a_in, b_in], axis=1), ((0, 0), (0, LANES - 2 * GDN_HEADS)))
    return jnp.concatenate([qkv, gate, ab], axis=1)


def _l1_mixer_kernel(x_ref, w_in_ref, conv_ref, alog_ref, dtb_ref, cnorm_ref, w_out_ref, lng_ref, lnb_ref,
                     tril_ref, wr_hi_ref, wr_lo_ref, br_ref,
                     o_ref, route_ref, h2_ref, qkv_ref, mix_ref, s_ref, u_ref, w_ref, aqk_ref, ext_ref, *, tt):
    ti = pl.program_id(1)

    @pl.when(ti == 0)
    def _():
        s_ref[...] = jnp.zeros_like(s_ref)
        ext_ref[:, 0:CONV_PAD, :] = jnp.zeros((L1_CONV // LANES, CONV_PAD, LANES), F32)

    x = x_ref[...]
    xb = x.astype(BF16)
    h2_ref[...] = jnp.dot(xb, w_in_ref[:, L1_CONV:L1_COLS], preferred_element_type=F32)

    half = tt // 2
    lanes_per_block = L1_PROJ_BLOCK // LANES
    for blk in range(L1_CONV // L1_PROJ_BLOCK):
        h_blk = jnp.dot(xb, w_in_ref[:, blk * L1_PROJ_BLOCK:(blk + 1) * L1_PROJ_BLOCK], preferred_element_type=F32)
        for l in range(lanes_per_block):
            ext_ref[blk * lanes_per_block + l, CONV_PAD:CONV_PAD + tt, :] = h_blk[:, l * LANES:(l + 1) * LANES]
        for l in range(lanes_per_block):
            lb = blk * lanes_per_block + l
            lanes = slice(lb * LANES, (lb + 1) * LANES)
            for parity in range(2):
                conv = None
                for j in range(CONV_WIDTH):
                    first_row = CONV_PAD - (CONV_WIDTH - 1 - j) + parity
                    term = ext_ref[lb, pl.ds(first_row, half, stride=2), :] * conv_ref[j:j + 1, lanes]
                    conv = term if conv is None else conv + term
                act = _silu(conv)
                if lb * LANES < L1_V:
                    scale = GDN_DK ** -0.5 if lb * LANES < L1_K else 1.0
                    act = act * (lax.rsqrt(jnp.sum(act * act, axis=-1, keepdims=True) + L2_EPS) * scale)
                qkv_ref[lb, pl.ds(parity, half, stride=2), :] = act
            ext_ref[lb, 0:CONV_PAD, :] = ext_ref[lb, tt:tt + CONV_PAD, :]

    ab = h2_ref[:, L1_AB - L1_CONV:L1_AB - L1_CONV + LANES]
    g_blk = -jnp.exp(alog_ref[...]) * _softplus(ab + dtb_ref[...])
    beta_blk = _sigmoid(ab)
    gc_blk = _mm_exact_lhs01(tril_ref[...], g_blk)
    eg_blk = jnp.exp(gc_blk)

    ci = lax.broadcasted_iota(jnp.int32, (CHUNK, CHUNK), 0)
    cj = lax.broadcasted_iota(jnp.int32, (CHUNK, CHUNK), 1)
    incl = cj <= ci
    strict = cj < ci

    heads = range(GDN_HEADS)
    col = lambda base, hd: slice(base + hd * GDN_DK, base + (hd + 1) * GDN_DK)
    blk_of = lambda base, hd: base // LANES + hd
    n_chunks = tt // CHUNK

    s = [s_ref[hd] for hd in heads]
    for c0 in range(0, n_chunks, L1_CHUNK_GROUP):
        inst = [(c, hd) for c in range(c0, c0 + L1_CHUNK_GROUP) for hd in heads]
        n_i = range(len(inst))
        rows = [slice(c * CHUNK, (c + 1) * CHUNK) for c, _ in inst]
        gc_c = {c: gc_blk[c * CHUNK:(c + 1) * CHUNK] for c in range(c0, c0 + L1_CHUNK_GROUP)}
        gc_t = {c: gc_c[c].T for c in gc_c}
        kdec_scale = {c: jnp.exp(gc_c[c][CHUNK - 1:CHUNK, :] - gc_c[c]) for c in gc_c}
        q_h = [qkv_ref[blk_of(L1_Q, hd), rows[i], :] for i, (c, hd) in enumerate(inst)]
        k_h = [qkv_ref[blk_of(L1_K, hd), rows[i], :] for i, (c, hd) in enumerate(inst)]
        v_h = [qkv_ref[blk_of(L1_V, hd), rows[i], :] for i, (c, hd) in enumerate(inst)]
        beta = [beta_blk[rows[i], L1_BETA_LANE + hd:L1_BETA_LANE + hd + 1] for i, (c, hd) in enumerate(inst)]
        eg = [eg_blk[rows[i], hd:hd + 1] for i, (c, hd) in enumerate(inst)]
        decay = [jnp.exp(jnp.where(incl, gc_c[c][:, hd:hd + 1] - gc_t[c][hd:hd + 1, :], NEG_BIG)) for c, hd in inst]
        kb = [k_h[i] * beta[i] for i in n_i]
        low = [jnp.where(strict, _mm_nt(kb[i], k_h[i]) * decay[i], 0.0) for i in n_i]
        a_qk = [_mm_nt(q_h[i], k_h[i]) * decay[i] for i in n_i]
        a_m = [-low[i] for i in n_i]
        m = [_mm(low[i], low[i]) for i in n_i]
        for it in range(5):
            am = [_mm(a_m[i], m[i]) for i in n_i]
            a_m = [a_m[i] + m[i] + am[i] for i in n_i]
            if it < 4:
                m = [_mm(m[i], m[i]) for i in n_i]
        rhs = [jnp.concatenate([v_h[i] * beta[i], kb[i] * eg[i]], axis=1) for i in n_i]
        uw = [rhs[i] + _mm(a_m[i], rhs[i]) for i in n_i]
        for i, (c, hd) in enumerate(inst):
            u_ref[rows[i], col(0, hd)] = uw[i][:, 0:GDN_DV]
            w_ref[rows[i], col(0, hd)] = uw[i][:, GDN_DV:GDN_DV + GDN_DK]
            aqk_ref[rows[i], hd * LANES:hd * LANES + CHUNK] = a_qk[i]
            qkv_ref[blk_of(L1_Q, hd), rows[i], :] = q_h[i] * eg[i]
            qkv_ref[blk_of(L1_K, hd), rows[i], :] = k_h[i] * kdec_scale[c][:, hd:hd + 1]

        for c in range(c0, c0 + L1_CHUNK_GROUP):
            crow = slice(c * CHUNK, (c + 1) * CHUNK)
            e_last = jnp.exp(gc_blk[(c + 1) * CHUNK - 1:(c + 1) * CHUNK, :])
            ws_qs = [_mm(jnp.concatenate([w_ref[crow, col(0, hd)], qkv_ref[blk_of(L1_Q, hd), crow, :]], axis=0), s[hd])
                     for hd in heads]
            v_new = [u_ref[crow, col(0, hd)] - ws_qs[hd][0:CHUNK] for hd in heads]
            o_h = [ws_qs[hd][CHUNK:2 * CHUNK] + _mm(aqk_ref[crow, hd * LANES:hd * LANES + CHUNK], v_new[hd])
                   for hd in heads]
            s = [s[hd] * e_last[:, hd:hd + 1] + _mm_tn(qkv_ref[blk_of(L1_K, hd), crow, :], v_new[hd]) for hd in heads]
            for hd in heads:
                mix_ref[crow, col(0, hd)] = o_h[hd]
    for hd in heads:
        s_ref[hd] = s[hd]

    for hd in range(GDN_HEADS):
        sl = slice(hd * GDN_DV, (hd + 1) * GDN_DV)
        o = mix_ref[:, sl]
        o = o * lax.rsqrt(jnp.mean(o * o, axis=-1, keepdims=True) + NORM_EPS) * cnorm_ref[:, sl]
        mix_ref[:, sl] = o * _silu(h2_ref[:, L1_GATE - L1_CONV + hd * GDN_DV:L1_GATE - L1_CONV + (hd + 1) * GDN_DV])
    y = _mm(mix_ref[...], w_out_ref[...])
    x1 = _layer_norm(DEEPNORM_ALPHA * x + y, lng_ref[...], lnb_ref[...])
    o_ref[...] = x1

    x_hi = x1.astype(BF16)
    x_lo = (x1 - x_hi.astype(F32)).astype(BF16)
    logits = (jnp.dot(x_hi, wr_hi_ref[...], preferred_element_type=F32)
              + jnp.dot(x_hi, wr_lo_ref[...], preferred_element_type=F32)
              + jnp.dot(x_lo, wr_hi_ref[...], preferred_element_type=F32)) + br_ref[...]
    lane = lax.broadcasted_iota(jnp.int32, (tt, ROUTE_LANES), 1)
    lane_f = lane.astype(F32)
    logits = jnp.where(lane < N_EXPERTS, logits, NEG_BIG)
    m1 = jnp.max(logits, axis=-1, keepdims=True)
    i1 = jnp.min(jnp.where(logits == m1, lane_f, float(ROUTE_LANES)), axis=-1, keepdims=True)
    rest = jnp.where(lane_f == i1, NEG_BIG, logits)
    m2 = jnp.max(rest, axis=-1, keepdims=True)
    i2 = jnp.min(jnp.where(rest == m2, lane_f, float(ROUTE_LANES)), axis=-1, keepdims=True)
    e21 = jnp.exp(m2 - m1)
    g1 = 1.0 / (1.0 + e21)
    g2 = e21 * g1
    route_ref[...] = jnp.where(lane == 0, i1, jnp.where(lane == 1, i2, jnp.where(lane == 2, g1, jnp.where(lane == 3, g2, 0.0))))


def _l1_mixer(x, w_in, conv_w, a_log, dt_bias, c_norm, w_out, ln_g, ln_b, w_router, b_router, *, tt):
    bsz, seq, d = x.shape
    w_in_p = _pack_l1_w_in(w_in).astype(BF16)
    lane_pad = lambda v: jnp.pad(v[None, :], ((0, 0), (0, LANES - v.shape[0])))
    wr = jnp.pad(w_router, ((0, 0), (0, ROUTE_LANES - N_EXPERTS)))
    wr_hi = wr.astype(BF16)
    wr_lo = (wr - wr_hi.astype(F32)).astype(BF16)
    params = [w_in_p, conv_w, lane_pad(a_log), lane_pad(dt_bias), c_norm.reshape(1, GDN_V), w_out.astype(BF16),
              ln_g[None, :], ln_b[None, :], jnp.asarray(_chunk_tril_np(tt), BF16), wr_hi, wr_lo, lane_pad(b_router)]
    tile = lambda w: pl.BlockSpec((None, tt, w), lambda b, t: (b, t, 0))
    return pl.pallas_call(
        functools.partial(_l1_mixer_kernel, tt=tt),
        grid=(bsz, seq // tt),
        in_specs=[tile(d)] + [_const_spec(a.shape) for a in params],
        out_specs=[tile(d), tile(ROUTE_LANES)],
        out_shape=[jax.ShapeDtypeStruct((bsz, seq, d), F32), jax.ShapeDtypeStruct((bsz, seq, ROUTE_LANES), F32)],
        scratch_shapes=[pltpu.VMEM((tt, L1_COLS - L1_CONV), F32),
                        pltpu.VMEM((L1_CONV // LANES, tt, LANES), F32), pltpu.VMEM((tt, GDN_V), F32),
                        pltpu.VMEM((GDN_HEADS, GDN_DK, GDN_DV), F32),
                        pltpu.VMEM((tt, GDN_V), F32), pltpu.VMEM((tt, GDN_QK), F32),
                        pltpu.VMEM((tt, GDN_HEADS * LANES), F32),
                        pltpu.VMEM((L1_CONV // LANES, tt + CONV_PAD, LANES), F32)],
        compiler_params=pltpu.CompilerParams(dimension_semantics=("arbitrary", "arbitrary"),
                                             vmem_limit_bytes=VMEM_LIMIT_BYTES),
        name="l1_mixer",
    )(x, *params)


ID_ROWS = 4


def _rank_kernel(route_ref, stril_ref, ids_ref, counts_ref, carry_ref):
    @pl.when(pl.program_id(0) == 0)
    def _():
        carry_ref[...] = jnp.zeros_like(carry_ref)

    r = route_ref[...]
    tr = r.shape[0]
    e1, e2 = r[:, 0:1], r[:, 1:2]
    lane = lax.broadcasted_iota(jnp.int32, (tr, ROUTE_LANES), 1)
    lane_f = lane.astype(F32)
    oh1 = (lane_f == e1).astype(F32)
    oh2 = (lane_f == e2).astype(F32)
    both = oh1 + oh2
    before = jnp.dot(stril_ref[...], both.astype(BF16), preferred_element_type=F32) + carry_ref[...]
    rank1 = jnp.sum(oh1 * before, axis=-1, keepdims=True)
    rank2 = jnp.sum(oh2 * before, axis=-1, keepdims=True)
    table = jnp.where(lane == 0, e1, jnp.where(lane == 1, e2, jnp.where(lane == 2, rank1,
                      jnp.where(lane == 3, rank2, 0.0))))
    ids_ref[...] = table.T[0:ID_ROWS, :].astype(jnp.int32)
    carry_ref[...] += jnp.sum(both, axis=0, keepdims=True)
    counts_ref[...] = carry_ref[...]


def _rank(route2d, *, tr):
    n = route2d.shape[0]
    i = np.arange(tr)
    stril = jnp.asarray((i[None, :] < i[:, None]).astype(np.float32), BF16)
    return pl.pallas_call(
        _rank_kernel,
        grid=(n // tr,),
        in_specs=[pl.BlockSpec((tr, ROUTE_LANES), lambda i: (i, 0)), _const_spec((tr, tr))],
        out_specs=[pl.BlockSpec((ID_ROWS, tr), lambda i: (0, i)), _const_spec((1, ROUTE_LANES))],
        out_shape=[jax.ShapeDtypeStruct((ID_ROWS, n), jnp.int32), jax.ShapeDtypeStruct((1, ROUTE_LANES), F32)],
        scratch_shapes=[pltpu.VMEM((1, ROUTE_LANES), F32)],
        compiler_params=pltpu.CompilerParams(dimension_semantics=("arbitrary",)),
        name="moe_rank",
    )(route2d, stril)


ROW_SUB = D_MODEL // LANES


def _to_row_tiles(dst_ref, src, n):
    for lb in range(ROW_SUB):
        dst_ref[pl.ds(lb, n, stride=ROW_SUB), :] = src[:, lb * LANES:(lb + 1) * LANES]


def _from_row_tiles(src_ref, n):
    return [src_ref[pl.ds(lb, n, stride=ROW_SUB), :] for lb in range(ROW_SUB)]


def _tile_copy(src_ref, src_sub, dst_ref, dst_sub, sem):
    src = src_ref.at[pl.ds(pl.multiple_of(src_sub, ROW_SUB), ROW_SUB), :]
    dst = dst_ref.at[pl.ds(pl.multiple_of(dst_sub, ROW_SUB), ROW_SUB), :]
    return pltpu.make_async_copy(src, dst, sem)


def _wait_rows(hbm_ref, buf_ref, sem, n, copies):
    for _ in range(copies):
        pltpu.make_async_copy(hbm_ref.at[pl.ds(0, n * ROW_SUB), :], buf_ref, sem).wait()


ZERO_ROWS = 64


def _dispatch_kernel(zfill_ref, p0_ref, p1_ref, x_ref, xs_hbm, xt0_ref, xt1_ref, zero_ref, sems, zsem, *, td):
    p_refs = (p0_ref, p1_ref)
    i = pl.program_id(0)
    last = pl.num_programs(0) - 1

    @pl.when(i == 0)
    def _():
        zero_ref[...] = jnp.zeros_like(zero_ref)
        n_ranges = zfill_ref.shape[0] // 2

        def zero_copy(start_sub, r):
            dst = xs_hbm.at[pl.ds(pl.multiple_of(start_sub + r * (ZERO_ROWS * ROW_SUB), ROW_SUB),
                                  ZERO_ROWS * ROW_SUB), :]
            return pltpu.make_async_copy(zero_ref, dst, zsem)

        def fill(e, wait):
            def body(r, carry):
                copy = zero_copy(zfill_ref[e], r)
                if wait:
                    copy.wait()
                else:
                    copy.start()
                return carry

            lax.fori_loop(0, zfill_ref[n_ranges + e], body, 0)

        for e in range(n_ranges):
            fill(e, wait=False)
        for e in range(n_ranges):
            fill(e, wait=True)

    def step(xt_ref, sem):
        @pl.when(i >= 2)
        def _():
            _wait_rows(xs_hbm, xt_ref, sem, td, TOP_K)

        group = td // ROW_SUB
        for g in range(ROW_SUB):
            x_g = x_ref[g * group:(g + 1) * group, :]
            for lb in range(ROW_SUB):
                xt_ref[pl.ds(g * group * ROW_SUB + lb, group, stride=ROW_SUB), :] = x_g[:, lb * LANES:(lb + 1) * LANES]
            for t in range(g * group, (g + 1) * group):
                for k in range(TOP_K):
                    _tile_copy(xt_ref, t * ROW_SUB, xs_hbm, p_refs[k][t], sem).start(priority=k)

    for slot, (xt_ref, sem) in enumerate(((xt0_ref, sems.at[0]), (xt1_ref, sems.at[1]))):
        @pl.when(i % 2 == slot)
        def _():
            step(xt_ref, sem)

    @pl.when(i == last)
    def _():
        _wait_rows(xs_hbm, xt0_ref, sems.at[0], td, TOP_K)
        _wait_rows(xs_hbm, xt1_ref, sems.at[1], td, TOP_K)


def _dispatch(zfill, pos, x2d, n_rows, *, td):
    n, d = x2d.shape
    smem_rows = pl.BlockSpec((td,), lambda i, zf: (i,), memory_space=pltpu.SMEM)
    assert n // td >= 2
    grid_spec = pltpu.PrefetchScalarGridSpec(
        num_scalar_prefetch=1,
        grid=(n // td,),
        in_specs=[smem_rows, smem_rows, pl.BlockSpec((td, d), lambda i, zf: (i, 0))],
        out_specs=pl.BlockSpec(memory_space=pl.ANY),
        scratch_shapes=[pltpu.VMEM((td * ROW_SUB, LANES), F32), pltpu.VMEM((td * ROW_SUB, LANES), F32),
                        pltpu.VMEM((ZERO_ROWS * ROW_SUB, LANES), F32),
                        pltpu.SemaphoreType.DMA((2,)), pltpu.SemaphoreType.DMA(())],
    )
    return pl.pallas_call(
        functools.partial(_dispatch_kernel, td=td),
        grid_spec=grid_spec,
        out_shape=jax.ShapeDtypeStruct((n_rows * ROW_SUB, LANES), F32),
        compiler_params=pltpu.CompilerParams(dimension_semantics=("arbitrary",)),
        name="moe_dispatch",
    )(zfill, pos[0], pos[1], x2d)


def _gmm_kernel(te_ref, used_ref, first_ref, x_ref, wgu_hbm, wd_hbm, o_ref,
                xb_ref, acc_ref, wgb_ref, wub_ref, wdb_ref, sg_ref, su_ref, sd_ref, sems, *, tm, tf):
    i = pl.program_id(0)
    n_tiles = pl.num_programs(0)
    w = _SwigluWeights(wgu_hbm, wd_hbm, (wgb_ref, wub_ref, wdb_ref), (sg_ref, su_ref, sd_ref), sems, tf)
    active = i < used_ref[0]
    is_first = first_ref[i] == 1

    @pl.when(active)
    def _():
        for lb, blk in enumerate(_from_row_tiles(x_ref, tm)):
            xb_ref[:, lb * LANES:(lb + 1) * LANES] = blk.astype(BF16)

    @pl.when(jnp.logical_and(active, is_first))
    def _():
        e = te_ref[i]

        @pl.when(i == 0)
        def _():
            w.request_head(e)

        for j in range(w.nf):
            w.land(e, j)
            w.apply_chunk(xb_ref, acc_ref, j)

    @pl.when(jnp.logical_and(active, jnp.logical_not(is_first)))
    def _():
        for j in range(w.nf):
            w.apply_chunk(xb_ref, acc_ref, j)

    @pl.when(active)
    def _():
        _to_row_tiles(o_ref, acc_ref[...], tm)
        nxt = jnp.minimum(i + 1, n_tiles - 1)

        @pl.when(jnp.logical_and(i + 1 < n_tiles, first_ref[nxt] == 1))
        def _():
            w.request_head(te_ref[nxt])

    @pl.when(jnp.logical_not(active))
    def _():
        o_ref[...] = jnp.zeros_like(o_ref)


def _gmm(tile_expert, n_used, tile_first, xs, w_gu, w_down, *, tm, tf):
    d = D_MODEL
    n_rows = xs.shape[0] // ROW_SUB
    nf = D_FF // tf
    n_tiles = n_rows // tm
    assert nf >= WEIGHT_SLOTS
    grid_spec = pltpu.PrefetchScalarGridSpec(
        num_scalar_prefetch=3,
        grid=(n_tiles,),
        in_specs=[pl.BlockSpec((tm * ROW_SUB, LANES), lambda i, te, used, first: (jnp.minimum(i, used[0] - 1), 0)),
                  pl.BlockSpec(memory_space=pl.ANY), pl.BlockSpec(memory_space=pl.ANY)],
        out_specs=pl.BlockSpec((tm * ROW_SUB, LANES), lambda i, te, used, first: (i, 0)),
        scratch_shapes=[pltpu.VMEM((tm, d), BF16), pltpu.VMEM((tm, d), F32)] + _SwigluWeights.scratch_shapes(tf),
    )
    return pl.pallas_call(
        functools.partial(_gmm_kernel, tm=tm, tf=tf),
        grid_spec=grid_spec,
        out_shape=jax.ShapeDtypeStruct((n_rows * ROW_SUB, LANES), F32),
        compiler_params=pltpu.CompilerParams(dimension_semantics=("arbitrary",),
                                             vmem_limit_bytes=VMEM_LIMIT_BYTES),
        name="moe_gmm",
    )(tile_expert, n_used, tile_first, xs, w_gu, w_down)


def _combine_kernel(p0_ref, p1_ref, p0_next_ref, p1_next_ref, x_ref, route_ref, lng_ref, lnb_ref, y_hbm, o_ref,
                    ya0_ref, yb0_ref, ya1_ref, yb1_ref, sems, *, tc):
    i = pl.program_id(0)
    n = pl.num_programs(0)
    slots = ((ya0_ref, yb0_ref, sems.at[0]), (ya1_ref, yb1_ref, sems.at[1]))

    def issue(p_refs, slot):
        ya_ref, yb_ref, sem = slots[slot]

        def body(t, carry):
            for k, (p_ref, buf) in enumerate(zip(p_refs, (ya_ref, yb_ref))):
                _tile_copy(y_hbm, p_ref[t], buf, t * ROW_SUB, sem).start(priority=k)
            return carry

        lax.fori_loop(0, tc, body, 0, unroll=8)

    @pl.when(i == 0)
    def _():
        issue((p0_ref, p1_ref), 0)

    group = tc // ROW_SUB

    for slot in range(2):
        @pl.when(i % 2 == slot)
        def _():
            ya_ref, yb_ref, sem = slots[slot]
            nya_ref, nyb_ref, nsem = slots[1 - slot]
            _wait_rows(y_hbm, ya_ref, sem, tc, TOP_K)
            r = route_ref[...]
            g1, g2 = r[:, 2:3], r[:, 3:4]
            z = []
            for lb in range(ROW_SUB):
                a = ya_ref[pl.ds(lb, tc, stride=ROW_SUB), :]
                b = yb_ref[pl.ds(lb, tc, stride=ROW_SUB), :]
                z.append(DEEPNORM_ALPHA * x_ref[:, lb * LANES:(lb + 1) * LANES] + (g1 * a + g2 * b))
                for t in range(lb * group, (lb + 1) * group):
                    for k, (p_ref, buf) in enumerate(zip((p0_next_ref, p1_next_ref), (nya_ref, nyb_ref))):
                        _tile_copy(y_hbm, p_ref[t], buf, t * ROW_SUB, nsem).start(priority=k)
            o_ref[...] = _layer_norm(jnp.concatenate(z, axis=1), lng_ref[...], lnb_ref[...])

            @pl.when(i == n - 1)
            def _():
                _wait_rows(y_hbm, nya_ref, nsem, tc, TOP_K)


def _combine(pos, x2d, route2d, ys, ln_g, ln_b, *, tc):
    n, d = x2d.shape
    steps = n // tc
    smem_rows = pl.BlockSpec((tc,), lambda i: (i,), memory_space=pltpu.SMEM)
    smem_next = pl.BlockSpec((tc,), lambda i: (jnp.minimum(i + 1, steps - 1),), memory_space=pltpu.SMEM)
    return pl.pallas_call(
        functools.partial(_combine_kernel, tc=tc),
        grid=(steps,),
        in_specs=[smem_rows, smem_rows, smem_next, smem_next,
                  pl.BlockSpec((tc, d), lambda i: (i, 0)),
                  pl.BlockSpec((tc, ROUTE_LANES), lambda i: (i, 0)),
                  _const_spec((1, d)), _const_spec((1, d)),
                  pl.BlockSpec(memory_space=pl.ANY)],
        out_specs=pl.BlockSpec((tc, d), lambda i: (i, 0)),
        scratch_shapes=[pltpu.VMEM((tc * ROW_SUB, LANES), F32) for _ in range(2 * TOP_K)]
                       + [pltpu.SemaphoreType.DMA((2,))],
        out_shape=jax.ShapeDtypeStruct((n, d), F32),
        compiler_params=pltpu.CompilerParams(dimension_semantics=("arbitrary",),
                                             vmem_limit_bytes=VMEM_LIMIT_BYTES),
        name="moe_combine",
    )(pos[0], pos[1], pos[0], pos[1], x2d, route2d, ln_g[None, :], ln_b[None, :], ys)


def _moe(x2d, route2d, w_gu, w_down, ln_g, ln_b, *, tm, tf, tr, td, tc):
    n, d = x2d.shape
    ids, counts = _rank(route2d, tr=tr)
    cnt = counts[0, :N_EXPERTS].astype(jnp.int32)
    tiles_per = (cnt + tm - 1) // tm
    tile_end = jnp.cumsum(tiles_per)
    offs = (tile_end - tiles_per) * tm
    n_tiles = (n * TOP_K) // tm + N_EXPERTS
    n_used = tile_end[-1:]
    t_idx = jnp.arange(n_tiles, dtype=jnp.int32)
    tile_expert = jnp.sum(jnp.minimum(t_idx, n_used - 1)[:, None] >= tile_end[None, :], axis=1).astype(jnp.int32)
    tile_start = tile_end - tiles_per
    tile_first = jnp.logical_and(t_idx == tile_start[tile_expert], t_idx < n_used).astype(jnp.int32)
    group_start = sum(jnp.where(ids[0:TOP_K] == e, offs[e], 0) for e in range(N_EXPERTS))
    pos = (group_start + ids[TOP_K:2 * TOP_K]) * ROW_SUB
    pad_start = ((offs + cnt) // ZERO_ROWS) * ZERO_ROWS
    pad_blocks = (offs + tiles_per * tm - pad_start) // ZERO_ROWS
    tail_start = n_used * tm
    tail_blocks = (n_tiles - n_used) * (tm // ZERO_ROWS)
    zfill = jnp.concatenate([pad_start * ROW_SUB, tail_start * ROW_SUB, pad_blocks, tail_blocks]).astype(jnp.int32)
    xs = _dispatch(zfill, pos, x2d, n_tiles * tm, td=td)
    ys = _gmm(tile_expert, n_used, tile_first, xs, w_gu, w_down, tm=tm, tf=tf)
    return _combine(pos, x2d, route2d, ys, ln_g, ln_b, tc=tc)


def kernel(x, positions, ab_w_in, gla_w_gate2, gla_b_gate, gla_norm, ret_norm, ab_w_out, ab_ln1_g, ab_ln1_b, ffn_w_gu, ffn_w_down, ab_ln2_g, ab_ln2_b, c_w_in, c_conv_w, c_a_log, c_dt_bias, c_norm, c_w_out, c_ln1_g, c_ln1_b, moe_w_router, moe_b_router, moe_w_gu, moe_w_down, c_ln2_g, c_ln2_b):
    bsz, seq, d = x.shape
    for layer in range(DEPTH):
        i = layer // 2
        if layer % 2 == 0:
            x = _l0_mixer(x, positions, ab_w_in[i], gla_w_gate2[i], gla_b_gate[i], gla_norm[i], ret_norm[i],
                          ab_w_out[i], ab_ln1_g[i], ab_ln1_b[i], tt=L0_TIME_TILE)
            x = _ffn(x.reshape(bsz * seq, d), ffn_w_gu[i], ffn_w_down[i], ab_ln2_g[i], ab_ln2_b[i],
                     tm=FFN_ROW_TILE, tf=FF_TILE).reshape(bsz, seq, d)
        else:
            x, route = _l1_mixer(x, c_w_in[i], c_conv_w[i], c_a_log[i], c_dt_bias[i], c_norm[i], c_w_out[i],
                                 c_ln1_g[i], c_ln1_b[i], moe_w_router[i], moe_b_router[i], tt=L1_TIME_TILE)
            x = _moe(x.reshape(bsz * seq, d), route.reshape(bsz * seq, ROUTE_LANES), moe_w_gu[i], moe_w_down[i],
                     c_ln2_g[i], c_ln2_b[i], tm=MOE_ROW_TILE, tf=FF_TILE, tr=ROUTE_TILE, td=ROUTE_TILE,
                     tc=ROUTE_TILE).reshape(bsz, seq, d)
    return x
```

```python
import functools

import numpy as np
import jax
import jax.numpy as jnp
from jax import lax
from jax.experimental import pallas as pl
from jax.experimental.pallas import tpu as pltpu

F32 = jnp.float32
BF16 = jnp.bfloat16

D_MODEL = 1024
DEPTH = 2
CHUNK = 64
GLA_HEADS, GLA_DK, GLA_DV, GLA_GATE_RANK, GLA_TAU = 4, 64, 128, 16, 16.0
RET_HEADS, RET_DK, RET_DV = 4, 64, 128
ROPE_BASE = 10000.0
GDN_HEADS, GDN_DK, GDN_DV = 8, 128, 128
CONV_WIDTH = 4
D_FF = 3584
N_EXPERTS = 8
TOP_K = 2
NORM_EPS = 1e-5
L2_EPS = 1e-6
DEEPNORM_ALPHA = (2.0 * DEPTH) ** 0.25

GLA_QK = GLA_HEADS * GLA_DK
GLA_V = GLA_HEADS * GLA_DV
RET_QK = RET_HEADS * RET_DK
RET_V = RET_HEADS * RET_DV
GDN_QK = GDN_HEADS * GDN_DK
GDN_V = GDN_HEADS * GDN_DV

LANES = 128
VMEM_LIMIT_BYTES = 56 * 1024 * 1024
NEG_BIG = -1e30

L0_TIME_TILE = 256
L1_TIME_TILE = 256
FFN_ROW_TILE = 512
MOE_ROW_TILE = 512
FF_TILE = 512
ROUTE_TILE = 512


def _mm(a, b):
    return jnp.dot(a.astype(BF16), b.astype(BF16), preferred_element_type=F32)


def _mm_nt(a, b):
    return lax.dot_general(a.astype(BF16), b.astype(BF16), (((1,), (1,)), ((), ())),
                           preferred_element_type=F32)


def _mm_tn(a, b):
    return lax.dot_general(a.astype(BF16), b.astype(BF16), (((0,), (0,)), ((), ())),
                           preferred_element_type=F32)


def _split3(x):
    hi = x.astype(BF16)
    r1 = x - hi.astype(F32)
    mid = r1.astype(BF16)
    lo = (r1 - mid.astype(F32)).astype(BF16)
    return hi, mid, lo


def _mm_exact_lhs01(m01, x):
    hi, mid, lo = _split3(x)
    return (jnp.dot(m01, hi, preferred_element_type=F32)
            + jnp.dot(m01, mid, preferred_element_type=F32)
            + jnp.dot(m01, lo, preferred_element_type=F32))


def _sigmoid(x):
    return 0.5 * jnp.tanh(0.5 * x) + 0.5


def _silu(x):
    return x * _sigmoid(x)


def _softplus(x):
    return jnp.maximum(x, 0.0) + jnp.log(1.0 + jnp.exp(-jnp.abs(x)))


def _layer_norm(x, g, b):
    mu = jnp.mean(x, axis=-1, keepdims=True)
    xc = x - mu
    var = jnp.mean(xc * xc, axis=-1, keepdims=True)
    return xc * lax.rsqrt(var + NORM_EPS) * g + b


def _chunk_tril_np(tt):
    i = np.arange(tt)
    same = (i[:, None] // CHUNK) == (i[None, :] // CHUNK)
    return (same & (i[None, :] <= i[:, None])).astype(np.float32)


def _const_spec(shape):
    nd = len(shape)
    return pl.BlockSpec(shape, lambda *_: (0,) * nd)


L0_GQ, L0_GK, L0_GV, L0_GR = 0, 256, 512, 1024
L0_RQ, L0_RK, L0_RV, L0_RG = 1536, 1792, 2048, 2560
L0_GA = 3072
L0_COLS = 3200
L0_CHUNK_GROUP = 1


def _pack_l0_w_in(w):
    offs = np.cumsum([0, GLA_QK, GLA_QK, GLA_V, GLA_GATE_RANK, GLA_V, RET_QK, RET_QK, RET_V, RET_V])
    gq, gk, gv, ga, gr, rq, rk, rv, rg = [w[:, offs[i]:offs[i + 1]] for i in range(9)]
    ga = jnp.pad(ga, ((0, 0), (0, LANES - GLA_GATE_RANK)))
    return jnp.concatenate([gq, gk, gv, gr, rq, rk, rv, rg, ga], axis=1)


def _ret_tables():
    h = np.arange(RET_HEADS, dtype=np.float64)
    log_gamma = np.log(1.0 - 2.0 ** (-5.0 - h))
    pos = np.arange(CHUNK, dtype=np.float64)
    diff = pos[:, None] - pos[None, :]
    dmat = np.where(diff >= 0, np.exp(log_gamma[:, None, None] * np.maximum(diff, 0.0)), 0.0)
    xi = np.exp(log_gamma[None, :] * (pos[:, None] + 1.0))
    zeta = np.exp(log_gamma[None, :] * (CHUNK - 1.0 - pos[:, None]))
    decay = np.exp(log_gamma * CHUNK)
    xi_full = np.repeat(xi, RET_DV, axis=1)
    zeta_full = np.repeat(zeta, RET_DK, axis=1)
    decay_full = np.repeat(decay, RET_DK)[None, :]
    return (dmat.astype(np.float32), xi_full.astype(np.float32), zeta_full.astype(np.float32),
            decay_full.astype(np.float32))


def _rope_tables():
    half = RET_DK // 2
    inv_freq = ROPE_BASE ** (-np.arange(0, RET_DK, 2, dtype=np.float32) / RET_DK)
    per_head = np.concatenate([inv_freq, inv_freq])
    freq_full = np.tile(per_head, RET_HEADS)[None, :].astype(np.float32)
    sign = np.tile(np.concatenate([-np.ones(half), np.ones(half)]), RET_HEADS)[None, :].astype(np.float32)
    return freq_full, sign


def _l0_mixer_kernel(x_ref, pos_ref, w_in_ref, wg2_ref, bg_ref, gnorm_ref, rnorm_ref, w_out_ref,
                     lng_ref, lnb_ref, tril_ref, dmat_ref, xi_ref, zeta_ref, rdecay_ref,
                     freq_ref, sign_ref, o_ref, h_ref, mix_ref, sg_ref, sr_ref, *, tt):
    ti = pl.program_id(1)

    @pl.when(ti == 0)
    def _():
        sg_ref[...] = jnp.zeros_like(sg_ref)
        sr_ref[...] = jnp.zeros_like(sr_ref)

    x = x_ref[...]
    h_ref[...] = _mm(x, w_in_ref[...])

    lane = lax.broadcasted_iota(jnp.int32, (1, LANES), 1)
    lo_half = lane < GLA_DK
    ci = lax.broadcasted_iota(jnp.int32, (CHUNK, CHUNK), 0)
    cj = lax.broadcasted_iota(jnp.int32, (CHUNK, CHUNK), 1)
    causal = cj <= ci

    z = _mm(h_ref[:, L0_GA:L0_GA + LANES], wg2_ref[...]) + bg_ref[...]
    log_a = -_softplus(-z) * (1.0 / GLA_TAU)
    b = _mm_exact_lhs01(tril_ref[...], log_a)
    eb = jnp.exp(b)
    q_dec = h_ref[:, L0_GQ:L0_GQ + GLA_QK] * (GLA_DK ** -0.5) * eb
    k_all = h_ref[:, L0_GK:L0_GK + GLA_QK]
    k_neg = k_all * jnp.exp(-b)

    ang = pos_ref[...] * freq_ref[:, 0:LANES]
    cos = jnp.concatenate([jnp.cos(ang)] * (RET_QK // LANES), axis=1)
    sin = jnp.concatenate([jnp.sin(ang)] * (RET_QK // LANES), axis=1) * sign_ref[...]
    half = RET_DK // 2
    lane256 = lax.broadcasted_iota(jnp.int32, (1, RET_QK), 1)
    first_half = (lane256 & (RET_DK - 1)) < half

    def rope(t):
        swapped = jnp.where(first_half, pltpu.roll(t, RET_QK - half, 1), pltpu.roll(t, half, 1))
        return t * cos + swapped * sin

    rq = rope(h_ref[:, L0_RQ:L0_RQ + RET_QK]) * (RET_DK ** -0.5)
    rk = rope(h_ref[:, L0_RK:L0_RK + RET_QK])

    n_chunks = tt // CHUNK
    heads = range(GLA_HEADS)
    pairs = range(GLA_HEADS // 2)
    pair_lanes = lambda hd: slice((hd // 2) * LANES, (hd // 2 + 1) * LANES)
    head_mask = lambda hd: lo_half if hd % 2 == 0 else jnp.logical_not(lo_half)
    vcol = lambda base, hd: slice(base + hd * GLA_DV, base + (hd + 1) * GLA_DV)
    sg = [sg_ref[p] for p in pairs]
    sr = [sr_ref[p] for p in pairs]
    for c0 in range(0, n_chunks, L0_CHUNK_GROUP):
        group = range(c0, c0 + L0_CHUNK_GROUP)
        rows = {c: slice(c * CHUNK, (c + 1) * CHUNK) for c in group}
        b_last = {c: b[(c + 1) * CHUNK - 1:(c + 1) * CHUNK, :] for c in group}
        k_end = {c: k_all[rows[c]] * jnp.exp(b_last[c] - b[rows[c]]) for c in group}
        rk_c = {c: rk[rows[c]] for c in group}
        rk_z = {c: rk_c[c] * zeta_ref[...] for c in group}
        inst = [(c, hd) for c in group for hd in heads]
        g_q = {ch: jnp.where(head_mask(ch[1]), q_dec[rows[ch[0]], pair_lanes(ch[1])], 0.0) for ch in inst}
        r_q = {ch: jnp.where(head_mask(ch[1]), rq[rows[ch[0]], pair_lanes(ch[1])], 0.0) for ch in inst}
        g_v = {ch: h_ref[rows[ch[0]], vcol(L0_GV, ch[1])] for ch in inst}
        r_v = {ch: h_ref[rows[ch[0]], vcol(L0_RV, ch[1])] for ch in inst}
        g_att = {(c, hd): jnp.where(causal, _mm_nt(g_q[c, hd], k_neg[rows[c], pair_lanes(hd)]), 0.0)
                 for c, hd in inst}
        r_att = {(c, hd): _mm_nt(r_q[c, hd], rk_c[c][:, pair_lanes(hd)]) * dmat_ref[hd] for c, hd in inst}
        g_d = {(c, hd): _mm_tn(g_v[c, hd], k_end[c][:, pair_lanes(hd)]) for c, hd in inst}
        r_d = {(c, hd): _mm_tn(r_v[c, hd], rk_z[c][:, pair_lanes(hd)]) for c, hd in inst}
        sg_in, sr_in = {}, {}
        for c in group:
            sg_in[c], sr_in[c] = sg, sr
            dec_c = jnp.exp(b_last[c])
            sg = [sg[p] * dec_c[:, p * LANES:(p + 1) * LANES] + jnp.where(lo_half, g_d[c, 2 * p], g_d[c, 2 * p + 1])
                  for p in pairs]
            sr = [sr[p] * rdecay_ref[:, p * LANES:(p + 1) * LANES]
                  + jnp.where(lo_half, r_d[c, 2 * p], r_d[c, 2 * p + 1]) for p in pairs]
        g_o = {(c, hd): _mm(g_att[c, hd], g_v[c, hd]) + _mm_nt(g_q[c, hd], sg_in[c][hd // 2]) for c, hd in inst}
        r_o = {(c, hd): _mm(r_att[c, hd], r_v[c, hd])
               + _mm_nt(r_q[c, hd], sr_in[c][hd // 2]) * xi_ref[:, vcol(0, hd)] for c, hd in inst}
        for c, hd in inst:
            mix_ref[rows[c], vcol(0, hd)] = g_o[c, hd]
            mix_ref[rows[c], vcol(GLA_V, hd)] = r_o[c, hd]
    for p in pairs:
        sg_ref[p] = sg[p]
        sr_ref[p] = sr[p]

    for hd in range(GLA_HEADS):
        sl = slice(hd * GLA_DV, (hd + 1) * GLA_DV)
        o = mix_ref[:, sl]
        o = o * lax.rsqrt(jnp.mean(o * o, axis=-1, keepdims=True) + NORM_EPS) * gnorm_ref[:, sl]
        mix_ref[:, sl] = o * _silu(h_ref[:, L0_GR + hd * GLA_DV:L0_GR + (hd + 1) * GLA_DV])
    for hd in range(RET_HEADS):
        sl = slice(hd * RET_DV, (hd + 1) * RET_DV)
        o = mix_ref[:, GLA_V + hd * RET_DV:GLA_V + (hd + 1) * RET_DV]
        oc = o - jnp.mean(o, axis=-1, keepdims=True)
        o = oc * lax.rsqrt(jnp.mean(oc * oc, axis=-1, keepdims=True) + NORM_EPS) * rnorm_ref[:, sl]
        mix_ref[:, GLA_V + hd * RET_DV:GLA_V + (hd + 1) * RET_DV] = (
            o * _silu(h_ref[:, L0_RG + hd * RET_DV:L0_RG + (hd + 1) * RET_DV]))

    y = _mm(mix_ref[...], w_out_ref[...])
    o_ref[...] = _layer_norm(DEEPNORM_ALPHA * x + y, lng_ref[...], lnb_ref[...])


def _l0_mixer(x, positions, w_in, w_gate2, b_gate, gla_norm, ret_norm, w_out, ln_g, ln_b, *, tt):
    bsz, seq, d = x.shape
    w_in_p = _pack_l0_w_in(w_in).astype(BF16)
    wg2 = jnp.pad(w_gate2, ((0, LANES - GLA_GATE_RANK), (0, 0))).astype(BF16)
    dmat, xi_full, zeta_full, rdecay = _ret_tables()
    freq_full, sign = _rope_tables()
    pos_f = positions.astype(F32)[..., None]
    consts = [jnp.asarray(_chunk_tril_np(tt), BF16), jnp.asarray(dmat), jnp.asarray(xi_full),
              jnp.asarray(zeta_full), jnp.asarray(rdecay), jnp.asarray(freq_full), jnp.asarray(sign)]
    params = [w_in_p, wg2, b_gate[None, :], gla_norm.reshape(1, GLA_V), ret_norm.reshape(1, RET_V),
              w_out.astype(BF16), ln_g[None, :], ln_b[None, :]]
    tile = lambda w: pl.BlockSpec((None, tt, w), lambda b, t: (b, t, 0))
    return pl.pallas_call(
        functools.partial(_l0_mixer_kernel, tt=tt),
        grid=(bsz, seq // tt),
        in_specs=[tile(d), tile(1)] + [_const_spec(a.shape) for a in params + consts],
        out_specs=tile(d),
        out_shape=jax.ShapeDtypeStruct((bsz, seq, d), F32),
        scratch_shapes=[pltpu.VMEM((tt, L0_COLS), F32), pltpu.VMEM((tt, GLA_V + RET_V), F32),
                        pltpu.VMEM((GLA_HEADS // 2, GLA_DV, LANES), F32),
                        pltpu.VMEM((RET_HEADS // 2, RET_DV, LANES), F32)],
        compiler_params=pltpu.CompilerParams(dimension_semantics=("arbitrary", "arbitrary"),
                                             vmem_limit_bytes=VMEM_LIMIT_BYTES),
        name="l0_mixer",
    )(x, pos_f, *params, *consts)


WEIGHT_SLOTS = 2


class _SwigluWeights:
    def __init__(self, wgu_hbm, wd_hbm, resident, staging, sems, tf):
        self.wgu_hbm, self.wd_hbm = wgu_hbm, wd_hbm
        self.wgb, self.wub, self.wdb = resident
        self.sg, self.su, self.sd = staging
        self.sems, self.tf, self.nf = sems, tf, D_FF // tf

    def _copies(self, e, j):
        slot = j % WEIGHT_SLOTS
        cols = pl.ds(j * self.tf, self.tf)
        up_cols = pl.ds(D_FF + j * self.tf, self.tf)
        return (pltpu.make_async_copy(self.wgu_hbm.at[e, :, cols], self.sg.at[slot], self.sems.at[slot]),
                pltpu.make_async_copy(self.wgu_hbm.at[e, :, up_cols], self.su.at[slot], self.sems.at[slot]),
                pltpu.make_async_copy(self.wd_hbm.at[e, cols, :], self.sd.at[slot], self.sems.at[slot]))

    def request(self, e, j):
        for c in self._copies(e, j):
            c.start()

    def request_head(self, e):
        for j in range(WEIGHT_SLOTS):
            self.request(e, j)

    def land(self, e, j):
        slot = j % WEIGHT_SLOTS
        for c in self._copies(e, j):
            c.wait()
        self.wgb[j] = self.sg[slot].astype(BF16)
        self.wub[j] = self.su[slot].astype(BF16)
        self.wdb[j] = self.sd[slot].astype(BF16)
        if j + WEIGHT_SLOTS < self.nf:
            self.request(e, j + WEIGHT_SLOTS)

    def apply_chunk(self, xb_ref, acc_ref, j):
        xb = xb_ref[...]
        gt = jnp.dot(xb, self.wgb[j], preferred_element_type=F32)
        up = jnp.dot(xb, self.wub[j], preferred_element_type=F32)
        part = jnp.dot((_silu(gt) * up).astype(BF16), self.wdb[j], preferred_element_type=F32)
        if j == 0:
            acc_ref[...] = part
        else:
            acc_ref[...] += part

    @staticmethod
    def scratch_shapes(tf):
        nf, d = D_FF // tf, D_MODEL
        return [pltpu.VMEM((nf, d, tf), BF16), pltpu.VMEM((nf, d, tf), BF16), pltpu.VMEM((nf, tf, d), BF16),
                pltpu.VMEM((WEIGHT_SLOTS, d, tf), F32), pltpu.VMEM((WEIGHT_SLOTS, d, tf), F32),
                pltpu.VMEM((WEIGHT_SLOTS, tf, d), F32), pltpu.SemaphoreType.DMA((WEIGHT_SLOTS,))]


def _ffn_kernel(x_ref, wgu_hbm, wd_hbm, lng_ref, lnb_ref, o_ref, xb_ref, acc_ref,
                wgb_ref, wub_ref, wdb_ref, sg_ref, su_ref, sd_ref, sems, *, tf):
    i = pl.program_id(0)
    w = _SwigluWeights(wgu_hbm, wd_hbm, (wgb_ref, wub_ref, wdb_ref), (sg_ref, su_ref, sd_ref), sems, tf)
    xb_ref[...] = x_ref[...].astype(BF16)

    @pl.when(i == 0)
    def _():
        w.request_head(0)
        for j in range(w.nf):
            w.land(0, j)
            w.apply_chunk(xb_ref, acc_ref, j)

    @pl.when(i > 0)
    def _():
        for j in range(w.nf):
            w.apply_chunk(xb_ref, acc_ref, j)

    o_ref[...] = _layer_norm(DEEPNORM_ALPHA * x_ref[...] + acc_ref[...], lng_ref[...], lnb_ref[...])


def _ffn(x2d, w_gu, w_down, ln_g, ln_b, *, tm, tf):
    n, d = x2d.shape
    assert D_FF // tf >= WEIGHT_SLOTS
    return pl.pallas_call(
        functools.partial(_ffn_kernel, tf=tf),
        grid=(n // tm,),
        in_specs=[pl.BlockSpec((tm, d), lambda i: (i, 0)),
                  pl.BlockSpec(memory_space=pl.ANY), pl.BlockSpec(memory_space=pl.ANY),
                  _const_spec((1, d)), _const_spec((1, d))],
        out_specs=pl.BlockSpec((tm, d), lambda i: (i, 0)),
        out_shape=jax.ShapeDtypeStruct((n, d), F32),
        scratch_shapes=[pltpu.VMEM((tm, d), BF16), pltpu.VMEM((tm, d), F32)] + _SwigluWeights.scratch_shapes(tf),
        compiler_params=pltpu.CompilerParams(dimension_semantics=("arbitrary",),
                                             vmem_limit_bytes=VMEM_LIMIT_BYTES),
        name="ffn",
    )(x2d, w_gu[None], w_down[None], ln_g[None, :], ln_b[None, :])


L1_Q, L1_K, L1_V, L1_GATE, L1_AB = 0, 1024, 2048, 3072, 4096
L1_CONV = 3 * GDN_QK
L1_COLS = 4224
L1_BETA_LANE = GDN_HEADS
CONV_PAD = 8
L1_PROJ_BLOCK = 512
L1_CHUNK_GROUP = 2
ROUTE_LANES = LANES


def _pack_l1_w_in(w):
    offs = np.cumsum([0, L1_CONV, GDN_HEADS, GDN_HEADS, GDN_V])
    qkv, a_in, b_in, gate = [w[:, offs[i]:offs[i + 1]] for i in range(4)]
    ab = jnp.pad(jnp.concatenate([a_in, b_in], axis=1), ((0, 0), (0, LANES - 2 * GDN_HEADS)))
    return jnp.concatenate([qkv, gate, ab], axis=1)


def _l1_mixer_kernel(x_ref, w_in_ref, conv_ref, alog_ref, dtb_ref, cnorm_ref, w_out_ref, lng_ref, lnb_ref,
                     tril_ref, wr_hi_ref, wr_lo_ref, br_ref, stril_ref,
                     o_ref, route_ref, ids_ref, counts_ref,
                     h2_ref, qkv_ref, mix_ref, s_ref, u_ref, w_ref, aqk_ref, ext_ref, carry_ref, *, tt):
    ti = pl.program_id(1)

    @pl.when(ti == 0)
    def _():
        s_ref[...] = jnp.zeros_like(s_ref)
        ext_ref[:, 0:CONV_PAD, :] = jnp.zeros((L1_CONV // LANES, CONV_PAD, LANES), F32)

    x = x_ref[...]
    xb = x.astype(BF16)
    h2_ref[...] = jnp.dot(xb, w_in_ref[:, L1_CONV:L1_COLS], preferred_element_type=F32)

    half = tt // 2
    lanes_per_block = L1_PROJ_BLOCK // LANES
    for blk in range(L1_CONV // L1_PROJ_BLOCK):
        h_blk = jnp.dot(xb, w_in_ref[:, blk * L1_PROJ_BLOCK:(blk + 1) * L1_PROJ_BLOCK], preferred_element_type=F32)
        for l in range(lanes_per_block):
            ext_ref[blk * lanes_per_block + l, CONV_PAD:CONV_PAD + tt, :] = h_blk[:, l * LANES:(l + 1) * LANES]
        for l in range(lanes_per_block):
            lb = blk * lanes_per_block + l
            lanes = slice(lb * LANES, (lb + 1) * LANES)
            for parity in range(2):
                conv = None
                for j in range(CONV_WIDTH):
                    first_row = CONV_PAD - (CONV_WIDTH - 1 - j) + parity
                    term = ext_ref[lb, pl.ds(first_row, half, stride=2), :] * conv_ref[j:j + 1, lanes]
                    conv = term if conv is None else conv + term
                act = _silu(conv)
                if lb * LANES < L1_V:
                    scale = GDN_DK ** -0.5 if lb * LANES < L1_K else 1.0
                    act = act * (lax.rsqrt(jnp.sum(act * act, axis=-1, keepdims=True) + L2_EPS) * scale)
                qkv_ref[lb, pl.ds(parity, half, stride=2), :] = act
            ext_ref[lb, 0:CONV_PAD, :] = ext_ref[lb, tt:tt + CONV_PAD, :]

    ab = h2_ref[:, L1_AB - L1_CONV:L1_AB - L1_CONV + LANES]
    g_blk = -jnp.exp(alog_ref[...]) * _softplus(ab + dtb_ref[...])
    beta_blk = _sigmoid(ab)
    gc_blk = _mm_exact_lhs01(tril_ref[...], g_blk)
    eg_blk = jnp.exp(gc_blk)

    ci = lax.broadcasted_iota(jnp.int32, (CHUNK, CHUNK), 0)
    cj = lax.broadcasted_iota(jnp.int32, (CHUNK, CHUNK), 1)
    incl = cj <= ci
    strict = cj < ci

    heads = range(GDN_HEADS)
    col = lambda base, hd: slice(base + hd * GDN_DK, base + (hd + 1) * GDN_DK)
    blk_of = lambda base, hd: base // LANES + hd
    n_chunks = tt // CHUNK

    s = [s_ref[hd] for hd in heads]
    for c0 in range(0, n_chunks, L1_CHUNK_GROUP):
        inst = [(c, hd) for c in range(c0, c0 + L1_CHUNK_GROUP) for hd in heads]
        n_i = range(len(inst))
        rows = [slice(c * CHUNK, (c + 1) * CHUNK) for c, _ in inst]
        gc_c = {c: gc_blk[c * CHUNK:(c + 1) * CHUNK] for c in range(c0, c0 + L1_CHUNK_GROUP)}
        gc_t = {c: gc_c[c].T for c in gc_c}
        kdec_scale = {c: jnp.exp(gc_c[c][CHUNK - 1:CHUNK, :] - gc_c[c]) for c in gc_c}
        q_h = [qkv_ref[blk_of(L1_Q, hd), rows[i], :] for i, (c, hd) in enumerate(inst)]
        k_h = [qkv_ref[blk_of(L1_K, hd), rows[i], :] for i, (c, hd) in enumerate(inst)]
        v_h = [qkv_ref[blk_of(L1_V, hd), rows[i], :] for i, (c, hd) in enumerate(inst)]
        beta = [beta_blk[rows[i], L1_BETA_LANE + hd:L1_BETA_LANE + hd + 1] for i, (c, hd) in enumerate(inst)]
        eg = [eg_blk[rows[i], hd:hd + 1] for i, (c, hd) in enumerate(inst)]
        decay = [jnp.exp(jnp.where(incl, gc_c[c][:, hd:hd + 1] - gc_t[c][hd:hd + 1, :], NEG_BIG)) for c, hd in inst]
        kb = [k_h[i] * beta[i] for i in n_i]
        low = [jnp.where(strict, _mm_nt(kb[i], k_h[i]) * decay[i], 0.0) for i in n_i]
        a_qk = [_mm_nt(q_h[i], k_h[i]) * decay[i] for i in n_i]
        a_m = [-low[i] for i in n_i]
        m = [_mm(low[i], low[i]) for i in n_i]
        for it in range(5):
            am = [_mm(a_m[i], m[i]) for i in n_i]
            a_m = [a_m[i] + m[i] + am[i] for i in n_i]
            if it < 4:
                m = [_mm(m[i], m[i]) for i in n_i]
        rhs = [jnp.concatenate([v_h[i] * beta[i], kb[i] * eg[i]], axis=1) for i in n_i]
        uw = [rhs[i] + _mm(a_m[i], rhs[i]) for i in n_i]
        for i, (c, hd) in enumerate(inst):
            u_ref[rows[i], col(0, hd)] = uw[i][:, 0:GDN_DV]
            w_ref[rows[i], col(0, hd)] = uw[i][:, GDN_DV:GDN_DV + GDN_DK]
            aqk_ref[rows[i], hd * LANES:hd * LANES + CHUNK] = a_qk[i]
            qkv_ref[blk_of(L1_Q, hd), rows[i], :] = q_h[i] * eg[i]
            qkv_ref[blk_of(L1_K, hd), rows[i], :] = k_h[i] * kdec_scale[c][:, hd:hd + 1]

        for c in range(c0, c0 + L1_CHUNK_GROUP):
            crow = slice(c * CHUNK, (c + 1) * CHUNK)
            e_last = jnp.exp(gc_blk[(c + 1) * CHUNK - 1:(c + 1) * CHUNK, :])
            ws_qs = [_mm(jnp.concatenate([w_ref[crow, col(0, hd)], qkv_ref[blk_of(L1_Q, hd), crow, :]], axis=0), s[hd])
                     for hd in heads]
            v_new = [u_ref[crow, col(0, hd)] - ws_qs[hd][0:CHUNK] for hd in heads]
            o_h = [ws_qs[hd][CHUNK:2 * CHUNK] + _mm(aqk_ref[crow, hd * LANES:hd * LANES + CHUNK], v_new[hd])
                   for hd in heads]
            s = [s[hd] * e_last[:, hd:hd + 1] + _mm_tn(qkv_ref[blk_of(L1_K, hd), crow, :], v_new[hd]) for hd in heads]
            for hd in heads:
                mix_ref[crow, col(0, hd)] = o_h[hd]
    for hd in heads:
        s_ref[hd] = s[hd]

    for hd in range(GDN_HEADS):
        sl = slice(hd * GDN_DV, (hd + 1) * GDN_DV)
        o = mix_ref[:, sl]
        o = o * lax.rsqrt(jnp.mean(o * o, axis=-1, keepdims=True) + NORM_EPS) * cnorm_ref[:, sl]
        mix_ref[:, sl] = o * _silu(h2_ref[:, L1_GATE - L1_CONV + hd * GDN_DV:L1_GATE - L1_CONV + (hd + 1) * GDN_DV])
    y = _mm(mix_ref[...], w_out_ref[...])
    x1 = _layer_norm(DEEPNORM_ALPHA * x + y, lng_ref[...], lnb_ref[...])
    o_ref[...] = x1

    x_hi = x1.astype(BF16)
    x_lo = (x1 - x_hi.astype(F32)).astype(BF16)
    logits = (jnp.dot(x_hi, wr_hi_ref[...], preferred_element_type=F32)
              + jnp.dot(x_hi, wr_lo_ref[...], preferred_element_type=F32)
              + jnp.dot(x_lo, wr_hi_ref[...], preferred_element_type=F32)) + br_ref[...]
    lane = lax.broadcasted_iota(jnp.int32, (tt, ROUTE_LANES), 1)
    lane_f = lane.astype(F32)
    logits = jnp.where(lane < N_EXPERTS, logits, NEG_BIG)
    m1 = jnp.max(logits, axis=-1, keepdims=True)
    i1 = jnp.min(jnp.where(logits == m1, lane_f, float(ROUTE_LANES)), axis=-1, keepdims=True)
    rest = jnp.where(lane_f == i1, NEG_BIG, logits)
    m2 = jnp.max(rest, axis=-1, keepdims=True)
    i2 = jnp.min(jnp.where(rest == m2, lane_f, float(ROUTE_LANES)), axis=-1, keepdims=True)
    e21 = jnp.exp(m2 - m1)
    g1 = 1.0 / (1.0 + e21)
    g2 = e21 * g1
    route_ref[...] = jnp.where(lane == 0, i1, jnp.where(lane == 1, i2, jnp.where(lane == 2, g1, jnp.where(lane == 3, g2, 0.0))))

    @pl.when(jnp.logical_and(pl.program_id(0) == 0, ti == 0))
    def _():
        carry_ref[...] = jnp.zeros_like(carry_ref)

    oh1 = (lane_f == i1).astype(F32)
    oh2 = (lane_f == i2).astype(F32)
    both = oh1 + oh2
    before = jnp.dot(stril_ref[...], both.astype(BF16), preferred_element_type=F32) + carry_ref[...]
    rank1 = jnp.sum(oh1 * before, axis=-1, keepdims=True)
    rank2 = jnp.sum(oh2 * before, axis=-1, keepdims=True)
    table = jnp.where(lane == 0, i1, jnp.where(lane == 1, i2, jnp.where(lane == 2, rank1,
                      jnp.where(lane == 3, rank2, 0.0))))
    ids_ref[...] = table.T[0:ID_ROWS, :].astype(jnp.int32)
    carry_ref[...] += jnp.sum(both, axis=0, keepdims=True)
    counts_ref[...] = carry_ref[...]


def _l1_mixer(x, w_in, conv_w, a_log, dt_bias, c_norm, w_out, ln_g, ln_b, w_router, b_router, *, tt):
    bsz, seq, d = x.shape
    w_in_p = _pack_l1_w_in(w_in).astype(BF16)
    lane_pad = lambda v: jnp.pad(v[None, :], ((0, 0), (0, LANES - v.shape[0])))
    wr = jnp.pad(w_router, ((0, 0), (0, ROUTE_LANES - N_EXPERTS)))
    wr_hi = wr.astype(BF16)
    wr_lo = (wr - wr_hi.astype(F32)).astype(BF16)
    params = [w_in_p, conv_w, lane_pad(a_log), lane_pad(dt_bias), c_norm.reshape(1, GDN_V), w_out.astype(BF16),
              ln_g[None, :], ln_b[None, :], jnp.asarray(_chunk_tril_np(tt), BF16), wr_hi, wr_lo, lane_pad(b_router)]
    i_row = np.arange(tt)
    params.append(jnp.asarray((i_row[None, :] < i_row[:, None]).astype(np.float32), BF16))
    nt = seq // tt
    tile = lambda w: pl.BlockSpec((None, tt, w), lambda b, t: (b, t, 0))
    return pl.pallas_call(
        functools.partial(_l1_mixer_kernel, tt=tt),
        grid=(bsz, nt),
        in_specs=[tile(d)] + [_const_spec(a.shape) for a in params],
        out_specs=[tile(d), tile(ROUTE_LANES), pl.BlockSpec((ID_ROWS, tt), lambda b, t: (0, b * nt + t)),
                   _const_spec((1, ROUTE_LANES))],
        out_shape=[jax.ShapeDtypeStruct((bsz, seq, d), F32), jax.ShapeDtypeStruct((bsz, seq, ROUTE_LANES), F32),
                   jax.ShapeDtypeStruct((ID_ROWS, bsz * seq), jnp.int32),
                   jax.ShapeDtypeStruct((1, ROUTE_LANES), F32)],
        scratch_shapes=[pltpu.VMEM((tt, L1_COLS - L1_CONV), F32),
                        pltpu.VMEM((L1_CONV // LANES, tt, LANES), F32), pltpu.VMEM((tt, GDN_V), F32),
                        pltpu.VMEM((GDN_HEADS, GDN_DK, GDN_DV), F32),
                        pltpu.VMEM((tt, GDN_V), F32), pltpu.VMEM((tt, GDN_QK), F32),
                        pltpu.VMEM((tt, GDN_HEADS * LANES), F32),
                        pltpu.VMEM((L1_CONV // LANES, tt + CONV_PAD, LANES), F32),
                        pltpu.VMEM((1, ROUTE_LANES), F32)],
        compiler_params=pltpu.CompilerParams(dimension_semantics=("arbitrary", "arbitrary"),
                                             vmem_limit_bytes=VMEM_LIMIT_BYTES),
        name="l1_mixer",
    )(x, *params)


ID_ROWS = 4


ROW_SUB = D_MODEL // LANES


def _to_row_tiles(dst_ref, src, n):
    for lb in range(ROW_SUB):
        dst_ref[pl.ds(lb, n, stride=ROW_SUB), :] = src[:, lb * LANES:(lb + 1) * LANES]


def _from_row_tiles(src_ref, n):
    return [src_ref[pl.ds(lb, n, stride=ROW_SUB), :] for lb in range(ROW_SUB)]


def _tile_copy(src_ref, src_sub, dst_ref, dst_sub, sem):
    src = src_ref.at[pl.ds(pl.multiple_of(src_sub, ROW_SUB), ROW_SUB), :]
    dst = dst_ref.at[pl.ds(pl.multiple_of(dst_sub, ROW_SUB), ROW_SUB), :]
    return pltpu.make_async_copy(src, dst, sem)


def _wait_rows(hbm_ref, buf_ref, sem, n, copies):
    for _ in range(copies):
        pltpu.make_async_copy(hbm_ref.at[pl.ds(0, n * ROW_SUB), :], buf_ref, sem).wait()


ZERO_ROWS = 64


def _dispatch_kernel(zfill_ref, p0_ref, p1_ref, x_ref, xs_hbm, xt0_ref, xt1_ref, zero_ref, sems, zsem, *, td):
    p_refs = (p0_ref, p1_ref)
    i = pl.program_id(0)
    last = pl.num_programs(0) - 1

    @pl.when(i == 0)
    def _():
        zero_ref[...] = jnp.zeros_like(zero_ref)
        n_ranges = zfill_ref.shape[0] // 2

        def zero_copy(start_sub, r):
            dst = xs_hbm.at[pl.ds(pl.multiple_of(start_sub + r * (ZERO_ROWS * ROW_SUB), ROW_SUB),
                                  ZERO_ROWS * ROW_SUB), :]
            return pltpu.make_async_copy(zero_ref, dst, zsem)

        def fill(e, wait):
            def body(r, carry):
                copy = zero_copy(zfill_ref[e], r)
                if wait:
                    copy.wait()
                else:
                    copy.start()
                return carry

            lax.fori_loop(0, zfill_ref[n_ranges + e], body, 0)

        for e in range(n_ranges):
            fill(e, wait=False)
        for e in range(n_ranges):
            fill(e, wait=True)

    def step(xt_ref, sem):
        @pl.when(i >= 2)
        def _():
            _wait_rows(xs_hbm, xt_ref, sem, td, TOP_K)

        group = td // ROW_SUB
        for g in range(ROW_SUB):
            x_g = x_ref[g * group:(g + 1) * group, :]
            for lb in range(ROW_SUB):
                xt_ref[pl.ds(g * group * ROW_SUB + lb, group, stride=ROW_SUB), :] = x_g[:, lb * LANES:(lb + 1) * LANES]
            for t in range(g * group, (g + 1) * group):
                for k in range(TOP_K):
                    _tile_copy(xt_ref, t * ROW_SUB, xs_hbm, p_refs[k][t], sem).start(priority=k)

    for slot, (xt_ref, sem) in enumerate(((xt0_ref, sems.at[0]), (xt1_ref, sems.at[1]))):
        @pl.when(i % 2 == slot)
        def _():
            step(xt_ref, sem)

    @pl.when(i == last)
    def _():
        _wait_rows(xs_hbm, xt0_ref, sems.at[0], td, TOP_K)
        _wait_rows(xs_hbm, xt1_ref, sems.at[1], td, TOP_K)


def _dispatch(zfill, pos, x2d, n_rows, *, td):
    n, d = x2d.shape
    smem_rows = pl.BlockSpec((td,), lambda i, zf: (i,), memory_space=pltpu.SMEM)
    assert n // td >= 2
    grid_spec = pltpu.PrefetchScalarGridSpec(
        num_scalar_prefetch=1,
        grid=(n // td,),
        in_specs=[smem_rows, smem_rows, pl.BlockSpec((td, d), lambda i, zf: (i, 0))],
        out_specs=pl.BlockSpec(memory_space=pl.ANY),
        scratch_shapes=[pltpu.VMEM((td * ROW_SUB, LANES), F32), pltpu.VMEM((td * ROW_SUB, LANES), F32),
                        pltpu.VMEM((ZERO_ROWS * ROW_SUB, LANES), F32),
                        pltpu.SemaphoreType.DMA((2,)), pltpu.SemaphoreType.DMA(())],
    )
    return pl.pallas_call(
        functools.partial(_dispatch_kernel, td=td),
        grid_spec=grid_spec,
        out_shape=jax.ShapeDtypeStruct((n_rows * ROW_SUB, LANES), F32),
        compiler_params=pltpu.CompilerParams(dimension_semantics=("arbitrary",)),
        name="moe_dispatch",
    )(zfill, pos[0], pos[1], x2d)


def _gmm_kernel(te_ref, used_ref, first_ref, x_ref, wgu_hbm, wd_hbm, o_ref,
                xb_ref, acc_ref, wgb_ref, wub_ref, wdb_ref, sg_ref, su_ref, sd_ref, sems, *, tm, tf):
    i = pl.program_id(0)
    n_tiles = pl.num_programs(0)
    w = _SwigluWeights(wgu_hbm, wd_hbm, (wgb_ref, wub_ref, wdb_ref), (sg_ref, su_ref, sd_ref), sems, tf)
    active = i < used_ref[0]
    is_first = first_ref[i] == 1

    @pl.when(active)
    def _():
        for lb, blk in enumerate(_from_row_tiles(x_ref, tm)):
            xb_ref[:, lb * LANES:(lb + 1) * LANES] = blk.astype(BF16)

    @pl.when(jnp.logical_and(active, is_first))
    def _():
        e = te_ref[i]

        @pl.when(i == 0)
        def _():
            w.request_head(e)

        for j in range(w.nf):
            w.land(e, j)
            w.apply_chunk(xb_ref, acc_ref, j)

    @pl.when(jnp.logical_and(active, jnp.logical_not(is_first)))
    def _():
        for j in range(w.nf):
            w.apply_chunk(xb_ref, acc_ref, j)

    @pl.when(active)
    def _():
        _to_row_tiles(o_ref, acc_ref[...], tm)
        nxt = jnp.minimum(i + 1, n_tiles - 1)

        @pl.when(jnp.logical_and(i + 1 < n_tiles, first_ref[nxt] == 1))
        def _():
            w.request_head(te_ref[nxt])

    @pl.when(jnp.logical_not(active))
    def _():
        o_ref[...] = jnp.zeros_like(o_ref)


def _gmm(tile_expert, n_used, tile_first, xs, w_gu, w_down, *, tm, tf):
    d = D_MODEL
    n_rows = xs.shape[0] // ROW_SUB
    nf = D_FF // tf
    n_tiles = n_rows // tm
    assert nf >= WEIGHT_SLOTS
    grid_spec = pltpu.PrefetchScalarGridSpec(
        num_scalar_prefetch=3,
        grid=(n_tiles,),
        in_specs=[pl.BlockSpec((tm * ROW_SUB, LANES), lambda i, te, used, first: (jnp.minimum(i, used[0] - 1), 0)),
                  pl.BlockSpec(memory_space=pl.ANY), pl.BlockSpec(memory_space=pl.ANY)],
        out_specs=pl.BlockSpec((tm * ROW_SUB, LANES), lambda i, te, used, first: (i, 0)),
        scratch_shapes=[pltpu.VMEM((tm, d), BF16), pltpu.VMEM((tm, d), F32)] + _SwigluWeights.scratch_shapes(tf),
    )
    return pl.pallas_call(
        functools.partial(_gmm_kernel, tm=tm, tf=tf),
        grid_spec=grid_spec,
        out_shape=jax.ShapeDtypeStruct((n_rows * ROW_SUB, LANES), F32),
        compiler_params=pltpu.CompilerParams(dimension_semantics=("arbitrary",),
                                             vmem_limit_bytes=VMEM_LIMIT_BYTES),
        name="moe_gmm",
    )(tile_expert, n_used, tile_first, xs, w_gu, w_down)


def _combine_kernel(p0_ref, p1_ref, p0_next_ref, p1_next_ref, x_ref, route_ref, lng_ref, lnb_ref, y_hbm, o_ref,
                    ya0_ref, yb0_ref, ya1_ref, yb1_ref, sems, *, tc):
    i = pl.program_id(0)
    n = pl.num_programs(0)
    slots = ((ya0_ref, yb0_ref, sems.at[0]), (ya1_ref, yb1_ref, sems.at[1]))

    def issue(p_refs, slot):
        ya_ref, yb_ref, sem = slots[slot]

        def body(t, carry):
            for k, (p_ref, buf) in enumerate(zip(p_refs, (ya_ref, yb_ref))):
                _tile_copy(y_hbm, p_ref[t], buf, t * ROW_SUB, sem).start(priority=k)
            return carry

        lax.fori_loop(0, tc, body, 0, unroll=8)

    @pl.when(i == 0)
    def _():
        issue((p0_ref, p1_ref), 0)

    group = tc // ROW_SUB

    for slot in range(2):
        @pl.when(i % 2 == slot)
        def _():
            ya_ref, yb_ref, sem = slots[slot]
            nya_ref, nyb_ref, nsem = slots[1 - slot]
            _wait_rows(y_hbm, ya_ref, sem, tc, TOP_K)
            r = route_ref[...]
            g1, g2 = r[:, 2:3], r[:, 3:4]
            z = []
            for lb in range(ROW_SUB):
                a = ya_ref[pl.ds(lb, tc, stride=ROW_SUB), :]
                b = yb_ref[pl.ds(lb, tc, stride=ROW_SUB), :]
                z.append(DEEPNORM_ALPHA * x_ref[:, lb * LANES:(lb + 1) * LANES] + (g1 * a + g2 * b))
                for t in range(lb * group, (lb + 1) * group):
                    for k, (p_ref, buf) in enumerate(zip((p0_next_ref, p1_next_ref), (nya_ref, nyb_ref))):
                        _tile_copy(y_hbm, p_ref[t], buf, t * ROW_SUB, nsem).start(priority=k)
            o_ref[...] = _layer_norm(jnp.concatenate(z, axis=1), lng_ref[...], lnb_ref[...])

            @pl.when(i == n - 1)
            def _():
                _wait_rows(y_hbm, nya_ref, nsem, tc, TOP_K)


def _combine(pos, x2d, route2d, ys, ln_g, ln_b, *, tc):
    n, d = x2d.shape
    steps = n // tc
    smem_rows = pl.BlockSpec((tc,), lambda i: (i,), memory_space=pltpu.SMEM)
    smem_next = pl.BlockSpec((tc,), lambda i: (jnp.minimum(i + 1, steps - 1),), memory_space=pltpu.SMEM)
    return pl.pallas_call(
        functools.partial(_combine_kernel, tc=tc),
        grid=(steps,),
        in_specs=[smem_rows, smem_rows, smem_next, smem_next,
                  pl.BlockSpec((tc, d), lambda i: (i, 0)),
                  pl.BlockSpec((tc, ROUTE_LANES), lambda i: (i, 0)),
                  _const_spec((1, d)), _const_spec((1, d)),
                  pl.BlockSpec(memory_space=pl.ANY)],
        out_specs=pl.BlockSpec((tc, d), lambda i: (i, 0)),
        scratch_shapes=[pltpu.VMEM((tc * ROW_SUB, LANES), F32) for _ in range(2 * TOP_K)]
                       + [pltpu.SemaphoreType.DMA((2,))],
        out_shape=jax.ShapeDtypeStruct((n, d), F32),
        compiler_params=pltpu.CompilerParams(dimension_semantics=("arbitrary",),
                                             vmem_limit_bytes=VMEM_LIMIT_BYTES),
        name="moe_combine",
    )(pos[0], pos[1], pos[0], pos[1], x2d, route2d, ln_g[None, :], ln_b[None, :], ys)


def _moe(x2d, route2d, ids, counts, w_gu, w_down, ln_g, ln_b, *, tm, tf, td, tc):
    n, d = x2d.shape
    cnt =counts[0, :N_EXPERTS].astype(jnp.int32)
    tiles_per = (cnt + tm - 1) // tm
    tile_end = jnp.cumsum(tiles_per)
    offs = (tile_end - tiles_per) * tm
    n_tiles = (n * TOP_K) // tm + N_EXPERTS
    n_used = tile_end[-1:]
    t_idx = jnp.arange(n_tiles, dtype=jnp.int32)
    tile_expert = jnp.sum(jnp.minimum(t_idx, n_used - 1)[:, None] >= tile_end[None, :], axis=1).astype(jnp.int32)
    tile_start = tile_end - tiles_per
    tile_first = jnp.logical_and(t_idx == tile_start[tile_expert], t_idx < n_used).astype(jnp.int32)
    group_start = sum(jnp.where(ids[0:TOP_K] == e, offs[e], 0) for e in range(N_EXPERTS))
    pos = (group_start + ids[TOP_K:2 * TOP_K]) * ROW_SUB
    pad_start = ((offs + cnt) // ZERO_ROWS) * ZERO_ROWS
    pad_blocks = (offs + tiles_per * tm - pad_start) // ZERO_ROWS
    tail_start = n_used * tm
    tail_blocks = (n_tiles - n_used) * (tm // ZERO_ROWS)
    zfill = jnp.concatenate([pad_start * ROW_SUB, tail_start * ROW_SUB, pad_blocks, tail_blocks]).astype(jnp.int32)
    xs = _dispatch(zfill, pos, x2d, n_tiles * tm, td=td)
    ys = _gmm(tile_expert, n_used, tile_first, xs, w_gu, w_down, tm=tm, tf=tf)
    return _combine(pos, x2d, route2d, ys, ln_g, ln_b, tc=tc)


def kernel(x, positions, ab_w_in, gla_w_gate2, gla_b_gate, gla_norm, ret_norm, ab_w_out, ab_ln1_g, ab_ln1_b, ffn_w_gu, ffn_w_down, ab_ln2_g, ab_ln2_b, c_w_in, c_conv_w, c_a_log, c_dt_bias, c_norm, c_w_out, c_ln1_g, c_ln1_b, moe_w_router, moe_b_router, moe_w_gu, moe_w_down, c_ln2_g, c_ln2_b):
    bsz, seq, d = x.shape
    for layer in range(DEPTH):
        i = layer // 2
        if layer % 2 == 0:
            x = _l0_mixer(x, positions, ab_w_in[i], gla_w_gate2[i], gla_b_gate[i], gla_norm[i], ret_norm[i],
                          ab_w_out[i], ab_ln1_g[i], ab_ln1_b[i], tt=L0_TIME_TILE)
            x = _ffn(x.reshape(bsz * seq, d), ffn_w_gu[i], ffn_w_down[i], ab_ln2_g[i], ab_ln2_b[i],
                     tm=FFN_ROW_TILE, tf=FF_TILE).reshape(bsz, seq, d)
        else:
            x, route, ids, counts = _l1_mixer(x, c_w_in[i], c_conv_w[i], c_a_log[i], c_dt_bias[i], c_norm[i],
                                              c_w_out[i], c_ln1_g[i], c_ln1_b[i], moe_w_router[i], moe_b_router[i],
                                              tt=L1_TIME_TILE)
            x = _moe(x.reshape(bsz * seq, d), route.reshape(bsz * seq, ROUTE_LANES), ids, counts, moe_w_gu[i],
                     moe_w_down[i], c_ln2_g[i], c_ln2_b[i], tm=MOE_ROW_TILE, tf=FF_TILE, td=ROUTE_TILE,
                     tc=ROUTE_TILE).reshape(bsz, seq, d)
    return x
```

```python
import functools

import numpy as np
import jax
import jax.numpy as jnp
from jax import lax
from jax.experimental import pallas as pl
from jax.experimental.pallas import tpu as pltpu

F32 = jnp.float32
BF16 = jnp.bfloat16

D_MODEL = 1024
DEPTH = 2
CHUNK = 64
GLA_HEADS, GLA_DK, GLA_DV, GLA_GATE_RANK, GLA_TAU = 4, 64, 128, 16, 16.0
RET_HEADS, RET_DK, RET_DV = 4, 64, 128
ROPE_BASE = 10000.0
GDN_HEADS, GDN_DK, GDN_DV = 8, 128, 128
CONV_WIDTH = 4
D_FF = 3584
N_EXPERTS = 8
TOP_K = 2
NORM_EPS = 1e-5
L2_EPS = 1e-6
DEEPNORM_ALPHA = (2.0 * DEPTH) ** 0.25

GLA_QK = GLA_HEADS * GLA_DK
GLA_V = GLA_HEADS * GLA_DV
RET_QK = RET_HEADS * RET_DK
RET_V = RET_HEADS * RET_DV
GDN_QK = GDN_HEADS * GDN_DK
GDN_V = GDN_HEADS * GDN_DV

LANES = 128
VMEM_LIMIT_BYTES = 56 * 1024 * 1024
NEG_BIG = -1e30

L0_TIME_TILE = 256
L1_TIME_TILE = 256
FFN_ROW_TILE = 512
MOE_ROW_TILE = 512
FF_TILE = 512
ROUTE_TILE = 512


def _mm(a, b):
    return jnp.dot(a.astype(BF16), b.astype(BF16), preferred_element_type=F32)


def _mm_nt(a, b):
    return lax.dot_general(a.astype(BF16), b.astype(BF16), (((1,), (1,)), ((), ())),
                           preferred_element_type=F32)


def _mm_tn(a, b):
    return lax.dot_general(a.astype(BF16), b.astype(BF16), (((0,), (0,)), ((), ())),
                           preferred_element_type=F32)


def _split3(x):
    hi = x.astype(BF16)
    r1 = x - hi.astype(F32)
    mid = r1.astype(BF16)
    lo = (r1 - mid.astype(F32)).astype(BF16)
    return hi, mid, lo


def _mm_exact_lhs01(m01, x):
    hi, mid, lo = _split3(x)
    return (jnp.dot(m01, hi, preferred_element_type=F32)
            + jnp.dot(m01, mid, preferred_element_type=F32)
            + jnp.dot(m01, lo, preferred_element_type=F32))


def _sigmoid(x):
    return 0.5 * jnp.tanh(0.5 * x) + 0.5


def _silu(x):
    return x * _sigmoid(x)


def _softplus(x):
    return jnp.maximum(x, 0.0) + jnp.log(1.0 + jnp.exp(-jnp.abs(x)))


def _layer_norm(x, g, b):
    mu = jnp.mean(x, axis=-1, keepdims=True)
    xc = x - mu
    var = jnp.mean(xc * xc, axis=-1, keepdims=True)
    return xc * lax.rsqrt(var + NORM_EPS) * g + b


def _chunk_tril_np(tt):
    i = np.arange(tt)
    same = (i[:, None] // CHUNK) == (i[None, :] // CHUNK)
    return (same & (i[None, :] <= i[:, None])).astype(np.float32)


def _const_spec(shape):
    nd = len(shape)
    return pl.BlockSpec(shape, lambda *_: (0,) * nd, pipeline_mode=pl.Buffered(1))


L0_GQ, L0_GK, L0_GV, L0_GR = 0, 256, 512, 1024
L0_RQ, L0_RK, L0_RV, L0_RG = 1536, 1792, 2048, 2560
L0_GA = 3072
L0_COLS = 3200
L0_CHUNK_GROUP = 1


def _pack_l0_w_in(w):
    offs = np.cumsum([0, GLA_QK, GLA_QK, GLA_V, GLA_GATE_RANK, GLA_V, RET_QK, RET_QK, RET_V, RET_V])
    gq, gk, gv, ga, gr, rq, rk, rv, rg = [w[:, offs[i]:offs[i + 1]] for i in range(9)]
    ga = jnp.pad(ga, ((0, 0), (0, LANES - GLA_GATE_RANK)))
    return jnp.concatenate([gq, gk, gv, gr, rq, rk, rv, rg, ga], axis=1)


def _ret_tables():
    h = np.arange(RET_HEADS, dtype=np.float64)
    log_gamma = np.log(1.0 - 2.0 ** (-5.0 - h))
    pos = np.arange(CHUNK, dtype=np.float64)
    diff = pos[:, None] - pos[None, :]
    dmat = np.where(diff >= 0, np.exp(log_gamma[:, None, None] * np.maximum(diff, 0.0)), 0.0)
    xi = np.exp(log_gamma[None, :] * (pos[:, None] + 1.0))
    zeta = np.exp(log_gamma[None, :] * (CHUNK - 1.0 - pos[:, None]))
    decay = np.exp(log_gamma * CHUNK)
    xi_full = np.repeat(xi, RET_DV, axis=1)
    zeta_full = np.repeat(zeta, RET_DK, axis=1)
    decay_full = np.repeat(decay, RET_DK)[None, :]
    return (dmat.astype(np.float32), xi_full.astype(np.float32), zeta_full.astype(np.float32),
            decay_full.astype(np.float32))


def _rope_tables():
    half = RET_DK // 2
    inv_freq = ROPE_BASE ** (-np.arange(0, RET_DK, 2, dtype=np.float32) / RET_DK)
    per_head = np.concatenate([inv_freq, inv_freq])
    freq_full = np.tile(per_head, RET_HEADS)[None, :].astype(np.float32)
    sign = np.tile(np.concatenate([-np.ones(half), np.ones(half)]), RET_HEADS)[None, :].astype(np.float32)
    return freq_full, sign


def _l0_mixer_kernel(x_ref, pos_ref, w_in_ref, wg2_ref, bg_ref, gnorm_ref, rnorm_ref, w_out_ref,
                     lng_ref, lnb_ref, tril_ref, dmat_ref, xi_ref, zeta_ref, rdecay_ref,
                     freq_ref, sign_ref, o_ref, h_ref, mix_ref, sg_ref, sr_ref, *, tt):
    ti = pl.program_id(1)

    @pl.when(ti == 0)
    def _():
        sg_ref[...] = jnp.zeros_like(sg_ref)
        sr_ref[...] = jnp.zeros_like(sr_ref)

    x = x_ref[...]
    h_ref[...] = _mm(x, w_in_ref[...])

    lane = lax.broadcasted_iota(jnp.int32, (1, LANES), 1)
    lo_half = lane < GLA_DK
    ci = lax.broadcasted_iota(jnp.int32, (CHUNK, CHUNK), 0)
    cj = lax.broadcasted_iota(jnp.int32, (CHUNK, CHUNK), 1)
    causal = cj <= ci

    z = _mm(h_ref[:, L0_GA:L0_GA + LANES], wg2_ref[...]) + bg_ref[...]
    log_a = -_softplus(-z) * (1.0 / GLA_TAU)
    b = _mm_exact_lhs01(tril_ref[...], log_a)
    eb = jnp.exp(b)
    q_dec = h_ref[:, L0_GQ:L0_GQ + GLA_QK] * (GLA_DK ** -0.5) * eb
    k_all = h_ref[:, L0_GK:L0_GK + GLA_QK]
    k_neg = k_all * jnp.exp(-b)

    ang = pos_ref[...] * freq_ref[:, 0:LANES]
    cos = jnp.concatenate([jnp.cos(ang)] * (RET_QK // LANES), axis=1)
    sin = jnp.concatenate([jnp.sin(ang)] * (RET_QK // LANES), axis=1) * sign_ref[...]
    half = RET_DK // 2
    lane256 = lax.broadcasted_iota(jnp.int32, (1, RET_QK), 1)
    first_half = (lane256 & (RET_DK - 1)) < half

    def rope(t):
        swapped = jnp.where(first_half, pltpu.roll(t, RET_QK - half, 1), pltpu.roll(t, half, 1))
        return t * cos + swapped * sin

    rq = rope(h_ref[:, L0_RQ:L0_RQ + RET_QK]) * (RET_DK ** -0.5)
    rk = rope(h_ref[:, L0_RK:L0_RK + RET_QK])

    n_chunks = tt // CHUNK
    heads = range(GLA_HEADS)
    pairs = range(GLA_HEADS // 2)
    pair_lanes = lambda hd: slice((hd // 2) * LANES, (hd // 2 + 1) * LANES)
    head_mask = lambda hd: lo_half if hd % 2 == 0 else jnp.logical_not(lo_half)
    vcol = lambda base, hd: slice(base + hd * GLA_DV, base + (hd + 1) * GLA_DV)
    sg = [sg_ref[p] for p in pairs]
    sr = [sr_ref[p] for p in pairs]
    for c0 in range(0, n_chunks, L0_CHUNK_GROUP):
        group = range(c0, c0 + L0_CHUNK_GROUP)
        rows = {c: slice(c * CHUNK, (c + 1) * CHUNK) for c in group}
        b_last = {c: b[(c + 1) * CHUNK - 1:(c + 1) * CHUNK, :] for c in group}
        k_end = {c: k_all[rows[c]] * jnp.exp(b_last[c] - b[rows[c]]) for c in group}
        rk_c = {c: rk[rows[c]] for c in group}
        rk_z = {c: rk_c[c] * zeta_ref[...] for c in group}
        inst = [(c, hd) for c in group for hd in heads]
        g_q = {ch: jnp.where(head_mask(ch[1]), q_dec[rows[ch[0]], pair_lanes(ch[1])], 0.0) for ch in inst}
        r_q = {ch: jnp.where(head_mask(ch[1]), rq[rows[ch[0]], pair_lanes(ch[1])], 0.0) for ch in inst}
        g_v = {ch: h_ref[rows[ch[0]], vcol(L0_GV, ch[1])] for ch in inst}
        r_v = {ch: h_ref[rows[ch[0]], vcol(L0_RV, ch[1])] for ch in inst}
        g_att = {(c, hd): jnp.where(causal, _mm_nt(g_q[c, hd], k_neg[rows[c], pair_lanes(hd)]), 0.0)
                 for c, hd in inst}
        r_att = {(c, hd): _mm_nt(r_q[c, hd], rk_c[c][:, pair_lanes(hd)]) * dmat_ref[hd] for c, hd in inst}
        g_d = {(c, hd): _mm_tn(g_v[c, hd], k_end[c][:, pair_lanes(hd)]) for c, hd in inst}
        r_d = {(c, hd): _mm_tn(r_v[c, hd], rk_z[c][:, pair_lanes(hd)]) for c, hd in inst}
        sg_in, sr_in = {}, {}
        for c in group:
            sg_in[c], sr_in[c] = sg, sr
            dec_c = jnp.exp(b_last[c])
            sg = [sg[p] * dec_c[:, p * LANES:(p + 1) * LANES] + jnp.where(lo_half, g_d[c, 2 * p], g_d[c, 2 * p + 1])
                  for p in pairs]
            sr = [sr[p] * rdecay_ref[:, p * LANES:(p + 1) * LANES]
                  + jnp.where(lo_half, r_d[c, 2 * p], r_d[c, 2 * p + 1]) for p in pairs]
        g_o = {(c, hd): _mm(g_att[c, hd], g_v[c, hd]) + _mm_nt(g_q[c, hd], sg_in[c][hd // 2]) for c, hd in inst}
        r_o = {(c, hd): _mm(r_att[c, hd], r_v[c, hd])
               + _mm_nt(r_q[c, hd], sr_in[c][hd // 2]) * xi_ref[:, vcol(0, hd)] for c, hd in inst}
        for c, hd in inst:
            mix_ref[rows[c], vcol(0, hd)] = g_o[c, hd]
            mix_ref[rows[c], vcol(GLA_V, hd)] = r_o[c, hd]
    for p in pairs:
        sg_ref[p] = sg[p]
        sr_ref[p] = sr[p]

    for hd in range(GLA_HEADS):
        sl = slice(hd * GLA_DV, (hd + 1) * GLA_DV)
        o = mix_ref[:, sl]
        o = o * lax.rsqrt(jnp.mean(o * o, axis=-1, keepdims=True) + NORM_EPS) * gnorm_ref[:, sl]
        mix_ref[:, sl] = o * _silu(h_ref[:, L0_GR + hd * GLA_DV:L0_GR + (hd + 1) * GLA_DV])
    for hd in range(RET_HEADS):
        sl = slice(hd * RET_DV, (hd + 1) * RET_DV)
        o = mix_ref[:, GLA_V + hd * RET_DV:GLA_V + (hd + 1) * RET_DV]
        oc = o - jnp.mean(o, axis=-1, keepdims=True)
        o = oc * lax.rsqrt(jnp.mean(oc * oc, axis=-1, keepdims=True) + NORM_EPS) * rnorm_ref[:, sl]
        mix_ref[:, GLA_V + hd * RET_DV:GLA_V + (hd + 1) * RET_DV] = (
            o * _silu(h_ref[:, L0_RG + hd * RET_DV:L0_RG + (hd + 1) * RET_DV]))

    y = _mm(mix_ref[...], w_out_ref[...])
    o_ref[...] = _layer_norm(DEEPNORM_ALPHA * x + y, lng_ref[...], lnb_ref[...])


def _l0_mixer(x, positions, w_in, w_gate2, b_gate, gla_norm, ret_norm, w_out, ln_g, ln_b, *, tt):
    bsz, seq, d = x.shape
    w_in_p = _pack_l0_w_in(w_in).astype(BF16)
    wg2 = jnp.pad(w_gate2, ((0, LANES - GLA_GATE_RANK), (0, 0))).astype(BF16)
    dmat, xi_full, zeta_full, rdecay = _ret_tables()
    freq_full, sign = _rope_tables()
    pos_f = positions.astype(F32)[..., None]
    consts = [jnp.asarray(_chunk_tril_np(tt), BF16), jnp.asarray(dmat), jnp.asarray(xi_full),
              jnp.asarray(zeta_full), jnp.asarray(rdecay), jnp.asarray(freq_full), jnp.asarray(sign)]
    params = [w_in_p, wg2, b_gate[None, :], gla_norm.reshape(1, GLA_V), ret_norm.reshape(1, RET_V),
              w_out.astype(BF16), ln_g[None, :], ln_b[None, :]]
    tile = lambda w: pl.BlockSpec((None, tt, w), lambda b, t: (b, t, 0))
    return pl.pallas_call(
        functools.partial(_l0_mixer_kernel, tt=tt),
        grid=(bsz, seq // tt),
        in_specs=[tile(d), tile(1)] + [_const_spec(a.shape) for a in params + consts],
        out_specs=tile(d),
        out_shape=jax.ShapeDtypeStruct((bsz, seq, d), F32),
        scratch_shapes=[pltpu.VMEM((tt, L0_COLS), F32), pltpu.VMEM((tt, GLA_V + RET_V), F32),
                        pltpu.VMEM((GLA_HEADS // 2, GLA_DV, LANES), F32),
                        pltpu.VMEM((RET_HEADS // 2, RET_DV, LANES), F32)],
        compiler_params=pltpu.CompilerParams(dimension_semantics=("arbitrary", "arbitrary"),
                                             vmem_limit_bytes=VMEM_LIMIT_BYTES),
        name="l0_mixer",
    )(x, pos_f, *params, *consts)


WEIGHT_SLOTS = 2


class _SwigluWeights:
    def __init__(self, wgu_hbm, wd_hbm, resident, staging, sems, tf):
        self.wgu_hbm, self.wd_hbm = wgu_hbm, wd_hbm
        self.wgb, self.wub, self.wdb = resident
        self.sg, self.su, self.sd = staging
        self.sems, self.tf, self.nf = sems, tf, D_FF // tf

    def _copies(self, e, j):
        slot = j % WEIGHT_SLOTS
        cols = pl.ds(j * self.tf, self.tf)
        up_cols = pl.ds(D_FF + j * self.tf, self.tf)
        return (pltpu.make_async_copy(self.wgu_hbm.at[e, :, cols], self.sg.at[slot], self.sems.at[slot]),
                pltpu.make_async_copy(self.wgu_hbm.at[e, :, up_cols], self.su.at[slot], self.sems.at[slot]),
                pltpu.make_async_copy(self.wd_hbm.at[e, cols, :], self.sd.at[slot], self.sems.at[slot]))

    def request(self, e, j):
        for c in self._copies(e, j):
            c.start()

    def request_head(self, e):
        for j in range(WEIGHT_SLOTS):
            self.request(e, j)

    def land(self, e, j):
        slot = j % WEIGHT_SLOTS
        for c in self._copies(e, j):
            c.wait()
        self.wgb[j] = self.sg[slot].astype(BF16)
        self.wub[j] = self.su[slot].astype(BF16)
        self.wdb[j] = self.sd[slot].astype(BF16)
        if j + WEIGHT_SLOTS < self.nf:
            self.request(e, j + WEIGHT_SLOTS)

    def apply_chunk(self, xb_ref, acc_ref, j):
        xb = xb_ref[...]
        gt = jnp.dot(xb, self.wgb[j], preferred_element_type=F32)
        up = jnp.dot(xb, self.wub[j], preferred_element_type=F32)
        part = jnp.dot((_silu(gt) * up).astype(BF16), self.wdb[j], preferred_element_type=F32)
        if j == 0:
            acc_ref[...] = part
        else:
            acc_ref[...] += part

    @staticmethod
    def scratch_shapes(tf):
        nf, d = D_FF // tf, D_MODEL
        return [pltpu.VMEM((nf, d, tf), BF16), pltpu.VMEM((nf, d, tf), BF16), pltpu.VMEM((nf, tf, d), BF16),
                pltpu.VMEM((WEIGHT_SLOTS, d, tf), F32), pltpu.VMEM((WEIGHT_SLOTS, d, tf), F32),
                pltpu.VMEM((WEIGHT_SLOTS, tf, d), F32), pltpu.SemaphoreType.DMA((WEIGHT_SLOTS,))]


def _ffn_kernel(x_ref, wgu_hbm, wd_hbm, lng_ref, lnb_ref, o_ref, xb_ref, acc_ref,
                wgb_ref, wub_ref, wdb_ref, sg_ref, su_ref, sd_ref, sems, *, tf):
    i = pl.program_id(0)
    w = _SwigluWeights(wgu_hbm, wd_hbm, (wgb_ref, wub_ref, wdb_ref), (sg_ref, su_ref, sd_ref), sems, tf)
    xb_ref[...] = x_ref[...].astype(BF16)

    @pl.when(i == 0)
    def _():
        w.request_head(0)
        for j in range(w.nf):
            w.land(0, j)
            w.apply_chunk(xb_ref, acc_ref, j)

    @pl.when(i > 0)
    def _():
        for j in range(w.nf):
            w.apply_chunk(xb_ref, acc_ref, j)

    o_ref[...] = _layer_norm(DEEPNORM_ALPHA * x_ref[...] + acc_ref[...], lng_ref[...], lnb_ref[...])


def _ffn(x2d, w_gu, w_down, ln_g, ln_b, *, tm, tf):
    n, d = x2d.shape
    assert D_FF // tf >= WEIGHT_SLOTS
    return pl.pallas_call(
        functools.partial(_ffn_kernel, tf=tf),
        grid=(n // tm,),
        in_specs=[pl.BlockSpec((tm, d), lambda i: (i, 0)),
                  pl.BlockSpec(memory_space=pl.ANY), pl.BlockSpec(memory_space=pl.ANY),
                  _const_spec((1, d)), _const_spec((1, d))],
        out_specs=pl.BlockSpec((tm, d), lambda i: (i, 0)),
        out_shape=jax.ShapeDtypeStruct((n, d), F32),
        scratch_shapes=[pltpu.VMEM((tm, d), BF16), pltpu.VMEM((tm, d), F32)] + _SwigluWeights.scratch_shapes(tf),
        compiler_params=pltpu.CompilerParams(dimension_semantics=("arbitrary",),
                                             vmem_limit_bytes=VMEM_LIMIT_BYTES),
        name="ffn",
    )(x2d, w_gu[None], w_down[None], ln_g[None, :], ln_b[None, :])


L1_Q, L1_K, L1_V, L1_GATE, L1_AB = 0, 1024, 2048, 3072, 4096
L1_CONV = 3 * GDN_QK
L1_COLS = 4224
L1_BETA_LANE = GDN_HEADS
CONV_PAD = 8
L1_PROJ_BLOCK = 512
L1_CHUNK_GROUP = 2
ROUTE_LANES = LANES


def _pack_l1_w_in(w):
    offs = np.cumsum([0, L1_CONV, GDN_HEADS, GDN_HEADS, GDN_V])
    qkv, a_in, b_in, gate = [w[:, offs[i]:offs[i + 1]] for i in range(4)]
    ab = jnp.pad(jnp.concatenate([a_in, b_in], axis=1), ((0, 0), (0, LANES - 2 * GDN_HEADS)))
    return jnp.concatenate([qkv, gate, ab], axis=1)


def _l1_mixer_kernel(x_ref, w_in_ref, conv_ref, alog_ref, dtb_ref, cnorm_ref, w_out_ref, lng_ref, lnb_ref,
                     tril_ref, wr_hi_ref, wr_lo_ref, br_ref, stril_ref,
                     o_ref, route_ref, ids_ref, counts_ref,
                     h2_ref, qkv_ref, mix_ref, s_ref, u_ref, w_ref, aqk_ref, ext_ref, carry_ref, *, tt):
    ti = pl.program_id(1)

    @pl.when(ti == 0)
    def _():
        s_ref[...] = jnp.zeros_like(s_ref)
        ext_ref[:, 0:CONV_PAD, :] = jnp.zeros((L1_CONV // LANES, CONV_PAD, LANES), F32)

    x = x_ref[...]
    xb = x.astype(BF16)
    h2_ref[...] = jnp.dot(xb, w_in_ref[:, L1_CONV:L1_COLS], preferred_element_type=F32)

    half = tt // 2
    lanes_per_block = L1_PROJ_BLOCK // LANES
    for blk in range(L1_CONV // L1_PROJ_BLOCK):
        h_blk = jnp.dot(xb, w_in_ref[:, blk * L1_PROJ_BLOCK:(blk + 1) * L1_PROJ_BLOCK], preferred_element_type=F32)
        for l in range(lanes_per_block):
            ext_ref[blk * lanes_per_block + l, CONV_PAD:CONV_PAD + tt, :] = h_blk[:, l * LANES:(l + 1) * LANES]
        for l in range(lanes_per_block):
            lb = blk * lanes_per_block + l
            lanes = slice(lb * LANES, (lb + 1) * LANES)
            for parity in range(2):
                conv = None
                for j in range(CONV_WIDTH):
                    first_row = CONV_PAD - (CONV_WIDTH - 1 - j) + parity
                    term = ext_ref[lb, pl.ds(first_row, half, stride=2), :] * conv_ref[j:j + 1, lanes]
                    conv = term if conv is None else conv + term
                act = _silu(conv)
                if lb * LANES < L1_V:
                    scale = GDN_DK ** -0.5 if lb * LANES < L1_K else 1.0
                    act = act * (lax.rsqrt(jnp.sum(act * act, axis=-1, keepdims=True) + L2_EPS) * scale)
                qkv_ref[lb, pl.ds(parity, half, stride=2), :] = act
            ext_ref[lb, 0:CONV_PAD, :] = ext_ref[lb, tt:tt + CONV_PAD, :]

    ab = h2_ref[:, L1_AB - L1_CONV:L1_AB - L1_CONV + LANES]
    g_blk = -jnp.exp(alog_ref[...]) * _softplus(ab + dtb_ref[...])
    beta_blk = _sigmoid(ab)
    gc_blk = _mm_exact_lhs01(tril_ref[...], g_blk)
    eg_blk = jnp.exp(gc_blk)

    ci = lax.broadcasted_iota(jnp.int32, (CHUNK, CHUNK), 0)
    cj = lax.broadcasted_iota(jnp.int32, (CHUNK, CHUNK), 1)
    incl = cj <= ci
    strict = cj < ci

    heads = range(GDN_HEADS)
    col = lambda base, hd: slice(base + hd * GDN_DK, base + (hd + 1) * GDN_DK)
    blk_of = lambda base, hd: base // LANES + hd
    n_chunks = tt // CHUNK

    s = [s_ref[hd] for hd in heads]
    for c0 in range(0, n_chunks, L1_CHUNK_GROUP):
        inst = [(c, hd) for c in range(c0, c0 + L1_CHUNK_GROUP) for hd in heads]
        n_i = range(len(inst))
        rows = [slice(c * CHUNK, (c + 1) * CHUNK) for c, _ in inst]
        gc_c = {c: gc_blk[c * CHUNK:(c + 1) * CHUNK] for c in range(c0, c0 + L1_CHUNK_GROUP)}
        gc_t = {c: gc_c[c].T for c in gc_c}
        kdec_scale = {c: jnp.exp(gc_c[c][CHUNK - 1:CHUNK, :] - gc_c[c]) for c in gc_c}
        q_h = [qkv_ref[blk_of(L1_Q, hd), rows[i], :] for i, (c, hd) in enumerate(inst)]
        k_h = [qkv_ref[blk_of(L1_K, hd), rows[i], :] for i, (c, hd) in enumerate(inst)]
        v_h = [qkv_ref[blk_of(L1_V, hd), rows[i], :] for i, (c, hd) in enumerate(inst)]
        beta = [beta_blk[rows[i], L1_BETA_LANE + hd:L1_BETA_LANE + hd + 1] for i, (c, hd) in enumerate(inst)]
        eg = [eg_blk[rows[i], hd:hd + 1] for i, (c, hd) in enumerate(inst)]
        decay = [jnp.exp(jnp.where(incl, gc_c[c][:, hd:hd + 1] - gc_t[c][hd:hd + 1, :], NEG_BIG)) for c, hd in inst]
        kb = [k_h[i] * beta[i] for i in n_i]
        low = [jnp.where(strict, _mm_nt(kb[i], k_h[i]) * decay[i], 0.0) for i in n_i]
        a_qk = [_mm_nt(q_h[i], k_h[i]) * decay[i] for i in n_i]
        a_m = [-low[i] for i in n_i]
        m = [_mm(low[i], low[i]) for i in n_i]
        for it in range(5):
            am = [_mm(a_m[i], m[i]) for i in n_i]
            a_m = [a_m[i] + m[i] + am[i] for i in n_i]
            if it < 4:
                m = [_mm(m[i], m[i]) for i in n_i]
        rhs = [jnp.concatenate([v_h[i] * beta[i], kb[i] * eg[i]], axis=1) for i in n_i]
        uw = [rhs[i] + _mm(a_m[i], rhs[i]) for i in n_i]
        for i, (c, hd) in enumerate(inst):
            u_ref[rows[i], col(0, hd)] = uw[i][:, 0:GDN_DV]
            w_ref[rows[i], col(0, hd)] = uw[i][:, GDN_DV:GDN_DV + GDN_DK]
            aqk_ref[rows[i], hd * LANES:hd * LANES + CHUNK] = a_qk[i]
            qkv_ref[blk_of(L1_Q, hd), rows[i], :] = q_h[i] * eg[i]
            qkv_ref[blk_of(L1_K, hd), rows[i], :] = k_h[i] * kdec_scale[c][:, hd:hd + 1]

        for c in range(c0, c0 + L1_CHUNK_GROUP):
            crow = slice(c * CHUNK, (c + 1) * CHUNK)
            e_last = jnp.exp(gc_blk[(c + 1) * CHUNK - 1:(c + 1) * CHUNK, :])
            ws_qs = [_mm(jnp.concatenate([w_ref[crow, col(0, hd)], qkv_ref[blk_of(L1_Q, hd), crow, :]], axis=0), s[hd])
                     for hd in heads]
            v_new = [u_ref[crow, col(0, hd)] - ws_qs[hd][0:CHUNK] for hd in heads]
            o_h = [ws_qs[hd][CHUNK:2 * CHUNK] + _mm(aqk_ref[crow, hd * LANES:hd * LANES + CHUNK], v_new[hd])
                   for hd in heads]
            s = [s[hd] * e_last[:, hd:hd + 1] + _mm_tn(qkv_ref[blk_of(L1_K, hd), crow, :], v_new[hd]) for hd in heads]
            for hd in heads:
                mix_ref[crow, col(0, hd)] = o_h[hd]
    for hd in heads:
        s_ref[hd] = s[hd]

    for hd in range(GDN_HEADS):
        sl = slice(hd * GDN_DV, (hd + 1) * GDN_DV)
        o = mix_ref[:, sl]
        o = o * lax.rsqrt(jnp.mean(o * o, axis=-1, keepdims=True) + NORM_EPS) * cnorm_ref[:, sl]
        mix_ref[:, sl] = o * _silu(h2_ref[:, L1_GATE - L1_CONV + hd * GDN_DV:L1_GATE - L1_CONV + (hd + 1) * GDN_DV])
    y = _mm(mix_ref[...], w_out_ref[...])
    x1 = _layer_norm(DEEPNORM_ALPHA * x + y, lng_ref[...], lnb_ref[...])
    o_ref[...] = x1

    x_hi = x1.astype(BF16)
    x_lo = (x1 - x_hi.astype(F32)).astype(BF16)
    logits = (jnp.dot(x_hi, wr_hi_ref[...], preferred_element_type=F32)
              + jnp.dot(x_hi, wr_lo_ref[...], preferred_element_type=F32)
              + jnp.dot(x_lo, wr_hi_ref[...], preferred_element_type=F32)) + br_ref[...]
    lane = lax.broadcasted_iota(jnp.int32, (tt, ROUTE_LANES), 1)
    lane_f = lane.astype(F32)
    logits = jnp.where(lane < N_EXPERTS, logits, NEG_BIG)
    m1 = jnp.max(logits, axis=-1, keepdims=True)
    i1 = jnp.min(jnp.where(logits == m1, lane_f, float(ROUTE_LANES)), axis=-1, keepdims=True)
    rest = jnp.where(lane_f == i1, NEG_BIG, logits)
    m2 = jnp.max(rest, axis=-1, keepdims=True)
    i2 = jnp.min(jnp.where(rest == m2, lane_f, float(ROUTE_LANES)), axis=-1, keepdims=True)
    e21 = jnp.exp(m2 - m1)
    g1 = 1.0 / (1.0 + e21)
    g2 = e21 * g1
    route_ref[...] = jnp.where(lane == 0, i1, jnp.where(lane == 1, i2, jnp.where(lane == 2, g1, jnp.where(lane == 3, g2, 0.0))))

    @pl.when(jnp.logical_and(pl.program_id(0) == 0, ti == 0))
    def _():
        carry_ref[...] = jnp.zeros_like(carry_ref)

    oh1 = (lane_f == i1).astype(F32)
    oh2 = (lane_f == i2).astype(F32)
    both = oh1 + oh2
    before = jnp.dot(stril_ref[...], both.astype(BF16), preferred_element_type=F32) + carry_ref[...]
    rank1 = jnp.sum(oh1 * before, axis=-1, keepdims=True)
    rank2 = jnp.sum(oh2 * before, axis=-1, keepdims=True)
    table = jnp.where(lane == 0, i1, jnp.where(lane == 1, i2, jnp.where(lane == 2, rank1,
                      jnp.where(lane == 3, rank2, 0.0))))
    ids_ref[...] = table.T[0:ID_ROWS, :].astype(jnp.int32)
    carry_ref[...] += jnp.sum(both, axis=0, keepdims=True)
    counts_ref[...] = carry_ref[...]


def _l1_mixer(x, w_in, conv_w, a_log, dt_bias, c_norm, w_out, ln_g, ln_b, w_router, b_router, *, tt):
    bsz, seq, d = x.shape
    w_in_p = _pack_l1_w_in(w_in).astype(BF16)
    lane_pad = lambda v: jnp.pad(v[None, :], ((0, 0), (0, LANES - v.shape[0])))
    wr = jnp.pad(w_router, ((0, 0), (0, ROUTE_LANES - N_EXPERTS)))
    wr_hi = wr.astype(BF16)
    wr_lo = (wr - wr_hi.astype(F32)).astype(BF16)
    params = [w_in_p, conv_w, lane_pad(a_log), lane_pad(dt_bias), c_norm.reshape(1, GDN_V), w_out.astype(BF16),
              ln_g[None, :], ln_b[None, :], jnp.asarray(_chunk_tril_np(tt), BF16), wr_hi, wr_lo, lane_pad(b_router)]
    i_row = np.arange(tt)
    params.append(jnp.asarray((i_row[None, :] < i_row[:, None]).astype(np.float32), BF16))
    nt = seq // tt
    tile = lambda w: pl.BlockSpec((None, tt, w), lambda b, t: (b, t, 0))
    return pl.pallas_call(
        functools.partial(_l1_mixer_kernel, tt=tt),
        grid=(bsz, nt),
        in_specs=[tile(d)] + [_const_spec(a.shape) for a in params],
        out_specs=[tile(d), tile(ROUTE_LANES), pl.BlockSpec((ID_ROWS, tt), lambda b, t: (0, b * nt + t)),
                   _const_spec((1, ROUTE_LANES))],
        out_shape=[jax.ShapeDtypeStruct((bsz, seq, d), F32), jax.ShapeDtypeStruct((bsz, seq, ROUTE_LANES), F32),
                   jax.ShapeDtypeStruct((ID_ROWS, bsz * seq), jnp.int32),
                   jax.ShapeDtypeStruct((1, ROUTE_LANES), F32)],
        scratch_shapes=[pltpu.VMEM((tt, L1_COLS - L1_CONV), F32),
                        pltpu.VMEM((L1_CONV // LANES, tt, LANES), F32), pltpu.VMEM((tt, GDN_V), F32),
                        pltpu.VMEM((GDN_HEADS, GDN_DK, GDN_DV), F32),
                        pltpu.VMEM((tt, GDN_V), F32), pltpu.VMEM((tt, GDN_QK), F32),
                        pltpu.VMEM((tt, GDN_HEADS * LANES), F32),
                        pltpu.VMEM((L1_CONV // LANES, tt + CONV_PAD, LANES), F32),
                        pltpu.VMEM((1, ROUTE_LANES), F32)],
        compiler_params=pltpu.CompilerParams(dimension_semantics=("arbitrary", "arbitrary"),
                                             vmem_limit_bytes=VMEM_LIMIT_BYTES),
        name="l1_mixer",
    )(x, *params)


ID_ROWS = 4


ROW_SUB = D_MODEL // LANES


def _to_row_tiles(dst_ref, src, n):
    for lb in range(ROW_SUB):
        dst_ref[pl.ds(lb, n, stride=ROW_SUB), :] = src[:, lb * LANES:(lb + 1) * LANES]


def _from_row_tiles(src_ref, n):
    return [src_ref[pl.ds(lb, n, stride=ROW_SUB), :] for lb in range(ROW_SUB)]


def _tile_copy(src_ref, src_sub, dst_ref, dst_sub, sem):
    src = src_ref.at[pl.ds(pl.multiple_of(src_sub, ROW_SUB), ROW_SUB), :]
    dst = dst_ref.at[pl.ds(pl.multiple_of(dst_sub, ROW_SUB), ROW_SUB), :]
    return pltpu.make_async_copy(src, dst, sem)


def _wait_rows(hbm_ref, buf_ref, sem, n, copies):
    for _ in range(copies):
        pltpu.make_async_copy(hbm_ref.at[pl.ds(0, n * ROW_SUB), :], buf_ref, sem).wait()


ZERO_ROWS = 64


def _dispatch_kernel(zfill_ref, p0_ref, p1_ref, x_ref, xs_hbm, xt0_ref, xt1_ref, zero_ref, sems, zsem, *, td):
    p_refs = (p0_ref, p1_ref)
    i = pl.program_id(0)
    last = pl.num_programs(0) - 1

    @pl.when(i == 0)
    def _():
        zero_ref[...] = jnp.zeros_like(zero_ref)
        n_ranges = zfill_ref.shape[0] // 2

        def zero_copy(start_sub, r):
            dst = xs_hbm.at[pl.ds(pl.multiple_of(start_sub + r * (ZERO_ROWS * ROW_SUB), ROW_SUB),
                                  ZERO_ROWS * ROW_SUB), :]
            return pltpu.make_async_copy(zero_ref, dst, zsem)

        def fill(e, wait):
            def body(r, carry):
                copy = zero_copy(zfill_ref[e], r)
                if wait:
                    copy.wait()
                else:
                    copy.start()
                return carry

            lax.fori_loop(0, zfill_ref[n_ranges + e], body, 0)

        for e in range(n_ranges):
            fill(e, wait=False)
        for e in range(n_ranges):
            fill(e, wait=True)

    def step(xt_ref, sem):
        @pl.when(i >= 2)
        def _():
            _wait_rows(xs_hbm, xt_ref, sem, td, TOP_K)

        group = td // ROW_SUB
        for g in range(ROW_SUB):
            x_g = x_ref[g * group:(g + 1) * group, :]
            for lb in range(ROW_SUB):
                xt_ref[pl.ds(g * group * ROW_SUB + lb, group, stride=ROW_SUB), :] = x_g[:, lb * LANES:(lb + 1) * LANES]
            for t in range(g * group, (g + 1) * group):
                for k in range(TOP_K):
                    _tile_copy(xt_ref, t * ROW_SUB, xs_hbm, p_refs[k][t], sem).start(priority=k)

    for slot, (xt_ref, sem) in enumerate(((xt0_ref, sems.at[0]), (xt1_ref, sems.at[1]))):
        @pl.when(i % 2 == slot)
        def _():
            step(xt_ref, sem)

    @pl.when(i == last)
    def _():
        _wait_rows(xs_hbm, xt0_ref, sems.at[0], td, TOP_K)
        _wait_rows(xs_hbm, xt1_ref, sems.at[1], td, TOP_K)


def _dispatch(zfill, pos, x2d, n_rows, *, td):
    n, d = x2d.shape
    smem_rows = pl.BlockSpec((td,), lambda i, zf: (i,), memory_space=pltpu.SMEM)
    assert n // td >= 2
    grid_spec = pltpu.PrefetchScalarGridSpec(
        num_scalar_prefetch=1,
        grid=(n // td,),
        in_specs=[smem_rows, smem_rows, pl.BlockSpec((td, d), lambda i, zf: (i, 0))],
        out_specs=pl.BlockSpec(memory_space=pl.ANY),
        scratch_shapes=[pltpu.VMEM((td * ROW_SUB, LANES), F32), pltpu.VMEM((td * ROW_SUB, LANES), F32),
                        pltpu.VMEM((ZERO_ROWS * ROW_SUB, LANES), F32),
                        pltpu.SemaphoreType.DMA((2,)), pltpu.SemaphoreType.DMA(())],
    )
    return pl.pallas_call(
        functools.partial(_dispatch_kernel, td=td),
        grid_spec=grid_spec,
        out_shape=jax.ShapeDtypeStruct((n_rows * ROW_SUB, LANES), F32),
        compiler_params=pltpu.CompilerParams(dimension_semantics=("arbitrary",)),
        name="moe_dispatch",
    )(zfill, pos[0], pos[1], x2d)


def _gmm_kernel(te_ref, used_ref, first_ref, x_ref, wgu_hbm, wd_hbm, o_ref,
                xb_ref, acc_ref, wgb_ref, wub_ref, wdb_ref, sg_ref, su_ref, sd_ref, sems, *, tm, tf):
    i = pl.program_id(0)
    n_tiles = pl.num_programs(0)
    w = _SwigluWeights(wgu_hbm, wd_hbm, (wgb_ref, wub_ref, wdb_ref), (sg_ref, su_ref, sd_ref), sems, tf)
    active = i < used_ref[0]
    is_first = first_ref[i] == 1

    @pl.when(active)
    def _():
        for lb, blk in enumerate(_from_row_tiles(x_ref, tm)):
            xb_ref[:, lb * LANES:(lb + 1) * LANES] = blk.astype(BF16)

    @pl.when(jnp.logical_and(active, is_first))
    def _():
        e = te_ref[i]

        @pl.when(i == 0)
        def _():
            w.request_head(e)

        for j in range(w.nf):
            w.land(e, j)
            w.apply_chunk(xb_ref, acc_ref, j)

    @pl.when(jnp.logical_and(active, jnp.logical_not(is_first)))
    def _():
        for j in range(w.nf):
            w.apply_chunk(xb_ref, acc_ref, j)

    @pl.when(active)
    def _():
        _to_row_tiles(o_ref, acc_ref[...], tm)
        nxt = jnp.minimum(i + 1, n_tiles - 1)

        @pl.when(jnp.logical_and(i + 1 < n_tiles, first_ref[nxt] == 1))
        def _():
            w.request_head(te_ref[nxt])

    @pl.when(jnp.logical_not(active))
    def _():
        o_ref[...] = jnp.zeros_like(o_ref)


def _gmm(tile_expert, n_used, tile_first, xs, w_gu, w_down, *, tm, tf):
    d = D_MODEL
    n_rows = xs.shape[0] // ROW_SUB
    nf = D_FF // tf
    n_tiles = n_rows // tm
    assert nf >= WEIGHT_SLOTS
    grid_spec = pltpu.PrefetchScalarGridSpec(
        num_scalar_prefetch=3,
        grid=(n_tiles,),
        in_specs=[pl.BlockSpec((tm * ROW_SUB, LANES), lambda i, te, used, first: (jnp.minimum(i, used[0] - 1), 0)),
                  pl.BlockSpec(memory_space=pl.ANY), pl.BlockSpec(memory_space=pl.ANY)],
        out_specs=pl.BlockSpec((tm * ROW_SUB, LANES), lambda i, te, used, first: (i, 0)),
        scratch_shapes=[pltpu.VMEM((tm, d), BF16), pltpu.VMEM((tm, d), F32)] + _SwigluWeights.scratch_shapes(tf),
    )
    return pl.pallas_call(
        functools.partial(_gmm_kernel, tm=tm, tf=tf),
        grid_spec=grid_spec,
        out_shape=jax.ShapeDtypeStruct((n_rows * ROW_SUB, LANES), F32),
        compiler_params=pltpu.CompilerParams(dimension_semantics=("arbitrary",),
                                             vmem_limit_bytes=VMEM_LIMIT_BYTES),
        name="moe_gmm",
    )(tile_expert, n_used, tile_first, xs, w_gu, w_down)


def _combine_kernel(p0_ref, p1_ref, p0_next_ref, p1_next_ref, x_ref, route_ref, lng_ref, lnb_ref, y_hbm, o_ref,
                    ya0_ref, yb0_ref, ya1_ref, yb1_ref, sems, *, tc):
    i = pl.program_id(0)
    n = pl.num_programs(0)
    slots = ((ya0_ref, yb0_ref, sems.at[0]), (ya1_ref, yb1_ref, sems.at[1]))

    def issue(p_refs, slot):
        ya_ref, yb_ref, sem = slots[slot]

        def body(t, carry):
            for k, (p_ref, buf) in enumerate(zip(p_refs, (ya_ref, yb_ref))):
                _tile_copy(y_hbm, p_ref[t], buf, t * ROW_SUB, sem).start(priority=k)
            return carry

        lax.fori_loop(0, tc, body, 0, unroll=8)

    @pl.when(i == 0)
    def _():
        issue((p0_ref, p1_ref), 0)

    group = tc // ROW_SUB

    for slot in range(2):
        @pl.when(i % 2 == slot)
        def _():
            ya_ref, yb_ref, sem = slots[slot]
            nya_ref, nyb_ref, nsem = slots[1 - slot]
            _wait_rows(y_hbm, ya_ref, sem, tc, TOP_K)
            r = route_ref[...]
            g1, g2 = r[:, 2:3], r[:, 3:4]
            z = []
            for lb in range(ROW_SUB):
                a = ya_ref[pl.ds(lb, tc, stride=ROW_SUB), :]
                b = yb_ref[pl.ds(lb, tc, stride=ROW_SUB), :]
                z.append(DEEPNORM_ALPHA * x_ref[:, lb * LANES:(lb + 1) * LANES] + (g1 * a + g2 * b))
                for t in range(lb * group, (lb + 1) * group):
                    for k, (p_ref, buf) in enumerate(zip((p0_next_ref, p1_next_ref), (nya_ref, nyb_ref))):
                        _tile_copy(y_hbm, p_ref[t], buf, t * ROW_SUB, nsem).start(priority=k)
            o_ref[...] = _layer_norm(jnp.concatenate(z, axis=1), lng_ref[...], lnb_ref[...])

            @pl.when(i == n - 1)
            def _():
                _wait_rows(y_hbm, nya_ref, nsem, tc, TOP_K)


def _combine(pos, x2d, route2d, ys, ln_g, ln_b, *, tc):
    n, d = x2d.shape
    steps = n // tc
    smem_rows = pl.BlockSpec((tc,), lambda i: (i,), memory_space=pltpu.SMEM)
    smem_next = pl.BlockSpec((tc,), lambda i: (jnp.minimum(i + 1, steps - 1),), memory_space=pltpu.SMEM)
    return pl.pallas_call(
        functools.partial(_combine_kernel, tc=tc),
        grid=(steps,),
        in_specs=[smem_rows, smem_rows, smem_next, smem_next,
                  pl.BlockSpec((tc, d), lambda i: (i, 0)),
                  pl.BlockSpec((tc, ROUTE_LANES), lambda i: (i, 0)),
                  _const_spec((1, d)), _const_spec((1, d)),
                  pl.BlockSpec(memory_space=pl.ANY)],
        out_specs=pl.BlockSpec((tc, d), lambda i: (i, 0)),
        scratch_shapes=[pltpu.VMEM((tc * ROW_SUB, LANES), F32) for _ in range(2 * TOP_K)]
                       + [pltpu.SemaphoreType.DMA((2,))],
        out_shape=jax.ShapeDtypeStruct((n, d), F32),
        compiler_params=pltpu.CompilerParams(dimension_semantics=("arbitrary",),
                                             vmem_limit_bytes=VMEM_LIMIT_BYTES),
        name="moe_combine",
    )(pos[0], pos[1], pos[0], pos[1], x2d, route2d, ln_g[None, :], ln_b[None, :], ys)


def _moe(x2d, route2d, ids, counts, w_gu, w_down, ln_g, ln_b, *, tm, tf, td, tc):
    n, d = x2d.shape
    cnt =counts[0, :N_EXPERTS].astype(jnp.int32)
    tiles_per = (cnt + tm - 1) // tm
    tile_end = jnp.cumsum(tiles_per)
    offs = (tile_end - tiles_per) * tm
    n_tiles = (n * TOP_K) // tm + N_EXPERTS
    n_used = tile_end[-1:]
    t_idx = jnp.arange(n_tiles, dtype=jnp.int32)
    tile_expert = jnp.sum(jnp.minimum(t_idx, n_used - 1)[:, None] >= tile_end[None, :], axis=1).astype(jnp.int32)
    tile_start = tile_end - tiles_per
    tile_first = jnp.logical_and(t_idx == tile_start[tile_expert], t_idx < n_used).astype(jnp.int32)
    group_start = sum(jnp.where(ids[0:TOP_K] == e, offs[e], 0) for e in range(N_EXPERTS))
    pos = (group_start + ids[TOP_K:2 * TOP_K]) * ROW_SUB
    pad_start = ((offs + cnt) // ZERO_ROWS) * ZERO_ROWS
    pad_blocks = (offs + tiles_per * tm - pad_start) // ZERO_ROWS
    tail_start = n_used * tm
    tail_blocks = (n_tiles - n_used) * (tm // ZERO_ROWS)
    zfill = jnp.concatenate([pad_start * ROW_SUB, tail_start * ROW_SUB, pad_blocks, tail_blocks]).astype(jnp.int32)
    xs = _dispatch(zfill, pos, x2d, n_tiles * tm, td=td)
    ys = _gmm(tile_expert, n_used, tile_first, xs, w_gu, w_down, tm=tm, tf=tf)
    return _combine(pos, x2d, route2d, ys, ln_g, ln_b, tc=tc)


def kernel(x, positions, ab_w_in, gla_w_gate2, gla_b_gate, gla_norm, ret_norm, ab_w_out, ab_ln1_g, ab_ln1_b, ffn_w_gu, ffn_w_down, ab_ln2_g, ab_ln2_b, c_w_in, c_conv_w, c_a_log, c_dt_bias, c_norm, c_w_out, c_ln1_g, c_ln1_b, moe_w_router, moe_b_router, moe_w_gu, moe_w_down, c_ln2_g, c_ln2_b):
    bsz, seq, d = x.shape
    for layer in range(DEPTH):
        i = layer // 2
        if layer % 2 == 0:
            x = _l0_mixer(x, positions, ab_w_in[i], gla_w_gate2[i], gla_b_gate[i], gla_norm[i], ret_norm[i],
                          ab_w_out[i], ab_ln1_g[i], ab_ln1_b[i], tt=L0_TIME_TILE)
            x = _ffn(x.reshape(bsz * seq, d), ffn_w_gu[i], ffn_w_down[i], ab_ln2_g[i], ab_ln2_b[i],
                     tm=FFN_ROW_TILE, tf=FF_TILE).reshape(bsz, seq, d)
        else:
            x, route, ids, counts = _l1_mixer(x, c_w_in[i], c_conv_w[i], c_a_log[i], c_dt_bias[i], c_norm[i],
                                              c_w_out[i], c_ln1_g[i], c_ln1_b[i], moe_w_router[i], moe_b_router[i],
                                              tt=L1_TIME_TILE)
            x = _moe(x.reshape(bsz * seq, d), route.reshape(bsz * seq, ROUTE_LANES), ids, counts, moe_w_gu[i],
                     moe_w_down[i], c_ln2_g[i], c_ln2_b[i], tm=MOE_ROW_TILE, tf=FF_TILE, td=ROUTE_TILE,
                     tc=ROUTE_TILE).reshape(bsz, seq, d)
    return x
```
